```python
import jax, jax.numpy as jnp
from jax import lax
import numpy as np

D_MODEL = 1024
BATCH = 8
SEQ = 2048
DEPTH = 2
DEC_BATCH = 128
DEC_SEQ = 1
PAST_LEN = 16384
PAGE_SIZE = 128

N_BRANCH = 4
BRANCH_W = D_MODEL // N_BRANCH
RWKV_N = 64
RWKV_H = BRANCH_W // RWKV_N
RWKV_W_LORA = 32
RWKV_A_LORA = 32
RWKV_G_LORA = 64
RWKV_COLS = 3 * BRANCH_W + RWKV_W_LORA + RWKV_A_LORA + RWKV_G_LORA
RWKV_GN_EPS = 64e-5
ML_DH = 64
ML_H = BRANCH_W // ML_DH
ML_CONV = 4
GLA_H = 4
GLA_DV = BRANCH_W // GLA_H
GLA_DK = GLA_DV // 2
GLA_LORA = 16
GLA_TAU = 16.0
RG_BLOCKS = 4
RG_BW = BRANCH_W // RG_BLOCKS
RG_CONV = 4
RG_C = 8.0
CHUNK = 64
D_FF = 2816
N_EXPERTS = 8
TOP_K = 2
N_DENSE = (DEPTH + 1) // 2
N_MOE = DEPTH // 2
DN_ALPHA = (2 * DEPTH) ** 0.25
DN_BETA = (8 * DEPTH) ** -0.25
LN_EPS = 1e-5

IN_SIZES = (
    RWKV_COLS,
    2 * BRANCH_W,
    BRANCH_W,
    BRANCH_W,
    ML_H,
    ML_H,
    GLA_H * GLA_DK,
    GLA_H * GLA_DK,
    BRANCH_W,
    BRANCH_W,
    GLA_LORA,
    BRANCH_W,
    BRANCH_W,
    N_BRANCH * D_MODEL,
)
IN_COLS = sum(IN_SIZES)

STATE_SHAPES = (
    (RWKV_COLS,),
    (RWKV_H, RWKV_N, RWKV_N),
    (ML_CONV - 1, 2 * BRANCH_W),
    (ML_H, ML_DH, ML_DH),
    (ML_H, ML_DH),
    (ML_H,),
    (GLA_H, GLA_DK, GLA_DV),
    (RG_CONV - 1, BRANCH_W),
    (BRANCH_W,),
)

kernel_name = 'hybrid_rwkv7_mlstm_gla_rglru_decode_step'

F32 = jnp.float32


def _split(u, sizes):
    idx = [int(i) for i in np.cumsum(sizes)[:-1]]
    return jnp.split(u, idx, axis=-1)


def _layer_norm(x, g, b):
    xf = x.astype(F32)
    mu = xf.mean(-1, keepdims=True)
    var = jnp.square(xf - mu).mean(-1, keepdims=True)
    return ((xf - mu) * lax.rsqrt(var + LN_EPS) * g + b).astype(x.dtype)


def _head_norm(x, eps):
    mu = x.mean(-1, keepdims=True)
    var = jnp.square(x - mu).mean(-1, keepdims=True)
    return (x - mu) * lax.rsqrt(var + eps)


def _head_rms(x, eps):
    return x * lax.rsqrt(jnp.square(x).mean(-1, keepdims=True) + eps)


def _causal_conv(u, buf, w, b):
    width = w.shape[0]
    L = u.shape[1]
    full = jnp.concatenate([buf.astype(u.dtype), u], axis=1)
    out = b
    for j in range(width):
        out = out + full[:, j:j + L] * w[j]
    return out, full[:, full.shape[1] - (width - 1):]


def _to_chunks(t, lc):
    B, L = t.shape[:2]
    t = t.reshape((B, L // lc, lc) + t.shape[2:])
    return jnp.moveaxis(jnp.moveaxis(t, 1, 0), 2, 3)


def _from_chunks(t):
    nc, B, H, lc, d = t.shape
    return jnp.swapaxes(jnp.moveaxis(t, 0, 1), 2, 3).reshape(B, nc * lc, H, d)


def _chunk_len(L):
    return CHUNK if L % CHUNK == 0 else L


def _rwkv7_scan(r, decay, k, v, kk, a, S0):
    def step(S, inp):
        r_t, w_t, k_t, v_t, kk_t, a_t = inp
        sa = jnp.einsum('bhij,bhj->bhi', S, -kk_t)
        S = (S * w_t[:, :, None, :] + sa[..., None] * (kk_t * a_t)[:, :, None, :]
             + v_t[..., None] * k_t[:, :, None, :])
        return S, jnp.einsum('bhij,bhj->bhi', S, r_t)
    xs = tuple(jnp.moveaxis(t, 1, 0) for t in (r, decay, k, v, kk, a))
    S, ys = lax.scan(step, S0, xs)
    return jnp.moveaxis(ys, 0, 1), S


def _rwkv7(u, shift_prev, S0, mu, w0, w_up, a0, a_up, g_up, k_k, k_a, r_k, ln_w, ln_b):
    B, L, _ = u.shape
    prev = jnp.concatenate([shift_prev[:, None].astype(u.dtype), u[:, :-1]], axis=1)
    um = u + mu * (prev - u)
    r, k, v, wd, ad, gd = _split(um, (BRANCH_W, BRANCH_W, BRANCH_W, RWKV_W_LORA, RWKV_A_LORA, RWKV_G_LORA))

    def heads(t):
        return t.astype(F32).reshape(B, L, RWKV_H, RWKV_N)

    w_log = -jax.nn.softplus(-(w0 + jnp.tanh(wd) @ w_up).astype(F32)) - 0.5
    decay = heads(jnp.exp(-jnp.exp(w_log)))
    a = heads(jax.nn.sigmoid((a0 + ad @ a_up).astype(F32)))
    g = (jax.nn.sigmoid(gd) @ g_up).astype(F32)
    r4, k4, v4 = heads(r), heads(k), heads(v)
    kk = k4 * k_k.astype(F32).reshape(RWKV_H, RWKV_N)
    kk = kk / jnp.maximum(jnp.sqrt(jnp.sum(kk * kk, -1, keepdims=True)), 1e-12)
    k4 = k4 * (1.0 + (a - 1.0) * k_a.astype(F32).reshape(RWKV_H, RWKV_N))
    o, S = _rwkv7_scan(r4, decay, k4, v4, kk, a, S0.astype(F32))
    o = (_head_norm(o, RWKV_GN_EPS) * ln_w.reshape(RWKV_H, RWKV_N)
         + ln_b.reshape(RWKV_H, RWKV_N))
    o = o + jnp.sum(r4 * k4 * r_k, -1, keepdims=True) * v4
    out = o.reshape(B, L, BRANCH_W) * g
    return out.astype(u.dtype), u[:, -1], S


def _mlstm_chunks(q, k, v, i_pre, logf, C0, n0, m0):
    L = q.shape[1]
    lc = _chunk_len(L)
    mask = jnp.tril(jnp.ones((lc, lc), bool))

    def step(carry, inp):
        C, n, m = carry
        qc, kc, vc, ic, fc = inp
        b = jnp.cumsum(fc, axis=-1)
        log_d = jnp.where(mask, b[..., :, None] - b[..., None, :] + ic[..., None, :], -jnp.inf)
        log_inter = b + m[..., None]
        m_t = jnp.maximum(log_inter, log_d.max(-1))
        dmat = jnp.exp(log_d - m_t[..., None])
        inter = jnp.exp(log_inter - m_t)
        s = jnp.einsum('bhtd,bhsd->bhts', qc, kc) * dmat
        num = (jnp.einsum('bhts,bhsv->bhtv', s, vc)
               + inter[..., None] * jnp.einsum('bhvk,bhtk->bhtv', C, qc))
        dot = s.sum(-1) + inter * jnp.einsum('bhk,bhtk->bht', n, qc)
        h = num / jnp.maximum(jnp.abs(dot), jnp.exp(-m_t))[..., None]
        m_new = m_t[..., -1]
        w_s = jnp.exp(b[..., -1:] - b + ic - m_new[..., None])
        keep = jnp.exp(b[..., -1] + m - m_new)
        C = keep[..., None, None] * C + jnp.einsum('bhs,bhsv,bhsk->bhvk', w_s, vc, kc)
        n = keep[..., None] * n + jnp.einsum('bhs,bhsk->bhk', w_s, kc)
        return (C, n, m_new), h

    xs = tuple(_to_chunks(t, lc) for t in (q, k, v, i_pre, logf))
    (C, n, m), hs = lax.scan(step, (C0, n0, m0), xs)
    return _from_chunks(hs), C, n, m


def _mlstm(qk_cols, v_cols, o_cols, i_cols, f_cols, conv_buf, C0, n0, m0, conv_w, conv_b, norm_w):
    B, L, _ = qk_cols.shape
    qk, new_buf = _causal_conv(qk_cols, conv_buf, conv_w, conv_b)
    qk = jax.nn.silu(qk).astype(F32)
    q, k = jnp.split(qk, 2, axis=-1)
    q = q.reshape(B, L, ML_H, ML_DH)
    k = k.reshape(B, L, ML_H, ML_DH) * (ML_DH ** -0.5)
    v = v_cols.astype(F32).reshape(B, L, ML_H, ML_DH)
    logf = jax.nn.log_sigmoid(f_cols.astype(F32))
    h, C, n, m = _mlstm_chunks(q, k, v, i_cols.astype(F32), logf,
                               C0.astype(F32), n0.astype(F32), m0.astype(F32))
    h = _head_norm(h, 1e-6).reshape(B, L, BRANCH_W) * norm_w * jax.nn.sigmoid(o_cols.astype(F32))
    return h.astype(qk_cols.dtype), new_buf, C, n, m


def _gla_chunks(q, k, v, log_a, S0):
    L = q.shape[1]
    lc = _chunk_len(L)
    mask = jnp.tril(jnp.ones((lc, lc), bool))[..., None]

    def step(S, inp):
        qc, kc, vc, lac = inp
        b = jnp.cumsum(lac, axis=2)
        diff = jnp.where(mask, b[:, :, :, None, :] - b[:, :, None, :, :], -jnp.inf)
        attn = jnp.einsum('bhtd,bhsd,bhtsd->bhts', qc, kc, jnp.exp(diff))
        o = (jnp.einsum('bhts,bhsv->bhtv', attn, vc)
             + jnp.einsum('bhtd,bhdv->bhtv', qc * jnp.exp(b), S))
        b_last = b[:, :, -1:, :]
        S = (jnp.exp(b_last[:, :, 0])[..., None] * S
             + jnp.einsum('bhsd,bhsv->bhdv', kc * jnp.exp(b_last - b), vc))
        return S, o

    xs = tuple(_to_chunks(t, lc) for t in (q, k, v, log_a))
    S, os_ = lax.scan(step, S0, xs)
    return _from_chunks(os_), S


def _gla(q_cols, k_cols, v_cols, g_cols, a_cols, S0, alpha_up, alpha_b, norm_w):
    B, L, _ = q_cols.shape
    q = q_cols.astype(F32).reshape(B, L, GLA_H, GLA_DK) * (GLA_DK ** -0.5)
    k = k_cols.astype(F32).reshape(B, L, GLA_H, GLA_DK)
    v = v_cols.astype(F32).reshape(B, L, GLA_H, GLA_DV)
    log_a = (jax.nn.log_sigmoid((a_cols @ alpha_up + alpha_b).astype(F32)) / GLA_TAU
             ).reshape(B, L, GLA_H, GLA_DK)
    o, S = _gla_chunks(q, k, v, log_a, S0.astype(F32))
    o = _head_rms(o, 1e-6).reshape(B, L, BRANCH_W) * norm_w * jax.nn.silu(g_cols.astype(F32))
    return o.astype(q_cols.dtype), S


def _rglru(x_cols, y_cols, conv_buf, h0, conv_w, conv_b, wa, ba, wx, bx, lam):
    B, L, _ = x_cols.shape
    xc, new_buf = _causal_conv(x_cols, conv_buf, conv_w, conv_b)
    xc = xc.astype(F32)
    xb = xc.reshape(B, L, RG_BLOCKS, RG_BW)
    r = jax.nn.sigmoid(jnp.einsum('blgi,gij->blgj', xb, wa).reshape(B, L, BRANCH_W) + ba)
    i = jax.nn.sigmoid(jnp.einsum('blgi,gij->blgj', xb, wx).reshape(B, L, BRANCH_W) + bx)
    log_a = -RG_C * r * jax.nn.softplus(-lam.astype(F32))
    a = jnp.exp(log_a)
    u = jnp.sqrt(-jnp.expm1(2.0 * log_a)) * (i * xc)
    u = u.at[:, 0].add(a[:, 0] * h0.astype(F32))

    def combine(p, q):
        return p[0] * q[0], q[0] * p[1] + q[1]

    _, h = lax.associative_scan(combine, (a, u), axis=1)
    y = h * jax.nn.gelu(y_cols.astype(F32))
    return y.astype(x_cols.dtype), new_buf, h[:, -1]


def _mixer(x, st, p, l):
    B, L, D = x.shape
    u = x @ p['w_in'][l] + p['b_in'][l]
    (rw, mqk, mv, mo, mi, mf, gq, gk, gv, gg, ga, rx, ry, gates) = _split(u, IN_SIZES)
    sh0, S0, mconv0, C0, n0, m0, gS0, rconv0, h0 = st
    o_a, sh1, S1 = _rwkv7(rw, sh0, S0, p['rwkv_mu'][l], p['rwkv_w0'][l], p['rwkv_w_up'][l],
                          p['rwkv_a0'][l], p['rwkv_a_up'][l], p['rwkv_g_up'][l], p['rwkv_k_k'][l],
                          p['rwkv_k_a'][l], p['rwkv_r_k'][l], p['rwkv_ln_w'][l], p['rwkv_ln_b'][l])
    o_b, mconv1, C1, n1, m1 = _mlstm(mqk, mv, mo, mi, mf, mconv0, C0, n0, m0,
                                     p['mlstm_conv_w'][l], p['mlstm_conv_b'][l], p['mlstm_norm_w'][l])
    o_c, gS1 = _gla(gq, gk, gv, gg, ga, gS0, p['gla_alpha_up'][l], p['gla_alpha_b'][l],
                    p['gla_norm_w'][l])
    o_d, rconv1, h1 = _rglru(rx, ry, rconv0, h0, p['rglru_conv_w'][l], p['rglru_conv_b'][l],
                             p['rglru_wa'][l], p['rglru_ba'][l], p['rglru_wx'][l], p['rglru_bx'][l],
                             p['rglru_lambda'][l])
    branches = jnp.stack([o_a, o_b, o_c, o_d], axis=2)
    up = jnp.einsum('blgc,gcd->blgd', branches, p['w_branch'][l])
    gate = jax.nn.sigmoid(gates.reshape(B, L, N_BRANCH, D))
    merged = jnp.einsum('blgd,blgd->bld', gate, up)
    out = merged @ p['w_out'][l]
    return out, (sh1, S1, mconv1, C1, n1, m1, gS1, rconv1, h1)


def _swiglu(x, wg, wu, wd):
    return (jax.nn.silu(x @ wg) * (x @ wu)) @ wd


def _moe(x, router, router_b, wg, wu, wd):
    B, L, D = x.shape
    xt = x.reshape(B * L, D)
    logits = (xt @ router).astype(F32) + router_b
    top_v, top_i = lax.top_k(logits, TOP_K)
    top_w = jax.nn.softmax(top_v, axis=-1)
    gate = jnp.sum(jax.nn.one_hot(top_i, N_EXPERTS, dtype=F32) * top_w[..., None], axis=1)
    y = jnp.zeros((B * L, D), F32)
    for e in range(N_EXPERTS):
        y = y + gate[:, e:e + 1] * _swiglu(xt, wg[e], wu[e], wd[e])
    return y.astype(x.dtype).reshape(B, L, D)


def _trunk(x, states, p):
    new = [[] for _ in STATE_SHAPES]
    for l in range(DEPTH):
        st = tuple(s[l] for s in states)
        m_out, st_new = _mixer(x, st, p, l)
        x = _layer_norm(DN_ALPHA * x + m_out, p['ln1_g'][l], p['ln1_b'][l])
        j = l // 2
        if l % 2 == 0:
            f = _swiglu(x, p['ffn_wg'][j], p['ffn_wu'][j], p['ffn_wd'][j])
        else:
            f = _moe(x, p['moe_router'][j], p['moe_router_b'][j], p['moe_wg'][j],
                     p['moe_wu'][j], p['moe_wd'][j])
        x = _layer_norm(DN_ALPHA * x + f, p['ln2_g'][l], p['ln2_b'][l])
        for idx, s in enumerate(st_new):
            new[idx].append(s.astype(x.dtype))
    return x, [jnp.stack(s, 0) for s in new]


def setup_inputs(seed: int = 0) -> dict:
    key = jax.random.key(seed)
    ks = iter(jax.random.split(key, 64))

    def nrm(shape, scale):
        return jax.random.normal(next(ks), shape, F32) * scale

    Ld, W, D = DEPTH, BRANCH_W, D_MODEL
    x_prompt = nrm((BATCH, SEQ, D), 1.0)
    x_sample = nrm((DEC_BATCH, DEC_SEQ, D), 1.0)
    state_rwkv_shift = nrm((Ld, DEC_BATCH, RWKV_COLS), 1.0)
    state_rwkv_S = nrm((Ld, DEC_BATCH, RWKV_H, RWKV_N, RWKV_N), 0.1)
    state_mlstm_conv = nrm((Ld, DEC_BATCH, ML_CONV - 1, 2 * W), 1.0)
    state_mlstm_C = nrm((Ld, DEC_BATCH, ML_H, ML_DH, ML_DH), 0.1)
    state_mlstm_n = nrm((Ld, DEC_BATCH, ML_H, ML_DH), 0.5)
    state_mlstm_m = nrm((Ld, DEC_BATCH, ML_H), 0.5)
    state_gla_S = nrm((Ld, DEC_BATCH, GLA_H, GLA_DK, GLA_DV), 0.1)
    state_rglru_conv = nrm((Ld, DEC_BATCH, RG_CONV - 1, W), 1.0)
    state_rglru_h = nrm((Ld, DEC_BATCH, W), 0.5)

    w_in = nrm((Ld, D, IN_COLS), D ** -0.5)
    f_off = int(sum(IN_SIZES[:5]))
    b_in = nrm((Ld, IN_COLS), 0.02).at[:, f_off:f_off + ML_H].add(jnp.linspace(3.0, 6.0, ML_H))
    rwkv_mu = jax.random.uniform(next(ks), (Ld, RWKV_COLS), F32)
    rwkv_w0 = jnp.tile(jnp.linspace(-5.0, 1.0, RWKV_N), RWKV_H)[None] + nrm((Ld, W), 0.1)
    rwkv_w_up = nrm((Ld, RWKV_W_LORA, W), 0.1)
    rwkv_a0 = nrm((Ld, W), 0.1)
    rwkv_a_up = nrm((Ld, RWKV_A_LORA, W), 0.1)
    rwkv_g_up = nrm((Ld, RWKV_G_LORA, W), RWKV_G_LORA ** -0.5)
    rwkv_k_k = 0.85 + nrm((Ld, W), 0.02)
    rwkv_k_a = 1.0 + nrm((Ld, W), 0.02)
    rwkv_r_k = nrm((Ld, RWKV_H, RWKV_N), 0.1)
    rwkv_ln_w = 1.0 + nrm((Ld, W), 0.02)
    rwkv_ln_b = nrm((Ld, W), 0.02)
    mlstm_conv_w = nrm((Ld, ML_CONV, 2 * W), 0.5)
    mlstm_conv_b = nrm((Ld, 2 * W), 0.02)
    mlstm_norm_w = 1.0 + nrm((Ld, W), 0.02)
    gla_alpha_up = nrm((Ld, GLA_LORA, GLA_H * GLA_DK), GLA_LORA ** -0.5)
    gla_alpha_b = 2.0 + nrm((Ld, GLA_H * GLA_DK), 0.1)
    gla_norm_w = 1.0 + nrm((Ld, W), 0.02)
    rglru_conv_w = nrm((Ld, RG_CONV, W), 0.5)
    rglru_conv_b = nrm((Ld, W), 0.02)
    rglru_wa = nrm((Ld, RG_BLOCKS, RG_BW, RG_BW), RG_BW ** -0.5)
    rglru_ba = nrm((Ld, W), 0.02)
    rglru_wx = nrm((Ld, RG_BLOCKS, RG_BW, RG_BW), RG_BW ** -0.5)
    rglru_bx = nrm((Ld, W), 0.02)
    a_pow = jax.random.uniform(next(ks), (Ld, W), F32, minval=0.9, maxval=0.999)
    s = a_pow ** (1.0 / RG_C)
    rglru_lambda = jnp.log(s) - jnp.log1p(-s)
    w_branch = nrm((Ld, N_BRANCH, W, D), (W ** -0.5) * DN_BETA)
    w_out = nrm((Ld, D, D), (D ** -0.5) * DN_BETA)
    ln1_g = 1.0 + nrm((Ld, D), 0.02)
    ln1_b = nrm((Ld, D), 0.02)
    ffn_wg = nrm((N_DENSE, D, D_FF), D ** -0.5)
    ffn_wu = nrm((N_DENSE, D, D_FF), D ** -0.5)
    ffn_wd = nrm((N_DENSE, D_FF, D), (D_FF ** -0.5) * DN_BETA)
    moe_router = nrm((N_MOE, D, N_EXPERTS), D ** -0.5)
    moe_router_b = nrm((N_MOE, N_EXPERTS), 0.01)
    moe_wg = nrm((N_MOE, N_EXPERTS, D, D_FF), D ** -0.5)
    moe_wu = nrm((N_MOE, N_EXPERTS, D, D_FF), D ** -0.5)
    moe_wd = nrm((N_MOE, N_EXPERTS, D_FF, D), (D_FF ** -0.5) * DN_BETA)
    ln2_g = 1.0 + nrm((Ld, D), 0.02)
    ln2_b = nrm((Ld, D), 0.02)
    return {
        'x_prompt': x_prompt, 'x_sample': x_sample,
        'state_rwkv_shift': state_rwkv_shift, 'state_rwkv_S': state_rwkv_S,
        'state_mlstm_conv': state_mlstm_conv, 'state_mlstm_C': state_mlstm_C,
        'state_mlstm_n': state_mlstm_n, 'state_mlstm_m': state_mlstm_m,
        'state_gla_S': state_gla_S, 'state_rglru_conv': state_rglru_conv,
        'state_rglru_h': state_rglru_h,
        'w_in': w_in, 'b_in': b_in,
        'rwkv_mu': rwkv_mu, 'rwkv_w0': rwkv_w0, 'rwkv_w_up': rwkv_w_up, 'rwkv_a0': rwkv_a0,
        'rwkv_a_up': rwkv_a_up, 'rwkv_g_up': rwkv_g_up, 'rwkv_k_k': rwkv_k_k, 'rwkv_k_a': rwkv_k_a,
        'rwkv_r_k': rwkv_r_k, 'rwkv_ln_w': rwkv_ln_w, 'rwkv_ln_b': rwkv_ln_b,
        'mlstm_conv_w': mlstm_conv_w, 'mlstm_conv_b': mlstm_conv_b, 'mlstm_norm_w': mlstm_norm_w,
        'gla_alpha_up': gla_alpha_up, 'gla_alpha_b': gla_alpha_b, 'gla_norm_w': gla_norm_w,
        'rglru_conv_w': rglru_conv_w, 'rglru_conv_b': rglru_conv_b, 'rglru_wa': rglru_wa,
        'rglru_ba': rglru_ba, 'rglru_wx': rglru_wx, 'rglru_bx': rglru_bx,
        'rglru_lambda': rglru_lambda,
        'w_branch': w_branch, 'w_out': w_out, 'ln1_g': ln1_g, 'ln1_b': ln1_b,
        'ffn_wg': ffn_wg, 'ffn_wu': ffn_wu, 'ffn_wd': ffn_wd,
        'moe_router': moe_router, 'moe_router_b': moe_router_b,
        'moe_wg': moe_wg, 'moe_wu': moe_wu, 'moe_wd': moe_wd,
        'ln2_g': ln2_g, 'ln2_b': ln2_b,
    }


def reference(x_prompt, x_sample, state_rwkv_shift, state_rwkv_S, state_mlstm_conv, state_mlstm_C,
              state_mlstm_n, state_mlstm_m, state_gla_S, state_rglru_conv, state_rglru_h,
              w_in, b_in, rwkv_mu, rwkv_w0, rwkv_w_up, rwkv_a0, rwkv_a_up, rwkv_g_up, rwkv_k_k,
              rwkv_k_a, rwkv_r_k, rwkv_ln_w, rwkv_ln_b, mlstm_conv_w, mlstm_conv_b, mlstm_norm_w,
              gla_alpha_up, gla_alpha_b, gla_norm_w, rglru_conv_w, rglru_conv_b, rglru_wa, rglru_ba,
              rglru_wx, rglru_bx, rglru_lambda, w_branch, w_out, ln1_g, ln1_b, ffn_wg, ffn_wu, ffn_wd,
              moe_router, moe_router_b, moe_wg, moe_wu, moe_wd, ln2_g, ln2_b):
    p = dict(w_in=w_in, b_in=b_in, rwkv_mu=rwkv_mu, rwkv_w0=rwkv_w0, rwkv_w_up=rwkv_w_up,
             rwkv_a0=rwkv_a0, rwkv_a_up=rwkv_a_up, rwkv_g_up=rwkv_g_up, rwkv_k_k=rwkv_k_k,
             rwkv_k_a=rwkv_k_a, rwkv_r_k=rwkv_r_k, rwkv_ln_w=rwkv_ln_w, rwkv_ln_b=rwkv_ln_b,
             mlstm_conv_w=mlstm_conv_w, mlstm_conv_b=mlstm_conv_b, mlstm_norm_w=mlstm_norm_w,
             gla_alpha_up=gla_alpha_up, gla_alpha_b=gla_alpha_b, gla_norm_w=gla_norm_w,
             rglru_conv_w=rglru_conv_w, rglru_conv_b=rglru_conv_b, rglru_wa=rglru_wa,
             rglru_ba=rglru_ba, rglru_wx=rglru_wx, rglru_bx=rglru_bx, rglru_lambda=rglru_lambda,
             w_branch=w_branch, w_out=w_out, ln1_g=ln1_g, ln1_b=ln1_b,
             ffn_wg=ffn_wg, ffn_wu=ffn_wu, ffn_wd=ffn_wd,
             moe_router=moe_router, moe_router_b=moe_router_b,
             moe_wg=moe_wg, moe_wu=moe_wu, moe_wd=moe_wd, ln2_g=ln2_g, ln2_b=ln2_b)
    zero_states = [jnp.zeros((DEPTH, x_prompt.shape[0]) + s, x_prompt.dtype) for s in STATE_SHAPES]
    y_prompt, ps = _trunk(x_prompt, zero_states, p)
    sample_states = [state_rwkv_shift, state_rwkv_S, state_mlstm_conv, state_mlstm_C, state_mlstm_n,
                     state_mlstm_m, state_gla_S, state_rglru_conv, state_rglru_h]
    y_sample, ss = _trunk(x_sample, sample_states, p)
    (p_rwkv_shift, p_rwkv_S, p_mlstm_conv, p_mlstm_C, p_mlstm_n, p_mlstm_m,
     p_gla_S, p_rglru_conv, p_rglru_h) = ps
    (s_rwkv_shift, s_rwkv_S, s_mlstm_conv, s_mlstm_C, s_mlstm_n, s_mlstm_m,
     s_gla_S, s_rglru_conv, s_rglru_h) = ss
    return (y_prompt, y_sample,
            p_rwkv_shift, p_rwkv_S, p_mlstm_conv, p_mlstm_C, p_mlstm_n, p_mlstm_m,
            p_gla_S, p_rglru_conv, p_rglru_h,
            s_rwkv_shift, s_rwkv_S, s_mlstm_conv, s_mlstm_C, s_mlstm_n, s_mlstm_m,
            s_gla_S, s_rglru_conv, s_rglru_h)
```

```python
import functools

import jax
import jax.numpy as jnp
import numpy as np
from jax import lax
from jax.experimental import pallas as pl
from jax.experimental.pallas import tpu as pltpu

F32 = jnp.float32
BF16 = jnp.bfloat16

D_MODEL = 1024
DEPTH = 2
N_BRANCH = 4
BRANCH_W = 256
HEADS = 4
HEAD_W = 64
RWKV_COLS = 896
RWKV_GN_EPS = 64e-5
GLA_DK = 32
GLA_LORA = 16
GLA_TAU = 16.0
RG_C = 8.0
D_FF = 2816
N_EXPERTS = 8
DN_ALPHA = (2 * DEPTH) ** 0.25
LN_EPS = 1e-5

IN_SIZES = (RWKV_COLS, 512, 256, 256, 4, 4, 128, 128, 256, 256, 16, 256, 256, 4096)

W_RW = 896
W_ML = 1280
W_GL = 896
W_RG = 512
W_PACK = W_RW + W_ML + W_GL + W_RG

VMEM_LIMIT = 56 * 1024 * 1024


def _cp(*sem):
    return pltpu.CompilerParams(dimension_semantics=sem, vmem_limit_bytes=VMEM_LIMIT)


def _dot(a, b):
    return jnp.dot(a.astype(BF16), b.astype(BF16), preferred_element_type=F32)


def _dot_nt(a, b):
    return lax.dot_general(a.astype(BF16), b.astype(BF16), (((1,), (1,)), ((), ())),
                           preferred_element_type=F32)


def _dot_tn(a, b):
    return lax.dot_general(a.astype(BF16), b.astype(BF16), (((0,), (0,)), ((), ())),
                           preferred_element_type=F32)


def _split(x):
    hi = x.astype(BF16)
    lo = (x - hi.astype(F32)).astype(BF16)
    return hi, lo


def _dot_exact_lhs(a, x):
    hi, lo = _split(x)
    a = a.astype(BF16)
    return (jnp.dot(a, hi, preferred_element_type=F32) + jnp.dot(a, lo, preferred_element_type=F32))


def _dot_exact_rhs(x, a):
    hi, lo = _split(x)
    a = a.astype(BF16)
    return (jnp.dot(hi, a, preferred_element_type=F32) + jnp.dot(lo, a, preferred_element_type=F32))


def _dot3(x, w):
    xh, xl = _split(x)
    wh, wl = _split(w)
    return (jnp.dot(xh, wh, preferred_element_type=F32) + jnp.dot(xl, wh, preferred_element_type=F32)
            + jnp.dot(xh, wl, preferred_element_type=F32))


def _sigmoid(x):
    return 1.0 / (1.0 + jnp.exp(-x))


def _softplus(x):
    return jnp.maximum(x, 0.0) + jnp.log(1.0 + jnp.exp(-jnp.abs(x)))


def _log_sigmoid(x):
    return -_softplus(-x)


def _silu(x):
    return x * _sigmoid(x)


def _gelu_tanh(x):
    c = np.float32(np.sqrt(2.0 / np.pi))
    return 0.5 * x * (1.0 + jnp.tanh(c * (x + 0.044715 * (x * x * x))))


def _layer_norm(y, g, b):
    mu = jnp.mean(y, axis=-1, keepdims=True)
    yc = y - mu
    var = jnp.mean(yc * yc, axis=-1, keepdims=True)
    return yc * lax.rsqrt(var + LN_EPS) * g + b


def _iota(shape, dim):
    return lax.broadcasted_iota(jnp.int32, shape, dim)


def _head_mean_mat(width=BRANCH_W, group=HEAD_W):
    r = _iota((width, width), 0) // group
    c = _iota((width, width), 1) // group
    return jnp.where(r == c, 1.0 / group, 0.0).astype(BF16)


def _tril_ones(n):
    return jnp.where(_iota((n, n), 0) >= _iota((n, n), 1), 1.0, 0.0).astype(BF16)


def _const_spec(shape):
    nd = len(shape)
    return pl.BlockSpec(shape, lambda *_: (0,) * nd)


def _inproj_kernel(x_ref, w_ref, b_ref, o_rw, o_ml, o_gl, o_rg):
    x = x_ref[...].astype(BF16)
    off = 0
    for o_ref in (o_rw, o_ml, o_gl, o_rg):
        n = o_ref.shape[-1]
        o_ref[...] = (jnp.dot(x, w_ref[:, off:off + n], preferred_element_type=F32)
                      + b_ref[:, off:off + n])
        off += n


def _inproj(x, w, b, tm):
    n = x.shape[0]
    widths = (W_RW, W_ML, W_GL, W_RG)
    return pl.pallas_call(
        _inproj_kernel,
        out_shape=[jax.ShapeDtypeStruct((n, wd), F32) for wd in widths],
        grid=(n // tm,),
        in_specs=[pl.BlockSpec((tm, D_MODEL), lambda i: (i, 0)),
                  _const_spec((D_MODEL, W_PACK)), _const_spec((1, W_PACK))],
        out_specs=[pl.BlockSpec((tm, wd), lambda i: (i, 0)) for wd in widths],
        compiler_params=_cp("parallel"),
        name="inproj",
    )(x, w, b)


def _merge_kernel(x_ref, oa, ob, oc, od, wg_ref, bg_ref, wb_ref, wo_ref, g_ref, b_ref, out_ref):
    x = x_ref[...]
    xb = x.astype(BF16)
    merged = None
    for g, o_ref in enumerate((oa, ob, oc, od)):
        sl = slice(g * D_MODEL, (g + 1) * D_MODEL)
        gate = _sigmoid(jnp.dot(xb, wg_ref[:, sl], preferred_element_type=F32) + bg_ref[:, sl])
        up = jnp.dot(o_ref[...].astype(BF16), wb_ref[g], preferred_element_type=F32)
        merged = gate * up if merged is None else merged + gate * up
    out = jnp.dot(merged.astype(BF16), wo_ref[...], preferred_element_type=F32)
    out_ref[...] = _layer_norm(DN_ALPHA * x + out, g_ref[...], b_ref[...])


def _merge(x, branches, wg, bg, wb, wo, ln_g, ln_b, tm):
    n = x.shape[0]
    tok = lambda wd: pl.BlockSpec((tm, wd), lambda i: (i, 0))
    return pl.pallas_call(
        _merge_kernel,
        out_shape=jax.ShapeDtypeStruct((n, D_MODEL), F32),
        grid=(n // tm,),
        in_specs=[tok(D_MODEL)] + [tok(BRANCH_W)] * 4 + [
            _const_spec((D_MODEL, N_BRANCH * D_MODEL)), _const_spec((1, N_BRANCH * D_MODEL)),
            _const_spec((N_BRANCH, BRANCH_W, D_MODEL)), _const_spec((D_MODEL, D_MODEL)),
            _const_spec((1, D_MODEL)), _const_spec((1, D_MODEL))],
        out_specs=tok(D_MODEL),
        compiler_params=_cp("parallel"),
        name="merge",
    )(x, *branches, wg, bg, wb, wo, ln_g, ln_b)


FF_CHUNK = D_FF // 2


def _ffn_kernel(x_ref, wg_ref, wu_ref, wd_ref, g_ref, b_ref, out_ref):
    x = x_ref[...]
    xb = x.astype(BF16)
    acc = None
    for c in range(D_FF // FF_CHUNK):
        sl = slice(c * FF_CHUNK, (c + 1) * FF_CHUNK)
        h = (_silu(jnp.dot(xb, wg_ref[:, sl], preferred_element_type=F32))
             * jnp.dot(xb, wu_ref[:, sl], preferred_element_type=F32))
        part = jnp.dot(h.astype(BF16), wd_ref[sl, :], preferred_element_type=F32)
        acc = part if acc is None else acc + part
    out_ref[...] = _layer_norm(DN_ALPHA * x + acc, g_ref[...], b_ref[...])


def _ffn(x, wg, wu, wd, ln_g, ln_b, tm):
    n = x.shape[0]
    tok = pl.BlockSpec((tm, D_MODEL), lambda i: (i, 0))
    return pl.pallas_call(
        _ffn_kernel,
        out_shape=jax.ShapeDtypeStruct((n, D_MODEL), F32),
        grid=(n // tm,),
        in_specs=[tok, _const_spec((D_MODEL, D_FF)), _const_spec((D_MODEL, D_FF)),
                  _const_spec((D_FF, D_MODEL)), _const_spec((1, D_MODEL)), _const_spec((1, D_MODEL))],
        out_specs=tok,
        compiler_params=_cp("parallel"),
        name="ffn",
    )(x, wg, wu, wd, ln_g, ln_b)


def _router_kernel(x_ref, w_ref, b_ref, gate_ref):
    logits = _dot3(x_ref[...], w_ref[...]) + b_ref[...]
    lane = _iota(logits.shape, 1)
    neg = jnp.float32(-jnp.inf)
    logits = jnp.where(lane < N_EXPERTS, logits, neg)
    m1 = jnp.max(logits, axis=-1, keepdims=True)
    i1 = jnp.min(jnp.where(logits == m1, lane, 128), axis=-1, keepdims=True)
    rest = jnp.where(lane == i1, neg, logits)
    m2 = jnp.max(rest, axis=-1, keepdims=True)
    i2 = jnp.min(jnp.where(rest == m2, lane, 128), axis=-1, keepdims=True)
    e2 = jnp.exp(m2 - m1)
    w1 = 1.0 / (1.0 + e2)
    w2 = e2 / (1.0 + e2)
    gate_ref[...] = jnp.where(lane == i1, w1, 0.0) + jnp.where(lane == i2, w2, 0.0)


def _router(x, w, b, tm):
    n = x.shape[0]
    return pl.pallas_call(
        _router_kernel,
        out_shape=jax.ShapeDtypeStruct((n, 128), F32),
        grid=(n // tm,),
        in_specs=[pl.BlockSpec((tm, D_MODEL), lambda i: (i, 0)),
                  _const_spec((D_MODEL, 128)), _const_spec((1, 128))],
        out_specs=pl.BlockSpec((tm, 128), lambda i: (i, 0)),
        compiler_params=_cp("parallel"),
        name="router",
    )(x, w, b)


def _moe_kernel(x_ref, gate_ref, wg_ref, wu_ref, wd_ref, g_ref, b_ref, out_ref, acc_ref):
    e = pl.program_id(1)
    c = pl.program_id(2)

    @pl.when((e == 0) & (c == 0))
    def _():
        acc_ref[...] = jnp.zeros_like(acc_ref)

    x = x_ref[...]
    xb = x.astype(BF16)
    gate = gate_ref[...]
    ge = jnp.sum(jnp.where(_iota(gate.shape, 1) == e, gate, 0.0), axis=-1, keepdims=True)
    h = (_silu(jnp.dot(xb, wg_ref[...], preferred_element_type=F32))
         * jnp.dot(xb, wu_ref[...], preferred_element_type=F32))
    acc_ref[...] += ge * jnp.dot(h.astype(BF16), wd_ref[...], preferred_element_type=F32)

    @pl.when((e == N_EXPERTS - 1) & (c == D_FF // FF_CHUNK - 1))
    def _():
        out_ref[...] = _layer_norm(DN_ALPHA * x + acc_ref[...], g_ref[...], b_ref[...])


def _moe(x, gate, wg, wu, wd, ln_g, ln_b, tm):
    n = x.shape[0]
    nch = D_FF // FF_CHUNK
    tok = lambda wd_: pl.BlockSpec((tm, wd_), lambda i, e, c: (i, 0))
    return pl.pallas_call(
        _moe_kernel,
        out_shape=jax.ShapeDtypeStruct((n, D_MODEL), F32),
        grid=(n // tm, N_EXPERTS, nch),
        in_specs=[tok(D_MODEL), tok(128),
                  pl.BlockSpec((None, D_MODEL, FF_CHUNK), lambda i, e, c: (e, 0, c)),
                  pl.BlockSpec((None, D_MODEL, FF_CHUNK), lambda i, e, c: (e, 0, c)),
                  pl.BlockSpec((None, FF_CHUNK, D_MODEL), lambda i, e, c: (e, c, 0)),
                  pl.BlockSpec((1, D_MODEL), lambda i, e, c: (0, 0)),
                  pl.BlockSpec((1, D_MODEL), lambda i, e, c: (0, 0))],
        out_specs=tok(D_MODEL),
        scratch_shapes=[pltpu.VMEM((tm, D_MODEL), F32)],
        compiler_params=_cp("parallel", "arbitrary", "arbitrary"),
        name="moe",
    )(x, gate, wg, wu, wd, ln_g, ln_b)


def _shift_rows(x, s, fill):
    rolled = pltpu.roll(x, s, 0)
    return jnp.where(_iota(x.shape, 0) >= s, rolled, fill)


def _causal_conv4(ext_ref, x, cw_ref, cb_ref, lc):
    ext_ref[8:8 + lc, :] = x
    out = cb_ref[...] + cw_ref[3:4, :] * x
    for j in range(3):
        out = out + cw_ref[j:j + 1, :] * ext_ref[5 + j:5 + j + lc, :]
    ext_ref[0:8, :] = ext_ref[lc:lc + 8, :]
    return out


RG_CHUNK = 256


def _rglru_kernel(u_ref, cw_ref, cb_ref, wa_ref, ba_ref, wx_ref, bx_ref, lam_ref,
                  y_ref, h_ref, ext_ref, hc_ref):
    lc = RG_CHUNK

    @pl.when(pl.program_id(1) == 0)
    def _():
        ext_ref[0:8, :] = jnp.zeros((8, BRANCH_W), F32)
        hc_ref[...] = jnp.zeros_like(hc_ref)

    xc = _causal_conv4(ext_ref, u_ref[0, :, 0:BRANCH_W], cw_ref, cb_ref, lc)
    r = _sigmoid(_dot(xc, wa_ref[...]) + ba_ref[...])
    i = _sigmoid(_dot(xc, wx_ref[...]) + bx_ref[...])
    log_a = -RG_C * r * _softplus(-lam_ref[...])
    a = jnp.exp(log_a)
    u = jnp.sqrt(1.0 - jnp.exp(2.0 * log_a)) * (i * xc)
    s = 1
    while s < lc:
        u = u + a * _shift_rows(u, s, 0.0)
        a = a * _shift_rows(a, s, 1.0)
        s *= 2
    h = u + a * hc_ref[...]
    hc_ref[...] = h[lc - 1:lc, :]
    h_ref[0] = h[lc - 1:lc, :]
    y_ref[0] = h * _gelu_tanh(u_ref[0, :, BRANCH_W:2 * BRANCH_W])


def _rglru_prompt(u, cw, cb, wa, ba, wx, bx, lam):
    bsz, seq, _ = u.shape
    lc = RG_CHUNK
    return pl.pallas_call(
        _rglru_kernel,
        out_shape=[jax.ShapeDtypeStruct((bsz, seq, BRANCH_W), F32),
                   jax.ShapeDtypeStruct((bsz, 1, BRANCH_W), F32)],
        grid=(bsz, seq // lc),
        in_specs=[pl.BlockSpec((1, lc, W_RG), lambda b, c: (b, c, 0)),
                  _const_spec((4, BRANCH_W)), _const_spec((1, BRANCH_W)),
                  _const_spec((BRANCH_W, BRANCH_W)), _const_spec((1, BRANCH_W)),
                  _const_spec((BRANCH_W, BRANCH_W)), _const_spec((1, BRANCH_W)),
                  _const_spec((1, BRANCH_W))],
        out_specs=[pl.BlockSpec((1, lc, BRANCH_W), lambda b, c: (b, c, 0)),
                   pl.BlockSpec((1, 1, BRANCH_W), lambda b, c: (b, 0, 0))],
        scratch_shapes=[pltpu.VMEM((lc + 8, BRANCH_W), F32), pltpu.VMEM((1, BRANCH_W), F32)],
        compiler_params=_cp("parallel", "arbitrary"),
        name="rglru",
    )(u, cw, cb, wa, ba, wx, bx, lam)


ML_CHUNK = 128


def _mlstm_kernel(u_ref, cw_ref, cb_ref, nw_ref, o_ref, c_ref, n_ref, m_ref, ext_ref):
    lc = ML_CHUNK
    W = BRANCH_W

    @pl.when(pl.program_id(1) == 0)
    def _():
        ext_ref[0:8, :] = jnp.zeros((8, 2 * W), F32)
        c_ref[...] = jnp.zeros_like(c_ref)
        n_ref[...] = jnp.zeros_like(n_ref)
        m_ref[...] = jnp.zeros_like(m_ref)

    qk = _silu(_causal_conv4(ext_ref, u_ref[0, :, 0:2 * W], cw_ref, cb_ref, lc))
    q = qk[:, 0:W]
    k = qk[:, W:2 * W] * (HEAD_W ** -0.5)
    v = u_ref[0, :, 2 * W:3 * W]
    og = u_ref[0, :, 3 * W:4 * W]
    ipre = u_ref[0, :, 4 * W:4 * W + 128]
    logf = _log_sigmoid(u_ref[0, :, 4 * W + 128:4 * W + 256])
    bcum = _dot_exact_lhs(_tril_ones(lc), logf)
    rowsrc = ipre - bcum
    rows_t = rowsrc.T
    lane = _iota((1, W), 1) // HEAD_W
    causal = _iota((lc, lc), 0) >= _iota((lc, lc), 1)
    cmat = c_ref[0]
    nrow = n_ref[0]
    h_acc = jnp.zeros((lc, W), F32)
    ws_full = jnp.zeros((lc, W), F32)
    keep_full = jnp.zeros((1, W), F32)
    kb = k.astype(BF16)
    vb = v.astype(BF16)
    for h in range(HEADS):
        hm = lane == h
        b_col = bcum[:, h:h + 1]
        log_d = jnp.where(causal, b_col + rows_t[h:h + 1, :], -jnp.inf)
        m_prev = m_ref[0, h:h + 1, 0:1]
        log_inter = b_col + m_prev
        m_t = jnp.maximum(log_inter, jnp.max(log_d, axis=-1, keepdims=True))
        dmat = jnp.exp(log_d - m_t)
        inter = jnp.exp(log_inter - m_t)
        qh = jnp.where(hm, q, 0.0)
        s = _dot_nt(qh, kb) * dmat
        num = _dot(s, vb) + inter * _dot_nt(qh, cmat)
        dot = (jnp.sum(s, axis=-1, keepdims=True)
               + inter * jnp.sum(qh * nrow, axis=-1, keepdims=True))
        den = jnp.maximum(jnp.abs(dot), jnp.exp(-m_t))
        h_acc = h_acc + jnp.where(hm, num / den, 0.0)
        m_new = m_t[lc - 1:lc, :]
        b_last = b_col[lc - 1:lc, :]
        ws_col = jnp.exp(rowsrc[:, h:h + 1] + (b_last - m_new))
        keep = jnp.exp(b_last + m_prev - m_new)
        ws_full = ws_full + jnp.where(hm, ws_col, 0.0)
        keep_full = keep_full + jnp.where(hm, keep, 0.0)
        m_ref[0, h:h + 1, :] = jnp.broadcast_to(m_new, (1, 128))
    blk = (_iota((W, W), 0) // HEAD_W) == (_iota((W, W), 1) // HEAD_W)
    vw = v * ws_full
    c_ref[0] = keep_full * cmat + jnp.where(blk, _dot_tn(vw, kb), 0.0)
    n_ref[0] = keep_full * nrow + jnp.sum(ws_full * k, axis=0, keepdims=True)
    pm = _head_mean_mat()
    mu = _dot_exact_rhs(h_acc, pm)
    hc = h_acc - mu
    var = _dot_exact_rhs(hc * hc, pm)
    o_ref[0] = hc * lax.rsqrt(var + 1e-6) * nw_ref[...] * _sigmoid(og)


def _mlstm_prompt(u, cw, cb, nw):
    bsz, seq, _ = u.shape
    lc = ML_CHUNK
    W = BRANCH_W
    return pl.pallas_call(
        _mlstm_kernel,
        out_shape=[jax.ShapeDtypeStruct((bsz, seq, W), F32),
                   jax.ShapeDtypeStruct((bsz, W, W), F32),
                   jax.ShapeDtypeStruct((bsz, 1, W), F32),
                   jax.ShapeDtypeStruct((bsz, 8, 128), F32)],
        grid=(bsz, seq // lc),
        in_specs=[pl.BlockSpec((1, lc, W_ML), lambda b, c: (b, c, 0)),
                  _const_spec((4, 2 * W)), _const_spec((1, 2 * W)), _const_spec((1, W))],
        out_specs=[pl.BlockSpec((1, lc, W), lambda b, c: (b, c, 0)),
                   pl.BlockSpec((1, W, W), lambda b, c: (b, 0, 0)),
                   pl.BlockSpec((1, 1, W), lambda b, c: (b, 0, 0)),
                   pl.BlockSpec((1, 8, 128), lambda b, c: (b, 0, 0))],
        scratch_shapes=[pltpu.VMEM((lc + 8, 2 * W), F32)],
        compiler_params=_cp("parallel", "arbitrary"),
        name="mlstm",
    )(u, cw, cb, nw)


GLA_CHUNK = 64
GLA_QK = HEADS * GLA_DK


def _gla_kernel(u_ref, au_ref, ab_ref, nw_ref, o_ref, s_ref):
    lc = GLA_CHUNK
    W = BRANCH_W

    @pl.when(pl.program_id(1) == 0)
    def _():
        s_ref[...] = jnp.zeros_like(s_ref)

    q = u_ref[0, :, 0:GLA_QK] * (GLA_DK ** -0.5)
    k = u_ref[0, :, GLA_QK:2 * GLA_QK]
    v = u_ref[0, :, 2 * GLA_QK:2 * GLA_QK + W]
    g = u_ref[0, :, 2 * GLA_QK + W:2 * GLA_QK + 2 * W]
    ac = u_ref[0, :, 2 * GLA_QK + 2 * W:2 * GLA_QK + 2 * W + 128]
    la = _log_sigmoid(_dot(ac, au_ref[...]) + ab_ref[...]) / GLA_TAU
    b = _dot_exact_lhs(_tril_ones(lc), la)
    st = s_ref[0]
    o = _dot_nt(q * jnp.exp(b), st)
    gsum = ((_iota((GLA_QK, W), 0) // GLA_DK) == (_iota((GLA_QK, W), 1) // HEAD_W)).astype(BF16)
    row = _iota((lc, GLA_QK), 0)

    def offset(d, o):
        kd = pltpu.roll(k, d, 0)
        bd = pltpu.roll(b, d, 0)
        vd = pltpu.roll(v, d, 0)
        e = jnp.where(row >= d, q * kd * jnp.exp(jnp.minimum(b - bd, 0.0)), 0.0)
        return o + jnp.dot(e.astype(BF16), gsum, preferred_element_type=F32) * vd

    o = lax.fori_loop(0, lc, offset, o)
    b_last = b[lc - 1:lc, :]
    blk = (_iota((W, GLA_QK), 0) // HEAD_W) == (_iota((W, GLA_QK), 1) // GLA_DK)
    s_ref[0] = jnp.exp(b_last) * st + jnp.where(blk, _dot_tn(v, k * jnp.exp(b_last - b)), 0.0)
    ms = _dot_exact_rhs(o * o, _head_mean_mat())
    o_ref[0] = o * lax.rsqrt(ms + 1e-6) * nw_ref[...] * _silu(g)


def _gla_prompt(u, au, ab, nw):
    bsz, seq, _ = u.shape
    lc = GLA_CHUNK
    W = BRANCH_W
    return pl.pallas_call(
        _gla_kernel,
        out_shape=[jax.ShapeDtypeStruct((bsz, seq, W), F32),
                   jax.ShapeDtypeStruct((bsz, W, GLA_QK), F32)],
        grid=(bsz, seq // lc),
        in_specs=[pl.BlockSpec((1, lc, W_GL), lambda b, c: (b, c, 0)),
                  _const_spec((128, GLA_QK)), _const_spec((1, GLA_QK)), _const_spec((1, W))],
        out_specs=[pl.BlockSpec((1, lc, W), lambda b, c: (b, c, 0)),
                   pl.BlockSpec((1, W, GLA_QK), lambda b, c: (b, 0, 0))],
        compiler_params=_cp("parallel", "arbitrary"),
        name="gla",
    )(u, au, ab, nw)


RW_CHUNK = 256


def _head_sum_mat(width=BRANCH_W, group=HEAD_W):
    r = _iota((width, width), 0) // group
    c = _iota((width, width), 1) // group
    return jnp.where(r == c, 1.0, 0.0).astype(BF16)


def _rwkv_pointwise(um, p_ref, lora_ref):
    W = BRANCH_W
    r = um[:, 0:W]
    k = um[:, W:2 * W]
    v = um[:, 2 * W:3 * W]
    lo = um[:, 3 * W:3 * W + 128]
    w_log = -_softplus(-(p_ref[0:1, :] + _dot(jnp.tanh(lo), lora_ref[0]))) - 0.5
    decay = jnp.exp(-jnp.exp(w_log))
    a = _sigmoid(p_ref[1:2, :] + _dot(lo, lora_ref[1]))
    g = _dot(_sigmoid(lo), lora_ref[2])
    hs = _head_sum_mat()
    kk = k * p_ref[2:3, :]
    kk = kk / jnp.maximum(jnp.sqrt(_dot_exact_rhs(kk * kk, hs)), 1e-12)
    k2 = k * (1.0 + (a - 1.0) * p_ref[3:4, :])
    bonus = _dot_exact_rhs(r * k2 * p_ref[4:5, :], hs)
    return decay, kk, kk * a, k2, r, v, g, bonus * v


def _rwkv_prep_kernel(u_ref, mu_ref, p_ref, lora_ref, *rest):
    outs, ext_ref = rest[:8], rest[8]
    T = RW_CHUNK

    @pl.when(pl.program_id(1) == 0)
    def _():
        ext_ref[0:8, :] = jnp.zeros((8, RWKV_COLS), F32)

    u = u_ref[0]
    ext_ref[8:8 + T, :] = u
    prev = ext_ref[7:7 + T, :]
    ext_ref[0:8, :] = ext_ref[T:T + 8, :]
    um = u + mu_ref[...] * (prev - u)
    for o_ref, val in zip(outs, _rwkv_pointwise(um, p_ref, lora_ref)):
        o_ref[0] = val


def _rwkv_prep(u, mu, p, lora):
    bsz, seq, _ = u.shape
    T = RW_CHUNK
    W = BRANCH_W
    return pl.pallas_call(
        _rwkv_prep_kernel,
        out_shape=[jax.ShapeDtypeStruct((bsz, seq, W), F32)] * 8,
        grid=(bsz, seq // T),
        in_specs=[pl.BlockSpec((1, T, RWKV_COLS), lambda b, c: (b, c, 0)),
                  _const_spec((1, RWKV_COLS)), _const_spec((8, W)), _const_spec((3, 128, W))],
        out_specs=[pl.BlockSpec((1, T, W), lambda b, c: (b, c, 0))] * 8,
        scratch_shapes=[pltpu.VMEM((T + 8, RWKV_COLS), F32)],
        compiler_params=_cp("parallel", "arbitrary"),
        name="rwkv_prep",
    )(u, mu, p, lora)


def _delta_rule_step(s_ref, ni, w, kk, kka, k, r, v_row, y_store, s_out_ref=None):
    s_out = s_ref if s_out_ref is None else s_out_ref
    sa = [jnp.sum(s_ref[i] * kk, axis=0, keepdims=True) for i in range(ni)]
    for i in range(ni):
        sn = s_ref[i] * w - sa[i] * kka + v_row(i) * k
        s_out[i] = sn
        y_store(i, jnp.sum(sn * r, axis=0, keepdims=True))


RW_SCAN_T = 64


def _rwkv_scan_kernel(w_ref, kk_ref, kka_ref, k_ref, r_ref, v_ref, s0_ref, y_ref, s_out_ref, s_ref):
    ni = s_ref.shape[0]
    tb = pl.program_id(1)

    @pl.when(tb == 0)
    def _():
        s_ref[...] = s0_ref[...]

    def token(t, carry):
        def y_store(i, row):
            y_ref[t, i:i + 1, :] = row
        _delta_rule_step(s_ref, ni, w_ref[t], kk_ref[t], kka_ref[t], k_ref[t], r_ref[t],
                         lambda i: v_ref[t, i:i + 1, :], y_store)
        return carry

    lax.fori_loop(0, w_ref.shape[0], token, 0)

    @pl.when(tb == pl.num_programs(1) - 1)
    def _():
        s_out_ref[...] = s_ref[...]


def _rwkv_scan(jvecs, v, s0):
    groups, seq, ni, _ = v.shape
    T = min(RW_SCAN_T, seq)
    jspec = pl.BlockSpec((None, T, HEAD_W, 128), lambda g, t: (g, t, 0, 0))
    ispec = pl.BlockSpec((None, T, ni, 128), lambda g, t: (g, t, 0, 0))
    sspec = pl.BlockSpec((None, ni, HEAD_W, 128), lambda g, t: (g, 0, 0, 0))
    return pl.pallas_call(
        _rwkv_scan_kernel,
        out_shape=[jax.ShapeDtypeStruct(v.shape, F32), jax.ShapeDtypeStruct(s0.shape, F32)],
        grid=(groups, seq // T),
        in_specs=[jspec] * 5 + [ispec, sspec],
        out_specs=[ispec, sspec],
        scratch_shapes=[pltpu.VMEM((ni, HEAD_W, 128), F32)],
        compiler_params=_cp("parallel", "arbitrary"),
        name="rwkv_scan",
    )(*jvecs, v, s0)


def _rwkv_post_kernel(y_ref, g_ref, bv_ref, ln_ref, o_ref):
    y = y_ref[0]
    pm = _head_mean_mat()
    mu = _dot_exact_rhs(y, pm)
    yc = y - mu
    var = _dot_exact_rhs(yc * yc, pm)
    o = yc * lax.rsqrt(var + RWKV_GN_EPS) * ln_ref[0:1, :] + ln_ref[1:2, :]
    o_ref[0] = (o + bv_ref[0]) * g_ref[0]


def _rwkv_post(y, g, bv, ln):
    bsz, seq, W = y.shape
    T = RW_CHUNK
    tok = pl.BlockSpec((1, T, W), lambda b, c: (b, c, 0))
    return pl.pallas_call(
        _rwkv_post_kernel,
        out_shape=jax.ShapeDtypeStruct((bsz, seq, W), F32),
        grid=(bsz, seq // T),
        in_specs=[tok, tok, tok, _const_spec((2, W))],
        out_specs=tok,
        compiler_params=_cp("parallel", "parallel"),
        name="rwkv_post",
    )(y, g, bv, ln)


(PC_W0, PC_A0, PC_KK, PC_KA, PC_RK, PC_LNW, PC_LNB, PC_MLNW, PC_GLNW,
 PC_RCW, PC_RCB, PC_RBA, PC_RBX, PC_RLAM) = (0, 1, 2, 3, 4, 5, 6, 7, 8, 9, 13, 14, 15, 16)
PC_COLS = 17
DEC_B = 128


def _rowsum(x):
    return jnp.sum(x, axis=0, keepdims=True)


def _decode_kernel(
        ur_ref, uk_ref, uv_ref, ulo_ref, pr_ref, pk_ref, pv_ref, plo_ref,
        mur_ref, muk_ref, muv_ref, mulo_ref, lora_ref, srw_ref,
        mq_ref, mk_ref, mv_ref, mo_ref, mi_ref, mf_ref, bq_ref, bk_ref, cwq_ref, cwk_ref,
        c_ref, n_ref, m_ref,
        gq_ref, gk_ref, gv_ref, gg_ref, ga_ref, au_ref, ab_ref, gs_ref,
        rx_ref, ry_ref, rb_ref, wa_ref, wx_ref, h_ref,
        pc_ref,
        oa_ref, ob_ref, oc_ref, od_ref, srw_o, c_o, n_o, m_o, gs_o, h_o,
        y_scr):
    col = lambda j: pc_ref[:, j:j + 1]

    def shift(u_ref, p_ref, mu_ref):
        u = u_ref[...]
        return u + mu_ref[...] * (p_ref[...] - u)
    r = shift(ur_ref, pr_ref, mur_ref)
    k = shift(uk_ref, pk_ref, muk_ref)
    v = shift(uv_ref, pv_ref, muv_ref)
    lo = shift(ulo_ref, plo_ref, mulo_ref)
    w_log = -_softplus(-(col(PC_W0) + _dot(lora_ref[0], jnp.tanh(lo)))) - 0.5
    decay = jnp.exp(-jnp.exp(w_log))
    a = _sigmoid(col(PC_A0) + _dot(lora_ref[1], lo))
    g = _dot(lora_ref[2], _sigmoid(lo))
    kk = k * col(PC_KK)
    kk = kk / jnp.maximum(jnp.sqrt(_rowsum(kk * kk)), 1e-12)
    k2 = k * (1.0 + (a - 1.0) * col(PC_KA))
    bonus = _rowsum(r * k2 * col(PC_RK))

    def y_store(i, row):
        y_scr[i:i + 1, :] = row
    _delta_rule_step(srw_ref, HEAD_W, decay, kk, kk * a, k2, r, lambda i: v[i:i + 1, :], y_store, srw_o)
    y = y_scr[...]
    yc = y - jnp.mean(y, axis=0, keepdims=True)
    var = jnp.mean(yc * yc, axis=0, keepdims=True)
    o = yc * lax.rsqrt(var + RWKV_GN_EPS) * col(PC_LNW) + col(PC_LNB)
    oa_ref[...] = (o + bonus * v) * g

    def conv(u_ref, b_ref, cw_ref):
        out = cw_ref[:, 4:5] + cw_ref[:, 3:4] * u_ref[...]
        for j in range(3):
            out = out + cw_ref[:, j:j + 1] * b_ref[j]
        return out
    q = _silu(conv(mq_ref, bq_ref, cwq_ref))
    k = _silu(conv(mk_ref, bk_ref, cwk_ref)) * (HEAD_W ** -0.5)
    v = mv_ref[...]
    ipre = mi_ref[...]
    logf = _log_sigmoid(mf_ref[...])
    m_prev = m_ref[...]
    m_t = jnp.maximum(logf + m_prev, ipre)
    inter = jnp.exp(logf + m_prev - m_t)
    wsc = jnp.exp(ipre - m_t)
    s = _rowsum(q * k) * wsc
    n_prev = n_ref[...]
    den = jnp.maximum(jnp.abs(s + inter * _rowsum(n_prev * q)), jnp.exp(-m_t))
    for i in range(HEAD_W):
        ci = c_ref[i]
        vi = v[i:i + 1, :]
        y_scr[i:i + 1, :] = (s * vi + inter * _rowsum(ci * q)) / den
        c_o[i] = inter * ci + (wsc * vi) * k
    n_o[...] = inter * n_prev + wsc * k
    m_o[...] = m_t
    y = y_scr[...]
    yc = y - jnp.mean(y, axis=0, keepdims=True)
    var = jnp.mean(yc * yc, axis=0, keepdims=True)
    ob_ref[...] = yc * lax.rsqrt(var + 1e-6) * col(PC_MLNW) * _sigmoid(mo_ref[...])

    q = gq_ref[...] * (GLA_DK ** -0.5)
    k = gk_ref[...]
    v = gv_ref[...]
    eb = jnp.exp(_log_sigmoid(_dot(au_ref[...], ga_ref[...]) + ab_ref[...]) / GLA_TAU)
    attn = _rowsum(q * k)
    qe = q * eb
    for i in range(HEAD_W):
        si = gs_ref[i]
        vi = v[i:i + 1, :]
        y_scr[i:i + 1, :] = attn * vi + _rowsum(si * qe)
        gs_o[i] = eb * si + k * vi
    y = y_scr[...]
    ms = jnp.mean(y * y, axis=0, keepdims=True)
    oc_ref[...] = y * lax.rsqrt(ms + 1e-6) * col(PC_GLNW) * _silu(gg_ref[...])

    xc = col(PC_RCB) + col(PC_RCW + 3) * rx_ref[...]
    for j in range(3):
        xc = xc + col(PC_RCW + j) * rb_ref[j]
    rg = _sigmoid(_dot(wa_ref[...], xc) + col(PC_RBA))
    ig = _sigmoid(_dot(wx_ref[...], xc) + col(PC_RBX))
    log_a = -RG_C * rg * _softplus(-col(PC_RLAM))
    hn = jnp.exp(log_a) * h_ref[...] + jnp.sqrt(1.0 - jnp.exp(2.0 * log_a)) * (ig * xc)
    h_o[...] = hn
    od_ref[...] = hn * _gelu_tanh(ry_ref[...])


def _decode_mixers(ut_rw, prev_t, mu_c, lora_t, s_rw,
                   ut_ml, mconv_t, ml_cw, c_st, n_st, m_st,
                   ut_gl, au_t, ab_c, g_st,
                   ut_rg, rconv_t, wa_t, wx_t, h_st, pcols):
    nb = DEC_B
    H, HW = HEADS, HEAD_W

    def blk(arr, view, block, index):
        a = arr.reshape(view)
        nd = len(block)
        return a, pl.BlockSpec(block, index)

    ins = []
    v14 = (14, HW, nb)
    v7 = (7, 128, nb)
    for arr in (ut_rw, prev_t):
        ins.append(blk(arr, v14, (None, HW, nb), lambda h: (h, 0, 0)))
        ins.append(blk(arr, v14, (None, HW, nb), lambda h: (4 + h, 0, 0)))
        ins.append(blk(arr, v14, (None, HW, nb), lambda h: (8 + h, 0, 0)))
        ins.append(blk(arr, v7, (None, 128, nb), lambda h: (6, 0, 0)))
    ins.append(blk(mu_c, (14, HW, 1), (None, HW, 1), lambda h: (h, 0, 0)))
    ins.append(blk(mu_c, (14, HW, 1), (None, HW, 1), lambda h: (4 + h, 0, 0)))
    ins.append(blk(mu_c, (14, HW, 1), (None, HW, 1), lambda h: (8 + h, 0, 0)))
    ins.append(blk(mu_c, (7, 128, 1), (None, 128, 1), lambda h: (6, 0, 0)))
    ins.append(blk(lora_t, (3, BRANCH_W, 128), (3, HW, 128), lambda h: (0, h, 0)))
    st_spec = lambda: pl.BlockSpec((None, HW, HW, nb), lambda h: (h, 0, 0, 0))
    ins.append((s_rw, st_spec()))
    v20 = (20, HW, nb)
    for j in range(4):
        ins.append(blk(ut_ml, v20, (None, HW, nb), functools.partial(lambda h, j: (4 * j + h, 0, 0), j=j)))
    ins.append(blk(ut_ml[1024:1028], (4, 1, nb), (None, 1, nb), lambda h: (h, 0, 0)))
    ins.append(blk(ut_ml[1152:1156], (4, 1, nb), (None, 1, nb), lambda h: (h, 0, 0)))
    ins.append(blk(mconv_t, (3, 8, HW, nb), (3, None, HW, nb), lambda h: (0, h, 0, 0)))
    ins.append(blk(mconv_t, (3, 8, HW, nb), (3, None, HW, nb), lambda h: (0, 4 + h, 0, 0)))
    ins.append(blk(ml_cw, (8, HW, 5), (None, HW, 5), lambda h: (h, 0, 0)))
    ins.append(blk(ml_cw, (8, HW, 5), (None, HW, 5), lambda h: (4 + h, 0, 0)))
    ins.append((c_st, st_spec()))
    ins.append(blk(n_st, (H, HW, nb), (None, HW, nb), lambda h: (h, 0, 0)))
    ins.append(blk(m_st, (H, 1, nb), (None, 1, nb), lambda h: (h, 0, 0)))
    ins.append(blk(ut_gl, (28, GLA_DK, nb), (None, GLA_DK, nb), lambda h: (h, 0, 0)))
    ins.append(blk(ut_gl, (28, GLA_DK, nb), (None, GLA_DK, nb), lambda h: (4 + h, 0, 0)))
    ins.append(blk(ut_gl, v14, (None, HW, nb), lambda h: (4 + h, 0, 0)))
    ins.append(blk(ut_gl, v14, (None, HW, nb), lambda h: (8 + h, 0, 0)))
    ins.append(blk(ut_gl, v7, (None, 128, nb), lambda h: (6, 0, 0)))
    ins.append(blk(au_t, (H, GLA_DK, 128), (None, GLA_DK, 128), lambda h: (h, 0, 0)))
    ins.append(blk(ab_c, (H, GLA_DK, 1), (None, GLA_DK, 1), lambda h: (h, 0, 0)))
    ins.append((g_st, pl.BlockSpec((None, HW, GLA_DK, nb), lambda h: (h, 0, 0, 0))))
    ins.append(blk(ut_rg, (8, HW, nb), (None, HW, nb), lambda h: (h, 0, 0)))
    ins.append(blk(ut_rg, (8, HW, nb), (None, HW, nb), lambda h: (4 + h, 0, 0)))
    ins.append(blk(rconv_t, (3, H, HW, nb), (3, None, HW, nb), lambda h: (0, h, 0, 0)))
    ins.append((wa_t, pl.BlockSpec((None, HW, HW), lambda h: (h, 0, 0))))
    ins.append((wx_t, pl.BlockSpec((None, HW, HW), lambda h: (h, 0, 0))))
    ins.append(blk(h_st, (H, HW, nb), (None, HW, nb), lambda h: (h, 0, 0)))
    ins.append(blk(pcols, (H, HW, PC_COLS), (None, HW, PC_COLS), lambda h: (h, 0, 0)))

    vec = lambda: (jax.ShapeDtypeStruct((H, HW, nb), F32), pl.BlockSpec((None, HW, nb), lambda h: (h, 0, 0)))
    mat = lambda: (jax.ShapeDtypeStruct((H, HW, HW, nb), F32), st_spec())
    outs = [vec(), vec(), vec(), vec(), mat(), mat(), vec(),
            (jax.ShapeDtypeStruct((H, 1, nb), F32), pl.BlockSpec((None, 1, nb), lambda h: (h, 0, 0))),
            (jax.ShapeDtypeStruct((H, HW, GLA_DK, nb), F32),
             pl.BlockSpec((None, HW, GLA_DK, nb), lambda h: (h, 0, 0, 0))),
            vec()]
    return pl.pallas_call(
        _decode_kernel,
        out_shape=[o[0] for o in outs],
        grid=(H,),
        in_specs=[s for _, s in ins],
        out_specs=[o[1] for o in outs],
        scratch_shapes=[pltpu.VMEM((HW, nb), F32)],
        compiler_params=_cp("parallel"),
        name="decode_mixers",
    )(*[a for a, _ in ins])


def _pad_cols(a, width):
    return jnp.pad(a, ((0, 0), (0, width - a.shape[1])))


def _pad_rows(a, rows, at=0):
    return jnp.pad(a, ((at, rows - at - a.shape[0]), (0, 0)))


def _pack_layer(p, l):
    offs = np.concatenate([[0], np.cumsum(IN_SIZES)])
    w_in, b_in = p['w_in'][l], p['b_in'][l][None, :]
    seg = lambda a, i: a[:, int(offs[i]):int(offs[i + 1])]

    def regroup(a):
        return jnp.concatenate(
            [seg(a, 0), seg(a, 1), seg(a, 2), seg(a, 3), _pad_cols(seg(a, 4), 128), _pad_cols(seg(a, 5), 128),
             seg(a, 6), seg(a, 7), seg(a, 8), seg(a, 9), _pad_cols(seg(a, 10), 128), seg(a, 11), seg(a, 12)],
            axis=1)
    k = dict(
        w_pack=regroup(w_in).astype(BF16), b_pack=regroup(b_in),
        w_gate=seg(w_in, 13).astype(BF16), b_gate=seg(b_in, 13),
        w_branch=p['w_branch'][l].astype(BF16), w_out=p['w_out'][l].astype(BF16),
        ln1_g=p['ln1_g'][l][None], ln1_b=p['ln1_b'][l][None],
        ln2_g=p['ln2_g'][l][None], ln2_b=p['ln2_b'][l][None],
    )
    w_up, a_up, g_up = p['rwkv_w_up'][l], p['rwkv_a_up'][l], p['rwkv_g_up'][l]
    lora = jnp.stack([_pad_rows(w_up, 128, 0), _pad_rows(a_up, 128, 32), _pad_rows(g_up, 128, 64)])
    k['rw_lora'] = lora.astype(BF16)
    k['rw_lora_t'] = jnp.swapaxes(lora, 1, 2).astype(BF16)
    k['rw_mu'] = p['rwkv_mu'][l][None]
    rw_rows = [p['rwkv_w0'][l], p['rwkv_a0'][l], p['rwkv_k_k'][l], p['rwkv_k_a'][l],
               p['rwkv_r_k'][l].reshape(BRANCH_W)]
    k['rw_p'] = jnp.stack(rw_rows + [jnp.zeros((BRANCH_W,), F32)] * 3)
    k['rw_ln'] = jnp.stack([p['rwkv_ln_w'][l], p['rwkv_ln_b'][l]])
    k['ml_cw'], k['ml_cb'] = p['mlstm_conv_w'][l], p['mlstm_conv_b'][l][None]
    k['ml_nw'] = p['mlstm_norm_w'][l][None]
    k['gl_au'] = _pad_rows(p['gla_alpha_up'][l], 128).astype(BF16)
    k['gl_ab'] = p['gla_alpha_b'][l][None]
    k['gl_nw'] = p['gla_norm_w'][l][None]
    wa, wx = p['rglru_wa'][l], p['rglru_wx'][l]
    eye = jnp.eye(HEADS, dtype=F32)
    bd = lambda w: jnp.einsum('gh,gij->gihj', eye, w).reshape(BRANCH_W, BRANCH_W).astype(BF16)
    k['rg_cw'], k['rg_cb'] = p['rglru_conv_w'][l], p['rglru_conv_b'][l][None]
    k['rg_wa'], k['rg_wx'] = bd(wa), bd(wx)
    k['rg_ba'], k['rg_bx'] = p['rglru_ba'][l][None], p['rglru_bx'][l][None]
    k['rg_lam'] = p['rglru_lambda'][l][None]
    k['rg_wa_t'] = jnp.swapaxes(wa, 1, 2).astype(BF16)
    k['rg_wx_t'] = jnp.swapaxes(wx, 1, 2).astype(BF16)
    cols = rw_rows + [p['rwkv_ln_w'][l], p['rwkv_ln_b'][l], p['mlstm_norm_w'][l], p['gla_norm_w'][l]]
    cols += [p['rglru_conv_w'][l][j] for j in range(4)]
    cols += [p['rglru_conv_b'][l], p['rglru_ba'][l], p['rglru_bx'][l], p['rglru_lambda'][l]]
    k['pcols'] = jnp.stack(cols, axis=1)
    k['ml_cw_t'] = jnp.concatenate([p['mlstm_conv_w'][l].T, p['mlstm_conv_b'][l][:, None]], axis=1)
    return k


def _diag_blocks(a, rb, cb):
    return jnp.stack([a[:, h * rb:(h + 1) * rb, h * cb:(h + 1) * cb] for h in range(HEADS)], axis=1)


def _prompt_mixers(x, k):
    B, L, _ = x.shape
    W = BRANCH_W
    u_rw, u_ml, u_gl, u_rg = _inproj(x.reshape(B * L, D_MODEL), k['w_pack'], k['b_pack'], 512)
    u_rw, u_ml, u_gl, u_rg = (u.reshape(B, L, -1) for u in (u_rw, u_ml, u_gl, u_rg))
    nch = B * HEADS
    rep = 128 // nch
    dec, kk, kka, k2, r, v, g, bv = _rwkv_prep(u_rw, k['rw_mu'], k['rw_p'], k['rw_lora'])

    def key_major(a):
        a = a.reshape(B, L, HEADS, HEAD_W).transpose(1, 3, 0, 2).reshape(L, HEAD_W, nch)
        return jnp.tile(a, (1, 1, rep))[None]
    vi = v.reshape(B, L, HEADS, HEAD_W // rep, rep).transpose(1, 3, 4, 0, 2).reshape(1, L, HEAD_W // rep, 128)
    s0 = jnp.zeros((1, HEAD_W // rep, HEAD_W, 128), F32)
    y, s1 = _rwkv_scan([key_major(a) for a in (dec, kk, kka, k2, r)], vi, s0)
    y = y.reshape(L, HEAD_W // rep, rep, B, HEADS).transpose(3, 0, 4, 1, 2).reshape(B, L, W)
    s_rw = s1.reshape(HEAD_W // rep, HEAD_W, rep, B, HEADS).transpose(3, 4, 0, 2, 1).reshape(B, HEADS, HEAD_W, HEAD_W)
    o_a = _rwkv_post(y, g, bv, k['rw_ln'])
    o_b, c_bd, n_row, m_row = _mlstm_prompt(u_ml, k['ml_cw'], k['ml_cb'], k['ml_nw'])
    o_c, s_bd = _gla_prompt(u_gl, k['gl_au'], k['gl_ab'], k['gl_nw'])
    o_d, h1 = _rglru_prompt(u_rg, k['rg_cw'], k['rg_cb'], k['rg_wa'], k['rg_ba'], k['rg_wx'], k['rg_bx'],
                            k['rg_lam'])
    states = (u_rw[:, L - 1], s_rw, u_ml[:, L - 3:, :2 * W], _diag_blocks(c_bd, HEAD_W, HEAD_W),
              n_row.reshape(B, HEADS, HEAD_W), m_row[:, :HEADS, 0],
              jnp.swapaxes(_diag_blocks(s_bd, HEAD_W, GLA_DK), 2, 3), u_rg[:, L - 3:, :W], h1[:, 0])
    return [o.reshape(B * L, W) for o in (o_a, o_b, o_c, o_d)], states


def _sample_mixers(x, st, k):
    W = BRANCH_W
    sh0, S0, mconv0, C0, n0, m0, gS0, rconv0, h0 = st
    u_rw, u_ml, u_gl, u_rg = _inproj(x, k['w_pack'], k['b_pack'], DEC_B)
    outs = _decode_mixers(
        u_rw.T, sh0.T, k['rw_mu'].T, k['rw_lora_t'], S0.transpose(1, 2, 3, 0),
        u_ml.T, mconv0.transpose(1, 2, 0), k['ml_cw_t'], C0.transpose(1, 2, 3, 0), n0.transpose(1, 2, 0), m0.T,
        u_gl.T, k['gl_au'].T, k['gl_ab'].T, gS0.transpose(1, 3, 2, 0),
        u_rg.T, rconv0.transpose(1, 2, 0), k['rg_wa_t'], k['rg_wx_t'], h0.T, k['pcols'])
    oa, ob, oc, od, s_rw, c_st, n_st, m_st, g_st, h_st = outs
    branches = [o.reshape(W, DEC_B).T for o in (oa, ob, oc, od)]
    states = (u_rw, s_rw.transpose(3, 0, 1, 2),
              jnp.concatenate([mconv0[:, 1:], u_ml[:, None, :2 * W]], axis=1),
              c_st.transpose(3, 0, 1, 2), n_st.transpose(2, 0, 1), m_st[:, 0, :].T,
              g_st.transpose(3, 0, 2, 1),
              jnp.concatenate([rconv0[:, 1:], u_rg[:, None, :W]], axis=1), h_st.reshape(W, DEC_B).T)
    return branches, states


def _trunk(x, states, packs, moe, is_prompt):
    n = x.shape[0]
    tm = 512 if is_prompt else DEC_B
    new_states = []
    for l in range(DEPTH):
        k = packs[l]
        if is_prompt:
            branches, st = _prompt_mixers(x.reshape(states[0], states[1], D_MODEL), k)
        else:
            branches, st = _sample_mixers(x, tuple(s[l] for s in states), k)
        new_states.append(st)
        x = _merge(x, branches, k['w_gate'], k['b_gate'], k['w_branch'], k['w_out'], k['ln1_g'], k['ln1_b'], tm)
        j = l // 2
        if l % 2 == 0:
            x = _ffn(x, moe['ffn_wg'][j], moe['ffn_wu'][j], moe['ffn_wd'][j], k['ln2_g'], k['ln2_b'], tm)
        else:
            gate = _router(x, moe['router'][j], moe['router_b'][j], tm)
            x = _moe(x, gate, moe['moe_wg'][j], moe['moe_wu'][j], moe['moe_wd'][j], k['ln2_g'], k['ln2_b'], tm)
    return x, [jnp.stack([st[i] for st in new_states], axis=0) for i in range(9)]


def kernel(x_prompt, x_sample, state_rwkv_shift, state_rwkv_S, state_mlstm_conv, state_mlstm_C,
           state_mlstm_n, state_mlstm_m, state_gla_S, state_rglru_conv, state_rglru_h,
           w_in, b_in, rwkv_mu, rwkv_w0, rwkv_w_up, rwkv_a0, rwkv_a_up, rwkv_g_up, rwkv_k_k,
           rwkv_k_a, rwkv_r_k, rwkv_ln_w, rwkv_ln_b, mlstm_conv_w, mlstm_conv_b, mlstm_norm_w,
           gla_alpha_up, gla_alpha_b, gla_norm_w, rglru_conv_w, rglru_conv_b, rglru_wa, rglru_ba,
           rglru_wx, rglru_bx, rglru_lambda, w_branch, w_out, ln1_g, ln1_b, ffn_wg, ffn_wu, ffn_wd,
           moe_router, moe_router_b, moe_wg, moe_wu, moe_wd, ln2_g, ln2_b):
    p = dict(w_in=w_in, b_in=b_in, rwkv_mu=rwkv_mu, rwkv_w0=rwkv_w0, rwkv_w_up=rwkv_w_up,
             rwkv_a0=rwkv_a0, rwkv_a_up=rwkv_a_up, rwkv_g_up=rwkv_g_up, rwkv_k_k=rwkv_k_k,
             rwkv_k_a=rwkv_k_a, rwkv_r_k=rwkv_r_k, rwkv_ln_w=rwkv_ln_w, rwkv_ln_b=rwkv_ln_b,
             mlstm_conv_w=mlstm_conv_w, mlstm_conv_b=mlstm_conv_b, mlstm_norm_w=mlstm_norm_w,
             gla_alpha_up=gla_alpha_up, gla_alpha_b=gla_alpha_b, gla_norm_w=gla_norm_w,
             rglru_conv_w=rglru_conv_w, rglru_conv_b=rglru_conv_b, rglru_wa=rglru_wa,
             rglru_ba=rglru_ba, rglru_wx=rglru_wx, rglru_bx=rglru_bx, rglru_lambda=rglru_lambda,
             w_branch=w_branch, w_out=w_out, ln1_g=ln1_g, ln1_b=ln1_b, ln2_g=ln2_g, ln2_b=ln2_b)
    packs = [_pack_layer(p, l) for l in range(DEPTH)]
    moe = dict(ffn_wg=ffn_wg.astype(BF16), ffn_wu=ffn_wu.astype(BF16), ffn_wd=ffn_wd.astype(BF16),
               router=jnp.pad(moe_router, ((0, 0), (0, 0), (0, 128 - N_EXPERTS))),
               router_b=jnp.pad(moe_router_b, ((0, 0), (0, 128 - N_EXPERTS)))[:, None, :],
               moe_wg=moe_wg.astype(BF16), moe_wu=moe_wu.astype(BF16), moe_wd=moe_wd.astype(BF16))
    B, L, _ = x_prompt.shape
    y_p, ps = _trunk(x_prompt.reshape(B * L, D_MODEL), (B, L), packs, moe, True)
    sample_states = (state_rwkv_shift, state_rwkv_S, state_mlstm_conv, state_mlstm_C, state_mlstm_n,
                     state_mlstm_m, state_gla_S, state_rglru_conv, state_rglru_h)
    nb, ls, _ = x_sample.shape
    y_s, ss = _trunk(x_sample.reshape(nb * ls, D_MODEL), sample_states, packs, moe, False)
    return (y_p.reshape(B, L, D_MODEL), y_s.reshape(nb, ls, D_MODEL), *ps, *ss)
```

```python
import functools

import jax
import jax.numpy as jnp
import numpy as np
from jax import lax
from jax.experimental import pallas as pl
from jax.experimental.pallas import tpu as pltpu

F32 = jnp.float32
BF16 = jnp.bfloat16

D_MODEL = 1024
DEPTH = 2
N_BRANCH = 4
BRANCH_W = 256
HEADS = 4
HEAD_W = 64
RWKV_COLS = 896
RWKV_GN_EPS = 64e-5
GLA_DK = 32
GLA_LORA = 16
GLA_TAU = 16.0
RG_C = 8.0
D_FF = 2816
N_EXPERTS = 8
DN_ALPHA = (2 * DEPTH) ** 0.25
LN_EPS = 1e-5

IN_SIZES = (RWKV_COLS, 512, 256, 256, 4, 4, 128, 128, 256, 256, 16, 256, 256, 4096)

W_RW = 896
W_ML = 1280
W_GL = 896
W_RG = 512
W_PACK = W_RW + W_ML + W_GL + W_RG

VMEM_LIMIT = 56 * 1024 * 1024


def _cp(*sem):
    return pltpu.CompilerParams(dimension_semantics=sem, vmem_limit_bytes=VMEM_LIMIT)


def _dot(a, b):
    return jnp.dot(a.astype(BF16), b.astype(BF16), preferred_element_type=F32)


def _dot_nt(a, b):
    return lax.dot_general(a.astype(BF16), b.astype(BF16), (((1,), (1,)), ((), ())),
                           preferred_element_type=F32)


def _dot_tn(a, b):
    return lax.dot_general(a.astype(BF16), b.astype(BF16), (((0,), (0,)), ((), ())),
                           preferred_element_type=F32)


def _split(x):
    hi = x.astype(BF16)
    lo = (x - hi.astype(F32)).astype(BF16)
    return hi, lo


def _dot_exact_lhs(a, x):
    hi, lo = _split(x)
    a = a.astype(BF16)
    return (jnp.dot(a, hi, preferred_element_type=F32) + jnp.dot(a, lo, preferred_element_type=F32))


def _dot_exact_rhs(x, a):
    hi, lo = _split(x)
    a = a.astype(BF16)
    return (jnp.dot(hi, a, preferred_element_type=F32) + jnp.dot(lo, a, preferred_element_type=F32))


def _dot3(x, w):
    xh, xl = _split(x)
    wh, wl = _split(w)
    return (jnp.dot(xh, wh, preferred_element_type=F32) + jnp.dot(xl, wh, preferred_element_type=F32)
            + jnp.dot(xh, wl, preferred_element_type=F32))


def _sigmoid(x):
    return 1.0 / (1.0 + jnp.exp(-x))


def _softplus(x):
    return jnp.maximum(x, 0.0) + jnp.log(1.0 + jnp.exp(-jnp.abs(x)))


def _log_sigmoid(x):
    return -_softplus(-x)


def _silu(x):
    return x * _sigmoid(x)


def _gelu_tanh(x):
    c = np.float32(np.sqrt(2.0 / np.pi))
    return 0.5 * x * (1.0 + jnp.tanh(c * (x + 0.044715 * (x * x * x))))


def _layer_norm(y, g, b):
    mu = jnp.mean(y, axis=-1, keepdims=True)
    yc = y - mu
    var = jnp.mean(yc * yc, axis=-1, keepdims=True)
    return yc * lax.rsqrt(var + LN_EPS) * g + b


def _iota(shape, dim):
    return lax.broadcasted_iota(jnp.int32, shape, dim)


def _head_mean_mat(width=BRANCH_W, group=HEAD_W):
    r = _iota((width, width), 0) // group
    c = _iota((width, width), 1) // group
    return jnp.where(r == c, 1.0 / group, 0.0).astype(BF16)


def _tril_ones(n):
    return jnp.where(_iota((n, n), 0) >= _iota((n, n), 1), 1.0, 0.0).astype(BF16)


def _const_spec(shape):
    nd = len(shape)
    return pl.BlockSpec(shape, lambda *_: (0,) * nd)


def _inproj_kernel(x_ref, w_ref, b_ref, o_rw, o_ml, o_gl, o_rg):
    x = x_ref[...].astype(BF16)
    off = 0
    for o_ref in (o_rw, o_ml, o_gl, o_rg):
        n = o_ref.shape[-1]
        o_ref[...] = (jnp.dot(x, w_ref[:, off:off + n], preferred_element_type=F32)
                      + b_ref[:, off:off + n])
        off += n


def _inproj(x, w, b, tm):
    n = x.shape[0]
    widths = (W_RW, W_ML, W_GL, W_RG)
    return pl.pallas_call(
        _inproj_kernel,
        out_shape=[jax.ShapeDtypeStruct((n, wd), F32) for wd in widths],
        grid=(n // tm,),
        in_specs=[pl.BlockSpec((tm, D_MODEL), lambda i: (i, 0)),
                  _const_spec((D_MODEL, W_PACK)), _const_spec((1, W_PACK))],
        out_specs=[pl.BlockSpec((tm, wd), lambda i: (i, 0)) for wd in widths],
        compiler_params=_cp("parallel"),
        name="inproj",
    )(x, w, b)


def _merge_kernel(x_ref, oa, ob, oc, od, wg_ref, bg_ref, wb_ref, wo_ref, g_ref, b_ref, out_ref):
    x = x_ref[...]
    xb = x.astype(BF16)
    merged = None
    for g, o_ref in enumerate((oa, ob, oc, od)):
        sl = slice(g * D_MODEL, (g + 1) * D_MODEL)
        gate = _sigmoid(jnp.dot(xb, wg_ref[:, sl], preferred_element_type=F32) + bg_ref[:, sl])
        up = jnp.dot(o_ref[...].astype(BF16), wb_ref[g], preferred_element_type=F32)
        merged = gate * up if merged is None else merged + gate * up
    out = jnp.dot(merged.astype(BF16), wo_ref[...], preferred_element_type=F32)
    out_ref[...] = _layer_norm(DN_ALPHA * x + out, g_ref[...], b_ref[...])


def _merge(x, branches, wg, bg, wb, wo, ln_g, ln_b, tm):
    n = x.shape[0]
    tok = lambda wd: pl.BlockSpec((tm, wd), lambda i: (i, 0))
    return pl.pallas_call(
        _merge_kernel,
        out_shape=jax.ShapeDtypeStruct((n, D_MODEL), F32),
        grid=(n // tm,),
        in_specs=[tok(D_MODEL)] + [tok(BRANCH_W)] * 4 + [
            _const_spec((D_MODEL, N_BRANCH * D_MODEL)), _const_spec((1, N_BRANCH * D_MODEL)),
            _const_spec((N_BRANCH, BRANCH_W, D_MODEL)), _const_spec((D_MODEL, D_MODEL)),
            _const_spec((1, D_MODEL)), _const_spec((1, D_MODEL))],
        out_specs=tok(D_MODEL),
        compiler_params=_cp("parallel"),
        name="merge",
    )(x, *branches, wg, bg, wb, wo, ln_g, ln_b)


FF_CHUNK = D_FF // 2


def _ffn_kernel(x_ref, wg_ref, wu_ref, wd_ref, g_ref, b_ref, out_ref):
    x = x_ref[...]
    xb = x.astype(BF16)
    acc = None
    for c in range(D_FF // FF_CHUNK):
        sl = slice(c * FF_CHUNK, (c + 1) * FF_CHUNK)
        h = (_silu(jnp.dot(xb, wg_ref[:, sl], preferred_element_type=F32))
             * jnp.dot(xb, wu_ref[:, sl], preferred_element_type=F32))
        part = jnp.dot(h.astype(BF16), wd_ref[sl, :], preferred_element_type=F32)
        acc = part if acc is None else acc + part
    out_ref[...] = _layer_norm(DN_ALPHA * x + acc, g_ref[...], b_ref[...])


def _ffn(x, wg, wu, wd, ln_g, ln_b, tm):
    n = x.shape[0]
    tok = pl.BlockSpec((tm, D_MODEL), lambda i: (i, 0))
    return pl.pallas_call(
        _ffn_kernel,
        out_shape=jax.ShapeDtypeStruct((n, D_MODEL), F32),
        grid=(n // tm,),
        in_specs=[tok, _const_spec((D_MODEL, D_FF)), _const_spec((D_MODEL, D_FF)),
                  _const_spec((D_FF, D_MODEL)), _const_spec((1, D_MODEL)), _const_spec((1, D_MODEL))],
        out_specs=tok,
        compiler_params=_cp("parallel"),
        name="ffn",
    )(x, wg, wu, wd, ln_g, ln_b)


def _router_kernel(x_ref, w_ref, b_ref, gate_ref):
    logits = _dot3(x_ref[...], w_ref[...]) + b_ref[...]
    lane = _iota(logits.shape, 1)
    neg = jnp.float32(-jnp.inf)
    logits = jnp.where(lane < N_EXPERTS, logits, neg)
    m1 = jnp.max(logits, axis=-1, keepdims=True)
    i1 = jnp.min(jnp.where(logits == m1, lane, 128), axis=-1, keepdims=True)
    rest = jnp.where(lane == i1, neg, logits)
    m2 = jnp.max(rest, axis=-1, keepdims=True)
    i2 = jnp.min(jnp.where(rest == m2, lane, 128), axis=-1, keepdims=True)
    e2 = jnp.exp(m2 - m1)
    w1 = 1.0 / (1.0 + e2)
    w2 = e2 / (1.0 + e2)
    gate_ref[...] = jnp.where(lane == i1, w1, 0.0) + jnp.where(lane == i2, w2, 0.0)


def _router(x, w, b, tm):
    n = x.shape[0]
    return pl.pallas_call(
        _router_kernel,
        out_shape=jax.ShapeDtypeStruct((n, 128), F32),
        grid=(n // tm,),
        in_specs=[pl.BlockSpec((tm, D_MODEL), lambda i: (i, 0)),
                  _const_spec((D_MODEL, 128)), _const_spec((1, 128))],
        out_specs=pl.BlockSpec((tm, 128), lambda i: (i, 0)),
        compiler_params=_cp("parallel"),
        name="router",
    )(x, w, b)


def _moe_kernel(x_ref, gate_ref, wg_ref, wu_ref, wd_ref, g_ref, b_ref, out_ref, acc_ref):
    e = pl.program_id(1)
    c = pl.program_id(2)

    @pl.when((e == 0) & (c == 0))
    def _():
        acc_ref[...] = jnp.zeros_like(acc_ref)

    x = x_ref[...]
    xb = x.astype(BF16)
    gate = gate_ref[...]
    ge = jnp.sum(jnp.where(_iota(gate.shape, 1) == e, gate, 0.0), axis=-1, keepdims=True)
    h = (_silu(jnp.dot(xb, wg_ref[...], preferred_element_type=F32))
         * jnp.dot(xb, wu_ref[...], preferred_element_type=F32))
    acc_ref[...] += ge * jnp.dot(h.astype(BF16), wd_ref[...], preferred_element_type=F32)

    @pl.when((e == N_EXPERTS - 1) & (c == D_FF // FF_CHUNK - 1))
    def _():
        out_ref[...] = _layer_norm(DN_ALPHA * x + acc_ref[...], g_ref[...], b_ref[...])


def _moe(x, gate, wg, wu, wd, ln_g, ln_b, tm):
    n = x.shape[0]
    nch = D_FF // FF_CHUNK
    tok = lambda wd_: pl.BlockSpec((tm, wd_), lambda i, e, c: (i, 0))
    return pl.pallas_call(
        _moe_kernel,
        out_shape=jax.ShapeDtypeStruct((n, D_MODEL), F32),
        grid=(n // tm, N_EXPERTS, nch),
        in_specs=[tok(D_MODEL), tok(128),
                  pl.BlockSpec((None, D_MODEL, FF_CHUNK), lambda i, e, c: (e, 0, c)),
                  pl.BlockSpec((None, D_MODEL, FF_CHUNK), lambda i, e, c: (e, 0, c)),
                  pl.BlockSpec((None, FF_CHUNK, D_MODEL), lambda i, e, c: (e, c, 0)),
                  pl.BlockSpec((1, D_MODEL), lambda i, e, c: (0, 0)),
                  pl.BlockSpec((1, D_MODEL), lambda i, e, c: (0, 0))],
        out_specs=tok(D_MODEL),
        scratch_shapes=[pltpu.VMEM((tm, D_MODEL), F32)],
        compiler_params=_cp("parallel", "arbitrary", "arbitrary"),
        name="moe",
    )(x, gate, wg, wu, wd, ln_g, ln_b)


def _shift_rows(x, s, fill):
    rolled = pltpu.roll(x, s, 0)
    return jnp.where(_iota(x.shape, 0) >= s, rolled, fill)


def _causal_conv4(ext_ref, x, cw_ref, cb_ref, lc):
    ext_ref[8:8 + lc, :] = x
    out = cb_ref[...] + cw_ref[3:4, :] * x
    for j in range(3):
        out = out + cw_ref[j:j + 1, :] * ext_ref[5 + j:5 + j + lc, :]
    ext_ref[0:8, :] = ext_ref[lc:lc + 8, :]
    return out


RG_CHUNK = 256


def _rglru_kernel(u_ref, cw_ref, cb_ref, wa_ref, ba_ref, wx_ref, bx_ref, lam_ref,
                  y_ref, h_ref, ext_ref, hc_ref):
    lc = RG_CHUNK

    @pl.when(pl.program_id(1) == 0)
    def _():
        ext_ref[0:8, :] = jnp.zeros((8, BRANCH_W), F32)
        hc_ref[...] = jnp.zeros_like(hc_ref)

    xc = _causal_conv4(ext_ref, u_ref[0, :, 0:BRANCH_W], cw_ref, cb_ref, lc)
    r = _sigmoid(_dot(xc, wa_ref[...]) + ba_ref[...])
    i = _sigmoid(_dot(xc, wx_ref[...]) + bx_ref[...])
    log_a = -RG_C * r * _softplus(-lam_ref[...])
    a = jnp.exp(log_a)
    u = jnp.sqrt(1.0 - jnp.exp(2.0 * log_a)) * (i * xc)
    s = 1
    while s < lc:
        u = u + a * _shift_rows(u, s, 0.0)
        a = a * _shift_rows(a, s, 1.0)
        s *= 2
    h = u + a * hc_ref[...]
    hc_ref[...] = h[lc - 1:lc, :]
    h_ref[0] = h[lc - 1:lc, :]
    y_ref[0] = h * _gelu_tanh(u_ref[0, :, BRANCH_W:2 * BRANCH_W])


def _rglru_prompt(u, cw, cb, wa, ba, wx, bx, lam):
    bsz, seq, _ = u.shape
    lc = RG_CHUNK
    return pl.pallas_call(
        _rglru_kernel,
        out_shape=[jax.ShapeDtypeStruct((bsz, seq, BRANCH_W), F32),
                   jax.ShapeDtypeStruct((bsz, 1, BRANCH_W), F32)],
        grid=(bsz, seq // lc),
        in_specs=[pl.BlockSpec((1, lc, W_RG), lambda b, c: (b, c, 0)),
                  _const_spec((4, BRANCH_W)), _const_spec((1, BRANCH_W)),
                  _const_spec((BRANCH_W, BRANCH_W)), _const_spec((1, BRANCH_W)),
                  _const_spec((BRANCH_W, BRANCH_W)), _const_spec((1, BRANCH_W)),
                  _const_spec((1, BRANCH_W))],
        out_specs=[pl.BlockSpec((1, lc, BRANCH_W), lambda b, c: (b, c, 0)),
                   pl.BlockSpec((1, 1, BRANCH_W), lambda b, c: (b, 0, 0))],
        scratch_shapes=[pltpu.VMEM((lc + 8, BRANCH_W), F32), pltpu.VMEM((1, BRANCH_W), F32)],
        compiler_params=_cp("parallel", "arbitrary"),
        name="rglru",
    )(u, cw, cb, wa, ba, wx, bx, lam)


ML_CHUNK = 128


def _mlstm_kernel(u_ref, cw_ref, cb_ref, nw_ref, o_ref, c_ref, n_ref, m_ref, ext_ref):
    lc = ML_CHUNK
    W = BRANCH_W

    @pl.when(pl.program_id(1) == 0)
    def _():
        ext_ref[0:8, :] = jnp.zeros((8, 2 * W), F32)
        c_ref[...] = jnp.zeros_like(c_ref)
        n_ref[...] = jnp.zeros_like(n_ref)
        m_ref[...] = jnp.zeros_like(m_ref)

    qk = _silu(_causal_conv4(ext_ref, u_ref[0, :, 0:2 * W], cw_ref, cb_ref, lc))
    q = qk[:, 0:W]
    k = qk[:, W:2 * W] * (HEAD_W ** -0.5)
    v = u_ref[0, :, 2 * W:3 * W]
    og = u_ref[0, :, 3 * W:4 * W]
    ipre = u_ref[0, :, 4 * W:4 * W + 128]
    logf = _log_sigmoid(u_ref[0, :, 4 * W + 128:4 * W + 256])
    bcum = _dot_exact_lhs(_tril_ones(lc), logf)
    rowsrc = ipre - bcum
    rows_t = rowsrc.T
    lane = _iota((1, W), 1) // HEAD_W
    causal = _iota((lc, lc), 0) >= _iota((lc, lc), 1)
    cmat = c_ref[0]
    nrow = n_ref[0]
    h_acc = jnp.zeros((lc, W), F32)
    ws_full = jnp.zeros((lc, W), F32)
    keep_full = jnp.zeros((1, W), F32)
    kb = k.astype(BF16)
    vb = v.astype(BF16)
    for h in range(HEADS):
        hm = lane == h
        b_col = bcum[:, h:h + 1]
        log_d = jnp.where(causal, b_col + rows_t[h:h + 1, :], -jnp.inf)
        m_prev = m_ref[0, h:h + 1, 0:1]
        log_inter = b_col + m_prev
        m_t = jnp.maximum(log_inter, jnp.max(log_d, axis=-1, keepdims=True))
        dmat = jnp.exp(log_d - m_t)
        inter = jnp.exp(log_inter - m_t)
        qh = jnp.where(hm, q, 0.0)
        s = _dot_nt(qh, kb) * dmat
        num = _dot(s, vb) + inter * _dot_nt(qh, cmat)
        dot = (jnp.sum(s, axis=-1, keepdims=True)
               + inter * jnp.sum(qh * nrow, axis=-1, keepdims=True))
        den = jnp.maximum(jnp.abs(dot), jnp.exp(-m_t))
        h_acc = h_acc + jnp.where(hm, num / den, 0.0)
        m_new = m_t[lc - 1:lc, :]
        b_last = b_col[lc - 1:lc, :]
        ws_col = jnp.exp(rowsrc[:, h:h + 1] + (b_last - m_new))
        keep = jnp.exp(b_last + m_prev - m_new)
        ws_full = ws_full + jnp.where(hm, ws_col, 0.0)
        keep_full = keep_full + jnp.where(hm, keep, 0.0)
        m_ref[0, h:h + 1, :] = jnp.broadcast_to(m_new, (1, 128))
    blk = (_iota((W, W), 0) // HEAD_W) == (_iota((W, W), 1) // HEAD_W)
    vw = v * ws_full
    c_ref[0] = keep_full * cmat + jnp.where(blk, _dot_tn(vw, kb), 0.0)
    n_ref[0] = keep_full * nrow + jnp.sum(ws_full * k, axis=0, keepdims=True)
    pm = _head_mean_mat()
    mu = _dot_exact_rhs(h_acc, pm)
    hc = h_acc - mu
    var = _dot_exact_rhs(hc * hc, pm)
    o_ref[0] = hc * lax.rsqrt(var + 1e-6) * nw_ref[...] * _sigmoid(og)


def _mlstm_prompt(u, cw, cb, nw):
    bsz, seq, _ = u.shape
    lc = ML_CHUNK
    W = BRANCH_W
    return pl.pallas_call(
        _mlstm_kernel,
        out_shape=[jax.ShapeDtypeStruct((bsz, seq, W), F32),
                   jax.ShapeDtypeStruct((bsz, W, W), F32),
                   jax.ShapeDtypeStruct((bsz, 1, W), F32),
                   jax.ShapeDtypeStruct((bsz, 8, 128), F32)],
        grid=(bsz, seq // lc),
        in_specs=[pl.BlockSpec((1, lc, W_ML), lambda b, c: (b, c, 0)),
                  _const_spec((4, 2 * W)), _const_spec((1, 2 * W)), _const_spec((1, W))],
        out_specs=[pl.BlockSpec((1, lc, W), lambda b, c: (b, c, 0)),
                   pl.BlockSpec((1, W, W), lambda b, c: (b, 0, 0)),
                   pl.BlockSpec((1, 1, W), lambda b, c: (b, 0, 0)),
                   pl.BlockSpec((1, 8, 128), lambda b, c: (b, 0, 0))],
        scratch_shapes=[pltpu.VMEM((lc + 8, 2 * W), F32)],
        compiler_params=_cp("parallel", "arbitrary"),
        name="mlstm",
    )(u, cw, cb, nw)


GLA_CHUNK = 128
GLA_QK = HEADS * GLA_DK


def _gla_level_ref(b, m, lc):
    if 2 * m == lc:
        return jnp.broadcast_to(b[m - 1:m, :], b.shape)
    if m >= 4:
        nb = lc // (2 * m)
        mid = b.reshape(nb, 2 * m, b.shape[1])[:, m - 1:m, :]
        return jnp.broadcast_to(mid, (nb, 2 * m, b.shape[1])).reshape(b.shape)
    pos = _iota(b.shape, 0) % (2 * m)
    out = b
    for p in range(2 * m):
        sh = p - (m - 1)
        if sh != 0:
            out = jnp.where(pos == p, pltpu.roll(b, sh % lc, 0), out)
    return out


def _gla_kernel(u_ref, au_ref, ab_ref, nw_ref, o_ref, s_ref):
    lc = GLA_CHUNK
    W = BRANCH_W

    @pl.when(pl.program_id(1) == 0)
    def _():
        s_ref[...] = jnp.zeros_like(s_ref)

    q = u_ref[0, :, 0:GLA_QK] * (GLA_DK ** -0.5)
    k = u_ref[0, :, GLA_QK:2 * GLA_QK]
    v = u_ref[0, :, 2 * GLA_QK:2 * GLA_QK + W]
    g = u_ref[0, :, 2 * GLA_QK + W:2 * GLA_QK + 2 * W]
    ac = u_ref[0, :, 2 * GLA_QK + 2 * W:2 * GLA_QK + 2 * W + 128]
    la = _log_sigmoid(_dot(ac, au_ref[...]) + ab_ref[...]) / GLA_TAU
    b = _dot_exact_lhs(_tril_ones(lc), la)
    st = s_ref[0]
    o = _dot_nt(q * jnp.exp(b), st)
    gsum = ((_iota((GLA_QK, W), 0) // GLA_DK) == (_iota((GLA_QK, W), 1) // HEAD_W)).astype(BF16)
    o = o + jnp.dot((q * k).astype(BF16), gsum, preferred_element_type=F32) * v

    hq = _iota((lc, GLA_QK), 1) // GLA_DK
    row1 = _iota((lc, 1), 0)
    rt = _iota((HEADS * lc, lc), 0) % lc
    cs = _iota((HEADS * lc, lc), 1)
    kb_rows = None
    attn = jnp.zeros((HEADS * lc, lc), F32)
    m = lc // 2
    while m >= 1:
        ref = _gla_level_ref(b, m, lc)
        upper = (row1 % (2 * m)) >= m
        qt = q * jnp.exp(jnp.where(upper, b - ref, -jnp.inf))
        kt = k * jnp.exp(jnp.where(upper, -jnp.inf, ref - b))
        qs = jnp.concatenate([jnp.where(hq == h, qt, 0.0) for h in range(HEADS)], axis=0)
        part = _dot_nt(qs, kt)
        if 2 * m == lc:
            attn = attn + part
        else:
            attn = attn + jnp.where((rt // (2 * m)) == (cs // (2 * m)), part, 0.0)
        m //= 2
    av = _dot(attn, v)
    hv = _iota((lc, W), 1) // HEAD_W
    for h in range(HEADS):
        o = o + jnp.where(hv == h, av[h * lc:(h + 1) * lc, :], 0.0)
    b_last = b[lc - 1:lc, :]
    blk = (_iota((W, GLA_QK), 0) // HEAD_W) == (_iota((W, GLA_QK), 1) // GLA_DK)
    s_ref[0] = jnp.exp(b_last) * st + jnp.where(blk, _dot_tn(v, k * jnp.exp(b_last - b)), 0.0)
    ms = _dot_exact_rhs(o * o, _head_mean_mat())
    o_ref[0] = o * lax.rsqrt(ms + 1e-6) * nw_ref[...] * _silu(g)


def _gla_prompt(u, au, ab, nw):
    bsz, seq, _ = u.shape
    lc = GLA_CHUNK
    W = BRANCH_W
    return pl.pallas_call(
        _gla_kernel,
        out_shape=[jax.ShapeDtypeStruct((bsz, seq, W), F32),
                   jax.ShapeDtypeStruct((bsz, W, GLA_QK), F32)],
        grid=(bsz, seq // lc),
        in_specs=[pl.BlockSpec((1, lc, W_GL), lambda b, c: (b, c, 0)),
                  _const_spec((128, GLA_QK)), _const_spec((1, GLA_QK)), _const_spec((1, W))],
        out_specs=[pl.BlockSpec((1, lc, W), lambda b, c: (b, c, 0)),
                   pl.BlockSpec((1, W, GLA_QK), lambda b, c: (b, 0, 0))],
        compiler_params=_cp("parallel", "arbitrary"),
        name="gla",
    )(u, au, ab, nw)


RW_CHUNK = 256


def _head_sum_mat(width=BRANCH_W, group=HEAD_W):
    r = _iota((width, width), 0) // group
    c = _iota((width, width), 1) // group
    return jnp.where(r == c, 1.0, 0.0).astype(BF16)


def _rwkv_pointwise(um, p_ref, lora_ref):
    W = BRANCH_W
    r = um[:, 0:W]
    k = um[:, W:2 * W]
    v = um[:, 2 * W:3 * W]
    lo = um[:, 3 * W:3 * W + 128]
    w_log = -_softplus(-(p_ref[0:1, :] + _dot(jnp.tanh(lo), lora_ref[0]))) - 0.5
    decay = jnp.exp(-jnp.exp(w_log))
    a = _sigmoid(p_ref[1:2, :] + _dot(lo, lora_ref[1]))
    g = _dot(_sigmoid(lo), lora_ref[2])
    hs = _head_sum_mat()
    kk = k * p_ref[2:3, :]
    kk = kk / jnp.maximum(jnp.sqrt(_dot_exact_rhs(kk * kk, hs)), 1e-12)
    k2 = k * (1.0 + (a - 1.0) * p_ref[3:4, :])
    bonus = _dot_exact_rhs(r * k2 * p_ref[4:5, :], hs)
    return decay, kk, kk * a, k2, r, v, g, bonus * v


def _rwkv_prep_kernel(u_ref, mu_ref, p_ref, lora_ref, *rest):
    outs, ext_ref = rest[:8], rest[8]
    T = RW_CHUNK

    @pl.when(pl.program_id(1) == 0)
    def _():
        ext_ref[0:8, :] = jnp.zeros((8, RWKV_COLS), F32)

    u = u_ref[0]
    ext_ref[8:8 + T, :] = u
    prev = ext_ref[7:7 + T, :]
    ext_ref[0:8, :] = ext_ref[T:T + 8, :]
    um = u + mu_ref[...] * (prev - u)
    for o_ref, val in zip(outs, _rwkv_pointwise(um, p_ref, lora_ref)):
        o_ref[0] = val


def _rwkv_prep(u, mu, p, lora):
    bsz, seq, _ = u.shape
    T = RW_CHUNK
    W = BRANCH_W
    return pl.pallas_call(
        _rwkv_prep_kernel,
        out_shape=[jax.ShapeDtypeStruct((bsz, seq, W), F32)] * 8,
        grid=(bsz, seq // T),
        in_specs=[pl.BlockSpec((1, T, RWKV_COLS), lambda b, c: (b, c, 0)),
                  _const_spec((1, RWKV_COLS)), _const_spec((8, W)), _const_spec((3, 128, W))],
        out_specs=[pl.BlockSpec((1, T, W), lambda b, c: (b, c, 0))] * 8,
        scratch_shapes=[pltpu.VMEM((T + 8, RWKV_COLS), F32)],
        compiler_params=_cp("parallel", "arbitrary"),
        name="rwkv_prep",
    )(u, mu, p, lora)


def _delta_rule_step(s_ref, ni, w, kk, kka, k, r, v_row, y_store, s_out_ref=None):
    s_out = s_ref if s_out_ref is None else s_out_ref
    sa = [jnp.sum(s_ref[i] * kk, axis=0, keepdims=True) for i in range(ni)]
    for i in range(ni):
        sn = s_ref[i] * w - sa[i] * kka + v_row(i) * k
        s_out[i] = sn
        y_store(i, jnp.sum(sn * r, axis=0, keepdims=True))


RW_SCAN_T = 64


def _rwkv_scan_kernel(w_ref, kk_ref, kka_ref, k_ref, r_ref, v_ref, s0_ref, y_ref, s_out_ref, s_ref):
    ni = s_ref.shape[0]
    tb = pl.program_id(1)

    @pl.when(tb == 0)
    def _():
        s_ref[...] = s0_ref[...]

    def token(t, carry):
        def y_store(i, row):
            y_ref[t, i:i + 1, :] = row
        _delta_rule_step(s_ref, ni, w_ref[t], kk_ref[t], kka_ref[t], k_ref[t], r_ref[t],
                         lambda i: v_ref[t, i:i + 1, :], y_store)
        return carry

    lax.fori_loop(0, w_ref.shape[0], token, 0)

    @pl.when(tb == pl.num_programs(1) - 1)
    def _():
        s_out_ref[...] = s_ref[...]


def _rwkv_scan(jvecs, v, s0):
    groups, seq, ni, _ = v.shape
    T = min(RW_SCAN_T, seq)
    jspec = pl.BlockSpec((None, T, HEAD_W, 128), lambda g, t: (g, t, 0, 0))
    ispec = pl.BlockSpec((None, T, ni, 128), lambda g, t: (g, t, 0, 0))
    sspec = pl.BlockSpec((None, ni, HEAD_W, 128), lambda g, t: (g, 0, 0, 0))
    return pl.pallas_call(
        _rwkv_scan_kernel,
        out_shape=[jax.ShapeDtypeStruct(v.shape, F32), jax.ShapeDtypeStruct(s0.shape, F32)],
        grid=(groups, seq // T),
        in_specs=[jspec] * 5 + [ispec, sspec],
        out_specs=[ispec, sspec],
        scratch_shapes=[pltpu.VMEM((ni, HEAD_W, 128), F32)],
        compiler_params=_cp("parallel", "arbitrary"),
        name="rwkv_scan",
    )(*jvecs, v, s0)


def _rwkv_post_kernel(y_ref, g_ref, bv_ref, ln_ref, o_ref):
    y = y_ref[0]
    pm = _head_mean_mat()
    mu = _dot_exact_rhs(y, pm)
    yc = y - mu
    var = _dot_exact_rhs(yc * yc, pm)
    o = yc * lax.rsqrt(var + RWKV_GN_EPS) * ln_ref[0:1, :] + ln_ref[1:2, :]
    o_ref[0] = (o + bv_ref[0]) * g_ref[0]


def _rwkv_post(y, g, bv, ln):
    bsz, seq, W = y.shape
    T = RW_CHUNK
    tok = pl.BlockSpec((1, T, W), lambda b, c: (b, c, 0))
    return pl.pallas_call(
        _rwkv_post_kernel,
        out_shape=jax.ShapeDtypeStruct((bsz, seq, W), F32),
        grid=(bsz, seq // T),
        in_specs=[tok, tok, tok, _const_spec((2, W))],
        out_specs=tok,
        compiler_params=_cp("parallel", "parallel"),
        name="rwkv_post",
    )(y, g, bv, ln)


(PC_W0, PC_A0, PC_KK, PC_KA, PC_RK, PC_LNW, PC_LNB, PC_MLNW, PC_GLNW,
 PC_RCW, PC_RCB, PC_RBA, PC_RBX, PC_RLAM) = (0, 1, 2, 3, 4, 5, 6, 7, 8, 9, 13, 14, 15, 16)
PC_COLS = 17
DEC_B = 128


def _rowsum(x):
    return jnp.sum(x, axis=0, keepdims=True)


def _decode_kernel(
        ur_ref, uk_ref, uv_ref, ulo_ref, pr_ref, pk_ref, pv_ref, plo_ref,
        mur_ref, muk_ref, muv_ref, mulo_ref, lora_ref, srw_ref,
        mq_ref, mk_ref, mv_ref, mo_ref, mi_ref, mf_ref, bq_ref, bk_ref, cwq_ref, cwk_ref,
        c_ref, n_ref, m_ref,
        gq_ref, gk_ref, gv_ref, gg_ref, ga_ref, au_ref, ab_ref, gs_ref,
        rx_ref, ry_ref, rb_ref, wa_ref, wx_ref, h_ref,
        pc_ref,
        oa_ref, ob_ref, oc_ref, od_ref, srw_o, c_o, n_o, m_o, gs_o, h_o,
        y_scr):
    col = lambda j: pc_ref[:, j:j + 1]

    def shift(u_ref, p_ref, mu_ref):
        u = u_ref[...]
        return u + mu_ref[...] * (p_ref[...] - u)
    r = shift(ur_ref, pr_ref, mur_ref)
    k = shift(uk_ref, pk_ref, muk_ref)
    v = shift(uv_ref, pv_ref, muv_ref)
    lo = shift(ulo_ref, plo_ref, mulo_ref)
    w_log = -_softplus(-(col(PC_W0) + _dot(lora_ref[0], jnp.tanh(lo)))) - 0.5
    decay = jnp.exp(-jnp.exp(w_log))
    a = _sigmoid(col(PC_A0) + _dot(lora_ref[1], lo))
    g = _dot(lora_ref[2], _sigmoid(lo))
    kk = k * col(PC_KK)
    kk = kk / jnp.maximum(jnp.sqrt(_rowsum(kk * kk)), 1e-12)
    k2 = k * (1.0 + (a - 1.0) * col(PC_KA))
    bonus = _rowsum(r * k2 * col(PC_RK))

    def y_store(i, row):
        y_scr[i:i + 1, :] = row
    _delta_rule_step(srw_ref, HEAD_W, decay, kk, kk * a, k2, r, lambda i: v[i:i + 1, :], y_store, srw_o)
    y = y_scr[...]
    yc = y - jnp.mean(y, axis=0, keepdims=True)
    var = jnp.mean(yc * yc, axis=0, keepdims=True)
    o = yc * lax.rsqrt(var + RWKV_GN_EPS) * col(PC_LNW) + col(PC_LNB)
    oa_ref[...] = (o + bonus * v) * g

    def conv(u_ref, b_ref, cw_ref):
        out = cw_ref[:, 4:5] + cw_ref[:, 3:4] * u_ref[...]
        for j in range(3):
            out = out + cw_ref[:, j:j + 1] * b_ref[j]
        return out
    q = _silu(conv(mq_ref, bq_ref, cwq_ref))
    k = _silu(conv(mk_ref, bk_ref, cwk_ref)) * (HEAD_W ** -0.5)
    v = mv_ref[...]
    ipre = mi_ref[...]
    logf = _log_sigmoid(mf_ref[...])
    m_prev = m_ref[...]
    m_t = jnp.maximum(logf + m_prev, ipre)
    inter = jnp.exp(logf + m_prev - m_t)
    wsc = jnp.exp(ipre - m_t)
    s = _rowsum(q * k) * wsc
    n_prev = n_ref[...]
    den = jnp.maximum(jnp.abs(s + inter * _rowsum(n_prev * q)), jnp.exp(-m_t))
    for i in range(HEAD_W):
        ci = c_ref[i]
        vi = v[i:i + 1, :]
        y_scr[i:i + 1, :] = (s * vi + inter * _rowsum(ci * q)) / den
        c_o[i] = inter * ci + (wsc * vi) * k
    n_o[...] = inter * n_prev + wsc * k
    m_o[...] = m_t
    y = y_scr[...]
    yc = y - jnp.mean(y, axis=0, keepdims=True)
    var = jnp.mean(yc * yc, axis=0, keepdims=True)
    ob_ref[...] = yc * lax.rsqrt(var + 1e-6) * col(PC_MLNW) * _sigmoid(mo_ref[...])

    q = gq_ref[...] * (GLA_DK ** -0.5)
    k = gk_ref[...]
    v = gv_ref[...]
    eb = jnp.exp(_log_sigmoid(_dot(au_ref[...], ga_ref[...]) + ab_ref[...]) / GLA_TAU)
    attn = _rowsum(q * k)
    qe = q * eb
    for i in range(HEAD_W):
        si = gs_ref[i]
        vi = v[i:i + 1, :]
        y_scr[i:i + 1, :] = attn * vi + _rowsum(si * qe)
        gs_o[i] = eb * si + k * vi
    y = y_scr[...]
    ms = jnp.mean(y * y, axis=0, keepdims=True)
    oc_ref[...] = y * lax.rsqrt(ms + 1e-6) * col(PC_GLNW) * _silu(gg_ref[...])

    xc = col(PC_RCB) + col(PC_RCW + 3) * rx_ref[...]
    for j in range(3):
        xc = xc + col(PC_RCW + j) * rb_ref[j]
    rg = _sigmoid(_dot(wa_ref[...], xc) + col(PC_RBA))
    ig = _sigmoid(_dot(wx_ref[...], xc) + col(PC_RBX))
    log_a = -RG_C * rg * _softplus(-col(PC_RLAM))
    hn = jnp.exp(log_a) * h_ref[...] + jnp.sqrt(1.0 - jnp.exp(2.0 * log_a)) * (ig * xc)
    h_o[...] = hn
    od_ref[...] = hn * _gelu_tanh(ry_ref[...])


def _decode_mixers(ut_rw, prev_t, mu_c, lora_t, s_rw,
                   ut_ml, mconv_t, ml_cw, c_st, n_st, m_st,
                   ut_gl, au_t, ab_c, g_st,
                   ut_rg, rconv_t, wa_t, wx_t, h_st, pcols):
    nb = DEC_B
    H, HW = HEADS, HEAD_W

    def blk(arr, view, block, index):
        a = arr.reshape(view)
        nd = len(block)
        return a, pl.BlockSpec(block, index)

    ins = []
    v14 = (14, HW, nb)
    v7 = (7, 128, nb)
    for arr in (ut_rw, prev_t):
        ins.append(blk(arr, v14, (None, HW, nb), lambda h: (h, 0, 0)))
        ins.append(blk(arr, v14, (None, HW, nb), lambda h: (4 + h, 0, 0)))
        ins.append(blk(arr, v14, (None, HW, nb), lambda h: (8 + h, 0, 0)))
        ins.append(blk(arr, v7, (None, 128, nb), lambda h: (6, 0, 0)))
    ins.append(blk(mu_c, (14, HW, 1), (None, HW, 1), lambda h: (h, 0, 0)))
    ins.append(blk(mu_c, (14, HW, 1), (None, HW, 1), lambda h: (4 + h, 0, 0)))
    ins.append(blk(mu_c, (14, HW, 1), (None, HW, 1), lambda h: (8 + h, 0, 0)))
    ins.append(blk(mu_c, (7, 128, 1), (None, 128, 1), lambda h: (6, 0, 0)))
    ins.append(blk(lora_t, (3, BRANCH_W, 128), (3, HW, 128), lambda h: (0, h, 0)))
    st_spec = lambda: pl.BlockSpec((None, HW, HW, nb), lambda h: (h, 0, 0, 0))
    ins.append((s_rw, st_spec()))
    v20 = (20, HW, nb)
    for j in range(4):
        ins.append(blk(ut_ml, v20, (None, HW, nb), functools.partial(lambda h, j: (4 * j + h, 0, 0), j=j)))
    ins.append(blk(ut_ml[1024:1028], (4, 1, nb), (None, 1, nb), lambda h: (h, 0, 0)))
    ins.append(blk(ut_ml[1152:1156], (4, 1, nb), (None, 1, nb), lambda h: (h, 0, 0)))
    ins.append(blk(mconv_t, (3, 8, HW, nb), (3, None, HW, nb), lambda h: (0, h, 0, 0)))
    ins.append(blk(mconv_t, (3, 8, HW, nb), (3, None, HW, nb), lambda h: (0, 4 + h, 0, 0)))
    ins.append(blk(ml_cw, (8, HW, 5), (None, HW, 5), lambda h: (h, 0, 0)))
    ins.append(blk(ml_cw, (8, HW, 5), (None, HW, 5), lambda h: (4 + h, 0, 0)))
    ins.append((c_st, st_spec()))
    ins.append(blk(n_st, (H, HW, nb), (None, HW, nb), lambda h: (h, 0, 0)))
    ins.append(blk(m_st, (H, 1, nb), (None, 1, nb), lambda h: (h, 0, 0)))
    ins.append(blk(ut_gl, (28, GLA_DK, nb), (None, GLA_DK, nb), lambda h: (h, 0, 0)))
    ins.append(blk(ut_gl, (28, GLA_DK, nb), (None, GLA_DK, nb), lambda h: (4 + h, 0, 0)))
    ins.append(blk(ut_gl, v14, (None, HW, nb), lambda h: (4 + h, 0, 0)))
    ins.append(blk(ut_gl, v14, (None, HW, nb), lambda h: (8 + h, 0, 0)))
    ins.append(blk(ut_gl, v7, (None, 128, nb), lambda h: (6, 0, 0)))
    ins.append(blk(au_t, (H, GLA_DK, 128), (None, GLA_DK, 128), lambda h: (h, 0, 0)))
    ins.append(blk(ab_c, (H, GLA_DK, 1), (None, GLA_DK, 1), lambda h: (h, 0, 0)))
    ins.append((g_st, pl.BlockSpec((None, HW, GLA_DK, nb), lambda h: (h, 0, 0, 0))))
    ins.append(blk(ut_rg, (8, HW, nb), (None, HW, nb), lambda h: (h, 0, 0)))
    ins.append(blk(ut_rg, (8, HW, nb), (None, HW, nb), lambda h: (4 + h, 0, 0)))
    ins.append(blk(rconv_t, (3, H, HW, nb), (3, None, HW, nb), lambda h: (0, h, 0, 0)))
    ins.append((wa_t, pl.BlockSpec((None, HW, HW), lambda h: (h, 0, 0))))
    ins.append((wx_t, pl.BlockSpec((None, HW, HW), lambda h: (h, 0, 0))))
    ins.append(blk(h_st, (H, HW, nb), (None, HW, nb), lambda h: (h, 0, 0)))
    ins.append(blk(pcols, (H, HW, PC_COLS), (None, HW, PC_COLS), lambda h: (h, 0, 0)))

    vec = lambda: (jax.ShapeDtypeStruct((H, HW, nb), F32), pl.BlockSpec((None, HW, nb), lambda h: (h, 0, 0)))
    mat = lambda: (jax.ShapeDtypeStruct((H, HW, HW, nb), F32), st_spec())
    outs = [vec(), vec(), vec(), vec(), mat(), mat(), vec(),
            (jax.ShapeDtypeStruct((H, 1, nb), F32), pl.BlockSpec((None, 1, nb), lambda h: (h, 0, 0))),
            (jax.ShapeDtypeStruct((H, HW, GLA_DK, nb), F32),
             pl.BlockSpec((None, HW, GLA_DK, nb), lambda h: (h, 0, 0, 0))),
            vec()]
    return pl.pallas_call(
        _decode_kernel,
        out_shape=[o[0] for o in outs],
        grid=(H,),
        in_specs=[s for _, s in ins],
        out_specs=[o[1] for o in outs],
        scratch_shapes=[pltpu.VMEM((HW, nb), F32)],
        compiler_params=_cp("parallel"),
        name="decode_mixers",
    )(*[a for a, _ in ins])


def _pad_cols(a, width):
    return jnp.pad(a, ((0, 0), (0, width - a.shape[1])))


def _pad_rows(a, rows, at=0):
    return jnp.pad(a, ((at, rows - at - a.shape[0]), (0, 0)))


def _pack_layer(p, l):
    offs = np.concatenate([[0], np.cumsum(IN_SIZES)])
    w_in, b_in = p['w_in'][l], p['b_in'][l][None, :]
    seg = lambda a, i: a[:, int(offs[i]):int(offs[i + 1])]

    def regroup(a):
        return jnp.concatenate(
            [seg(a, 0), seg(a, 1), seg(a, 2), seg(a, 3), _pad_cols(seg(a, 4), 128), _pad_cols(seg(a, 5), 128),
             seg(a, 6), seg(a, 7), seg(a, 8), seg(a, 9), _pad_cols(seg(a, 10), 128), seg(a, 11), seg(a, 12)],
            axis=1)
    k = dict(
        w_pack=regroup(w_in).astype(BF16), b_pack=regroup(b_in),
        w_gate=seg(w_in, 13).astype(BF16), b_gate=seg(b_in, 13),
        w_branch=p['w_branch'][l].astype(BF16), w_out=p['w_out'][l].astype(BF16),
        ln1_g=p['ln1_g'][l][None], ln1_b=p['ln1_b'][l][None],
        ln2_g=p['ln2_g'][l][None], ln2_b=p['ln2_b'][l][None],
    )
    w_up, a_up, g_up = p['rwkv_w_up'][l], p['rwkv_a_up'][l], p['rwkv_g_up'][l]
    lora = jnp.stack([_pad_rows(w_up, 128, 0), _pad_rows(a_up, 128, 32), _pad_rows(g_up, 128, 64)])
    k['rw_lora'] = lora.astype(BF16)
    k['rw_lora_t'] = jnp.swapaxes(lora, 1, 2).astype(BF16)
    k['rw_mu'] = p['rwkv_mu'][l][None]
    rw_rows = [p['rwkv_w0'][l], p['rwkv_a0'][l], p['rwkv_k_k'][l], p['rwkv_k_a'][l],
               p['rwkv_r_k'][l].reshape(BRANCH_W)]
    k['rw_p'] = jnp.stack(rw_rows + [jnp.zeros((BRANCH_W,), F32)] * 3)
    k['rw_ln'] = jnp.stack([p['rwkv_ln_w'][l], p['rwkv_ln_b'][l]])
    k['ml_cw'], k['ml_cb'] = p['mlstm_conv_w'][l], p['mlstm_conv_b'][l][None]
    k['ml_nw'] = p['mlstm_norm_w'][l][None]
    k['gl_au'] = _pad_rows(p['gla_alpha_up'][l], 128).astype(BF16)
    k['gl_ab'] = p['gla_alpha_b'][l][None]
    k['gl_nw'] = p['gla_norm_w'][l][None]
    wa, wx = p['rglru_wa'][l], p['rglru_wx'][l]
    eye = jnp.eye(HEADS, dtype=F32)
    bd = lambda w: jnp.einsum('gh,gij->gihj', eye, w).reshape(BRANCH_W, BRANCH_W).astype(BF16)
    k['rg_cw'], k['rg_cb'] = p['rglru_conv_w'][l], p['rglru_conv_b'][l][None]
    k['rg_wa'], k['rg_wx'] = bd(wa), bd(wx)
    k['rg_ba'], k['rg_bx'] = p['rglru_ba'][l][None], p['rglru_bx'][l][None]
    k['rg_lam'] = p['rglru_lambda'][l][None]
    k['rg_wa_t'] = jnp.swapaxes(wa, 1, 2).astype(BF16)
    k['rg_wx_t'] = jnp.swapaxes(wx, 1, 2).astype(BF16)
    cols = rw_rows + [p['rwkv_ln_w'][l], p['rwkv_ln_b'][l], p['mlstm_norm_w'][l], p['gla_norm_w'][l]]
    cols += [p['rglru_conv_w'][l][j] for j in range(4)]
    cols += [p['rglru_conv_b'][l], p['rglru_ba'][l], p['rglru_bx'][l], p['rglru_lambda'][l]]
    k['pcols'] = jnp.stack(cols, axis=1)
    k['ml_cw_t'] = jnp.concatenate([p['mlstm_conv_w'][l].T, p['mlstm_conv_b'][l][:, None]], axis=1)
    return k


def _diag_blocks(a, rb, cb):
    return jnp.stack([a[:, h * rb:(h + 1) * rb, h * cb:(h + 1) * cb] for h in range(HEADS)], axis=1)


def _prompt_mixers(x, k):
    B, L, _ = x.shape
    W = BRANCH_W
    u_rw, u_ml, u_gl, u_rg = _inproj(x.reshape(B * L, D_MODEL), k['w_pack'], k['b_pack'], 512)
    u_rw, u_ml, u_gl, u_rg = (u.reshape(B, L, -1) for u in (u_rw, u_ml, u_gl, u_rg))
    nch = B * HEADS
    rep = 128 // nch
    dec, kk, kka, k2, r, v, g, bv = _rwkv_prep(u_rw, k['rw_mu'], k['rw_p'], k['rw_lora'])

    def key_major(a):
        a = a.reshape(B, L, HEADS, HEAD_W).transpose(1, 3, 0, 2).reshape(L, HEAD_W, nch)
        return jnp.tile(a, (1, 1, rep))[None]
    vi = v.reshape(B, L, HEADS, HEAD_W // rep, rep).transpose(1, 3, 4, 0, 2).reshape(1, L, HEAD_W // rep, 128)
    s0 = jnp.zeros((1, HEAD_W // rep, HEAD_W, 128), F32)
    y, s1 = _rwkv_scan([key_major(a) for a in (dec, kk, kka, k2, r)], vi, s0)
    y = y.reshape(L, HEAD_W // rep, rep, B, HEADS).transpose(3, 0, 4, 1, 2).reshape(B, L, W)
    s_rw = s1.reshape(HEAD_W // rep, HEAD_W, rep, B, HEADS).transpose(3, 4, 0, 2, 1).reshape(B, HEADS, HEAD_W, HEAD_W)
    o_a = _rwkv_post(y, g, bv, k['rw_ln'])
    o_b, c_bd, n_row, m_row = _mlstm_prompt(u_ml, k['ml_cw'], k['ml_cb'], k['ml_nw'])
    o_c, s_bd = _gla_prompt(u_gl, k['gl_au'], k['gl_ab'], k['gl_nw'])
    o_d, h1 = _rglru_prompt(u_rg, k['rg_cw'], k['rg_cb'], k['rg_wa'], k['rg_ba'], k['rg_wx'], k['rg_bx'],
                            k['rg_lam'])
    states = (u_rw[:, L - 1], s_rw, u_ml[:, L - 3:, :2 * W], _diag_blocks(c_bd, HEAD_W, HEAD_W),
              n_row.reshape(B, HEADS, HEAD_W), m_row[:, :HEADS, 0],
              jnp.swapaxes(_diag_blocks(s_bd, HEAD_W, GLA_DK), 2, 3), u_rg[:, L - 3:, :W], h1[:, 0])
    return [o.reshape(B * L, W) for o in (o_a, o_b, o_c, o_d)], states


def _sample_mixers(x, st, k):
    W = BRANCH_W
    sh0, S0, mconv0, C0, n0, m0, gS0, rconv0, h0 = st
    u_rw, u_ml, u_gl, u_rg = _inproj(x, k['w_pack'], k['b_pack'], DEC_B)
    outs = _decode_mixers(
        u_rw.T, sh0.T, k['rw_mu'].T, k['rw_lora_t'], S0.transpose(1, 2, 3, 0),
        u_ml.T, mconv0.transpose(1, 2, 0), k['ml_cw_t'], C0.transpose(1, 2, 3, 0), n0.transpose(1, 2, 0), m0.T,
        u_gl.T, k['gl_au'].T, k['gl_ab'].T, gS0.transpose(1, 3, 2, 0),
        u_rg.T, rconv0.transpose(1, 2, 0), k['rg_wa_t'], k['rg_wx_t'], h0.T, k['pcols'])
    oa, ob, oc, od, s_rw, c_st, n_st, m_st, g_st, h_st = outs
    branches = [o.reshape(W, DEC_B).T for o in (oa, ob, oc, od)]
    states = (u_rw, s_rw.transpose(3, 0, 1, 2),
              jnp.concatenate([mconv0[:, 1:], u_ml[:, None, :2 * W]], axis=1),
              c_st.transpose(3, 0, 1, 2), n_st.transpose(2, 0, 1), m_st[:, 0, :].T,
              g_st.transpose(3, 0, 2, 1),
              jnp.concatenate([rconv0[:, 1:], u_rg[:, None, :W]], axis=1), h_st.reshape(W, DEC_B).T)
    return branches, states


def _trunk(x, states, packs, moe, is_prompt):
    n = x.shape[0]
    tm = 512 if is_prompt else DEC_B
    new_states = []
    for l in range(DEPTH):
        k = packs[l]
        if is_prompt:
            branches, st = _prompt_mixers(x.reshape(states[0], states[1], D_MODEL), k)
        else:
            branches, st = _sample_mixers(x, tuple(s[l] for s in states), k)
        new_states.append(st)
        x = _merge(x, branches, k['w_gate'], k['b_gate'], k['w_branch'], k['w_out'], k['ln1_g'], k['ln1_b'], tm)
        j = l // 2
        if l % 2 == 0:
            x = _ffn(x, moe['ffn_wg'][j], moe['ffn_wu'][j], moe['ffn_wd'][j], k['ln2_g'], k['ln2_b'], tm)
        else:
            gate = _router(x, moe['router'][j], moe['router_b'][j], tm)
            x = _moe(x, gate, moe['moe_wg'][j], moe['moe_wu'][j], moe['moe_wd'][j], k['ln2_g'], k['ln2_b'], tm)
    return x, [jnp.stack([st[i] for st in new_states], axis=0) for i in range(9)]


def kernel(x_prompt, x_sample, state_rwkv_shift, state_rwkv_S, state_mlstm_conv, state_mlstm_C,
           state_mlstm_n, state_mlstm_m, state_gla_S, state_rglru_conv, state_rglru_h,
           w_in, b_in, rwkv_mu, rwkv_w0, rwkv_w_up, rwkv_a0, rwkv_a_up, rwkv_g_up, rwkv_k_k,
           rwkv_k_a, rwkv_r_k, rwkv_ln_w, rwkv_ln_b, mlstm_conv_w, mlstm_conv_b, mlstm_norm_w,
           gla_alpha_up, gla_alpha_b, gla_norm_w, rglru_conv_w, rglru_conv_b, rglru_wa, rglru_ba,
           rglru_wx, rglru_bx, rglru_lambda, w_branch, w_out, ln1_g, ln1_b, ffn_wg, ffn_wu, ffn_wd,
           moe_router, moe_router_b, moe_wg, moe_wu, moe_wd, ln2_g, ln2_b):
    p = dict(w_in=w_in, b_in=b_in, rwkv_mu=rwkv_mu, rwkv_w0=rwkv_w0, rwkv_w_up=rwkv_w_up,
             rwkv_a0=rwkv_a0, rwkv_a_up=rwkv_a_up, rwkv_g_up=rwkv_g_up, rwkv_k_k=rwkv_k_k,
             rwkv_k_a=rwkv_k_a, rwkv_r_k=rwkv_r_k, rwkv_ln_w=rwkv_ln_w, rwkv_ln_b=rwkv_ln_b,
             mlstm_conv_w=mlstm_conv_w, mlstm_conv_b=mlstm_conv_b, mlstm_norm_w=mlstm_norm_w,
             gla_alpha_up=gla_alpha_up, gla_alpha_b=gla_alpha_b, gla_norm_w=gla_norm_w,
             rglru_conv_w=rglru_conv_w, rglru_conv_b=rglru_conv_b, rglru_wa=rglru_wa,
             rglru_ba=rglru_ba, rglru_wx=rglru_wx, rglru_bx=rglru_bx, rglru_lambda=rglru_lambda,
             w_branch=w_branch, w_out=w_out, ln1_g=ln1_g, ln1_b=ln1_b, ln2_g=ln2_g, ln2_b=ln2_b)
    packs = [_pack_layer(p, l) for l in range(DEPTH)]
    moe = dict(ffn_wg=ffn_wg.astype(BF16), ffn_wu=ffn_wu.astype(BF16), ffn_wd=ffn_wd.astype(BF16),
               router=jnp.pad(moe_router, ((0, 0), (0, 0), (0, 128 - N_EXPERTS))),
               router_b=jnp.pad(moe_router_b, ((0, 0), (0, 128 - N_EXPERTS)))[:, None, :],
               moe_wg=moe_wg.astype(BF16), moe_wu=moe_wu.astype(BF16), moe_wd=moe_wd.astype(BF16))
    B, L, _ = x_prompt.shape
    y_p, ps = _trunk(x_prompt.reshape(B * L, D_MODEL), (B, L), packs, moe, True)
    sample_states = (state_rwkv_shift, state_rwkv_S, state_mlstm_conv, state_mlstm_C, state_mlstm_n,
                     state_mlstm_m, state_gla_S, state_rglru_conv, state_rglru_h)
    nb, ls, _ = x_sample.shape
    y_s, ss = _trunk(x_sample.reshape(nb * ls, D_MODEL), sample_states, packs, moe, False)
    return (y_p.reshape(B, L, D_MODEL), y_s.reshape(nb, ls, D_MODEL), *ps, *ss)
```

```python
import functools

import jax
import jax.numpy as jnp
import numpy as np
from jax import lax
from jax.experimental import pallas as pl
from jax.experimental.pallas import tpu as pltpu

F32 = jnp.float32
BF16 = jnp.bfloat16

D_MODEL = 1024
DEPTH = 2
N_BRANCH = 4
BRANCH_W = 256
HEADS = 4
HEAD_W = 64
RWKV_COLS = 896
RWKV_GN_EPS = 64e-5
GLA_DK = 32
GLA_LORA = 16
GLA_TAU = 16.0
RG_C = 8.0
D_FF = 2816
N_EXPERTS = 8
DN_ALPHA = (2 * DEPTH) ** 0.25
LN_EPS = 1e-5

IN_SIZES = (RWKV_COLS, 512, 256, 256, 4, 4, 128, 128, 256, 256, 16, 256, 256, 4096)

W_RW = 896
W_ML = 1280
W_GL = 896
W_RG = 512
W_PACK = W_RW + W_ML + W_GL + W_RG

VMEM_LIMIT = 56 * 1024 * 1024


def _cp(*sem):
    return pltpu.CompilerParams(dimension_semantics=sem, vmem_limit_bytes=VMEM_LIMIT)


def _dot(a, b):
    return jnp.dot(a.astype(BF16), b.astype(BF16), preferred_element_type=F32)


def _dot_nt(a, b):
    return lax.dot_general(a.astype(BF16), b.astype(BF16), (((1,), (1,)), ((), ())),
                           preferred_element_type=F32)


def _dot_tn(a, b):
    return lax.dot_general(a.astype(BF16), b.astype(BF16), (((0,), (0,)), ((), ())),
                           preferred_element_type=F32)


def _split(x):
    hi = x.astype(BF16)
    lo = (x - hi.astype(F32)).astype(BF16)
    return hi, lo


def _dot_exact_lhs(a, x):
    hi, lo = _split(x)
    a = a.astype(BF16)
    return (jnp.dot(a, hi, preferred_element_type=F32) + jnp.dot(a, lo, preferred_element_type=F32))


def _dot_exact_rhs(x, a):
    hi, lo = _split(x)
    a = a.astype(BF16)
    return (jnp.dot(hi, a, preferred_element_type=F32) + jnp.dot(lo, a, preferred_element_type=F32))


def _dot3(x, w):
    xh, xl = _split(x)
    wh, wl = _split(w)
    return (jnp.dot(xh, wh, preferred_element_type=F32) + jnp.dot(xl, wh, preferred_element_type=F32)
            + jnp.dot(xh, wl, preferred_element_type=F32))


def _sigmoid(x):
    return 1.0 / (1.0 + jnp.exp(-x))


def _softplus(x):
    return jnp.maximum(x, 0.0) + jnp.log(1.0 + jnp.exp(-jnp.abs(x)))


def _log_sigmoid(x):
    return -_softplus(-x)


def _silu(x):
    return x * _sigmoid(x)


def _gelu_tanh(x):
    c = np.float32(np.sqrt(2.0 / np.pi))
    return 0.5 * x * (1.0 + jnp.tanh(c * (x + 0.044715 * (x * x * x))))


def _layer_norm(y, g, b):
    mu = jnp.mean(y, axis=-1, keepdims=True)
    yc = y - mu
    var = jnp.mean(yc * yc, axis=-1, keepdims=True)
    return yc * lax.rsqrt(var + LN_EPS) * g + b


def _iota(shape, dim):
    return lax.broadcasted_iota(jnp.int32, shape, dim)


def _head_mean_mat(width=BRANCH_W, group=HEAD_W):
    r = _iota((width, width), 0) // group
    c = _iota((width, width), 1) // group
    return jnp.where(r == c, 1.0 / group, 0.0).astype(BF16)


def _tril_ones(n):
    return jnp.where(_iota((n, n), 0) >= _iota((n, n), 1), 1.0, 0.0).astype(BF16)


def _const_spec(shape):
    nd = len(shape)
    return pl.BlockSpec(shape, lambda *_: (0,) * nd)


def _inproj_kernel(x_ref, w_ref, b_ref, o_rw, o_ml, o_gl, o_rg):
    x = x_ref[...].astype(BF16)
    off = 0
    for o_ref in (o_rw, o_ml, o_gl, o_rg):
        n = o_ref.shape[-1]
        o_ref[...] = (jnp.dot(x, w_ref[:, off:off + n], preferred_element_type=F32)
                      + b_ref[:, off:off + n])
        off += n


def _inproj(x, w, b, tm):
    n = x.shape[0]
    widths = (W_RW, W_ML, W_GL, W_RG)
    return pl.pallas_call(
        _inproj_kernel,
        out_shape=[jax.ShapeDtypeStruct((n, wd), F32) for wd in widths],
        grid=(n // tm,),
        in_specs=[pl.BlockSpec((tm, D_MODEL), lambda i: (i, 0)),
                  _const_spec((D_MODEL, W_PACK)), _const_spec((1, W_PACK))],
        out_specs=[pl.BlockSpec((tm, wd), lambda i: (i, 0)) for wd in widths],
        compiler_params=_cp("parallel"),
        name="inproj",
    )(x, w, b)


def _merge_kernel(x_ref, oa, ob, oc, od, wg_ref, bg_ref, wb_ref, wo_ref, g_ref, b_ref, out_ref):
    x = x_ref[...]
    xb = x.astype(BF16)
    merged = None
    for g, o_ref in enumerate((oa, ob, oc, od)):
        sl = slice(g * D_MODEL, (g + 1) * D_MODEL)
        gate = _sigmoid(jnp.dot(xb, wg_ref[:, sl], preferred_element_type=F32) + bg_ref[:, sl])
        up = jnp.dot(o_ref[...].astype(BF16), wb_ref[g], preferred_element_type=F32)
        merged = gate * up if merged is None else merged + gate * up
    out = jnp.dot(merged.astype(BF16), wo_ref[...], preferred_element_type=F32)
    out_ref[...] = _layer_norm(DN_ALPHA * x + out, g_ref[...], b_ref[...])


def _merge(x, branches, wg, bg, wb, wo, ln_g, ln_b, tm):
    n = x.shape[0]
    tok = lambda wd: pl.BlockSpec((tm, wd), lambda i: (i, 0))
    return pl.pallas_call(
        _merge_kernel,
        out_shape=jax.ShapeDtypeStruct((n, D_MODEL), F32),
        grid=(n // tm,),
        in_specs=[tok(D_MODEL)] + [tok(BRANCH_W)] * 4 + [
            _const_spec((D_MODEL, N_BRANCH * D_MODEL)), _const_spec((1, N_BRANCH * D_MODEL)),
            _const_spec((N_BRANCH, BRANCH_W, D_MODEL)), _const_spec((D_MODEL, D_MODEL)),
            _const_spec((1, D_MODEL)), _const_spec((1, D_MODEL))],
        out_specs=tok(D_MODEL),
        compiler_params=_cp("parallel"),
        name="merge",
    )(x, *branches, wg, bg, wb, wo, ln_g, ln_b)


FF_CHUNK = D_FF // 2


def _ffn_kernel(x_ref, wg_ref, wu_ref, wd_ref, g_ref, b_ref, out_ref):
    x = x_ref[...]
    xb = x.astype(BF16)
    acc = None
    for c in range(D_FF // FF_CHUNK):
        sl = slice(c * FF_CHUNK, (c + 1) * FF_CHUNK)
        h = (_silu(jnp.dot(xb, wg_ref[:, sl], preferred_element_type=F32))
             * jnp.dot(xb, wu_ref[:, sl], preferred_element_type=F32))
        part = jnp.dot(h.astype(BF16), wd_ref[sl, :], preferred_element_type=F32)
        acc = part if acc is None else acc + part
    out_ref[...] = _layer_norm(DN_ALPHA * x + acc, g_ref[...], b_ref[...])


def _ffn(x, wg, wu, wd, ln_g, ln_b, tm):
    n = x.shape[0]
    tok = pl.BlockSpec((tm, D_MODEL), lambda i: (i, 0))
    return pl.pallas_call(
        _ffn_kernel,
        out_shape=jax.ShapeDtypeStruct((n, D_MODEL), F32),
        grid=(n // tm,),
        in_specs=[tok, _const_spec((D_MODEL, D_FF)), _const_spec((D_MODEL, D_FF)),
                  _const_spec((D_FF, D_MODEL)), _const_spec((1, D_MODEL)), _const_spec((1, D_MODEL))],
        out_specs=tok,
        compiler_params=_cp("parallel"),
        name="ffn",
    )(x, wg, wu, wd, ln_g, ln_b)


(RT_E1, RT_E2, RT_R1, RT_R2, RT_W1, RT_W2) = range(6)
MOE_TILE = 512


def _router_kernel(x_ref, w_ref, b_ref, cin_ref, meta_ref, cnt_ref, carry_ref):
    @pl.when(pl.program_id(0) == 0)
    def _():
        carry_ref[...] = cin_ref[...]

    logits = _dot3(x_ref[...], w_ref[...]) + b_ref[...]
    tm = logits.shape[0]
    lane = _iota(logits.shape, 1)
    neg = jnp.float32(-jnp.inf)
    logits = jnp.where(lane < N_EXPERTS, logits, neg)
    m1 = jnp.max(logits, axis=-1, keepdims=True)
    i1 = jnp.min(jnp.where(logits == m1, lane, 128), axis=-1, keepdims=True)
    rest = jnp.where(lane == i1, neg, logits)
    m2 = jnp.max(rest, axis=-1, keepdims=True)
    i2 = jnp.min(jnp.where(rest == m2, lane, 128), axis=-1, keepdims=True)
    e2 = jnp.exp(m2 - m1)
    w1 = 1.0 / (1.0 + e2)
    w2 = e2 / (1.0 + e2)
    oh1 = lane == i1
    oh2 = lane == i2
    picks = jnp.where(oh1 | oh2, 1.0, 0.0)
    below = jnp.where(_iota((tm, tm), 0) > _iota((tm, tm), 1), 1.0, 0.0).astype(BF16)
    base = carry_ref[...] + jnp.dot(below, picks.astype(BF16), preferred_element_type=F32)
    r1 = jnp.sum(jnp.where(oh1, base, 0.0), axis=-1, keepdims=True)
    r2 = jnp.sum(jnp.where(oh2, base, 0.0), axis=-1, keepdims=True)
    carry_ref[...] = carry_ref[...] + jnp.sum(picks, axis=0, keepdims=True)
    cnt_ref[...] = carry_ref[...]
    meta = jnp.zeros(logits.shape, F32)
    for ln, val in ((RT_E1, i1.astype(F32)), (RT_E2, i2.astype(F32)), (RT_R1, r1), (RT_R2, r2),
                    (RT_W1, w1), (RT_W2, w2)):
        meta = jnp.where(lane == ln, val, meta)
    meta_ref[...] = meta


def _router(x, w, b, cnt_in, tm):
    n = x.shape[0]
    return pl.pallas_call(
        _router_kernel,
        out_shape=[jax.ShapeDtypeStruct((n, 128), F32), jax.ShapeDtypeStruct((1, 128), F32)],
        grid=(n // tm,),
        in_specs=[pl.BlockSpec((tm, D_MODEL), lambda i: (i, 0)),
                  _const_spec((D_MODEL, 128)), _const_spec((1, 128)), _const_spec((1, 128))],
        out_specs=[pl.BlockSpec((tm, 128), lambda i: (i, 0)), _const_spec((1, 128))],
        scratch_shapes=[pltpu.VMEM((1, 128), F32)],
        compiler_params=_cp("arbitrary"),
        name="router",
    )(x, w, b, cnt_in)


def _row_copies(n_rows, start_one, wait_shape_src, wait_shape_dst, sem, n_streams):
    def body(r, carry):
        start_one(r)
        return carry
    lax.fori_loop(0, n_rows, body, 0)
    for _ in range(n_streams):
        pltpu.make_async_copy(wait_shape_src, wait_shape_dst, sem).wait()


def _dispatch_kernel(d1_ref, d2_ref, x_ref, xs_in_ref, xs_ref, sem):
    del xs_in_ref
    tm = x_ref.shape[0]

    def start_one(r):
        src = x_ref.at[pl.ds(r, 1)]
        pltpu.make_async_copy(src, xs_ref.at[pl.ds(d1_ref[0, 0, r], 1)], sem).start()
        pltpu.make_async_copy(src, xs_ref.at[pl.ds(d2_ref[0, 0, r], 1)], sem).start()
    _row_copies(tm, start_one, x_ref, xs_ref.at[pl.ds(0, tm)], sem, 2)


def _dispatch(x, d1, d2, xs, tm):
    n = x.shape[0]
    idx = lambda a: a.reshape(n // tm, 1, tm)
    ispec = pl.BlockSpec((1, 1, tm), lambda i: (i, 0, 0), memory_space=pltpu.SMEM)
    return pl.pallas_call(
        _dispatch_kernel,
        out_shape=jax.ShapeDtypeStruct(xs.shape, F32),
        grid=(n // tm,),
        in_specs=[ispec, ispec, pl.BlockSpec((tm, D_MODEL), lambda i: (i, 0)),
                  pl.BlockSpec(memory_space=pl.ANY)],
        out_specs=pl.BlockSpec(memory_space=pl.ANY),
        scratch_shapes=[pltpu.SemaphoreType.DMA],
        input_output_aliases={3: 0},
        compiler_params=_cp("arbitrary"),
        name="moe_dispatch",
    )(idx(d1), idx(d2), x, xs)


def _experts_kernel(te_ref, nu_ref, xs_ref, wg_ref, wu_ref, wd_ref, ys_ref):
    del te_ref

    @pl.when(pl.program_id(0) < nu_ref[0])
    def _():
        xb = xs_ref[...].astype(BF16)
        acc = None
        for c in range(D_FF // FF_CHUNK):
            sl = slice(c * FF_CHUNK, (c + 1) * FF_CHUNK)
            h = (_silu(jnp.dot(xb, wg_ref[:, sl], preferred_element_type=F32))
                 * jnp.dot(xb, wu_ref[:, sl], preferred_element_type=F32))
            part = jnp.dot(h.astype(BF16), wd_ref[sl, :], preferred_element_type=F32)
            acc = part if acc is None else acc + part
        ys_ref[...] = acc

    @pl.when(pl.program_id(0) >= nu_ref[0])
    def _():
        ys_ref[...] = jnp.zeros_like(ys_ref)


def _experts(xs, tile_expert, n_used, wg, wu, wd):
    n_tiles = xs.shape[0] // MOE_TILE
    row = pl.BlockSpec((MOE_TILE, D_MODEL), lambda i, te, nu: (i, 0))
    return pl.pallas_call(
        _experts_kernel,
        out_shape=jax.ShapeDtypeStruct(xs.shape, F32),
        grid_spec=pltpu.PrefetchScalarGridSpec(
            num_scalar_prefetch=2, grid=(n_tiles,),
            in_specs=[row,
                      pl.BlockSpec((None, D_MODEL, D_FF), lambda i, te, nu: (te[i], 0, 0)),
                      pl.BlockSpec((None, D_MODEL, D_FF), lambda i, te, nu: (te[i], 0, 0)),
                      pl.BlockSpec((None, D_FF, D_MODEL), lambda i, te, nu: (te[i], 0, 0))],
            out_specs=row),
        compiler_params=_cp("arbitrary"),
        name="moe_experts",
    )(tile_expert, n_used, xs, wg, wu, wd)


def _combine_kernel(d1_ref, d2_ref, x_ref, meta_ref, ys_ref, g_ref, b_ref, out_ref, y1_ref, y2_ref, sem):
    tm = x_ref.shape[0]

    def start_one(r):
        pltpu.make_async_copy(ys_ref.at[pl.ds(d1_ref[0, 0, r], 1)], y1_ref.at[pl.ds(r, 1)], sem).start()
        pltpu.make_async_copy(ys_ref.at[pl.ds(d2_ref[0, 0, r], 1)], y2_ref.at[pl.ds(r, 1)], sem).start()
    _row_copies(tm, start_one, ys_ref.at[pl.ds(0, tm)], y1_ref, sem, 2)
    meta = meta_ref[...]
    f = meta[:, RT_W1:RT_W1 + 1] * y1_ref[...] + meta[:, RT_W2:RT_W2 + 1] * y2_ref[...]
    out_ref[...] = _layer_norm(DN_ALPHA * x_ref[...] + f, g_ref[...], b_ref[...])


def _combine(x, meta, d1, d2, ys, ln_g, ln_b, tm):
    n = x.shape[0]
    idx = lambda a: a.reshape(n // tm, 1, tm)
    ispec = pl.BlockSpec((1, 1, tm), lambda i: (i, 0, 0), memory_space=pltpu.SMEM)
    tok = lambda wd_: pl.BlockSpec((tm, wd_), lambda i: (i, 0))
    return pl.pallas_call(
        _combine_kernel,
        out_shape=jax.ShapeDtypeStruct((n, D_MODEL), F32),
        grid=(n // tm,),
        in_specs=[ispec, ispec, tok(D_MODEL), tok(128), pl.BlockSpec(memory_space=pl.ANY),
                  _const_spec((1, D_MODEL)), _const_spec((1, D_MODEL))],
        out_specs=tok(D_MODEL),
        scratch_shapes=[pltpu.VMEM((tm, D_MODEL), F32), pltpu.VMEM((tm, D_MODEL), F32),
                        pltpu.SemaphoreType.DMA],
        compiler_params=_cp("arbitrary"),
        name="moe_combine",
    )(idx(d1), idx(d2), x, meta, ys, ln_g, ln_b)


def _moe_layer(xs_groups, router_w, router_b, wg, wu, wd, ln_g, ln_b):
    tms = [min(512, x.shape[0]) for x in xs_groups]
    cnt = jnp.zeros((1, 128), F32)
    metas = []
    for x, tm in zip(xs_groups, tms):
        meta, cnt = _router(x, router_w, router_b, cnt, tm)
        metas.append(meta)
    n_total = sum(x.shape[0] for x in xs_groups)
    n_rows = -(-(2 * n_total + N_EXPERTS * (MOE_TILE - 1)) // MOE_TILE) * MOE_TILE
    n_tiles = n_rows // MOE_TILE
    counts = cnt[0, :N_EXPERTS].astype(jnp.int32)
    padded = (counts + MOE_TILE - 1) // MOE_TILE * MOE_TILE
    ends = jnp.cumsum(padded)
    starts = ends - padded
    tile_expert = jnp.minimum(
        jnp.sum((jnp.arange(n_tiles, dtype=jnp.int32)[:, None] * MOE_TILE >= ends[None, :]).astype(jnp.int32), axis=1),
        N_EXPERTS - 1).astype(jnp.int32)
    n_used = (ends[N_EXPERTS - 1] // MOE_TILE).astype(jnp.int32).reshape(1)
    xs = jnp.zeros((n_rows, D_MODEL), F32)
    dests = []
    for x, meta, tm in zip(xs_groups, metas, tms):
        e = meta[:, RT_E1:RT_E2 + 1].astype(jnp.int32)
        base = jnp.sum(jnp.where(e[:, :, None] == jnp.arange(N_EXPERTS, dtype=jnp.int32), starts, 0), axis=-1)
        d = base + meta[:, RT_R1:RT_R2 + 1].astype(jnp.int32)
        dests.append((d[:, 0], d[:, 1]))
        xs = _dispatch(x, d[:, 0], d[:, 1], xs, tm)
    ys = _experts(xs, tile_expert, n_used, wg, wu, wd)
    outs = []
    for x, meta, (d1, d2), tm in zip(xs_groups, metas, dests, tms):
        tmc = min(256, tm)
        outs.append(_combine(x, meta, d1, d2, ys, ln_g, ln_b, tmc))
    return outs


def _shift_rows(x, s, fill):
    rolled = pltpu.roll(x, s, 0)
    return jnp.where(_iota(x.shape, 0) >= s, rolled, fill)


def _causal_conv4(ext_ref, x, cw_ref, cb_ref, lc):
    ext_ref[8:8 + lc, :] = x
    out = cb_ref[...] + cw_ref[3:4, :] * x
    for j in range(3):
        out = out + cw_ref[j:j + 1, :] * ext_ref[5 + j:5 + j + lc, :]
    ext_ref[0:8, :] = ext_ref[lc:lc + 8, :]
    return out


RG_CHUNK = 256


def _rglru_kernel(u_ref, cw_ref, cb_ref, wa_ref, ba_ref, wx_ref, bx_ref, lam_ref,
                  y_ref, h_ref, ext_ref, hc_ref):
    lc = RG_CHUNK

    @pl.when(pl.program_id(1) == 0)
    def _():
        ext_ref[0:8, :] = jnp.zeros((8, BRANCH_W), F32)
        hc_ref[...] = jnp.zeros_like(hc_ref)

    xc = _causal_conv4(ext_ref, u_ref[0, :, 0:BRANCH_W], cw_ref, cb_ref, lc)
    r = _sigmoid(_dot(xc, wa_ref[...]) + ba_ref[...])
    i = _sigmoid(_dot(xc, wx_ref[...]) + bx_ref[...])
    log_a = -RG_C * r * _softplus(-lam_ref[...])
    a = jnp.exp(log_a)
    u = jnp.sqrt(1.0 - jnp.exp(2.0 * log_a)) * (i * xc)
    s = 1
    while s < lc:
        u = u + a * _shift_rows(u, s, 0.0)
        a = a * _shift_rows(a, s, 1.0)
        s *= 2
    h = u + a * hc_ref[...]
    hc_ref[...] = h[lc - 1:lc, :]
    h_ref[0] = h[lc - 1:lc, :]
    y_ref[0] = h * _gelu_tanh(u_ref[0, :, BRANCH_W:2 * BRANCH_W])


def _rglru_prompt(u, cw, cb, wa, ba, wx, bx, lam):
    bsz, seq, _ = u.shape
    lc = RG_CHUNK
    return pl.pallas_call(
        _rglru_kernel,
        out_shape=[jax.ShapeDtypeStruct((bsz, seq, BRANCH_W), F32),
                   jax.ShapeDtypeStruct((bsz, 1, BRANCH_W), F32)],
        grid=(bsz, seq // lc),
        in_specs=[pl.BlockSpec((1, lc, W_RG), lambda b, c: (b, c, 0)),
                  _const_spec((4, BRANCH_W)), _const_spec((1, BRANCH_W)),
                  _const_spec((BRANCH_W, BRANCH_W)), _const_spec((1, BRANCH_W)),
                  _const_spec((BRANCH_W, BRANCH_W)), _const_spec((1, BRANCH_W)),
                  _const_spec((1, BRANCH_W))],
        out_specs=[pl.BlockSpec((1, lc, BRANCH_W), lambda b, c: (b, c, 0)),
                   pl.BlockSpec((1, 1, BRANCH_W), lambda b, c: (b, 0, 0))],
        scratch_shapes=[pltpu.VMEM((lc + 8, BRANCH_W), F32), pltpu.VMEM((1, BRANCH_W), F32)],
        compiler_params=_cp("parallel", "arbitrary"),
        name="rglru",
    )(u, cw, cb, wa, ba, wx, bx, lam)


ML_CHUNK = 128


def _mlstm_kernel(u_ref, cw_ref, cb_ref, nw_ref, o_ref, c_ref, n_ref, m_ref, ext_ref):
    lc = ML_CHUNK
    W = BRANCH_W

    @pl.when(pl.program_id(1) == 0)
    def _():
        ext_ref[0:8, :] = jnp.zeros((8, 2 * W), F32)
        c_ref[...] = jnp.zeros_like(c_ref)
        n_ref[...] = jnp.zeros_like(n_ref)
        m_ref[...] = jnp.zeros_like(m_ref)

    qk = _silu(_causal_conv4(ext_ref, u_ref[0, :, 0:2 * W], cw_ref, cb_ref, lc))
    q = qk[:, 0:W]
    k = qk[:, W:2 * W] * (HEAD_W ** -0.5)
    v = u_ref[0, :, 2 * W:3 * W]
    og = u_ref[0, :, 3 * W:4 * W]
    ipre = u_ref[0, :, 4 * W:4 * W + 128]
    logf = _log_sigmoid(u_ref[0, :, 4 * W + 128:4 * W + 256])
    bcum = _dot_exact_lhs(_tril_ones(lc), logf)
    rowsrc = ipre - bcum
    rows_t = rowsrc.T
    lane = _iota((1, W), 1) // HEAD_W
    causal = _iota((lc, lc), 0) >= _iota((lc, lc), 1)
    cmat = c_ref[0]
    nrow = n_ref[0]
    h_acc = jnp.zeros((lc, W), F32)
    ws_full = jnp.zeros((lc, W), F32)
    keep_full = jnp.zeros((1, W), F32)
    kb = k.astype(BF16)
    vb = v.astype(BF16)
    for h in range(HEADS):
        hm = lane == h
        b_col = bcum[:, h:h + 1]
        log_d = jnp.where(causal, b_col + rows_t[h:h + 1, :], -jnp.inf)
        m_prev = m_ref[0, h:h + 1, 0:1]
        log_inter = b_col + m_prev
        m_t = jnp.maximum(log_inter, jnp.max(log_d, axis=-1, keepdims=True))
        dmat = jnp.exp(log_d - m_t)
        inter = jnp.exp(log_inter - m_t)
        qh = jnp.where(hm, q, 0.0)
        s = _dot_nt(qh, kb) * dmat
        num = _dot(s, vb) + inter * _dot_nt(qh, cmat)
        dot = (jnp.sum(s, axis=-1, keepdims=True)
               + inter * jnp.sum(qh * nrow, axis=-1, keepdims=True))
        den = jnp.maximum(jnp.abs(dot), jnp.exp(-m_t))
        h_acc = h_acc + jnp.where(hm, num / den, 0.0)
        m_new = m_t[lc - 1:lc, :]
        b_last = b_col[lc - 1:lc, :]
        ws_col = jnp.exp(rowsrc[:, h:h + 1] + (b_last - m_new))
        keep = jnp.exp(b_last + m_prev - m_new)
        ws_full = ws_full + jnp.where(hm, ws_col, 0.0)
        keep_full = keep_full + jnp.where(hm, keep, 0.0)
        m_ref[0, h:h + 1, :] = jnp.broadcast_to(m_new, (1, 128))
    blk = (_iota((W, W), 0) // HEAD_W) == (_iota((W, W), 1) // HEAD_W)
    vw = v * ws_full
    c_ref[0] = keep_full * cmat + jnp.where(blk, _dot_tn(vw, kb), 0.0)
    n_ref[0] = keep_full * nrow + jnp.sum(ws_full * k, axis=0, keepdims=True)
    pm = _head_mean_mat()
    mu = _dot_exact_rhs(h_acc, pm)
    hc = h_acc - mu
    var = _dot_exact_rhs(hc * hc, pm)
    o_ref[0] = hc * lax.rsqrt(var + 1e-6) * nw_ref[...] * _sigmoid(og)


def _mlstm_prompt(u, cw, cb, nw):
    bsz, seq, _ = u.shape
    lc = ML_CHUNK
    W = BRANCH_W
    return pl.pallas_call(
        _mlstm_kernel,
        out_shape=[jax.ShapeDtypeStruct((bsz, seq, W), F32),
                   jax.ShapeDtypeStruct((bsz, W, W), F32),
                   jax.ShapeDtypeStruct((bsz, 1, W), F32),
                   jax.ShapeDtypeStruct((bsz, 8, 128), F32)],
        grid=(bsz, seq // lc),
        in_specs=[pl.BlockSpec((1, lc, W_ML), lambda b, c: (b, c, 0)),
                  _const_spec((4, 2 * W)), _const_spec((1, 2 * W)), _const_spec((1, W))],
        out_specs=[pl.BlockSpec((1, lc, W), lambda b, c: (b, c, 0)),
                   pl.BlockSpec((1, W, W), lambda b, c: (b, 0, 0)),
                   pl.BlockSpec((1, 1, W), lambda b, c: (b, 0, 0)),
                   pl.BlockSpec((1, 8, 128), lambda b, c: (b, 0, 0))],
        scratch_shapes=[pltpu.VMEM((lc + 8, 2 * W), F32)],
        compiler_params=_cp("parallel", "arbitrary"),
        name="mlstm",
    )(u, cw, cb, nw)


GLA_CHUNK = 128
GLA_QK = HEADS * GLA_DK


def _gla_level_ref(b, m, lc):
    if 2 * m == lc:
        return jnp.broadcast_to(b[m - 1:m, :], b.shape)
    if m >= 4:
        nb = lc // (2 * m)
        mid = b.reshape(nb, 2 * m, b.shape[1])[:, m - 1:m, :]
        return jnp.broadcast_to(mid, (nb, 2 * m, b.shape[1])).reshape(b.shape)
    pos = _iota(b.shape, 0) % (2 * m)
    out = b
    for p in range(2 * m):
        sh = p - (m - 1)
        if sh != 0:
            out = jnp.where(pos == p, pltpu.roll(b, sh % lc, 0), out)
    return out


def _gla_kernel(u_ref, au_ref, ab_ref, nw_ref, o_ref, s_ref):
    lc = GLA_CHUNK
    W = BRANCH_W

    @pl.when(pl.program_id(1) == 0)
    def _():
        s_ref[...] = jnp.zeros_like(s_ref)

    q = u_ref[0, :, 0:GLA_QK] * (GLA_DK ** -0.5)
    k = u_ref[0, :, GLA_QK:2 * GLA_QK]
    v = u_ref[0, :, 2 * GLA_QK:2 * GLA_QK + W]
    g = u_ref[0, :, 2 * GLA_QK + W:2 * GLA_QK + 2 * W]
    ac = u_ref[0, :, 2 * GLA_QK + 2 * W:2 * GLA_QK + 2 * W + 128]
    la = _log_sigmoid(_dot(ac, au_ref[...]) + ab_ref[...]) / GLA_TAU
    b = _dot_exact_lhs(_tril_ones(lc), la)
    st = s_ref[0]
    o = _dot_nt(q * jnp.exp(b), st)
    gsum = ((_iota((GLA_QK, W), 0) // GLA_DK) == (_iota((GLA_QK, W), 1) // HEAD_W)).astype(BF16)
    o = o + jnp.dot((q * k).astype(BF16), gsum, preferred_element_type=F32) * v

    hq = _iota((lc, GLA_QK), 1) // GLA_DK
    row1 = _iota((lc, 1), 0)
    rt = _iota((HEADS * lc, lc), 0) % lc
    cs = _iota((HEADS * lc, lc), 1)
    kb_rows = None
    attn = jnp.zeros((HEADS * lc, lc), F32)
    m = lc // 2
    while m >= 1:
        ref = _gla_level_ref(b, m, lc)
        upper = (row1 % (2 * m)) >= m
        qt = q * jnp.exp(jnp.where(upper, b - ref, -jnp.inf))
        kt = k * jnp.exp(jnp.where(upper, -jnp.inf, ref - b))
        qs = jnp.concatenate([jnp.where(hq == h, qt, 0.0) for h in range(HEADS)], axis=0)
        part = _dot_nt(qs, kt)
        if 2 * m == lc:
            attn = attn + part
        else:
            attn = attn + jnp.where((rt // (2 * m)) == (cs // (2 * m)), part, 0.0)
        m //= 2
    av = _dot(attn, v)
    hv = _iota((lc, W), 1) // HEAD_W
    for h in range(HEADS):
        o = o + jnp.where(hv == h, av[h * lc:(h + 1) * lc, :], 0.0)
    b_last = b[lc - 1:lc, :]
    blk = (_iota((W, GLA_QK), 0) // HEAD_W) == (_iota((W, GLA_QK), 1) // GLA_DK)
    s_ref[0] = jnp.exp(b_last) * st + jnp.where(blk, _dot_tn(v, k * jnp.exp(b_last - b)), 0.0)
    ms = _dot_exact_rhs(o * o, _head_mean_mat())
    o_ref[0] = o * lax.rsqrt(ms + 1e-6) * nw_ref[...] * _silu(g)


def _gla_prompt(u, au, ab, nw):
    bsz, seq, _ = u.shape
    lc = GLA_CHUNK
    W = BRANCH_W
    return pl.pallas_call(
        _gla_kernel,
        out_shape=[jax.ShapeDtypeStruct((bsz, seq, W), F32),
                   jax.ShapeDtypeStruct((bsz, W, GLA_QK), F32)],
        grid=(bsz, seq // lc),
        in_specs=[pl.BlockSpec((1, lc, W_GL), lambda b, c: (b, c, 0)),
                  _const_spec((128, GLA_QK)), _const_spec((1, GLA_QK)), _const_spec((1, W))],
        out_specs=[pl.BlockSpec((1, lc, W), lambda b, c: (b, c, 0)),
                   pl.BlockSpec((1, W, GLA_QK), lambda b, c: (b, 0, 0))],
        compiler_params=_cp("parallel", "arbitrary"),
        name="gla",
    )(u, au, ab, nw)


RW_CHUNK = 256


def _head_sum_mat(width=BRANCH_W, group=HEAD_W):
    r = _iota((width, width), 0) // group
    c = _iota((width, width), 1) // group
    return jnp.where(r == c, 1.0, 0.0).astype(BF16)


def _rwkv_pointwise(um, p_ref, lora_ref):
    W = BRANCH_W
    r = um[:, 0:W]
    k = um[:, W:2 * W]
    v = um[:, 2 * W:3 * W]
    lo = um[:, 3 * W:3 * W + 128]
    w_log = -_softplus(-(p_ref[0:1, :] + _dot(jnp.tanh(lo), lora_ref[0]))) - 0.5
    decay = jnp.exp(-jnp.exp(w_log))
    a = _sigmoid(p_ref[1:2, :] + _dot(lo, lora_ref[1]))
    g = _dot(_sigmoid(lo), lora_ref[2])
    hs = _head_sum_mat()
    kk = k * p_ref[2:3, :]
    kk = kk / jnp.maximum(jnp.sqrt(_dot_exact_rhs(kk * kk, hs)), 1e-12)
    k2 = k * (1.0 + (a - 1.0) * p_ref[3:4, :])
    bonus = _dot_exact_rhs(r * k2 * p_ref[4:5, :], hs)
    return decay, kk, kk * a, k2, r, v, g, bonus * v


def _rwkv_prep_kernel(u_ref, mu_ref, p_ref, lora_ref, *rest):
    outs, ext_ref = rest[:8], rest[8]
    T = RW_CHUNK

    @pl.when(pl.program_id(1) == 0)
    def _():
        ext_ref[0:8, :] = jnp.zeros((8, RWKV_COLS), F32)

    u = u_ref[0]
    ext_ref[8:8 + T, :] = u
    prev = ext_ref[7:7 + T, :]
    ext_ref[0:8, :] = ext_ref[T:T + 8, :]
    um = u + mu_ref[...] * (prev - u)
    for o_ref, val in zip(outs, _rwkv_pointwise(um, p_ref, lora_ref)):
        o_ref[0] = val


def _rwkv_prep(u, mu, p, lora):
    bsz, seq, _ = u.shape
    T = RW_CHUNK
    W = BRANCH_W
    return pl.pallas_call(
        _rwkv_prep_kernel,
        out_shape=[jax.ShapeDtypeStruct((bsz, seq, W), F32)] * 8,
        grid=(bsz, seq // T),
        in_specs=[pl.BlockSpec((1, T, RWKV_COLS), lambda b, c: (b, c, 0)),
                  _const_spec((1, RWKV_COLS)), _const_spec((8, W)), _const_spec((3, 128, W))],
        out_specs=[pl.BlockSpec((1, T, W), lambda b, c: (b, c, 0))] * 8,
        scratch_shapes=[pltpu.VMEM((T + 8, RWKV_COLS), F32)],
        compiler_params=_cp("parallel", "arbitrary"),
        name="rwkv_prep",
    )(u, mu, p, lora)


def _delta_rule_step(s_ref, ni, w, kk, kka, k, r, v_row, y_store, s_out_ref=None):
    s_out = s_ref if s_out_ref is None else s_out_ref
    sa = [jnp.sum(s_ref[i] * kk, axis=0, keepdims=True) for i in range(ni)]
    for i in range(ni):
        sn = s_ref[i] * w - sa[i] * kka + v_row(i) * k
        s_out[i] = sn
        y_store(i, jnp.sum(sn * r, axis=0, keepdims=True))


RW_SCAN_T = 64


def _rwkv_scan_kernel(w_ref, kk_ref, kka_ref, k_ref, r_ref, v_ref, s0_ref, y_ref, s_out_ref, s_ref):
    ni = s_ref.shape[0]
    tb = pl.program_id(1)

    @pl.when(tb == 0)
    def _():
        s_ref[...] = s0_ref[...]

    def token(t, carry):
        def y_store(i, row):
            y_ref[t, i:i + 1, :] = row
        _delta_rule_step(s_ref, ni, w_ref[t], kk_ref[t], kka_ref[t], k_ref[t], r_ref[t],
                         lambda i: v_ref[t, i:i + 1, :], y_store)
        return carry

    lax.fori_loop(0, w_ref.shape[0], token, 0)

    @pl.when(tb == pl.num_programs(1) - 1)
    def _():
        s_out_ref[...] = s_ref[...]


def _rwkv_scan(jvecs, v, s0):
    groups, seq, ni, _ = v.shape
    T = min(RW_SCAN_T, seq)
    jspec = pl.BlockSpec((None, T, HEAD_W, 128), lambda g, t: (g, t, 0, 0))
    ispec = pl.BlockSpec((None, T, ni, 128), lambda g, t: (g, t, 0, 0))
    sspec = pl.BlockSpec((None, ni, HEAD_W, 128), lambda g, t: (g, 0, 0, 0))
    return pl.pallas_call(
        _rwkv_scan_kernel,
        out_shape=[jax.ShapeDtypeStruct(v.shape, F32), jax.ShapeDtypeStruct(s0.shape, F32)],
        grid=(groups, seq // T),
        in_specs=[jspec] * 5 + [ispec, sspec],
        out_specs=[ispec, sspec],
        scratch_shapes=[pltpu.VMEM((ni, HEAD_W, 128), F32)],
        compiler_params=_cp("parallel", "arbitrary"),
        name="rwkv_scan",
    )(*jvecs, v, s0)


def _rwkv_post_kernel(y_ref, g_ref, bv_ref, ln_ref, o_ref):
    y = y_ref[0]
    pm = _head_mean_mat()
    mu = _dot_exact_rhs(y, pm)
    yc = y - mu
    var = _dot_exact_rhs(yc * yc, pm)
    o = yc * lax.rsqrt(var + RWKV_GN_EPS) * ln_ref[0:1, :] + ln_ref[1:2, :]
    o_ref[0] = (o + bv_ref[0]) * g_ref[0]


def _rwkv_post(y, g, bv, ln):
    bsz, seq, W = y.shape
    T = RW_CHUNK
    tok = pl.BlockSpec((1, T, W), lambda b, c: (b, c, 0))
    return pl.pallas_call(
        _rwkv_post_kernel,
        out_shape=jax.ShapeDtypeStruct((bsz, seq, W), F32),
        grid=(bsz, seq // T),
        in_specs=[tok, tok, tok, _const_spec((2, W))],
        out_specs=tok,
        compiler_params=_cp("parallel", "parallel"),
        name="rwkv_post",
    )(y, g, bv, ln)


(PC_W0, PC_A0, PC_KK, PC_KA, PC_RK, PC_LNW, PC_LNB, PC_MLNW, PC_GLNW,
 PC_RCW, PC_RCB, PC_RBA, PC_RBX, PC_RLAM) = (0, 1, 2, 3, 4, 5, 6, 7, 8, 9, 13, 14, 15, 16)
PC_COLS = 17
DEC_B = 128


def _rowsum(x):
    return jnp.sum(x, axis=0, keepdims=True)


def _decode_kernel(
        ur_ref, uk_ref, uv_ref, ulo_ref, pr_ref, pk_ref, pv_ref, plo_ref,
        mur_ref, muk_ref, muv_ref, mulo_ref, lora_ref, srw_ref,
        mq_ref, mk_ref, mv_ref, mo_ref, mi_ref, mf_ref, bq_ref, bk_ref, cwq_ref, cwk_ref,
        c_ref, n_ref, m_ref,
        gq_ref, gk_ref, gv_ref, gg_ref, ga_ref, au_ref, ab_ref, gs_ref,
        rx_ref, ry_ref, rb_ref, wa_ref, wx_ref, h_ref,
        pc_ref,
        oa_ref, ob_ref, oc_ref, od_ref, srw_o, c_o, n_o, m_o, gs_o, h_o,
        y_scr):
    col = lambda j: pc_ref[:, j:j + 1]

    def shift(u_ref, p_ref, mu_ref):
        u = u_ref[...]
        return u + mu_ref[...] * (p_ref[...] - u)
    r = shift(ur_ref, pr_ref, mur_ref)
    k = shift(uk_ref, pk_ref, muk_ref)
    v = shift(uv_ref, pv_ref, muv_ref)
    lo = shift(ulo_ref, plo_ref, mulo_ref)
    w_log = -_softplus(-(col(PC_W0) + _dot(lora_ref[0], jnp.tanh(lo)))) - 0.5
    decay = jnp.exp(-jnp.exp(w_log))
    a = _sigmoid(col(PC_A0) + _dot(lora_ref[1], lo))
    g = _dot(lora_ref[2], _sigmoid(lo))
    kk = k * col(PC_KK)
    kk = kk / jnp.maximum(jnp.sqrt(_rowsum(kk * kk)), 1e-12)
    k2 = k * (1.0 + (a - 1.0) * col(PC_KA))
    bonus = _rowsum(r * k2 * col(PC_RK))

    def y_store(i, row):
        y_scr[i:i + 1, :] = row
    _delta_rule_step(srw_ref, HEAD_W, decay, kk, kk * a, k2, r, lambda i: v[i:i + 1, :], y_store, srw_o)
    y = y_scr[...]
    yc = y - jnp.mean(y, axis=0, keepdims=True)
    var = jnp.mean(yc * yc, axis=0, keepdims=True)
    o = yc * lax.rsqrt(var + RWKV_GN_EPS) * col(PC_LNW) + col(PC_LNB)
    oa_ref[...] = (o + bonus * v) * g

    def conv(u_ref, b_ref, cw_ref):
        out = cw_ref[:, 4:5] + cw_ref[:, 3:4] * u_ref[...]
        for j in range(3):
            out = out + cw_ref[:, j:j + 1] * b_ref[j]
        return out
    q = _silu(conv(mq_ref, bq_ref, cwq_ref))
    k = _silu(conv(mk_ref, bk_ref, cwk_ref)) * (HEAD_W ** -0.5)
    v = mv_ref[...]
    ipre = mi_ref[...]
    logf = _log_sigmoid(mf_ref[...])
    m_prev = m_ref[...]
    m_t = jnp.maximum(logf + m_prev, ipre)
    inter = jnp.exp(logf + m_prev - m_t)
    wsc = jnp.exp(ipre - m_t)
    s = _rowsum(q * k) * wsc
    n_prev = n_ref[...]
    den = jnp.maximum(jnp.abs(s + inter * _rowsum(n_prev * q)), jnp.exp(-m_t))
    for i in range(HEAD_W):
        ci = c_ref[i]
        vi = v[i:i + 1, :]
        y_scr[i:i + 1, :] = (s * vi + inter * _rowsum(ci * q)) / den
        c_o[i] = inter * ci + (wsc * vi) * k
    n_o[...] = inter * n_prev + wsc * k
    m_o[...] = m_t
    y = y_scr[...]
    yc = y - jnp.mean(y, axis=0, keepdims=True)
    var = jnp.mean(yc * yc, axis=0, keepdims=True)
    ob_ref[...] = yc * lax.rsqrt(var + 1e-6) * col(PC_MLNW) * _sigmoid(mo_ref[...])

    q = gq_ref[...] * (GLA_DK ** -0.5)
    k = gk_ref[...]
    v = gv_ref[...]
    eb = jnp.exp(_log_sigmoid(_dot(au_ref[...], ga_ref[...]) + ab_ref[...]) / GLA_TAU)
    attn = _rowsum(q * k)
    qe = q * eb
    for i in range(HEAD_W):
        si = gs_ref[i]
        vi = v[i:i + 1, :]
        y_scr[i:i + 1, :] = attn * vi + _rowsum(si * qe)
        gs_o[i] = eb * si + k * vi
    y = y_scr[...]
    ms = jnp.mean(y * y, axis=0, keepdims=True)
    oc_ref[...] = y * lax.rsqrt(ms + 1e-6) * col(PC_GLNW) * _silu(gg_ref[...])

    xc = col(PC_RCB) + col(PC_RCW + 3) * rx_ref[...]
    for j in range(3):
        xc = xc + col(PC_RCW + j) * rb_ref[j]
    rg = _sigmoid(_dot(wa_ref[...], xc) + col(PC_RBA))
    ig = _sigmoid(_dot(wx_ref[...], xc) + col(PC_RBX))
    log_a = -RG_C * rg * _softplus(-col(PC_RLAM))
    hn = jnp.exp(log_a) * h_ref[...] + jnp.sqrt(1.0 - jnp.exp(2.0 * log_a)) * (ig * xc)
    h_o[...] = hn
    od_ref[...] = hn * _gelu_tanh(ry_ref[...])


def _decode_mixers(ut_rw, prev_t, mu_c, lora_t, s_rw,
                   ut_ml, mconv_t, ml_cw, c_st, n_st, m_st,
                   ut_gl, au_t, ab_c, g_st,
                   ut_rg, rconv_t, wa_t, wx_t, h_st, pcols):
    nb = DEC_B
    H, HW = HEADS, HEAD_W

    def blk(arr, view, block, index):
        a = arr.reshape(view)
        nd = len(block)
        return a, pl.BlockSpec(block, index)

    ins = []
    v14 = (14, HW, nb)
    v7 = (7, 128, nb)
    for arr in (ut_rw, prev_t):
        ins.append(blk(arr, v14, (None, HW, nb), lambda h: (h, 0, 0)))
        ins.append(blk(arr, v14, (None, HW, nb), lambda h: (4 + h, 0, 0)))
        ins.append(blk(arr, v14, (None, HW, nb), lambda h: (8 + h, 0, 0)))
        ins.append(blk(arr, v7, (None, 128, nb), lambda h: (6, 0, 0)))
    ins.append(blk(mu_c, (14, HW, 1), (None, HW, 1), lambda h: (h, 0, 0)))
    ins.append(blk(mu_c, (14, HW, 1), (None, HW, 1), lambda h: (4 + h, 0, 0)))
    ins.append(blk(mu_c, (14, HW, 1), (None, HW, 1), lambda h: (8 + h, 0, 0)))
    ins.append(blk(mu_c, (7, 128, 1), (None, 128, 1), lambda h: (6, 0, 0)))
    ins.append(blk(lora_t, (3, BRANCH_W, 128), (3, HW, 128), lambda h: (0, h, 0)))
    st_spec = lambda: pl.BlockSpec((None, HW, HW, nb), lambda h: (h, 0, 0, 0))
    ins.append((s_rw, st_spec()))
    v20 = (20, HW, nb)
    for j in range(4):
        ins.append(blk(ut_ml, v20, (None, HW, nb), functools.partial(lambda h, j: (4 * j + h, 0, 0), j=j)))
    ins.append(blk(ut_ml[1024:1028], (4, 1, nb), (None, 1, nb), lambda h: (h, 0, 0)))
    ins.append(blk(ut_ml[1152:1156], (4, 1, nb), (None, 1, nb), lambda h: (h, 0, 0)))
    ins.append(blk(mconv_t, (3, 8, HW, nb), (3, None, HW, nb), lambda h: (0, h, 0, 0)))
    ins.append(blk(mconv_t, (3, 8, HW, nb), (3, None, HW, nb), lambda h: (0, 4 + h, 0, 0)))
    ins.append(blk(ml_cw, (8, HW, 5), (None, HW, 5), lambda h: (h, 0, 0)))
    ins.append(blk(ml_cw, (8, HW, 5), (None, HW, 5), lambda h: (4 + h, 0, 0)))
    ins.append((c_st, st_spec()))
    ins.append(blk(n_st, (H, HW, nb), (None, HW, nb), lambda h: (h, 0, 0)))
    ins.append(blk(m_st, (H, 1, nb), (None, 1, nb), lambda h: (h, 0, 0)))
    ins.append(blk(ut_gl, (28, GLA_DK, nb), (None, GLA_DK, nb), lambda h: (h, 0, 0)))
    ins.append(blk(ut_gl, (28, GLA_DK, nb), (None, GLA_DK, nb), lambda h: (4 + h, 0, 0)))
    ins.append(blk(ut_gl, v14, (None, HW, nb), lambda h: (4 + h, 0, 0)))
    ins.append(blk(ut_gl, v14, (None, HW, nb), lambda h: (8 + h, 0, 0)))
    ins.append(blk(ut_gl, v7, (None, 128, nb), lambda h: (6, 0, 0)))
    ins.append(blk(au_t, (H, GLA_DK, 128), (None, GLA_DK, 128), lambda h: (h, 0, 0)))
    ins.append(blk(ab_c, (H, GLA_DK, 1), (None, GLA_DK, 1), lambda h: (h, 0, 0)))
    ins.append((g_st, pl.BlockSpec((None, HW, GLA_DK, nb), lambda h: (h, 0, 0, 0))))
    ins.append(blk(ut_rg, (8, HW, nb), (None, HW, nb), lambda h: (h, 0, 0)))
    ins.append(blk(ut_rg, (8, HW, nb), (None, HW, nb), lambda h: (4 + h, 0, 0)))
    ins.append(blk(rconv_t, (3, H, HW, nb), (3, None, HW, nb), lambda h: (0, h, 0, 0)))
    ins.append((wa_t, pl.BlockSpec((None, HW, HW), lambda h: (h, 0, 0))))
    ins.append((wx_t, pl.BlockSpec((None, HW, HW), lambda h: (h, 0, 0))))
    ins.append(blk(h_st, (H, HW, nb), (None, HW, nb), lambda h: (h, 0, 0)))
    ins.append(blk(pcols, (H, HW, PC_COLS), (None, HW, PC_COLS), lambda h: (h, 0, 0)))

    vec = lambda: (jax.ShapeDtypeStruct((H, HW, nb), F32), pl.BlockSpec((None, HW, nb), lambda h: (h, 0, 0)))
    mat = lambda: (jax.ShapeDtypeStruct((H, HW, HW, nb), F32), st_spec())
    outs = [vec(), vec(), vec(), vec(), mat(), mat(), vec(),
            (jax.ShapeDtypeStruct((H, 1, nb), F32), pl.BlockSpec((None, 1, nb), lambda h: (h, 0, 0))),
            (jax.ShapeDtypeStruct((H, HW, GLA_DK, nb), F32),
             pl.BlockSpec((None, HW, GLA_DK, nb), lambda h: (h, 0, 0, 0))),
            vec()]
    return pl.pallas_call(
        _decode_kernel,
        out_shape=[o[0] for o in outs],
        grid=(H,),
        in_specs=[s for _, s in ins],
        out_specs=[o[1] for o in outs],
        scratch_shapes=[pltpu.VMEM((HW, nb), F32)],
        compiler_params=_cp("parallel"),
        name="decode_mixers",
    )(*[a for a, _ in ins])


def _pad_cols(a, width):
    return jnp.pad(a, ((0, 0), (0, width - a.shape[1])))


def _pad_rows(a, rows, at=0):
    return jnp.pad(a, ((at, rows - at - a.shape[0]), (0, 0)))


def _pack_layer(p, l):
    offs = np.concatenate([[0], np.cumsum(IN_SIZES)])
    w_in, b_in = p['w_in'][l], p['b_in'][l][None, :]
    seg = lambda a, i: a[:, int(offs[i]):int(offs[i + 1])]

    def regroup(a):
        return jnp.concatenate(
            [seg(a, 0), seg(a, 1), seg(a, 2), seg(a, 3), _pad_cols(seg(a, 4), 128), _pad_cols(seg(a, 5), 128),
             seg(a, 6), seg(a, 7), seg(a, 8), seg(a, 9), _pad_cols(seg(a, 10), 128), seg(a, 11), seg(a, 12)],
            axis=1)
    k = dict(
        w_pack=regroup(w_in).astype(BF16), b_pack=regroup(b_in),
        w_gate=seg(w_in, 13).astype(BF16), b_gate=seg(b_in, 13),
        w_branch=p['w_branch'][l].astype(BF16), w_out=p['w_out'][l].astype(BF16),
        ln1_g=p['ln1_g'][l][None], ln1_b=p['ln1_b'][l][None],
        ln2_g=p['ln2_g'][l][None], ln2_b=p['ln2_b'][l][None],
    )
    w_up, a_up, g_up = p['rwkv_w_up'][l], p['rwkv_a_up'][l], p['rwkv_g_up'][l]
    lora = jnp.stack([_pad_rows(w_up, 128, 0), _pad_rows(a_up, 128, 32), _pad_rows(g_up, 128, 64)])
    k['rw_lora'] = lora.astype(BF16)
    k['rw_lora_t'] = jnp.swapaxes(lora, 1, 2).astype(BF16)
    k['rw_mu'] = p['rwkv_mu'][l][None]
    rw_rows = [p['rwkv_w0'][l], p['rwkv_a0'][l], p['rwkv_k_k'][l], p['rwkv_k_a'][l],
               p['rwkv_r_k'][l].reshape(BRANCH_W)]
    k['rw_p'] = jnp.stack(rw_rows + [jnp.zeros((BRANCH_W,), F32)] * 3)
    k['rw_ln'] = jnp.stack([p['rwkv_ln_w'][l], p['rwkv_ln_b'][l]])
    k['ml_cw'], k['ml_cb'] = p['mlstm_conv_w'][l], p['mlstm_conv_b'][l][None]
    k['ml_nw'] = p['mlstm_norm_w'][l][None]
    k['gl_au'] = _pad_rows(p['gla_alpha_up'][l], 128).astype(BF16)
    k['gl_ab'] = p['gla_alpha_b'][l][None]
    k['gl_nw'] = p['gla_norm_w'][l][None]
    wa, wx = p['rglru_wa'][l], p['rglru_wx'][l]
    eye = jnp.eye(HEADS, dtype=F32)
    bd = lambda w: jnp.einsum('gh,gij->gihj', eye, w).reshape(BRANCH_W, BRANCH_W).astype(BF16)
    k['rg_cw'], k['rg_cb'] = p['rglru_conv_w'][l], p['rglru_conv_b'][l][None]
    k['rg_wa'], k['rg_wx'] = bd(wa), bd(wx)
    k['rg_ba'], k['rg_bx'] = p['rglru_ba'][l][None], p['rglru_bx'][l][None]
    k['rg_lam'] = p['rglru_lambda'][l][None]
    k['rg_wa_t'] = jnp.swapaxes(wa, 1, 2).astype(BF16)
    k['rg_wx_t'] = jnp.swapaxes(wx, 1, 2).astype(BF16)
    cols = rw_rows + [p['rwkv_ln_w'][l], p['rwkv_ln_b'][l], p['mlstm_norm_w'][l], p['gla_norm_w'][l]]
    cols += [p['rglru_conv_w'][l][j] for j in range(4)]
    cols += [p['rglru_conv_b'][l], p['rglru_ba'][l], p['rglru_bx'][l], p['rglru_lambda'][l]]
    k['pcols'] = jnp.stack(cols, axis=1)
    k['ml_cw_t'] = jnp.concatenate([p['mlstm_conv_w'][l].T, p['mlstm_conv_b'][l][:, None]], axis=1)
    return k


def _diag_blocks(a, rb, cb):
    return jnp.stack([a[:, h * rb:(h + 1) * rb, h * cb:(h + 1) * cb] for h in range(HEADS)], axis=1)


def _prompt_mixers(x, k):
    B, L, _ = x.shape
    W = BRANCH_W
    u_rw, u_ml, u_gl, u_rg = _inproj(x.reshape(B * L, D_MODEL), k['w_pack'], k['b_pack'], 512)
    u_rw, u_ml, u_gl, u_rg = (u.reshape(B, L, -1) for u in (u_rw, u_ml, u_gl, u_rg))
    nch = B * HEADS
    rep = 128 // nch
    dec, kk, kka, k2, r, v, g, bv = _rwkv_prep(u_rw, k['rw_mu'], k['rw_p'], k['rw_lora'])

    def key_major(a):
        a = a.reshape(B, L, HEADS, HEAD_W).transpose(1, 3, 0, 2)[:, :, None]
        return jnp.broadcast_to(a, (L, HEAD_W, rep, B, HEADS)).reshape(1, L, HEAD_W, 128)
    vi = v.reshape(B, L, HEADS, HEAD_W // rep, rep).transpose(1, 3, 4, 0, 2).reshape(1, L, HEAD_W // rep, 128)
    s0 = jnp.zeros((1, HEAD_W // rep, HEAD_W, 128), F32)
    y, s1 = _rwkv_scan([key_major(a) for a in (dec, kk, kka, k2, r)], vi, s0)
    y = y.reshape(L, HEAD_W // rep, rep, B, HEADS).transpose(3, 0, 4, 1, 2).reshape(B, L, W)
    s_rw = s1.reshape(HEAD_W // rep, HEAD_W, rep, B, HEADS).transpose(3, 4, 0, 2, 1).reshape(B, HEADS, HEAD_W, HEAD_W)
    o_a = _rwkv_post(y, g, bv, k['rw_ln'])
    o_b, c_bd, n_row, m_row = _mlstm_prompt(u_ml, k['ml_cw'], k['ml_cb'], k['ml_nw'])
    o_c, s_bd = _gla_prompt(u_gl, k['gl_au'], k['gl_ab'], k['gl_nw'])
    o_d, h1 = _rglru_prompt(u_rg, k['rg_cw'], k['rg_cb'], k['rg_wa'], k['rg_ba'], k['rg_wx'], k['rg_bx'],
                            k['rg_lam'])
    states = (u_rw[:, L - 1], s_rw, u_ml[:, L - 3:, :2 * W], _diag_blocks(c_bd, HEAD_W, HEAD_W),
              n_row.reshape(B, HEADS, HEAD_W), m_row[:, :HEADS, 0],
              jnp.swapaxes(_diag_blocks(s_bd, HEAD_W, GLA_DK), 2, 3), u_rg[:, L - 3:, :W], h1[:, 0])
    return [o.reshape(B * L, W) for o in (o_a, o_b, o_c, o_d)], states


def _sample_mixers(x, st, k):
    W = BRANCH_W
    sh0, S0, mconv0, C0, n0, m0, gS0, rconv0, h0 = st
    u_rw, u_ml, u_gl, u_rg = _inproj(x, k['w_pack'], k['b_pack'], DEC_B)
    outs = _decode_mixers(
        u_rw.T, sh0.T, k['rw_mu'].T, k['rw_lora_t'], S0.transpose(1, 2, 3, 0),
        u_ml.T, mconv0.transpose(1, 2, 0), k['ml_cw_t'], C0.transpose(1, 2, 3, 0), n0.transpose(1, 2, 0), m0.T,
        u_gl.T, k['gl_au'].T, k['gl_ab'].T, gS0.transpose(1, 3, 2, 0),
        u_rg.T, rconv0.transpose(1, 2, 0), k['rg_wa_t'], k['rg_wx_t'], h0.T, k['pcols'])
    oa, ob, oc, od, s_rw, c_st, n_st, m_st, g_st, h_st = outs
    branches = [o.reshape(W, DEC_B).T for o in (oa, ob, oc, od)]
    states = (u_rw, s_rw.transpose(3, 0, 1, 2),
              jnp.concatenate([mconv0[:, 1:], u_ml[:, None, :2 * W]], axis=1),
              c_st.transpose(3, 0, 1, 2), n_st.transpose(2, 0, 1), m_st[:, 0, :].T,
              g_st.transpose(3, 0, 2, 1),
              jnp.concatenate([rconv0[:, 1:], u_rg[:, None, :W]], axis=1), h_st.reshape(W, DEC_B).T)
    return branches, states


def _trunk(xp, bl, xs, sample_states, packs, moe):
    tms = (512, DEC_B)
    new_p, new_s = [], []
    for l in range(DEPTH):
        k = packs[l]
        br_p, st_p = _prompt_mixers(xp.reshape(bl[0], bl[1], D_MODEL), k)
        br_s, st_s = _sample_mixers(xs, tuple(s[l] for s in sample_states), k)
        new_p.append(st_p)
        new_s.append(st_s)
        xp, xs = (_merge(x, br, k['w_gate'], k['b_gate'], k['w_branch'], k['w_out'], k['ln1_g'], k['ln1_b'], tm)
                  for x, br, tm in ((xp, br_p, tms[0]), (xs, br_s, tms[1])))
        j = l // 2
        if l % 2 == 0:
            xp, xs = (_ffn(x, moe['ffn_wg'][j], moe['ffn_wu'][j], moe['ffn_wd'][j], k['ln2_g'], k['ln2_b'], tm)
                      for x, tm in ((xp, tms[0]), (xs, tms[1])))
        else:
            xp, xs = _moe_layer([xp, xs], moe['router'][j], moe['router_b'][j], moe['moe_wg'][j],
                                moe['moe_wu'][j], moe['moe_wd'][j], k['ln2_g'], k['ln2_b'])
    stack = lambda sts: [jnp.stack([st[i] for st in sts], axis=0) for i in range(9)]
    return xp, xs, stack(new_p), stack(new_s)


def kernel(x_prompt, x_sample, state_rwkv_shift, state_rwkv_S, state_mlstm_conv, state_mlstm_C,
           state_mlstm_n, state_mlstm_m, state_gla_S, state_rglru_conv, state_rglru_h,
           w_in, b_in, rwkv_mu, rwkv_w0, rwkv_w_up, rwkv_a0, rwkv_a_up, rwkv_g_up, rwkv_k_k,
           rwkv_k_a, rwkv_r_k, rwkv_ln_w, rwkv_ln_b, mlstm_conv_w, mlstm_conv_b, mlstm_norm_w,
           gla_alpha_up, gla_alpha_b, gla_norm_w, rglru_conv_w, rglru_conv_b, rglru_wa, rglru_ba,
           rglru_wx, rglru_bx, rglru_lambda, w_branch, w_out, ln1_g, ln1_b, ffn_wg, ffn_wu, ffn_wd,
           moe_router, moe_router_b, moe_wg, moe_wu, moe_wd, ln2_g, ln2_b):
    p = dict(w_in=w_in, b_in=b_in, rwkv_mu=rwkv_mu, rwkv_w0=rwkv_w0, rwkv_w_up=rwkv_w_up,
             rwkv_a0=rwkv_a0, rwkv_a_up=rwkv_a_up, rwkv_g_up=rwkv_g_up, rwkv_k_k=rwkv_k_k,
             rwkv_k_a=rwkv_k_a, rwkv_r_k=rwkv_r_k, rwkv_ln_w=rwkv_ln_w, rwkv_ln_b=rwkv_ln_b,
             mlstm_conv_w=mlstm_conv_w, mlstm_conv_b=mlstm_conv_b, mlstm_norm_w=mlstm_norm_w,
             gla_alpha_up=gla_alpha_up, gla_alpha_b=gla_alpha_b, gla_norm_w=gla_norm_w,
             rglru_conv_w=rglru_conv_w, rglru_conv_b=rglru_conv_b, rglru_wa=rglru_wa,
             rglru_ba=rglru_ba, rglru_wx=rglru_wx, rglru_bx=rglru_bx, rglru_lambda=rglru_lambda,
             w_branch=w_branch, w_out=w_out, ln1_g=ln1_g, ln1_b=ln1_b, ln2_g=ln2_g, ln2_b=ln2_b)
    packs = [_pack_layer(p, l) for l in range(DEPTH)]
    moe = dict(ffn_wg=ffn_wg.astype(BF16), ffn_wu=ffn_wu.astype(BF16), ffn_wd=ffn_wd.astype(BF16),
               router=jnp.pad(moe_router, ((0, 0), (0, 0), (0, 128 - N_EXPERTS))),
               router_b=jnp.pad(moe_router_b, ((0, 0), (0, 128 - N_EXPERTS)))[:, None, :],
               moe_wg=moe_wg.astype(BF16), moe_wu=moe_wu.astype(BF16), moe_wd=moe_wd.astype(BF16))
    B, L, _ = x_prompt.shape
    sample_states = (state_rwkv_shift, state_rwkv_S, state_mlstm_conv, state_mlstm_C, state_mlstm_n,
                     state_mlstm_m, state_gla_S, state_rglru_conv, state_rglru_h)
    nb, ls, _ = x_sample.shape
    y_p, y_s, ps, ss = _trunk(x_prompt.reshape(B * L, D_MODEL), (B, L), x_sample.reshape(nb * ls, D_MODEL),
                              sample_states, packs, moe)
    return (y_p.reshape(B, L, D_MODEL), y_s.reshape(nb, ls, D_MODEL), *ps, *ss)
```

```python
import functools

import jax
import jax.numpy as jnp
import numpy as np
from jax import lax
from jax.experimental import pallas as pl
from jax.experimental.pallas import tpu as pltpu

F32 = jnp.float32
BF16 = jnp.bfloat16

D_MODEL = 1024
DEPTH = 2
N_BRANCH = 4
BRANCH_W = 256
HEADS = 4
HEAD_W = 64
RWKV_COLS = 896
RWKV_GN_EPS = 64e-5
GLA_DK = 32
GLA_LORA = 16
GLA_TAU = 16.0
RG_C = 8.0
D_FF = 2816
N_EXPERTS = 8
DN_ALPHA = (2 * DEPTH) ** 0.25
LN_EPS = 1e-5

IN_SIZES = (RWKV_COLS, 512, 256, 256, 4, 4, 128, 128, 256, 256, 16, 256, 256, 4096)

W_RW = 896
W_ML = 1280
W_GL = 896
W_RG = 512
W_PACK = W_RW + W_ML + W_GL + W_RG

VMEM_LIMIT = 56 * 1024 * 1024


def _cp(*sem):
    return pltpu.CompilerParams(dimension_semantics=sem, vmem_limit_bytes=VMEM_LIMIT)


def _dot(a, b):
    return jnp.dot(a.astype(BF16), b.astype(BF16), preferred_element_type=F32)


def _dot_nt(a, b):
    return lax.dot_general(a.astype(BF16), b.astype(BF16), (((1,), (1,)), ((), ())),
                           preferred_element_type=F32)


def _dot_tn(a, b):
    return lax.dot_general(a.astype(BF16), b.astype(BF16), (((0,), (0,)), ((), ())),
                           preferred_element_type=F32)


def _split(x):
    hi = x.astype(BF16)
    lo = (x - hi.astype(F32)).astype(BF16)
    return hi, lo


def _dot_exact_lhs(a, x):
    hi, lo = _split(x)
    a = a.astype(BF16)
    return (jnp.dot(a, hi, preferred_element_type=F32) + jnp.dot(a, lo, preferred_element_type=F32))


def _dot_exact_rhs(x, a):
    hi, lo = _split(x)
    a = a.astype(BF16)
    return (jnp.dot(hi, a, preferred_element_type=F32) + jnp.dot(lo, a, preferred_element_type=F32))


def _dot3(x, w):
    xh, xl = _split(x)
    wh, wl = _split(w)
    return (jnp.dot(xh, wh, preferred_element_type=F32) + jnp.dot(xl, wh, preferred_element_type=F32)
            + jnp.dot(xh, wl, preferred_element_type=F32))


def _sigmoid(x):
    return 1.0 / (1.0 + jnp.exp(-x))


def _softplus(x):
    return jnp.maximum(x, 0.0) + jnp.log(1.0 + jnp.exp(-jnp.abs(x)))


def _log_sigmoid(x):
    return -_softplus(-x)


def _silu(x):
    return x * _sigmoid(x)


def _gelu_tanh(x):
    c = np.float32(np.sqrt(2.0 / np.pi))
    return 0.5 * x * (1.0 + jnp.tanh(c * (x + 0.044715 * (x * x * x))))


def _layer_norm(y, g, b):
    mu = jnp.mean(y, axis=-1, keepdims=True)
    yc = y - mu
    var = jnp.mean(yc * yc, axis=-1, keepdims=True)
    return yc * lax.rsqrt(var + LN_EPS) * g + b


def _iota(shape, dim):
    return lax.broadcasted_iota(jnp.int32, shape, dim)


def _head_mean_mat(width=BRANCH_W, group=HEAD_W):
    r = _iota((width, width), 0) // group
    c = _iota((width, width), 1) // group
    return jnp.where(r == c, 1.0 / group, 0.0).astype(BF16)


def _tril_ones(n):
    return jnp.where(_iota((n, n), 0) >= _iota((n, n), 1), 1.0, 0.0).astype(BF16)


def _const_spec(shape):
    nd = len(shape)
    return pl.BlockSpec(shape, lambda *_: (0,) * nd)


def _inproj_kernel(x_ref, w_ref, b_ref, o_rw, o_ml, o_gl, o_rg):
    x = x_ref[...].astype(BF16)
    off = 0
    for o_ref in (o_rw, o_ml, o_gl, o_rg):
        n = o_ref.shape[-1]
        o_ref[...] = (jnp.dot(x, w_ref[:, off:off + n], preferred_element_type=F32)
                      + b_ref[:, off:off + n])
        off += n


def _inproj(x, w, b, tm):
    n = x.shape[0]
    widths = (W_RW, W_ML, W_GL, W_RG)
    return pl.pallas_call(
        _inproj_kernel,
        out_shape=[jax.ShapeDtypeStruct((n, wd), F32) for wd in widths],
        grid=(n // tm,),
        in_specs=[pl.BlockSpec((tm, D_MODEL), lambda i: (i, 0)),
                  _const_spec((D_MODEL, W_PACK)), _const_spec((1, W_PACK))],
        out_specs=[pl.BlockSpec((tm, wd), lambda i: (i, 0)) for wd in widths],
        compiler_params=_cp("parallel"),
        name="inproj",
    )(x, w, b)


def _merge_kernel(x_ref, oa, ob, oc, od, wg_ref, bg_ref, wb_ref, wo_ref, g_ref, b_ref, out_ref):
    x = x_ref[...]
    xb = x.astype(BF16)
    merged = None
    for g, o_ref in enumerate((oa, ob, oc, od)):
        sl = slice(g * D_MODEL, (g + 1) * D_MODEL)
        gate = _sigmoid(jnp.dot(xb, wg_ref[:, sl], preferred_element_type=F32) + bg_ref[:, sl])
        up = jnp.dot(o_ref[...].astype(BF16), wb_ref[g], preferred_element_type=F32)
        merged = gate * up if merged is None else merged + gate * up
    out = jnp.dot(merged.astype(BF16), wo_ref[...], preferred_element_type=F32)
    out_ref[...] = _layer_norm(DN_ALPHA * x + out, g_ref[...], b_ref[...])


def _merge(x, branches, wg, bg, wb, wo, ln_g, ln_b, tm):
    n = x.shape[0]
    tok = lambda wd: pl.BlockSpec((tm, wd), lambda i: (i, 0))
    return pl.pallas_call(
        _merge_kernel,
        out_shape=jax.ShapeDtypeStruct((n, D_MODEL), F32),
        grid=(n // tm,),
        in_specs=[tok(D_MODEL)] + [tok(BRANCH_W)] * 4 + [
            _const_spec((D_MODEL, N_BRANCH * D_MODEL)), _const_spec((1, N_BRANCH * D_MODEL)),
            _const_spec((N_BRANCH, BRANCH_W, D_MODEL)), _const_spec((D_MODEL, D_MODEL)),
            _const_spec((1, D_MODEL)), _const_spec((1, D_MODEL))],
        out_specs=tok(D_MODEL),
        compiler_params=_cp("parallel"),
        name="merge",
    )(x, *branches, wg, bg, wb, wo, ln_g, ln_b)


FF_CHUNK = D_FF // 2


def _ffn_kernel(x_ref, wg_ref, wu_ref, wd_ref, g_ref, b_ref, out_ref):
    x = x_ref[...]
    xb = x.astype(BF16)
    acc = None
    for c in range(D_FF // FF_CHUNK):
        sl = slice(c * FF_CHUNK, (c + 1) * FF_CHUNK)
        h = (_silu(jnp.dot(xb, wg_ref[:, sl], preferred_element_type=F32))
             * jnp.dot(xb, wu_ref[:, sl], preferred_element_type=F32))
        part = jnp.dot(h.astype(BF16), wd_ref[sl, :], preferred_element_type=F32)
        acc = part if acc is None else acc + part
    out_ref[...] = _layer_norm(DN_ALPHA * x + acc, g_ref[...], b_ref[...])


def _ffn(x, wg, wu, wd, ln_g, ln_b, tm):
    n = x.shape[0]
    tok = pl.BlockSpec((tm, D_MODEL), lambda i: (i, 0))
    return pl.pallas_call(
        _ffn_kernel,
        out_shape=jax.ShapeDtypeStruct((n, D_MODEL), F32),
        grid=(n // tm,),
        in_specs=[tok, _const_spec((D_MODEL, D_FF)), _const_spec((D_MODEL, D_FF)),
                  _const_spec((D_FF, D_MODEL)), _const_spec((1, D_MODEL)), _const_spec((1, D_MODEL))],
        out_specs=tok,
        compiler_params=_cp("parallel"),
        name="ffn",
    )(x, wg, wu, wd, ln_g, ln_b)


(RT_E1, RT_E2, RT_R1, RT_R2, RT_W1, RT_W2) = range(6)
MOE_TILE = 512


def _router_kernel(x_ref, w_ref, b_ref, cin_ref, meta_ref, cnt_ref, carry_ref):
    @pl.when(pl.program_id(0) == 0)
    def _():
        carry_ref[...] = cin_ref[...]

    logits = _dot3(x_ref[...], w_ref[...]) + b_ref[...]
    tm = logits.shape[0]
    lane = _iota(logits.shape, 1)
    neg = jnp.float32(-jnp.inf)
    logits = jnp.where(lane < N_EXPERTS, logits, neg)
    m1 = jnp.max(logits, axis=-1, keepdims=True)
    i1 = jnp.min(jnp.where(logits == m1, lane, 128), axis=-1, keepdims=True)
    rest = jnp.where(lane == i1, neg, logits)
    m2 = jnp.max(rest, axis=-1, keepdims=True)
    i2 = jnp.min(jnp.where(rest == m2, lane, 128), axis=-1, keepdims=True)
    e2 = jnp.exp(m2 - m1)
    w1 = 1.0 / (1.0 + e2)
    w2 = e2 / (1.0 + e2)
    oh1 = lane == i1
    oh2 = lane == i2
    picks = jnp.where(oh1 | oh2, 1.0, 0.0)
    below = jnp.where(_iota((tm, tm), 0) > _iota((tm, tm), 1), 1.0, 0.0).astype(BF16)
    base = carry_ref[...] + jnp.dot(below, picks.astype(BF16), preferred_element_type=F32)
    r1 = jnp.sum(jnp.where(oh1, base, 0.0), axis=-1, keepdims=True)
    r2 = jnp.sum(jnp.where(oh2, base, 0.0), axis=-1, keepdims=True)
    carry_ref[...] = carry_ref[...] + jnp.sum(picks, axis=0, keepdims=True)
    cnt_ref[...] = carry_ref[...]
    meta = jnp.zeros(logits.shape, F32)
    for ln, val in ((RT_E1, i1.astype(F32)), (RT_E2, i2.astype(F32)), (RT_R1, r1), (RT_R2, r2),
                    (RT_W1, w1), (RT_W2, w2)):
        meta = jnp.where(lane == ln, val, meta)
    meta_ref[...] = meta


def _router(x, w, b, cnt_in, tm):
    n = x.shape[0]
    return pl.pallas_call(
        _router_kernel,
        out_shape=[jax.ShapeDtypeStruct((n, 128), F32), jax.ShapeDtypeStruct((1, 128), F32)],
        grid=(n // tm,),
        in_specs=[pl.BlockSpec((tm, D_MODEL), lambda i: (i, 0)),
                  _const_spec((D_MODEL, 128)), _const_spec((1, 128)), _const_spec((1, 128))],
        out_specs=[pl.BlockSpec((tm, 128), lambda i: (i, 0)), _const_spec((1, 128))],
        scratch_shapes=[pltpu.VMEM((1, 128), F32)],
        compiler_params=_cp("arbitrary"),
        name="router",
    )(x, w, b, cnt_in)


def _row_copies(n_rows, start_one, wait_shape_src, wait_shape_dst, sem, n_streams):
    def body(r, carry):
        start_one(r)
        return carry
    lax.fori_loop(0, n_rows, body, 0, unroll=8)
    for _ in range(n_streams):
        pltpu.make_async_copy(wait_shape_src, wait_shape_dst, sem).wait()


def _dispatch_kernel(d1_ref, d2_ref, x_ref, xs_in_ref, xs_ref, sem):
    del xs_in_ref
    tm = x_ref.shape[0]

    def start_one(r):
        src = x_ref.at[pl.ds(r, 1)]
        pltpu.make_async_copy(src, xs_ref.at[pl.ds(d1_ref[0, 0, r], 1)], sem).start(priority=0)
        pltpu.make_async_copy(src, xs_ref.at[pl.ds(d2_ref[0, 0, r], 1)], sem).start(priority=1)
    _row_copies(tm, start_one, x_ref, xs_ref.at[pl.ds(0, tm)], sem, 2)


def _dispatch(x, d1, d2, xs, tm):
    n = x.shape[0]
    idx = lambda a: a.reshape(n // tm, 1, tm)
    ispec = pl.BlockSpec((1, 1, tm), lambda i: (i, 0, 0), memory_space=pltpu.SMEM)
    return pl.pallas_call(
        _dispatch_kernel,
        out_shape=jax.ShapeDtypeStruct(xs.shape, F32),
        grid=(n // tm,),
        in_specs=[ispec, ispec, pl.BlockSpec((tm, D_MODEL), lambda i: (i, 0)),
                  pl.BlockSpec(memory_space=pl.ANY)],
        out_specs=pl.BlockSpec(memory_space=pl.ANY),
        scratch_shapes=[pltpu.SemaphoreType.DMA],
        input_output_aliases={3: 0},
        compiler_params=_cp("arbitrary"),
        name="moe_dispatch",
    )(idx(d1), idx(d2), x, xs)


def _experts_kernel(te_ref, nu_ref, xs_ref, wg_ref, wu_ref, wd_ref, ys_ref):
    del te_ref

    @pl.when(pl.program_id(0) < nu_ref[0])
    def _():
        xb = xs_ref[...].astype(BF16)
        acc = None
        for c in range(D_FF // FF_CHUNK):
            sl = slice(c * FF_CHUNK, (c + 1) * FF_CHUNK)
            h = (_silu(jnp.dot(xb, wg_ref[:, sl], preferred_element_type=F32))
                 * jnp.dot(xb, wu_ref[:, sl], preferred_element_type=F32))
            part = jnp.dot(h.astype(BF16), wd_ref[sl, :], preferred_element_type=F32)
            acc = part if acc is None else acc + part
        ys_ref[...] = acc

    @pl.when(pl.program_id(0) >= nu_ref[0])
    def _():
        ys_ref[...] = jnp.zeros_like(ys_ref)


def _experts(xs, tile_expert, n_used, wg, wu, wd):
    n_tiles = xs.shape[0] // MOE_TILE
    row = pl.BlockSpec((MOE_TILE, D_MODEL), lambda i, te, nu: (i, 0))
    return pl.pallas_call(
        _experts_kernel,
        out_shape=jax.ShapeDtypeStruct(xs.shape, F32),
        grid_spec=pltpu.PrefetchScalarGridSpec(
            num_scalar_prefetch=2, grid=(n_tiles,),
            in_specs=[row,
                      pl.BlockSpec((None, D_MODEL, D_FF), lambda i, te, nu: (te[i], 0, 0)),
                      pl.BlockSpec((None, D_MODEL, D_FF), lambda i, te, nu: (te[i], 0, 0)),
                      pl.BlockSpec((None, D_FF, D_MODEL), lambda i, te, nu: (te[i], 0, 0))],
            out_specs=row),
        compiler_params=_cp("arbitrary"),
        name="moe_experts",
    )(tile_expert, n_used, xs, wg, wu, wd)


def _combine_kernel(d1_ref, d2_ref, x_ref, meta_ref, ys_ref, g_ref, b_ref, out_ref, y1_ref, y2_ref, sem):
    tm = x_ref.shape[0]

    def start_one(r):
        pltpu.make_async_copy(ys_ref.at[pl.ds(d1_ref[0, 0, r], 1)], y1_ref.at[pl.ds(r, 1)], sem).start(priority=0)
        pltpu.make_async_copy(ys_ref.at[pl.ds(d2_ref[0, 0, r], 1)], y2_ref.at[pl.ds(r, 1)], sem).start(priority=1)
    _row_copies(tm, start_one, ys_ref.at[pl.ds(0, tm)], y1_ref, sem, 2)
    meta = meta_ref[...]
    f = meta[:, RT_W1:RT_W1 + 1] * y1_ref[...] + meta[:, RT_W2:RT_W2 + 1] * y2_ref[...]
    out_ref[...] = _layer_norm(DN_ALPHA * x_ref[...] + f, g_ref[...], b_ref[...])


def _combine(x, meta, d1, d2, ys, ln_g, ln_b, tm):
    n = x.shape[0]
    idx = lambda a: a.reshape(n // tm, 1, tm)
    ispec = pl.BlockSpec((1, 1, tm), lambda i: (i, 0, 0), memory_space=pltpu.SMEM)
    tok = lambda wd_: pl.BlockSpec((tm, wd_), lambda i: (i, 0))
    return pl.pallas_call(
        _combine_kernel,
        out_shape=jax.ShapeDtypeStruct((n, D_MODEL), F32),
        grid=(n // tm,),
        in_specs=[ispec, ispec, tok(D_MODEL), tok(128), pl.BlockSpec(memory_space=pl.ANY),
                  _const_spec((1, D_MODEL)), _const_spec((1, D_MODEL))],
        out_specs=tok(D_MODEL),
        scratch_shapes=[pltpu.VMEM((tm, D_MODEL), F32), pltpu.VMEM((tm, D_MODEL), F32),
                        pltpu.SemaphoreType.DMA],
        compiler_params=_cp("arbitrary"),
        name="moe_combine",
    )(idx(d1), idx(d2), x, meta, ys, ln_g, ln_b)


def _moe_layer(xs_groups, router_w, router_b, wg, wu, wd, ln_g, ln_b):
    tms = [min(512, x.shape[0]) for x in xs_groups]
    cnt = jnp.zeros((1, 128), F32)
    metas = []
    for x, tm in zip(xs_groups, tms):
        meta, cnt = _router(x, router_w, router_b, cnt, tm)
        metas.append(meta)
    n_total = sum(x.shape[0] for x in xs_groups)
    n_rows = -(-(2 * n_total + N_EXPERTS * (MOE_TILE - 1)) // MOE_TILE) * MOE_TILE
    n_tiles = n_rows // MOE_TILE
    counts = cnt[0, :N_EXPERTS].astype(jnp.int32)
    padded = (counts + MOE_TILE - 1) // MOE_TILE * MOE_TILE
    ends = jnp.cumsum(padded)
    starts = ends - padded
    tile_expert = jnp.minimum(
        jnp.sum((jnp.arange(n_tiles, dtype=jnp.int32)[:, None] * MOE_TILE >= ends[None, :]).astype(jnp.int32), axis=1),
        N_EXPERTS - 1).astype(jnp.int32)
    n_used = (ends[N_EXPERTS - 1] // MOE_TILE).astype(jnp.int32).reshape(1)
    xs = jnp.zeros((n_rows, D_MODEL), F32)
    dests = []
    for x, meta, tm in zip(xs_groups, metas, tms):
        e = meta[:, RT_E1:RT_E2 + 1].astype(jnp.int32)
        base = jnp.sum(jnp.where(e[:, :, None] == jnp.arange(N_EXPERTS, dtype=jnp.int32), starts, 0), axis=-1)
        d = base + meta[:, RT_R1:RT_R2 + 1].astype(jnp.int32)
        dests.append((d[:, 0], d[:, 1]))
        xs = _dispatch(x, d[:, 0], d[:, 1], xs, tm)
    ys = _experts(xs, tile_expert, n_used, wg, wu, wd)
    outs = []
    for x, meta, (d1, d2), tm in zip(xs_groups, metas, dests, tms):
        tmc = min(256, tm)
        outs.append(_combine(x, meta, d1, d2, ys, ln_g, ln_b, tmc))
    return outs


def _shift_rows(x, s, fill):
    rolled = pltpu.roll(x, s, 0)
    return jnp.where(_iota(x.shape, 0) >= s, rolled, fill)


def _causal_conv4(ext_ref, x, cw_ref, cb_ref, lc):
    ext_ref[8:8 + lc, :] = x
    out = cb_ref[...] + cw_ref[3:4, :] * x
    for j in range(3):
        out = out + cw_ref[j:j + 1, :] * ext_ref[5 + j:5 + j + lc, :]
    ext_ref[0:8, :] = ext_ref[lc:lc + 8, :]
    return out


RG_CHUNK = 256


def _rglru_kernel(u_ref, cw_ref, cb_ref, wa_ref, ba_ref, wx_ref, bx_ref, lam_ref,
                  y_ref, h_ref, ext_ref, hc_ref):
    lc = RG_CHUNK

    @pl.when(pl.program_id(1) == 0)
    def _():
        ext_ref[0:8, :] = jnp.zeros((8, BRANCH_W), F32)
        hc_ref[...] = jnp.zeros_like(hc_ref)

    xc = _causal_conv4(ext_ref, u_ref[0, :, 0:BRANCH_W], cw_ref, cb_ref, lc)
    r = _sigmoid(_dot(xc, wa_ref[...]) + ba_ref[...])
    i = _sigmoid(_dot(xc, wx_ref[...]) + bx_ref[...])
    log_a = -RG_C * r * _softplus(-lam_ref[...])
    a = jnp.exp(log_a)
    u = jnp.sqrt(1.0 - jnp.exp(2.0 * log_a)) * (i * xc)
    s = 1
    while s < lc:
        u = u + a * _shift_rows(u, s, 0.0)
        a = a * _shift_rows(a, s, 1.0)
        s *= 2
    h = u + a * hc_ref[...]
    hc_ref[...] = h[lc - 1:lc, :]
    h_ref[0] = h[lc - 1:lc, :]
    y_ref[0] = h * _gelu_tanh(u_ref[0, :, BRANCH_W:2 * BRANCH_W])


def _rglru_prompt(u, cw, cb, wa, ba, wx, bx, lam):
    bsz, seq, _ = u.shape
    lc = RG_CHUNK
    return pl.pallas_call(
        _rglru_kernel,
        out_shape=[jax.ShapeDtypeStruct((bsz, seq, BRANCH_W), F32),
                   jax.ShapeDtypeStruct((bsz, 1, BRANCH_W), F32)],
        grid=(bsz, seq // lc),
        in_specs=[pl.BlockSpec((1, lc, W_RG), lambda b, c: (b, c, 0)),
                  _const_spec((4, BRANCH_W)), _const_spec((1, BRANCH_W)),
                  _const_spec((BRANCH_W, BRANCH_W)), _const_spec((1, BRANCH_W)),
                  _const_spec((BRANCH_W, BRANCH_W)), _const_spec((1, BRANCH_W)),
                  _const_spec((1, BRANCH_W))],
        out_specs=[pl.BlockSpec((1, lc, BRANCH_W), lambda b, c: (b, c, 0)),
                   pl.BlockSpec((1, 1, BRANCH_W), lambda b, c: (b, 0, 0))],
        scratch_shapes=[pltpu.VMEM((lc + 8, BRANCH_W), F32), pltpu.VMEM((1, BRANCH_W), F32)],
        compiler_params=_cp("parallel", "arbitrary"),
        name="rglru",
    )(u, cw, cb, wa, ba, wx, bx, lam)


ML_CHUNK = 128


ML_SEQS = 2


def _mlstm_kernel(u_ref, cw_ref, cb_ref, nw_ref, o_ref, c_ref, n_ref, m_ref, ext_ref):
    @pl.when(pl.program_id(1) == 0)
    def _():
        ext_ref[:, 0:8, :] = jnp.zeros((ext_ref.shape[0], 8, 2 * BRANCH_W), F32)
        c_ref[...] = jnp.zeros_like(c_ref)
        n_ref[...] = jnp.zeros_like(n_ref)
        m_ref[...] = jnp.zeros_like(m_ref)

    for sq in range(u_ref.shape[0]):
        _mlstm_chunk(sq, u_ref, cw_ref, cb_ref, nw_ref, o_ref, c_ref, n_ref, m_ref, ext_ref.at[sq])


def _mlstm_chunk(sq, u_ref, cw_ref, cb_ref, nw_ref, o_ref, c_ref, n_ref, m_ref, ext_ref):
    lc = ML_CHUNK
    W = BRANCH_W
    qk = _silu(_causal_conv4(ext_ref, u_ref[sq, :, 0:2 * W], cw_ref, cb_ref, lc))
    q = qk[:, 0:W]
    k = qk[:, W:2 * W] * (HEAD_W ** -0.5)
    v = u_ref[sq, :, 2 * W:3 * W]
    og = u_ref[sq, :, 3 * W:4 * W]
    ipre = u_ref[sq, :, 4 * W:4 * W + 128]
    logf = _log_sigmoid(u_ref[sq, :, 4 * W + 128:4 * W + 256])
    bcum = _dot_exact_lhs(_tril_ones(lc), logf)
    rowsrc = ipre - bcum
    rows_t = rowsrc.T
    lane = _iota((1, W), 1) // HEAD_W
    causal = _iota((lc, lc), 0) >= _iota((lc, lc), 1)
    cmat = c_ref[sq]
    nrow = n_ref[sq]
    h_acc = jnp.zeros((lc, W), F32)
    ws_full = jnp.zeros((lc, W), F32)
    keep_full = jnp.zeros((1, W), F32)
    kb = k.astype(BF16)
    vb = v.astype(BF16)
    for h in range(HEADS):
        hm = lane == h
        b_col = bcum[:, h:h + 1]
        log_d = jnp.where(causal, b_col + rows_t[h:h + 1, :], -jnp.inf)
        m_prev = m_ref[sq, h:h + 1, 0:1]
        log_inter = b_col + m_prev
        m_t = jnp.maximum(log_inter, jnp.max(log_d, axis=-1, keepdims=True))
        dmat = jnp.exp(log_d - m_t)
        inter = jnp.exp(log_inter - m_t)
        qh = jnp.where(hm, q, 0.0)
        s = _dot_nt(qh, kb) * dmat
        num = _dot(s, vb) + inter * _dot_nt(qh, cmat)
        dot = (jnp.sum(s, axis=-1, keepdims=True)
               + inter * jnp.sum(qh * nrow, axis=-1, keepdims=True))
        den = jnp.maximum(jnp.abs(dot), jnp.exp(-m_t))
        h_acc = h_acc + jnp.where(hm, num / den, 0.0)
        m_new = m_t[lc - 1:lc, :]
        b_last = b_col[lc - 1:lc, :]
        ws_col = jnp.exp(rowsrc[:, h:h + 1] + (b_last - m_new))
        keep = jnp.exp(b_last + m_prev - m_new)
        ws_full = ws_full + jnp.where(hm, ws_col, 0.0)
        keep_full = keep_full + jnp.where(hm, keep, 0.0)
        m_ref[sq, h:h + 1, :] = jnp.broadcast_to(m_new, (1, 128))
    blk = (_iota((W, W), 0) // HEAD_W) == (_iota((W, W), 1) // HEAD_W)
    vw = v * ws_full
    c_ref[sq] = keep_full * cmat + jnp.where(blk, _dot_tn(vw, kb), 0.0)
    n_ref[sq] = keep_full * nrow + jnp.sum(ws_full * k, axis=0, keepdims=True)
    pm = _head_mean_mat()
    mu = _dot_exact_rhs(h_acc, pm)
    hc = h_acc - mu
    var = _dot_exact_rhs(hc * hc, pm)
    o_ref[sq] = hc * lax.rsqrt(var + 1e-6) * nw_ref[...] * _sigmoid(og)


def _mlstm_prompt(u, cw, cb, nw):
    bsz, seq, _ = u.shape
    lc = ML_CHUNK
    W = BRANCH_W
    return pl.pallas_call(
        _mlstm_kernel,
        out_shape=[jax.ShapeDtypeStruct((bsz, seq, W), F32),
                   jax.ShapeDtypeStruct((bsz, W, W), F32),
                   jax.ShapeDtypeStruct((bsz, 1, W), F32),
                   jax.ShapeDtypeStruct((bsz, 8, 128), F32)],
        grid=(bsz // ML_SEQS, seq // lc),
        in_specs=[pl.BlockSpec((ML_SEQS, lc, W_ML), lambda b, c: (b, c, 0)),
                  _const_spec((4, 2 * W)), _const_spec((1, 2 * W)), _const_spec((1, W))],
        out_specs=[pl.BlockSpec((ML_SEQS, lc, W), lambda b, c: (b, c, 0)),
                   pl.BlockSpec((ML_SEQS, W, W), lambda b, c: (b, 0, 0)),
                   pl.BlockSpec((ML_SEQS, 1, W), lambda b, c: (b, 0, 0)),
                   pl.BlockSpec((ML_SEQS, 8, 128), lambda b, c: (b, 0, 0))],
        scratch_shapes=[pltpu.VMEM((ML_SEQS, lc + 8, 2 * W), F32)],
        compiler_params=_cp("parallel", "arbitrary"),
        name="mlstm",
    )(u, cw, cb, nw)


GLA_CHUNK = 128
GLA_QK = HEADS * GLA_DK


def _gla_level_ref(b, m, lc):
    if 2 * m == lc:
        return jnp.broadcast_to(b[m - 1:m, :], b.shape)
    if m >= 4:
        nb = lc // (2 * m)
        mid = b.reshape(nb, 2 * m, b.shape[1])[:, m - 1:m, :]
        return jnp.broadcast_to(mid, (nb, 2 * m, b.shape[1])).reshape(b.shape)
    pos = _iota(b.shape, 0) % (2 * m)
    out = b
    for p in range(2 * m):
        sh = p - (m - 1)
        if sh != 0:
            out = jnp.where(pos == p, pltpu.roll(b, sh % lc, 0), out)
    return out


def _gla_kernel(u_ref, au_ref, ab_ref, nw_ref, o_ref, s_ref):
    lc = GLA_CHUNK
    W = BRANCH_W

    @pl.when(pl.program_id(1) == 0)
    def _():
        s_ref[...] = jnp.zeros_like(s_ref)

    q = u_ref[0, :, 0:GLA_QK] * (GLA_DK ** -0.5)
    k = u_ref[0, :, GLA_QK:2 * GLA_QK]
    v = u_ref[0, :, 2 * GLA_QK:2 * GLA_QK + W]
    g = u_ref[0, :, 2 * GLA_QK + W:2 * GLA_QK + 2 * W]
    ac = u_ref[0, :, 2 * GLA_QK + 2 * W:2 * GLA_QK + 2 * W + 128]
    la = _log_sigmoid(_dot(ac, au_ref[...]) + ab_ref[...]) / GLA_TAU
    b = _dot_exact_lhs(_tril_ones(lc), la)
    st = s_ref[0]
    o = _dot_nt(q * jnp.exp(b), st)
    gsum = ((_iota((GLA_QK, W), 0) // GLA_DK) == (_iota((GLA_QK, W), 1) // HEAD_W)).astype(BF16)
    o = o + jnp.dot((q * k).astype(BF16), gsum, preferred_element_type=F32) * v

    hq = _iota((lc, GLA_QK), 1) // GLA_DK
    row1 = _iota((lc, 1), 0)
    rt = _iota((HEADS * lc, lc), 0) % lc
    cs = _iota((HEADS * lc, lc), 1)
    kb_rows = None
    attn = jnp.zeros((HEADS * lc, lc), F32)
    m = lc // 2
    while m >= 1:
        ref = _gla_level_ref(b, m, lc)
        upper = (row1 % (2 * m)) >= m
        qt = q * jnp.exp(jnp.where(upper, b - ref, -jnp.inf))
        kt = k * jnp.exp(jnp.where(upper, -jnp.inf, ref - b))
        qs = jnp.concatenate([jnp.where(hq == h, qt, 0.0) for h in range(HEADS)], axis=0)
        part = _dot_nt(qs, kt)
        if 2 * m == lc:
            attn = attn + part
        else:
            attn = attn + jnp.where((rt // (2 * m)) == (cs // (2 * m)), part, 0.0)
        m //= 2
    av = _dot(attn, v)
    hv = _iota((lc, W), 1) // HEAD_W
    for h in range(HEADS):
        o = o + jnp.where(hv == h, av[h * lc:(h + 1) * lc, :], 0.0)
    b_last = b[lc - 1:lc, :]
    blk = (_iota((W, GLA_QK), 0) // HEAD_W) == (_iota((W, GLA_QK), 1) // GLA_DK)
    s_ref[0] = jnp.exp(b_last) * st + jnp.where(blk, _dot_tn(v, k * jnp.exp(b_last - b)), 0.0)
    ms = _dot_exact_rhs(o * o, _head_mean_mat())
    o_ref[0] = o * lax.rsqrt(ms + 1e-6) * nw_ref[...] * _silu(g)


def _gla_prompt(u, au, ab, nw):
    bsz, seq, _ = u.shape
    lc = GLA_CHUNK
    W = BRANCH_W
    return pl.pallas_call(
        _gla_kernel,
        out_shape=[jax.ShapeDtypeStruct((bsz, seq, W), F32),
                   jax.ShapeDtypeStruct((bsz, W, GLA_QK), F32)],
        grid=(bsz, seq // lc),
        in_specs=[pl.BlockSpec((1, lc, W_GL), lambda b, c: (b, c, 0)),
                  _const_spec((128, GLA_QK)), _const_spec((1, GLA_QK)), _const_spec((1, W))],
        out_specs=[pl.BlockSpec((1, lc, W), lambda b, c: (b, c, 0)),
                   pl.BlockSpec((1, W, GLA_QK), lambda b, c: (b, 0, 0))],
        compiler_params=_cp("parallel", "arbitrary"),
        name="gla",
    )(u, au, ab, nw)


RW_CHUNK = 256


def _head_sum_mat(width=BRANCH_W, group=HEAD_W):
    r = _iota((width, width), 0) // group
    c = _iota((width, width), 1) // group
    return jnp.where(r == c, 1.0, 0.0).astype(BF16)


def _rwkv_pointwise(um, p_ref, lora_ref):
    W = BRANCH_W
    r = um[:, 0:W]
    k = um[:, W:2 * W]
    v = um[:, 2 * W:3 * W]
    lo = um[:, 3 * W:3 * W + 128]
    w_log = -_softplus(-(p_ref[0:1, :] + _dot(jnp.tanh(lo), lora_ref[0]))) - 0.5
    decay = jnp.exp(-jnp.exp(w_log))
    a = _sigmoid(p_ref[1:2, :] + _dot(lo, lora_ref[1]))
    g = _dot(_sigmoid(lo), lora_ref[2])
    hs = _head_sum_mat()
    kk = k * p_ref[2:3, :]
    kk = kk / jnp.maximum(jnp.sqrt(_dot_exact_rhs(kk * kk, hs)), 1e-12)
    k2 = k * (1.0 + (a - 1.0) * p_ref[3:4, :])
    bonus = _dot_exact_rhs(r * k2 * p_ref[4:5, :], hs)
    return decay, kk, kk * a, k2, r, v, g, bonus * v


RW_PREP_T = 128


def _rwkv_prep_kernel(u_ref, mu_ref, p_ref, lora_ref, *rest):
    planes, g_ref, bv_ref, ext_ref, xs_ref = rest[:6], rest[6], rest[7], rest[8], rest[9]
    T = RW_PREP_T
    half = T // 2

    @pl.when(pl.program_id(0) == 0)
    def _():
        ext_ref[:, 0:8, :] = jnp.zeros((ext_ref.shape[0], 8, RWKV_COLS), F32)

    lane = _iota((half, 128), 1)

    def per_sequence(b, carry):
        u = u_ref[b]
        ext_ref[b, 8:8 + T, :] = u
        prev = ext_ref[b, 7:7 + T, :]
        ext_ref[b, 0:8, :] = ext_ref[b, T:T + 8, :]
        um = u + mu_ref[...] * (prev - u)
        dec, kk, kka, k2, r, v, g, bv = _rwkv_pointwise(um, p_ref, lora_ref)
        g_ref[b] = g
        bv_ref[b] = bv
        for z_ref, val in zip(planes, (dec, kk, kka, k2, r, v)):
            for p in range(2):
                xs_ref[p] = val[:, 128 * p:128 * (p + 1)]
            for p in range(2):
                even = xs_ref[p, pl.ds(0, half, stride=2), :]
                odd = xs_ref[p, pl.ds(1, half, stride=2), :]
                z_ref[b * HEADS + 2 * p] = jnp.where(lane < HEAD_W, even, pltpu.roll(odd, HEAD_W, 1))
                z_ref[b * HEADS + 2 * p + 1] = jnp.where(lane < HEAD_W, pltpu.roll(even, HEAD_W, 1), odd)
        return carry

    lax.fori_loop(0, u_ref.shape[0], per_sequence, 0)


def _rwkv_prep(u, mu, p, lora):
    bsz, seq, _ = u.shape
    T = RW_PREP_T
    W = BRANCH_W
    nch = bsz * HEADS
    plane = jax.ShapeDtypeStruct((nch, seq // 2, 128), F32)
    nat = jax.ShapeDtypeStruct((bsz, seq, W), F32)
    return pl.pallas_call(
        _rwkv_prep_kernel,
        out_shape=[plane] * 6 + [nat, nat],
        grid=(seq // T,),
        in_specs=[pl.BlockSpec((bsz, T, RWKV_COLS), lambda i: (0, i, 0)),
                  _const_spec((1, RWKV_COLS)), _const_spec((8, W)), _const_spec((3, 128, W))],
        out_specs=[pl.BlockSpec((nch, T // 2, 128), lambda i: (0, i, 0))] * 6
        + [pl.BlockSpec((bsz, T, W), lambda i: (0, i, 0))] * 2,
        scratch_shapes=[pltpu.VMEM((bsz, T + 8, RWKV_COLS), F32), pltpu.VMEM((2, T, 128), F32)],
        compiler_params=_cp("arbitrary"),
        name="rwkv_prep",
    )(u, mu, p, lora)


def _delta_rule_step(s_ref, ni, w, kk, kka, k, r, v_row, y_store, s_out_ref=None):
    s_out = s_ref if s_out_ref is None else s_out_ref
    sa = [jnp.sum(s_ref[i] * kk, axis=0, keepdims=True) for i in range(ni)]
    for i in range(ni):
        sn = s_ref[i] * w - sa[i] * kka + v_row(i) * k
        s_out[i] = sn
        y_store(i, jnp.sum(sn * r, axis=0, keepdims=True))


RW_SCAN_PAIRS = 8


def _rwkv_scan_kernel(w_ref, kk_ref, kka_ref, k_ref, r_ref, v_ref, y_ref, s_out_ref, s_ref, t_ref):
    nch = w_ref.shape[0]
    rep = 128 // nch
    ni = s_ref.shape[0]

    @pl.when(pl.program_id(0) == 0)
    def _():
        s_ref[...] = jnp.zeros_like(s_ref)

    zs = (w_ref, kk_ref, kka_ref, k_ref, r_ref, v_ref)
    n_pairs = w_ref.shape[1]

    def stage(r, slot):
        for n, z_ref in enumerate(zs):
            rows = z_ref[:, r, :]
            step = ni if n == 5 else 0
            copies = [rows if s * step == 0 else pltpu.roll(rows, 128 - s * step, 1) for s in range(rep)]
            t_ref[slot, n] = jnp.concatenate(copies, axis=0).T

    def tokens(r, slot):
        for t2 in range(2):
            sl = pl.ds(t2 * HEAD_W, HEAD_W)

            def y_store(i, row, t2=t2):
                y_ref[2 * r + t2, i:i + 1, :] = row
            _delta_rule_step(s_ref, ni, t_ref[slot, 0, sl, :], t_ref[slot, 1, sl, :], t_ref[slot, 2, sl, :],
                             t_ref[slot, 3, sl, :], t_ref[slot, 4, sl, :],
                             lambda i, t2=t2: t_ref[slot, 5, pl.ds(t2 * HEAD_W + i, 1), :], y_store)

    stage(0, 0)

    def two_pairs(q, carry):
        stage(2 * q + 1, 1)
        tokens(2 * q, 0)
        stage(jnp.minimum(2 * q + 2, n_pairs - 1), 0)
        tokens(2 * q + 1, 1)
        return carry

    lax.fori_loop(0, n_pairs // 2, two_pairs, 0)

    @pl.when(pl.program_id(0) == pl.num_programs(0) - 1)
    def _():
        s_out_ref[...] = s_ref[...]


def _rwkv_scan(planes, seq):
    nch = planes[0].shape[0]
    ni = HEAD_W * nch // 128
    tp = min(RW_SCAN_PAIRS, seq // 2)
    zspec = pl.BlockSpec((nch, tp, 128), lambda t: (0, t, 0))
    return pl.pallas_call(
        _rwkv_scan_kernel,
        out_shape=[jax.ShapeDtypeStruct((seq, ni, 128), F32), jax.ShapeDtypeStruct((ni, HEAD_W, 128), F32)],
        grid=(seq // (2 * tp),),
        in_specs=[zspec] * 6,
        out_specs=[pl.BlockSpec((2 * tp, ni, 128), lambda t: (t, 0, 0)),
                   pl.BlockSpec((ni, HEAD_W, 128), lambda t: (0, 0, 0))],
        scratch_shapes=[pltpu.VMEM((ni, HEAD_W, 128), F32), pltpu.VMEM((2, 6, 128, 128), F32)],
        compiler_params=_cp("arbitrary"),
        name="rwkv_scan",
    )(*planes)


def _rwkv_post_kernel(y_ref, g_ref, bv_ref, ln_ref, o_ref):
    y = y_ref[0]
    pm = _head_mean_mat()
    mu = _dot_exact_rhs(y, pm)
    yc = y - mu
    var = _dot_exact_rhs(yc * yc, pm)
    o = yc * lax.rsqrt(var + RWKV_GN_EPS) * ln_ref[0:1, :] + ln_ref[1:2, :]
    o_ref[0] = (o + bv_ref[0]) * g_ref[0]


def _rwkv_post(y, g, bv, ln):
    bsz, seq, W = y.shape
    T = RW_CHUNK
    tok = pl.BlockSpec((1, T, W), lambda b, c: (b, c, 0))
    return pl.pallas_call(
        _rwkv_post_kernel,
        out_shape=jax.ShapeDtypeStruct((bsz, seq, W), F32),
        grid=(bsz, seq // T),
        in_specs=[tok, tok, tok, _const_spec((2, W))],
        out_specs=tok,
        compiler_params=_cp("parallel", "parallel"),
        name="rwkv_post",
    )(y, g, bv, ln)


(PC_W0, PC_A0, PC_KK, PC_KA, PC_RK, PC_LNW, PC_LNB, PC_MLNW, PC_GLNW,
 PC_RCW, PC_RCB, PC_RBA, PC_RBX, PC_RLAM) = (0, 1, 2, 3, 4, 5, 6, 7, 8, 9, 13, 14, 15, 16)
PC_COLS = 17
DEC_B = 128


def _rowsum(x):
    return jnp.sum(x, axis=0, keepdims=True)


def _decode_kernel(
        ur_ref, uk_ref, uv_ref, ulo_ref, pr_ref, pk_ref, pv_ref, plo_ref,
        mur_ref, muk_ref, muv_ref, mulo_ref, lora_ref, srw_ref,
        mq_ref, mk_ref, mv_ref, mo_ref, mi_ref, mf_ref, bq_ref, bk_ref, cwq_ref, cwk_ref,
        c_ref, n_ref, m_ref,
        gq_ref, gk_ref, gv_ref, gg_ref, ga_ref, au_ref, ab_ref, gs_ref,
        rx_ref, ry_ref, rb_ref, wa_ref, wx_ref, h_ref,
        pc_ref,
        oa_ref, ob_ref, oc_ref, od_ref, srw_o, c_o, n_o, m_o, gs_o, h_o,
        y_scr):
    col = lambda j: pc_ref[:, j:j + 1]

    def shift(u_ref, p_ref, mu_ref):
        u = u_ref[...]
        return u + mu_ref[...] * (p_ref[...] - u)
    r = shift(ur_ref, pr_ref, mur_ref)
    k = shift(uk_ref, pk_ref, muk_ref)
    v = shift(uv_ref, pv_ref, muv_ref)
    lo = shift(ulo_ref, plo_ref, mulo_ref)
    w_log = -_softplus(-(col(PC_W0) + _dot(lora_ref[0], jnp.tanh(lo)))) - 0.5
    decay = jnp.exp(-jnp.exp(w_log))
    a = _sigmoid(col(PC_A0) + _dot(lora_ref[1], lo))
    g = _dot(lora_ref[2], _sigmoid(lo))
    kk = k * col(PC_KK)
    kk = kk / jnp.maximum(jnp.sqrt(_rowsum(kk * kk)), 1e-12)
    k2 = k * (1.0 + (a - 1.0) * col(PC_KA))
    bonus = _rowsum(r * k2 * col(PC_RK))

    def y_store(i, row):
        y_scr[i:i + 1, :] = row
    _delta_rule_step(srw_ref, HEAD_W, decay, kk, kk * a, k2, r, lambda i: v[i:i + 1, :], y_store, srw_o)
    y = y_scr[...]
    yc = y - jnp.mean(y, axis=0, keepdims=True)
    var = jnp.mean(yc * yc, axis=0, keepdims=True)
    o = yc * lax.rsqrt(var + RWKV_GN_EPS) * col(PC_LNW) + col(PC_LNB)
    oa_ref[...] = (o + bonus * v) * g

    def conv(u_ref, b_ref, cw_ref):
        out = cw_ref[:, 4:5] + cw_ref[:, 3:4] * u_ref[...]
        for j in range(3):
            out = out + cw_ref[:, j:j + 1] * b_ref[j]
        return out
    q = _silu(conv(mq_ref, bq_ref, cwq_ref))
    k = _silu(conv(mk_ref, bk_ref, cwk_ref)) * (HEAD_W ** -0.5)
    v = mv_ref[...]
    ipre = mi_ref[...]
    logf = _log_sigmoid(mf_ref[...])
    m_prev = m_ref[...]
    m_t = jnp.maximum(logf + m_prev, ipre)
    inter = jnp.exp(logf + m_prev - m_t)
    wsc = jnp.exp(ipre - m_t)
    s = _rowsum(q * k) * wsc
    n_prev = n_ref[...]
    den = jnp.maximum(jnp.abs(s + inter * _rowsum(n_prev * q)), jnp.exp(-m_t))
    for i in range(HEAD_W):
        ci = c_ref[i]
        vi = v[i:i + 1, :]
        y_scr[i:i + 1, :] = (s * vi + inter * _rowsum(ci * q)) / den
        c_o[i] = inter * ci + (wsc * vi) * k
    n_o[...] = inter * n_prev + wsc * k
    m_o[...] = m_t
    y = y_scr[...]
    yc = y - jnp.mean(y, axis=0, keepdims=True)
    var = jnp.mean(yc * yc, axis=0, keepdims=True)
    ob_ref[...] = yc * lax.rsqrt(var + 1e-6) * col(PC_MLNW) * _sigmoid(mo_ref[...])

    q = gq_ref[...] * (GLA_DK ** -0.5)
    k = gk_ref[...]
    v = gv_ref[...]
    eb = jnp.exp(_log_sigmoid(_dot(au_ref[...], ga_ref[...]) + ab_ref[...]) / GLA_TAU)
    attn = _rowsum(q * k)
    qe = q * eb
    for i in range(HEAD_W):
        si = gs_ref[i]
        vi = v[i:i + 1, :]
        y_scr[i:i + 1, :] = attn * vi + _rowsum(si * qe)
        gs_o[i] = eb * si + k * vi
    y = y_scr[...]
    ms = jnp.mean(y * y, axis=0, keepdims=True)
    oc_ref[...] = y * lax.rsqrt(ms + 1e-6) * col(PC_GLNW) * _silu(gg_ref[...])

    xc = col(PC_RCB) + col(PC_RCW + 3) * rx_ref[...]
    for j in range(3):
        xc = xc + col(PC_RCW + j) * rb_ref[j]
    rg = _sigmoid(_dot(wa_ref[...], xc) + col(PC_RBA))
    ig = _sigmoid(_dot(wx_ref[...], xc) + col(PC_RBX))
    log_a = -RG_C * rg * _softplus(-col(PC_RLAM))
    hn = jnp.exp(log_a) * h_ref[...] + jnp.sqrt(1.0 - jnp.exp(2.0 * log_a)) * (ig * xc)
    h_o[...] = hn
    od_ref[...] = hn * _gelu_tanh(ry_ref[...])


def _decode_mixers(ut_rw, prev_t, mu_c, lora_t, s_rw,
                   ut_ml, mconv_t, ml_cw, c_st, n_st, m_st,
                   ut_gl, au_t, ab_c, g_st,
                   ut_rg, rconv_t, wa_t, wx_t, h_st, pcols):
    nb = DEC_B
    H, HW = HEADS, HEAD_W

    def blk(arr, view, block, index):
        a = arr.reshape(view)
        nd = len(block)
        return a, pl.BlockSpec(block, index)

    ins = []
    v14 = (14, HW, nb)
    v7 = (7, 128, nb)
    for arr in (ut_rw, prev_t):
        ins.append(blk(arr, v14, (None, HW, nb), lambda h: (h, 0, 0)))
        ins.append(blk(arr, v14, (None, HW, nb), lambda h: (4 + h, 0, 0)))
        ins.append(blk(arr, v14, (None, HW, nb), lambda h: (8 + h, 0, 0)))
        ins.append(blk(arr, v7, (None, 128, nb), lambda h: (6, 0, 0)))
    ins.append(blk(mu_c, (14, HW, 1), (None, HW, 1), lambda h: (h, 0, 0)))
    ins.append(blk(mu_c, (14, HW, 1), (None, HW, 1), lambda h: (4 + h, 0, 0)))
    ins.append(blk(mu_c, (14, HW, 1), (None, HW, 1), lambda h: (8 + h, 0, 0)))
    ins.append(blk(mu_c, (7, 128, 1), (None, 128, 1), lambda h: (6, 0, 0)))
    ins.append(blk(lora_t, (3, BRANCH_W, 128), (3, HW, 128), lambda h: (0, h, 0)))
    st_spec = lambda: pl.BlockSpec((None, HW, HW, nb), lambda h: (h, 0, 0, 0))
    ins.append((s_rw, st_spec()))
    v20 = (20, HW, nb)
    for j in range(4):
        ins.append(blk(ut_ml, v20, (None, HW, nb), functools.partial(lambda h, j: (4 * j + h, 0, 0), j=j)))
    ins.append(blk(ut_ml[1024:1028], (4, 1, nb), (None, 1, nb), lambda h: (h, 0, 0)))
    ins.append(blk(ut_ml[1152:1156], (4, 1, nb), (None, 1, nb), lambda h: (h, 0, 0)))
    ins.append(blk(mconv_t, (3, 8, HW, nb), (3, None, HW, nb), lambda h: (0, h, 0, 0)))
    ins.append(blk(mconv_t, (3, 8, HW, nb), (3, None, HW, nb), lambda h: (0, 4 + h, 0, 0)))
    ins.append(blk(ml_cw, (8, HW, 5), (None, HW, 5), lambda h: (h, 0, 0)))
    ins.append(blk(ml_cw, (8, HW, 5), (None, HW, 5), lambda h: (4 + h, 0, 0)))
    ins.append((c_st, st_spec()))
    ins.append(blk(n_st, (H, HW, nb), (None, HW, nb), lambda h: (h, 0, 0)))
    ins.append(blk(m_st, (H, 1, nb), (None, 1, nb), lambda h: (h, 0, 0)))
    ins.append(blk(ut_gl, (28, GLA_DK, nb), (None, GLA_DK, nb), lambda h: (h, 0, 0)))
    ins.append(blk(ut_gl, (28, GLA_DK, nb), (None, GLA_DK, nb), lambda h: (4 + h, 0, 0)))
    ins.append(blk(ut_gl, v14, (None, HW, nb), lambda h: (4 + h, 0, 0)))
    ins.append(blk(ut_gl, v14, (None, HW, nb), lambda h: (8 + h, 0, 0)))
    ins.append(blk(ut_gl, v7, (None, 128, nb), lambda h: (6, 0, 0)))
    ins.append(blk(au_t, (H, GLA_DK, 128), (None, GLA_DK, 128), lambda h: (h, 0, 0)))
    ins.append(blk(ab_c, (H, GLA_DK, 1), (None, GLA_DK, 1), lambda h: (h, 0, 0)))
    ins.append((g_st, pl.BlockSpec((None, HW, GLA_DK, nb), lambda h: (h, 0, 0, 0))))
    ins.append(blk(ut_rg, (8, HW, nb), (None, HW, nb), lambda h: (h, 0, 0)))
    ins.append(blk(ut_rg, (8, HW, nb), (None, HW, nb), lambda h: (4 + h, 0, 0)))
    ins.append(blk(rconv_t, (3, H, HW, nb), (3, None, HW, nb), lambda h: (0, h, 0, 0)))
    ins.append((wa_t, pl.BlockSpec((None, HW, HW), lambda h: (h, 0, 0))))
    ins.append((wx_t, pl.BlockSpec((None, HW, HW), lambda h: (h, 0, 0))))
    ins.append(blk(h_st, (H, HW, nb), (None, HW, nb), lambda h: (h, 0, 0)))
    ins.append(blk(pcols, (H, HW, PC_COLS), (None, HW, PC_COLS), lambda h: (h, 0, 0)))

    vec = lambda: (jax.ShapeDtypeStruct((H, HW, nb), F32), pl.BlockSpec((None, HW, nb), lambda h: (h, 0, 0)))
    mat = lambda: (jax.ShapeDtypeStruct((H, HW, HW, nb), F32), st_spec())
    outs = [vec(), vec(), vec(), vec(), mat(), mat(), vec(),
            (jax.ShapeDtypeStruct((H, 1, nb), F32), pl.BlockSpec((None, 1, nb), lambda h: (h, 0, 0))),
            (jax.ShapeDtypeStruct((H, HW, GLA_DK, nb), F32),
             pl.BlockSpec((None, HW, GLA_DK, nb), lambda h: (h, 0, 0, 0))),
            vec()]
    return pl.pallas_call(
        _decode_kernel,
        out_shape=[o[0] for o in outs],
        grid=(H,),
        in_specs=[s for _, s in ins],
        out_specs=[o[1] for o in outs],
        scratch_shapes=[pltpu.VMEM((HW, nb), F32)],
        compiler_params=_cp("parallel"),
        name="decode_mixers",
    )(*[a for a, _ in ins])


def _pad_cols(a, width):
    return jnp.pad(a, ((0, 0), (0, width - a.shape[1])))


def _pad_rows(a, rows, at=0):
    return jnp.pad(a, ((at, rows - at - a.shape[0]), (0, 0)))


def _pack_layer(p, l):
    offs = np.concatenate([[0], np.cumsum(IN_SIZES)])
    w_in, b_in = p['w_in'][l], p['b_in'][l][None, :]
    seg = lambda a, i: a[:, int(offs[i]):int(offs[i + 1])]

    def regroup(a):
        return jnp.concatenate(
            [seg(a, 0), seg(a, 1), seg(a, 2), seg(a, 3), _pad_cols(seg(a, 4), 128), _pad_cols(seg(a, 5), 128),
             seg(a, 6), seg(a, 7), seg(a, 8), seg(a, 9), _pad_cols(seg(a, 10), 128), seg(a, 11), seg(a, 12)],
            axis=1)
    k = dict(
        w_pack=regroup(w_in).astype(BF16), b_pack=regroup(b_in),
        w_gate=seg(w_in, 13).astype(BF16), b_gate=seg(b_in, 13),
        w_branch=p['w_branch'][l].astype(BF16), w_out=p['w_out'][l].astype(BF16),
        ln1_g=p['ln1_g'][l][None], ln1_b=p['ln1_b'][l][None],
        ln2_g=p['ln2_g'][l][None], ln2_b=p['ln2_b'][l][None],
    )
    w_up, a_up, g_up = p['rwkv_w_up'][l], p['rwkv_a_up'][l], p['rwkv_g_up'][l]
    lora = jnp.stack([_pad_rows(w_up, 128, 0), _pad_rows(a_up, 128, 32), _pad_rows(g_up, 128, 64)])
    k['rw_lora'] = lora.astype(BF16)
    k['rw_lora_t'] = jnp.swapaxes(lora, 1, 2).astype(BF16)
    k['rw_mu'] = p['rwkv_mu'][l][None]
    rw_rows = [p['rwkv_w0'][l], p['rwkv_a0'][l], p['rwkv_k_k'][l], p['rwkv_k_a'][l],
               p['rwkv_r_k'][l].reshape(BRANCH_W)]
    k['rw_p'] = jnp.stack(rw_rows + [jnp.zeros((BRANCH_W,), F32)] * 3)
    k['rw_ln'] = jnp.stack([p['rwkv_ln_w'][l], p['rwkv_ln_b'][l]])
    k['ml_cw'], k['ml_cb'] = p['mlstm_conv_w'][l], p['mlstm_conv_b'][l][None]
    k['ml_nw'] = p['mlstm_norm_w'][l][None]
    k['gl_au'] = _pad_rows(p['gla_alpha_up'][l], 128).astype(BF16)
    k['gl_ab'] = p['gla_alpha_b'][l][None]
    k['gl_nw'] = p['gla_norm_w'][l][None]
    wa, wx = p['rglru_wa'][l], p['rglru_wx'][l]
    eye = jnp.eye(HEADS, dtype=F32)
    bd = lambda w: jnp.einsum('gh,gij->gihj', eye, w).reshape(BRANCH_W, BRANCH_W).astype(BF16)
    k['rg_cw'], k['rg_cb'] = p['rglru_conv_w'][l], p['rglru_conv_b'][l][None]
    k['rg_wa'], k['rg_wx'] = bd(wa), bd(wx)
    k['rg_ba'], k['rg_bx'] = p['rglru_ba'][l][None], p['rglru_bx'][l][None]
    k['rg_lam'] = p['rglru_lambda'][l][None]
    k['rg_wa_t'] = jnp.swapaxes(wa, 1, 2).astype(BF16)
    k['rg_wx_t'] = jnp.swapaxes(wx, 1, 2).astype(BF16)
    cols = rw_rows + [p['rwkv_ln_w'][l], p['rwkv_ln_b'][l], p['mlstm_norm_w'][l], p['gla_norm_w'][l]]
    cols += [p['rglru_conv_w'][l][j] for j in range(4)]
    cols += [p['rglru_conv_b'][l], p['rglru_ba'][l], p['rglru_bx'][l], p['rglru_lambda'][l]]
    k['pcols'] = jnp.stack(cols, axis=1)
    k['ml_cw_t'] = jnp.concatenate([p['mlstm_conv_w'][l].T, p['mlstm_conv_b'][l][:, None]], axis=1)
    return k


def _diag_blocks(a, rb, cb):
    return jnp.stack([a[:, h * rb:(h + 1) * rb, h * cb:(h + 1) * cb] for h in range(HEADS)], axis=1)


def _prompt_mixers(x, k):
    B, L, _ = x.shape
    W = BRANCH_W
    u_rw, u_ml, u_gl, u_rg = _inproj(x.reshape(B * L, D_MODEL), k['w_pack'], k['b_pack'], 512)
    u_rw, u_ml, u_gl, u_rg = (u.reshape(B, L, -1) for u in (u_rw, u_ml, u_gl, u_rg))
    nch = B * HEADS
    rep = 128 // nch
    *planes, g, bv = _rwkv_prep(u_rw, k['rw_mu'], k['rw_p'], k['rw_lora'])
    y, s1 = _rwkv_scan(planes, L)
    y = y.reshape(L, HEAD_W // rep, rep, B, HEADS).transpose(3, 0, 4, 2, 1).reshape(B, L, W)
    s_rw = s1.reshape(HEAD_W // rep, HEAD_W, rep, B, HEADS).transpose(3, 4, 2, 0, 1).reshape(B, HEADS, HEAD_W, HEAD_W)
    o_a = _rwkv_post(y, g, bv, k['rw_ln'])
    o_b, c_bd, n_row, m_row = _mlstm_prompt(u_ml, k['ml_cw'], k['ml_cb'], k['ml_nw'])
    o_c, s_bd = _gla_prompt(u_gl, k['gl_au'], k['gl_ab'], k['gl_nw'])
    o_d, h1 = _rglru_prompt(u_rg, k['rg_cw'], k['rg_cb'], k['rg_wa'], k['rg_ba'], k['rg_wx'], k['rg_bx'],
                            k['rg_lam'])
    states = (u_rw[:, L - 1], s_rw, u_ml[:, L - 3:, :2 * W], _diag_blocks(c_bd, HEAD_W, HEAD_W),
              n_row.reshape(B, HEADS, HEAD_W), m_row[:, :HEADS, 0],
              jnp.swapaxes(_diag_blocks(s_bd, HEAD_W, GLA_DK), 2, 3), u_rg[:, L - 3:, :W], h1[:, 0])
    return [o.reshape(B * L, W) for o in (o_a, o_b, o_c, o_d)], states


def _sample_mixers(x, st, k):
    W = BRANCH_W
    sh0, S0, mconv0, C0, n0, m0, gS0, rconv0, h0 = st
    u_rw, u_ml, u_gl, u_rg = _inproj(x, k['w_pack'], k['b_pack'], DEC_B)
    outs = _decode_mixers(
        u_rw.T, sh0.T, k['rw_mu'].T, k['rw_lora_t'], S0.transpose(1, 2, 3, 0),
        u_ml.T, mconv0.transpose(1, 2, 0), k['ml_cw_t'], C0.transpose(1, 2, 3, 0), n0.transpose(1, 2, 0), m0.T,
        u_gl.T, k['gl_au'].T, k['gl_ab'].T, gS0.transpose(1, 3, 2, 0),
        u_rg.T, rconv0.transpose(1, 2, 0), k['rg_wa_t'], k['rg_wx_t'], h0.T, k['pcols'])
    oa, ob, oc, od, s_rw, c_st, n_st, m_st, g_st, h_st = outs
    branches = [o.reshape(W, DEC_B).T for o in (oa, ob, oc, od)]
    states = (u_rw, s_rw.transpose(3, 0, 1, 2),
              jnp.concatenate([mconv0[:, 1:], u_ml[:, None, :2 * W]], axis=1),
              c_st.transpose(3, 0, 1, 2), n_st.transpose(2, 0, 1), m_st[:, 0, :].T,
              g_st.transpose(3, 0, 2, 1),
              jnp.concatenate([rconv0[:, 1:], u_rg[:, None, :W]], axis=1), h_st.reshape(W, DEC_B).T)
    return branches, states


def _trunk(xp, bl, xs, sample_states, packs, moe):
    tms = (512, DEC_B)
    new_p, new_s = [], []
    for l in range(DEPTH):
        k = packs[l]
        br_p, st_p = _prompt_mixers(xp.reshape(bl[0], bl[1], D_MODEL), k)
        br_s, st_s = _sample_mixers(xs, tuple(s[l] for s in sample_states), k)
        new_p.append(st_p)
        new_s.append(st_s)
        xp, xs = (_merge(x, br, k['w_gate'], k['b_gate'], k['w_branch'], k['w_out'], k['ln1_g'], k['ln1_b'], tm)
                  for x, br, tm in ((xp, br_p, tms[0]), (xs, br_s, tms[1])))
        j = l // 2
        if l % 2 == 0:
            xp, xs = (_ffn(x, moe['ffn_wg'][j], moe['ffn_wu'][j], moe['ffn_wd'][j], k['ln2_g'], k['ln2_b'], tm)
                      for x, tm in ((xp, tms[0]), (xs, tms[1])))
        else:
            xp, xs = _moe_layer([xp, xs], moe['router'][j], moe['router_b'][j], moe['moe_wg'][j],
                                moe['moe_wu'][j], moe['moe_wd'][j], k['ln2_g'], k['ln2_b'])
    stack = lambda sts: [jnp.stack([st[i] for st in sts], axis=0) for i in range(9)]
    return xp, xs, stack(new_p), stack(new_s)


def kernel(x_prompt, x_sample, state_rwkv_shift, state_rwkv_S, state_mlstm_conv, state_mlstm_C,
           state_mlstm_n, state_mlstm_m, state_gla_S, state_rglru_conv, state_rglru_h,
           w_in, b_in, rwkv_mu, rwkv_w0, rwkv_w_up, rwkv_a0, rwkv_a_up, rwkv_g_up, rwkv_k_k,
           rwkv_k_a, rwkv_r_k, rwkv_ln_w, rwkv_ln_b, mlstm_conv_w, mlstm_conv_b, mlstm_norm_w,
           gla_alpha_up, gla_alpha_b, gla_norm_w, rglru_conv_w, rglru_conv_b, rglru_wa, rglru_ba,
           rglru_wx, rglru_bx, rglru_lambda, w_branch, w_out, ln1_g, ln1_b, ffn_wg, ffn_wu, ffn_wd,
           moe_router, moe_router_b, moe_wg, moe_wu, moe_wd, ln2_g, ln2_b):
    p = dict(w_in=w_in, b_in=b_in, rwkv_mu=rwkv_mu, rwkv_w0=rwkv_w0, rwkv_w_up=rwkv_w_up,
             rwkv_a0=rwkv_a0, rwkv_a_up=rwkv_a_up, rwkv_g_up=rwkv_g_up, rwkv_k_k=rwkv_k_k,
             rwkv_k_a=rwkv_k_a, rwkv_r_k=rwkv_r_k, rwkv_ln_w=rwkv_ln_w, rwkv_ln_b=rwkv_ln_b,
             mlstm_conv_w=mlstm_conv_w, mlstm_conv_b=mlstm_conv_b, mlstm_norm_w=mlstm_norm_w,
             gla_alpha_up=gla_alpha_up, gla_alpha_b=gla_alpha_b, gla_norm_w=gla_norm_w,
             rglru_conv_w=rglru_conv_w, rglru_conv_b=rglru_conv_b, rglru_wa=rglru_wa,
             rglru_ba=rglru_ba, rglru_wx=rglru_wx, rglru_bx=rglru_bx, rglru_lambda=rglru_lambda,
             w_branch=w_branch, w_out=w_out, ln1_g=ln1_g, ln1_b=ln1_b, ln2_g=ln2_g, ln2_b=ln2_b)
    packs = [_pack_layer(p, l) for l in range(DEPTH)]
    moe = dict(ffn_wg=ffn_wg.astype(BF16), ffn_wu=ffn_wu.astype(BF16), ffn_wd=ffn_wd.astype(BF16),
               router=jnp.pad(moe_router, ((0, 0), (0, 0), (0, 128 - N_EXPERTS))),
               router_b=jnp.pad(moe_router_b, ((0, 0), (0, 128 - N_EXPERTS)))[:, None, :],
               moe_wg=moe_wg.astype(BF16), moe_wu=moe_wu.astype(BF16), moe_wd=moe_wd.astype(BF16))
    B, L, _ = x_prompt.shape
    sample_states = (state_rwkv_shift, state_rwkv_S, state_mlstm_conv, state_mlstm_C, state_mlstm_n,
                     state_mlstm_m, state_gla_S, state_rglru_conv, state_rglru_h)
    nb, ls, _ = x_sample.shape
    y_p, y_s, ps, ss = _trunk(x_prompt.reshape(B * L, D_MODEL), (B, L), x_sample.reshape(nb * ls, D_MODEL),
                              sample_states, packs, moe)
    return (y_p.reshape(B, L, D_MODEL), y_s.reshape(nb, ls, D_MODEL), *ps, *ss)
```

```python
import functools

import jax
import jax.numpy as jnp
import numpy as np
from jax import lax
from jax.experimental import pallas as pl
from jax.experimental.pallas import tpu as pltpu

F32 = jnp.float32
BF16 = jnp.bfloat16

D_MODEL = 1024
DEPTH = 2
N_BRANCH = 4
BRANCH_W = 256
HEADS = 4
HEAD_W = 64
RWKV_COLS = 896
RWKV_GN_EPS = 64e-5
GLA_DK = 32
GLA_LORA = 16
GLA_TAU = 16.0
RG_C = 8.0
D_FF = 2816
N_EXPERTS = 8
DN_ALPHA = (2 * DEPTH) ** 0.25
LN_EPS = 1e-5

IN_SIZES = (RWKV_COLS, 512, 256, 256, 4, 4, 128, 128, 256, 256, 16, 256, 256, 4096)

W_RW = 896
W_ML = 1280
W_GL = 896
W_RG = 512
W_PACK = W_RW + W_ML + W_GL + W_RG

VMEM_LIMIT = 56 * 1024 * 1024


def _cp(*sem):
    return pltpu.CompilerParams(dimension_semantics=sem, vmem_limit_bytes=VMEM_LIMIT)


def _dot(a, b):
    return jnp.dot(a.astype(BF16), b.astype(BF16), preferred_element_type=F32)


def _dot_nt(a, b):
    return lax.dot_general(a.astype(BF16), b.astype(BF16), (((1,), (1,)), ((), ())),
                           preferred_element_type=F32)


def _dot_tn(a, b):
    return lax.dot_general(a.astype(BF16), b.astype(BF16), (((0,), (0,)), ((), ())),
                           preferred_element_type=F32)


def _split(x):
    hi = x.astype(BF16)
    lo = (x - hi.astype(F32)).astype(BF16)
    return hi, lo


def _dot_exact_lhs(a, x):
    hi, lo = _split(x)
    a = a.astype(BF16)
    return (jnp.dot(a, hi, preferred_element_type=F32) + jnp.dot(a, lo, preferred_element_type=F32))


def _dot_exact_rhs(x, a):
    hi, lo = _split(x)
    a = a.astype(BF16)
    return (jnp.dot(hi, a, preferred_element_type=F32) + jnp.dot(lo, a, preferred_element_type=F32))


def _dot3(x, w):
    xh, xl = _split(x)
    wh, wl = _split(w)
    return (jnp.dot(xh, wh, preferred_element_type=F32) + jnp.dot(xl, wh, preferred_element_type=F32)
            + jnp.dot(xh, wl, preferred_element_type=F32))


def _sigmoid(x):
    return 1.0 / (1.0 + jnp.exp(-x))


def _softplus(x):
    return jnp.maximum(x, 0.0) + jnp.log(1.0 + jnp.exp(-jnp.abs(x)))


def _log_sigmoid(x):
    return -_softplus(-x)


def _silu(x):
    return x * _sigmoid(x)


def _gelu_tanh(x):
    c = np.float32(np.sqrt(2.0 / np.pi))
    return 0.5 * x * (1.0 + jnp.tanh(c * (x + 0.044715 * (x * x * x))))


def _layer_norm(y, g, b):
    mu = jnp.mean(y, axis=-1, keepdims=True)
    yc = y - mu
    var = jnp.mean(yc * yc, axis=-1, keepdims=True)
    return yc * lax.rsqrt(var + LN_EPS) * g + b


def _iota(shape, dim):
    return lax.broadcasted_iota(jnp.int32, shape, dim)


def _head_mean_mat(width=BRANCH_W, group=HEAD_W):
    r = _iota((width, width), 0) // group
    c = _iota((width, width), 1) // group
    return jnp.where(r == c, 1.0 / group, 0.0).astype(BF16)


def _tril_ones(n):
    return jnp.where(_iota((n, n), 0) >= _iota((n, n), 1), 1.0, 0.0).astype(BF16)


def _const_spec(shape):
    nd = len(shape)
    return pl.BlockSpec(shape, lambda *_: (0,) * nd)


def _inproj_kernel(x_ref, w_ref, b_ref, o_rw, o_ml, o_gl, o_rg):
    x = x_ref[...].astype(BF16)
    off = 0
    for o_ref in (o_rw, o_ml, o_gl, o_rg):
        n = o_ref.shape[-1]
        o_ref[...] = (jnp.dot(x, w_ref[:, off:off + n], preferred_element_type=F32)
                      + b_ref[:, off:off + n])
        off += n


def _inproj(x, w, b, tm):
    n = x.shape[0]
    widths = (W_RW, W_ML, W_GL, W_RG)
    return pl.pallas_call(
        _inproj_kernel,
        out_shape=[jax.ShapeDtypeStruct((n, wd), F32) for wd in widths],
        grid=(n // tm,),
        in_specs=[pl.BlockSpec((tm, D_MODEL), lambda i: (i, 0)),
                  _const_spec((D_MODEL, W_PACK)), _const_spec((1, W_PACK))],
        out_specs=[pl.BlockSpec((tm, wd), lambda i: (i, 0)) for wd in widths],
        compiler_params=_cp("parallel"),
        name="inproj",
    )(x, w, b)


def _merge_body(x, outs, wg_ref, bg_ref, wb_ref, wo_ref, g_ref, b_ref, out_ref):
    xb = x.astype(BF16)
    merged = None
    for g, o in enumerate(outs):
        sl = slice(g * D_MODEL, (g + 1) * D_MODEL)
        gate = _sigmoid(jnp.dot(xb, wg_ref[:, sl], preferred_element_type=F32) + bg_ref[:, sl])
        up = jnp.dot(o.astype(BF16), wb_ref[g], preferred_element_type=F32)
        merged = gate * up if merged is None else merged + gate * up
    out = jnp.dot(merged.astype(BF16), wo_ref[...], preferred_element_type=F32)
    out_ref[...] = _layer_norm(DN_ALPHA * x + out, g_ref[...], b_ref[...])


def _merge_kernel(x_ref, oa, ob, oc, od, *rest):
    _merge_body(x_ref[...], [oa[...], ob[...], oc[...], od[...]], *rest)


def _rwkv_out(y, g, bv, ln_ref):
    pm = _head_mean_mat()
    yc = y - _dot_exact_rhs(y, pm)
    var = _dot_exact_rhs(yc * yc, pm)
    return (yc * lax.rsqrt(var + RWKV_GN_EPS) * ln_ref[0:1, :] + ln_ref[1:2, :] + bv) * g


def _merge_rwkv_kernel(x_ref, y_ref, gr_ref, bv_ref, ln_ref, ob, oc, od, *rest):
    oa = _rwkv_out(y_ref[...], gr_ref[...], bv_ref[...], ln_ref)
    _merge_body(x_ref[...], [oa, ob[...], oc[...], od[...]], *rest)


def _merge(x, branches, wg, bg, wb, wo, ln_g, ln_b, tm):
    n = x.shape[0]
    tok = lambda wd: pl.BlockSpec((tm, wd), lambda i: (i, 0))
    if isinstance(branches[0], tuple):
        y, g, bv, ln = branches[0]
        kern, first, first_specs = _merge_rwkv_kernel, (y, g, bv, ln), [tok(BRANCH_W)] * 3 + [_const_spec((2, BRANCH_W))]
    else:
        kern, first, first_specs = _merge_kernel, (branches[0],), [tok(BRANCH_W)]
    return pl.pallas_call(
        kern,
        out_shape=jax.ShapeDtypeStruct((n, D_MODEL), F32),
        grid=(n // tm,),
        in_specs=[tok(D_MODEL)] + first_specs + [tok(BRANCH_W)] * 3 + [
            _const_spec((D_MODEL, N_BRANCH * D_MODEL)), _const_spec((1, N_BRANCH * D_MODEL)),
            _const_spec((N_BRANCH, BRANCH_W, D_MODEL)), _const_spec((D_MODEL, D_MODEL)),
            _const_spec((1, D_MODEL)), _const_spec((1, D_MODEL))],
        out_specs=tok(D_MODEL),
        compiler_params=_cp("parallel"),
        name="merge",
    )(x, *first, *branches[1:], wg, bg, wb, wo, ln_g, ln_b)


FF_CHUNK = D_FF // 2


def _ffn_kernel(x_ref, wg_ref, wu_ref, wd_ref, g_ref, b_ref, out_ref):
    x = x_ref[...]
    xb = x.astype(BF16)
    acc = None
    for c in range(D_FF // FF_CHUNK):
        sl = slice(c * FF_CHUNK, (c + 1) * FF_CHUNK)
        h = (_silu(jnp.dot(xb, wg_ref[:, sl], preferred_element_type=F32))
             * jnp.dot(xb, wu_ref[:, sl], preferred_element_type=F32))
        part = jnp.dot(h.astype(BF16), wd_ref[sl, :], preferred_element_type=F32)
        acc = part if acc is None else acc + part
    out_ref[...] = _layer_norm(DN_ALPHA * x + acc, g_ref[...], b_ref[...])


def _ffn(x, wg, wu, wd, ln_g, ln_b, tm):
    n = x.shape[0]
    tok = pl.BlockSpec((tm, D_MODEL), lambda i: (i, 0))
    return pl.pallas_call(
        _ffn_kernel,
        out_shape=jax.ShapeDtypeStruct((n, D_MODEL), F32),
        grid=(n // tm,),
        in_specs=[tok, _const_spec((D_MODEL, D_FF)), _const_spec((D_MODEL, D_FF)),
                  _const_spec((D_FF, D_MODEL)), _const_spec((1, D_MODEL)), _const_spec((1, D_MODEL))],
        out_specs=tok,
        compiler_params=_cp("parallel"),
        name="ffn",
    )(x, wg, wu, wd, ln_g, ln_b)


(RT_E1, RT_E2, RT_R1, RT_R2, RT_W1, RT_W2) = range(6)
MOE_TILE = 512


def _router_kernel(x_ref, w_ref, b_ref, cin_ref, meta_ref, cnt_ref, carry_ref):
    @pl.when(pl.program_id(0) == 0)
    def _():
        carry_ref[...] = cin_ref[...]

    logits = _dot3(x_ref[...], w_ref[...]) + b_ref[...]
    tm = logits.shape[0]
    lane = _iota(logits.shape, 1)
    neg = jnp.float32(-jnp.inf)
    logits = jnp.where(lane < N_EXPERTS, logits, neg)
    m1 = jnp.max(logits, axis=-1, keepdims=True)
    i1 = jnp.min(jnp.where(logits == m1, lane, 128), axis=-1, keepdims=True)
    rest = jnp.where(lane == i1, neg, logits)
    m2 = jnp.max(rest, axis=-1, keepdims=True)
    i2 = jnp.min(jnp.where(rest == m2, lane, 128), axis=-1, keepdims=True)
    e2 = jnp.exp(m2 - m1)
    w1 = 1.0 / (1.0 + e2)
    w2 = e2 / (1.0 + e2)
    oh1 = lane == i1
    oh2 = lane == i2
    picks = jnp.where(oh1 | oh2, 1.0, 0.0)
    below = jnp.where(_iota((tm, tm), 0) > _iota((tm, tm), 1), 1.0, 0.0).astype(BF16)
    base = carry_ref[...] + jnp.dot(below, picks.astype(BF16), preferred_element_type=F32)
    r1 = jnp.sum(jnp.where(oh1, base, 0.0), axis=-1, keepdims=True)
    r2 = jnp.sum(jnp.where(oh2, base, 0.0), axis=-1, keepdims=True)
    carry_ref[...] = carry_ref[...] + jnp.sum(picks, axis=0, keepdims=True)
    cnt_ref[...] = carry_ref[...]
    meta = jnp.zeros(logits.shape, F32)
    for ln, val in ((RT_E1, i1.astype(F32)), (RT_E2, i2.astype(F32)), (RT_R1, r1), (RT_R2, r2),
                    (RT_W1, w1), (RT_W2, w2)):
        meta = jnp.where(lane == ln, val, meta)
    meta_ref[...] = meta


def _router(x, w, b, cnt_in, tm):
    n = x.shape[0]
    return pl.pallas_call(
        _router_kernel,
        out_shape=[jax.ShapeDtypeStruct((n, 128), F32), jax.ShapeDtypeStruct((1, 128), F32)],
        grid=(n // tm,),
        in_specs=[pl.BlockSpec((tm, D_MODEL), lambda i: (i, 0)),
                  _const_spec((D_MODEL, 128)), _const_spec((1, 128)), _const_spec((1, 128))],
        out_specs=[pl.BlockSpec((tm, 128), lambda i: (i, 0)), _const_spec((1, 128))],
        scratch_shapes=[pltpu.VMEM((1, 128), F32)],
        compiler_params=_cp("arbitrary"),
        name="router",
    )(x, w, b, cnt_in)


def _row_copies(n_rows, start_one, wait_shape_src, wait_shape_dst, sem, n_streams):
    def body(r, carry):
        start_one(r)
        return carry
    lax.fori_loop(0, n_rows, body, 0, unroll=8)
    for _ in range(n_streams):
        pltpu.make_async_copy(wait_shape_src, wait_shape_dst, sem).wait()


def _dispatch_kernel(d1_ref, d2_ref, x_ref, xs_in_ref, xs_ref, sem):
    del xs_in_ref
    tm = x_ref.shape[0]

    def start_one(r):
        src = x_ref.at[pl.ds(r, 1)]
        pltpu.make_async_copy(src, xs_ref.at[pl.ds(d1_ref[0, 0, r], 1)], sem).start(priority=0)
        pltpu.make_async_copy(src, xs_ref.at[pl.ds(d2_ref[0, 0, r], 1)], sem).start(priority=1)
    _row_copies(tm, start_one, x_ref, xs_ref.at[pl.ds(0, tm)], sem, 2)


def _dispatch(x, d1, d2, xs, tm):
    n = x.shape[0]
    idx = lambda a: a.reshape(n // tm, 1, tm)
    ispec = pl.BlockSpec((1, 1, tm), lambda i: (i, 0, 0), memory_space=pltpu.SMEM)
    return pl.pallas_call(
        _dispatch_kernel,
        out_shape=jax.ShapeDtypeStruct(xs.shape, F32),
        grid=(n // tm,),
        in_specs=[ispec, ispec, pl.BlockSpec((tm, D_MODEL), lambda i: (i, 0)),
                  pl.BlockSpec(memory_space=pl.ANY)],
        out_specs=pl.BlockSpec(memory_space=pl.ANY),
        scratch_shapes=[pltpu.SemaphoreType.DMA],
        input_output_aliases={3: 0},
        compiler_params=_cp("arbitrary"),
        name="moe_dispatch",
    )(idx(d1), idx(d2), x, xs)


def _experts_kernel(te_ref, nu_ref, xs_ref, wg_ref, wu_ref, wd_ref, ys_ref):
    del te_ref

    @pl.when(pl.program_id(0) < nu_ref[0])
    def _():
        xb = xs_ref[...].astype(BF16)
        acc = None
        for c in range(D_FF // FF_CHUNK):
            sl = slice(c * FF_CHUNK, (c + 1) * FF_CHUNK)
            h = (_silu(jnp.dot(xb, wg_ref[:, sl], preferred_element_type=F32))
                 * jnp.dot(xb, wu_ref[:, sl], preferred_element_type=F32))
            part = jnp.dot(h.astype(BF16), wd_ref[sl, :], preferred_element_type=F32)
            acc = part if acc is None else acc + part
        ys_ref[...] = acc

    @pl.when(pl.program_id(0) >= nu_ref[0])
    def _():
        ys_ref[...] = jnp.zeros_like(ys_ref)


def _experts(xs, tile_expert, n_used, wg, wu, wd):
    n_tiles = xs.shape[0] // MOE_TILE
    row = pl.BlockSpec((MOE_TILE, D_MODEL), lambda i, te, nu: (i, 0))
    return pl.pallas_call(
        _experts_kernel,
        out_shape=jax.ShapeDtypeStruct(xs.shape, F32),
        grid_spec=pltpu.PrefetchScalarGridSpec(
            num_scalar_prefetch=2, grid=(n_tiles,),
            in_specs=[row,
                      pl.BlockSpec((None, D_MODEL, D_FF), lambda i, te, nu: (te[i], 0, 0)),
                      pl.BlockSpec((None, D_MODEL, D_FF), lambda i, te, nu: (te[i], 0, 0)),
                      pl.BlockSpec((None, D_FF, D_MODEL), lambda i, te, nu: (te[i], 0, 0))],
            out_specs=row),
        compiler_params=_cp("arbitrary"),
        name="moe_experts",
    )(tile_expert, n_used, xs, wg, wu, wd)


def _combine_kernel(d1_ref, d2_ref, x_ref, meta_ref, ys_ref, g_ref, b_ref, out_ref, y1_ref, y2_ref, sem):
    tm = x_ref.shape[0]

    def start_one(r):
        pltpu.make_async_copy(ys_ref.at[pl.ds(d1_ref[0, 0, r], 1)], y1_ref.at[pl.ds(r, 1)], sem).start(priority=0)
        pltpu.make_async_copy(ys_ref.at[pl.ds(d2_ref[0, 0, r], 1)], y2_ref.at[pl.ds(r, 1)], sem).start(priority=1)
    _row_copies(tm, start_one, ys_ref.at[pl.ds(0, tm)], y1_ref, sem, 2)
    meta = meta_ref[...]
    f = meta[:, RT_W1:RT_W1 + 1] * y1_ref[...] + meta[:, RT_W2:RT_W2 + 1] * y2_ref[...]
    out_ref[...] = _layer_norm(DN_ALPHA * x_ref[...] + f, g_ref[...], b_ref[...])


def _combine(x, meta, d1, d2, ys, ln_g, ln_b, tm):
    n = x.shape[0]
    idx = lambda a: a.reshape(n // tm, 1, tm)
    ispec = pl.BlockSpec((1, 1, tm), lambda i: (i, 0, 0), memory_space=pltpu.SMEM)
    tok = lambda wd_: pl.BlockSpec((tm, wd_), lambda i: (i, 0))
    return pl.pallas_call(
        _combine_kernel,
        out_shape=jax.ShapeDtypeStruct((n, D_MODEL), F32),
        grid=(n // tm,),
        in_specs=[ispec, ispec, tok(D_MODEL), tok(128), pl.BlockSpec(memory_space=pl.ANY),
                  _const_spec((1, D_MODEL)), _const_spec((1, D_MODEL))],
        out_specs=tok(D_MODEL),
        scratch_shapes=[pltpu.VMEM((tm, D_MODEL), F32), pltpu.VMEM((tm, D_MODEL), F32),
                        pltpu.SemaphoreType.DMA],
        compiler_params=_cp("arbitrary"),
        name="moe_combine",
    )(idx(d1), idx(d2), x, meta, ys, ln_g, ln_b)


def _moe_layer(xs_groups, router_w, router_b, wg, wu, wd, ln_g, ln_b):
    tms = [min(512, x.shape[0]) for x in xs_groups]
    cnt = jnp.zeros((1, 128), F32)
    metas = []
    for x, tm in zip(xs_groups, tms):
        meta, cnt = _router(x, router_w, router_b, cnt, tm)
        metas.append(meta)
    n_total = sum(x.shape[0] for x in xs_groups)
    n_rows = -(-(2 * n_total + N_EXPERTS * (MOE_TILE - 1)) // MOE_TILE) * MOE_TILE
    n_tiles = n_rows // MOE_TILE
    counts = cnt[0, :N_EXPERTS].astype(jnp.int32)
    padded = (counts + MOE_TILE - 1) // MOE_TILE * MOE_TILE
    ends = jnp.cumsum(padded)
    starts = ends - padded
    tile_expert = jnp.minimum(
        jnp.sum((jnp.arange(n_tiles, dtype=jnp.int32)[:, None] * MOE_TILE >= ends[None, :]).astype(jnp.int32), axis=1),
        N_EXPERTS - 1).astype(jnp.int32)
    n_used = (ends[N_EXPERTS - 1] // MOE_TILE).astype(jnp.int32).reshape(1)
    xs = jnp.zeros((n_rows, D_MODEL), F32)
    dests = []
    for x, meta, tm in zip(xs_groups, metas, tms):
        e = meta[:, RT_E1:RT_E2 + 1].astype(jnp.int32)
        base = jnp.sum(jnp.where(e[:, :, None] == jnp.arange(N_EXPERTS, dtype=jnp.int32), starts, 0), axis=-1)
        d = base + meta[:, RT_R1:RT_R2 + 1].astype(jnp.int32)
        dests.append((d[:, 0], d[:, 1]))
        xs = _dispatch(x, d[:, 0], d[:, 1], xs, tm)
    ys = _experts(xs, tile_expert, n_used, wg, wu, wd)
    outs = []
    for x, meta, (d1, d2), tm in zip(xs_groups, metas, dests, tms):
        outs.append(_combine(x, meta, d1, d2, ys, ln_g, ln_b, tm))
    return outs


def _shift_rows(x, s, fill):
    rolled = pltpu.roll(x, s, 0)
    return jnp.where(_iota(x.shape, 0) >= s, rolled, fill)


def _causal_conv4(ext_ref, x, cw_ref, cb_ref, lc):
    ext_ref[8:8 + lc, :] = x
    out = cb_ref[...] + cw_ref[3:4, :] * x
    for j in range(3):
        out = out + cw_ref[j:j + 1, :] * ext_ref[5 + j:5 + j + lc, :]
    ext_ref[0:8, :] = ext_ref[lc:lc + 8, :]
    return out


RG_CHUNK = 256


def _rglru_kernel(u_ref, cw_ref, cb_ref, wa_ref, ba_ref, wx_ref, bx_ref, lam_ref,
                  y_ref, h_ref, ext_ref, hc_ref):
    lc = RG_CHUNK

    @pl.when(pl.program_id(1) == 0)
    def _():
        ext_ref[0:8, :] = jnp.zeros((8, BRANCH_W), F32)
        hc_ref[...] = jnp.zeros_like(hc_ref)

    xc = _causal_conv4(ext_ref, u_ref[0, :, 0:BRANCH_W], cw_ref, cb_ref, lc)
    r = _sigmoid(_dot(xc, wa_ref[...]) + ba_ref[...])
    i = _sigmoid(_dot(xc, wx_ref[...]) + bx_ref[...])
    log_a = -RG_C * r * _softplus(-lam_ref[...])
    a = jnp.exp(log_a)
    u = jnp.sqrt(1.0 - jnp.exp(2.0 * log_a)) * (i * xc)
    s = 1
    while s < lc:
        u = u + a * _shift_rows(u, s, 0.0)
        a = a * _shift_rows(a, s, 1.0)
        s *= 2
    h = u + a * hc_ref[...]
    hc_ref[...] = h[lc - 1:lc, :]
    h_ref[0] = h[lc - 1:lc, :]
    y_ref[0] = h * _gelu_tanh(u_ref[0, :, BRANCH_W:2 * BRANCH_W])


def _rglru_prompt(u, cw, cb, wa, ba, wx, bx, lam):
    bsz, seq, _ = u.shape
    lc = RG_CHUNK
    return pl.pallas_call(
        _rglru_kernel,
        out_shape=[jax.ShapeDtypeStruct((bsz, seq, BRANCH_W), F32),
                   jax.ShapeDtypeStruct((bsz, 1, BRANCH_W), F32)],
        grid=(bsz, seq // lc),
        in_specs=[pl.BlockSpec((1, lc, W_RG), lambda b, c: (b, c, 0)),
                  _const_spec((4, BRANCH_W)), _const_spec((1, BRANCH_W)),
                  _const_spec((BRANCH_W, BRANCH_W)), _const_spec((1, BRANCH_W)),
                  _const_spec((BRANCH_W, BRANCH_W)), _const_spec((1, BRANCH_W)),
                  _const_spec((1, BRANCH_W))],
        out_specs=[pl.BlockSpec((1, lc, BRANCH_W), lambda b, c: (b, c, 0)),
                   pl.BlockSpec((1, 1, BRANCH_W), lambda b, c: (b, 0, 0))],
        scratch_shapes=[pltpu.VMEM((lc + 8, BRANCH_W), F32), pltpu.VMEM((1, BRANCH_W), F32)],
        compiler_params=_cp("parallel", "arbitrary"),
        name="rglru",
    )(u, cw, cb, wa, ba, wx, bx, lam)


ML_CHUNK = 128


ML_SEQS = 2


def _mlstm_kernel(u_ref, cw_ref, cb_ref, nw_ref, o_ref, c_ref, n_ref, m_ref, ext_ref):
    @pl.when(pl.program_id(1) == 0)
    def _():
        ext_ref[:, 0:8, :] = jnp.zeros((ext_ref.shape[0], 8, 2 * BRANCH_W), F32)
        c_ref[...] = jnp.zeros_like(c_ref)
        n_ref[...] = jnp.zeros_like(n_ref)
        m_ref[...] = jnp.zeros_like(m_ref)

    for sq in range(u_ref.shape[0]):
        _mlstm_chunk(sq, u_ref, cw_ref, cb_ref, nw_ref, o_ref, c_ref, n_ref, m_ref, ext_ref.at[sq])


def _mlstm_chunk(sq, u_ref, cw_ref, cb_ref, nw_ref, o_ref, c_ref, n_ref, m_ref, ext_ref):
    lc = ML_CHUNK
    W = BRANCH_W
    qk = _silu(_causal_conv4(ext_ref, u_ref[sq, :, 0:2 * W], cw_ref, cb_ref, lc))
    q = qk[:, 0:W]
    k = qk[:, W:2 * W] * (HEAD_W ** -0.5)
    v = u_ref[sq, :, 2 * W:3 * W]
    og = u_ref[sq, :, 3 * W:4 * W]
    ipre = u_ref[sq, :, 4 * W:4 * W + 128]
    logf = _log_sigmoid(u_ref[sq, :, 4 * W + 128:4 * W + 256])
    bcum = _dot_exact_lhs(_tril_ones(lc), logf)
    rowsrc = ipre - bcum
    rows_t = rowsrc.T
    lane = _iota((1, W), 1) // HEAD_W
    hq = _iota((lc, W), 1) // HEAD_W
    cmat = c_ref[sq]
    nrow = n_ref[sq]
    kb = k.astype(BF16)
    vb = v.astype(BF16)
    stack = lambda f: jnp.concatenate([f(h) for h in range(HEADS)], axis=0)
    b_col = stack(lambda h: bcum[:, h:h + 1])
    m_prev = stack(lambda h: jnp.broadcast_to(m_ref[sq, h:h + 1, 0:1], (lc, 1)))
    causal = (_iota((HEADS * lc, lc), 0) % lc) >= _iota((HEADS * lc, lc), 1)
    log_d = jnp.where(causal, b_col + stack(lambda h: jnp.broadcast_to(rows_t[h:h + 1, :], (lc, lc))), -jnp.inf)
    log_inter = b_col + m_prev
    m_t = jnp.maximum(log_inter, jnp.max(log_d, axis=-1, keepdims=True))
    dmat = jnp.exp(log_d - m_t)
    inter = jnp.exp(log_inter - m_t)
    qs = stack(lambda h: jnp.where(hq == h, q, 0.0))
    s = _dot_nt(qs, kb) * dmat
    num = _dot(s, vb) + inter * _dot_nt(qs, cmat)
    dot = (jnp.sum(s, axis=-1, keepdims=True)
           + inter * jnp.sum(qs * nrow, axis=-1, keepdims=True))
    hh = num / jnp.maximum(jnp.abs(dot), jnp.exp(-m_t))
    h_acc = jnp.zeros((lc, W), F32)
    ws_full = jnp.zeros((lc, W), F32)
    keep_full = jnp.zeros((1, W), F32)
    for h in range(HEADS):
        hm = lane == h
        h_acc = h_acc + jnp.where(hm, hh[h * lc:(h + 1) * lc, :], 0.0)
        m_new = m_t[(h + 1) * lc - 1:(h + 1) * lc, :]
        b_last = bcum[lc - 1:lc, h:h + 1]
        ws_col = jnp.exp(rowsrc[:, h:h + 1] + (b_last - m_new))
        keep = jnp.exp(b_last + m_ref[sq, h:h + 1, 0:1] - m_new)
        ws_full = ws_full + jnp.where(hm, ws_col, 0.0)
        keep_full = keep_full + jnp.where(hm, keep, 0.0)
        m_ref[sq, h:h + 1, :] = jnp.broadcast_to(m_new, (1, 128))
    blk = (_iota((W, W), 0) // HEAD_W) == (_iota((W, W), 1) // HEAD_W)
    vw = v * ws_full
    c_ref[sq] = keep_full * cmat + jnp.where(blk, _dot_tn(vw, kb), 0.0)
    n_ref[sq] = keep_full * nrow + jnp.sum(ws_full * k, axis=0, keepdims=True)
    pm = _head_mean_mat()
    mu = _dot_exact_rhs(h_acc, pm)
    hc = h_acc - mu
    var = _dot_exact_rhs(hc * hc, pm)
    o_ref[sq] = hc * lax.rsqrt(var + 1e-6) * nw_ref[...] * _sigmoid(og)


def _mlstm_prompt(u, cw, cb, nw):
    bsz, seq, _ = u.shape
    lc = ML_CHUNK
    W = BRANCH_W
    return pl.pallas_call(
        _mlstm_kernel,
        out_shape=[jax.ShapeDtypeStruct((bsz, seq, W), F32),
                   jax.ShapeDtypeStruct((bsz, W, W), F32),
                   jax.ShapeDtypeStruct((bsz, 1, W), F32),
                   jax.ShapeDtypeStruct((bsz, 8, 128), F32)],
        grid=(bsz // ML_SEQS, seq // lc),
        in_specs=[pl.BlockSpec((ML_SEQS, lc, W_ML), lambda b, c: (b, c, 0)),
                  _const_spec((4, 2 * W)), _const_spec((1, 2 * W)), _const_spec((1, W))],
        out_specs=[pl.BlockSpec((ML_SEQS, lc, W), lambda b, c: (b, c, 0)),
                   pl.BlockSpec((ML_SEQS, W, W), lambda b, c: (b, 0, 0)),
                   pl.BlockSpec((ML_SEQS, 1, W), lambda b, c: (b, 0, 0)),
                   pl.BlockSpec((ML_SEQS, 8, 128), lambda b, c: (b, 0, 0))],
        scratch_shapes=[pltpu.VMEM((ML_SEQS, lc + 8, 2 * W), F32)],
        compiler_params=_cp("parallel", "arbitrary"),
        name="mlstm",
    )(u, cw, cb, nw)


GLA_CHUNK = 128
GLA_QK = HEADS * GLA_DK


def _gla_level_ref(b, m, lc):
    if 2 * m == lc:
        return jnp.broadcast_to(b[m - 1:m, :], b.shape)
    if m >= 4:
        nb = lc // (2 * m)
        mid = b.reshape(nb, 2 * m, b.shape[1])[:, m - 1:m, :]
        return jnp.broadcast_to(mid, (nb, 2 * m, b.shape[1])).reshape(b.shape)
    pos = _iota(b.shape, 0) % (2 * m)
    out = b
    for p in range(2 * m):
        sh = p - (m - 1)
        if sh != 0:
            out = jnp.where(pos == p, pltpu.roll(b, sh % lc, 0), out)
    return out


GLA_SEQS = 2


def _gla_kernel(u_ref, au_ref, ab_ref, nw_ref, o_ref, s_ref):
    @pl.when(pl.program_id(1) == 0)
    def _():
        s_ref[...] = jnp.zeros_like(s_ref)

    for sq in range(u_ref.shape[0]):
        _gla_chunk(sq, u_ref, au_ref, ab_ref, nw_ref, o_ref, s_ref)


def _gla_chunk(sq, u_ref, au_ref, ab_ref, nw_ref, o_ref, s_ref):
    lc = GLA_CHUNK
    W = BRANCH_W
    q = u_ref[sq, :, 0:GLA_QK] * (GLA_DK ** -0.5)
    k = u_ref[sq, :, GLA_QK:2 * GLA_QK]
    v = u_ref[sq, :, 2 * GLA_QK:2 * GLA_QK + W]
    g = u_ref[sq, :, 2 * GLA_QK + W:2 * GLA_QK + 2 * W]
    ac = u_ref[sq, :, 2 * GLA_QK + 2 * W:2 * GLA_QK + 2 * W + 128]
    la = _log_sigmoid(_dot(ac, au_ref[...]) + ab_ref[...]) / GLA_TAU
    b = _dot_exact_lhs(_tril_ones(lc), la)
    st = s_ref[sq]
    o = _dot_nt(q * jnp.exp(b), st)
    gsum = ((_iota((GLA_QK, W), 0) // GLA_DK) == (_iota((GLA_QK, W), 1) // HEAD_W)).astype(BF16)
    o = o + jnp.dot((q * k).astype(BF16), gsum, preferred_element_type=F32) * v

    hq = _iota((lc, GLA_QK), 1) // GLA_DK
    row1 = _iota((lc, 1), 0)
    rt = _iota((HEADS * lc, lc), 0) % lc
    cs = _iota((HEADS * lc, lc), 1)
    kb_rows = None
    attn = jnp.zeros((HEADS * lc, lc), F32)
    m = lc // 2
    while m >= 1:
        ref = _gla_level_ref(b, m, lc)
        upper = (row1 % (2 * m)) >= m
        qt = q * jnp.exp(jnp.where(upper, b - ref, -jnp.inf))
        kt = k * jnp.exp(jnp.where(upper, -jnp.inf, ref - b))
        qs = jnp.concatenate([jnp.where(hq == h, qt, 0.0) for h in range(HEADS)], axis=0)
        part = _dot_nt(qs, kt)
        if 2 * m == lc:
            attn = attn + part
        else:
            attn = attn + jnp.where((rt // (2 * m)) == (cs // (2 * m)), part, 0.0)
        m //= 2
    av = _dot(attn, v)
    hv = _iota((lc, W), 1) // HEAD_W
    for h in range(HEADS):
        o = o + jnp.where(hv == h, av[h * lc:(h + 1) * lc, :], 0.0)
    b_last = b[lc - 1:lc, :]
    blk = (_iota((W, GLA_QK), 0) // HEAD_W) == (_iota((W, GLA_QK), 1) // GLA_DK)
    s_ref[sq] = jnp.exp(b_last) * st + jnp.where(blk, _dot_tn(v, k * jnp.exp(b_last - b)), 0.0)
    ms = _dot_exact_rhs(o * o, _head_mean_mat())
    o_ref[sq] = o * lax.rsqrt(ms + 1e-6) * nw_ref[...] * _silu(g)


def _gla_prompt(u, au, ab, nw):
    bsz, seq, _ = u.shape
    lc = GLA_CHUNK
    W = BRANCH_W
    return pl.pallas_call(
        _gla_kernel,
        out_shape=[jax.ShapeDtypeStruct((bsz, seq, W), F32),
                   jax.ShapeDtypeStruct((bsz, W, GLA_QK), F32)],
        grid=(bsz // GLA_SEQS, seq // lc),
        in_specs=[pl.BlockSpec((GLA_SEQS, lc, W_GL), lambda b, c: (b, c, 0)),
                  _const_spec((128, GLA_QK)), _const_spec((1, GLA_QK)), _const_spec((1, W))],
        out_specs=[pl.BlockSpec((GLA_SEQS, lc, W), lambda b, c: (b, c, 0)),
                   pl.BlockSpec((GLA_SEQS, W, GLA_QK), lambda b, c: (b, 0, 0))],
        compiler_params=_cp("parallel", "arbitrary"),
        name="gla",
    )(u, au, ab, nw)


def _head_sum_mat(width=BRANCH_W, group=HEAD_W):
    r = _iota((width, width), 0) // group
    c = _iota((width, width), 1) // group
    return jnp.where(r == c, 1.0, 0.0).astype(BF16)


def _rwkv_pointwise(um, p_ref, lora_ref):
    W = BRANCH_W
    r = um[:, 0:W]
    k = um[:, W:2 * W]
    v = um[:, 2 * W:3 * W]
    lo = um[:, 3 * W:3 * W + 128]
    w_log = -_softplus(-(p_ref[0:1, :] + _dot(jnp.tanh(lo), lora_ref[0]))) - 0.5
    decay = jnp.exp(-jnp.exp(w_log))
    a = _sigmoid(p_ref[1:2, :] + _dot(lo, lora_ref[1]))
    g = _dot(_sigmoid(lo), lora_ref[2])
    hs = _head_sum_mat()
    kk = k * p_ref[2:3, :]
    kk = kk / jnp.maximum(jnp.sqrt(_dot_exact_rhs(kk * kk, hs)), 1e-12)
    k2 = k * (1.0 + (a - 1.0) * p_ref[3:4, :])
    bonus = _dot_exact_rhs(r * k2 * p_ref[4:5, :], hs)
    return decay, kk, kk * a, k2, r, v, g, bonus * v


RW_PREP_T = 128


def _rwkv_prep_kernel(u_ref, mu_ref, p_ref, lora_ref, *rest):
    planes, g_ref, bv_ref, ext_ref, xs_ref = rest[:6], rest[6], rest[7], rest[8], rest[9]
    T = RW_PREP_T
    half = T // 2

    @pl.when(pl.program_id(0) == 0)
    def _():
        ext_ref[:, 0:8, :] = jnp.zeros((ext_ref.shape[0], 8, RWKV_COLS), F32)

    lane = _iota((half, 128), 1)

    def per_sequence(b, carry):
        u = u_ref[b]
        ext_ref[b, 8:8 + T, :] = u
        prev = ext_ref[b, 7:7 + T, :]
        ext_ref[b, 0:8, :] = ext_ref[b, T:T + 8, :]
        um = u + mu_ref[...] * (prev - u)
        dec, kk, kka, k2, r, v, g, bv = _rwkv_pointwise(um, p_ref, lora_ref)
        g_ref[b] = g
        bv_ref[b] = bv
        for z_ref, val in zip(planes, (dec, kk, kka, k2, r, v)):
            for p in range(2):
                xs_ref[p] = val[:, 128 * p:128 * (p + 1)]
            for p in range(2):
                even = xs_ref[p, pl.ds(0, half, stride=2), :]
                odd = xs_ref[p, pl.ds(1, half, stride=2), :]
                z_ref[b * HEADS + 2 * p] = jnp.where(lane < HEAD_W, even, pltpu.roll(odd, HEAD_W, 1))
                z_ref[b * HEADS + 2 * p + 1] = jnp.where(lane < HEAD_W, pltpu.roll(even, HEAD_W, 1), odd)
        return carry

    lax.fori_loop(0, u_ref.shape[0], per_sequence, 0)


def _rwkv_prep(u, mu, p, lora):
    bsz, seq, _ = u.shape
    T = RW_PREP_T
    W = BRANCH_W
    nch = bsz * HEADS
    plane = jax.ShapeDtypeStruct((nch, seq // 2, 128), F32)
    nat = jax.ShapeDtypeStruct((bsz, seq, W), F32)
    return pl.pallas_call(
        _rwkv_prep_kernel,
        out_shape=[plane] * 6 + [nat, nat],
        grid=(seq // T,),
        in_specs=[pl.BlockSpec((bsz, T, RWKV_COLS), lambda i: (0, i, 0)),
                  _const_spec((1, RWKV_COLS)), _const_spec((8, W)), _const_spec((3, 128, W))],
        out_specs=[pl.BlockSpec((nch, T // 2, 128), lambda i: (0, i, 0))] * 6
        + [pl.BlockSpec((bsz, T, W), lambda i: (0, i, 0))] * 2,
        scratch_shapes=[pltpu.VMEM((bsz, T + 8, RWKV_COLS), F32), pltpu.VMEM((2, T, 128), F32)],
        compiler_params=_cp("arbitrary"),
        name="rwkv_prep",
    )(u, mu, p, lora)


def _delta_rule_step(s_ref, ni, w, kk, kka, k, r, v_row, y_store, s_out_ref=None):
    s_out = s_ref if s_out_ref is None else s_out_ref
    sa = [jnp.sum(s_ref[i] * kk, axis=0, keepdims=True) for i in range(ni)]
    for i in range(ni):
        sn = s_ref[i] * w - sa[i] * kka + v_row(i) * k
        s_out[i] = sn
        y_store(i, jnp.sum(sn * r, axis=0, keepdims=True))


RW_SCAN_PAIRS = 16


def _rwkv_scan_kernel(w_ref, kk_ref, kka_ref, k_ref, r_ref, v_ref, y_ref, s_out_ref, s_ref, t_ref):
    nch = w_ref.shape[0]
    rep = 128 // nch
    ni = s_ref.shape[0]

    @pl.when(pl.program_id(0) == 0)
    def _():
        s_ref[...] = jnp.zeros_like(s_ref)

    zs = (w_ref, kk_ref, kka_ref, k_ref, r_ref, v_ref)
    n_pairs = w_ref.shape[1]

    def stage(r, slot):
        for n, z_ref in enumerate(zs):
            rows = z_ref[:, r, :]
            step = ni if n == 5 else 0
            copies = [rows if s * step == 0 else pltpu.roll(rows, 128 - s * step, 1) for s in range(rep)]
            t_ref[slot, n] = jnp.concatenate(copies, axis=0).T

    def tokens(r, slot):
        for t2 in range(2):
            sl = pl.ds(t2 * HEAD_W, HEAD_W)

            def y_store(i, row, t2=t2):
                y_ref[2 * r + t2, i:i + 1, :] = row
            _delta_rule_step(s_ref, ni, t_ref[slot, 0, sl, :], t_ref[slot, 1, sl, :], t_ref[slot, 2, sl, :],
                             t_ref[slot, 3, sl, :], t_ref[slot, 4, sl, :],
                             lambda i, t2=t2: t_ref[slot, 5, pl.ds(t2 * HEAD_W + i, 1), :], y_store)

    stage(0, 0)

    def two_pairs(q, carry):
        stage(2 * q + 1, 1)
        tokens(2 * q, 0)
        stage(jnp.minimum(2 * q + 2, n_pairs - 1), 0)
        tokens(2 * q + 1, 1)
        return carry

    lax.fori_loop(0, n_pairs // 2, two_pairs, 0)

    @pl.when(pl.program_id(0) == pl.num_programs(0) - 1)
    def _():
        s_out_ref[...] = s_ref[...]


def _rwkv_scan(planes, seq):
    nch = planes[0].shape[0]
    ni = HEAD_W * nch // 128
    tp = min(RW_SCAN_PAIRS, seq // 2)
    zspec = pl.BlockSpec((nch, tp, 128), lambda t: (0, t, 0))
    return pl.pallas_call(
        _rwkv_scan_kernel,
        out_shape=[jax.ShapeDtypeStruct((seq, ni, 128), F32), jax.ShapeDtypeStruct((ni, HEAD_W, 128), F32)],
        grid=(seq // (2 * tp),),
        in_specs=[zspec] * 6,
        out_specs=[pl.BlockSpec((2 * tp, ni, 128), lambda t: (t, 0, 0)),
                   pl.BlockSpec((ni, HEAD_W, 128), lambda t: (0, 0, 0))],
        scratch_shapes=[pltpu.VMEM((ni, HEAD_W, 128), F32), pltpu.VMEM((2, 6, 128, 128), F32)],
        compiler_params=_cp("arbitrary"),
        name="rwkv_scan",
    )(*planes)


(PC_W0, PC_A0, PC_KK, PC_KA, PC_RK, PC_LNW, PC_LNB, PC_MLNW, PC_GLNW,
 PC_RCW, PC_RCB, PC_RBA, PC_RBX, PC_RLAM) = (0, 1, 2, 3, 4, 5, 6, 7, 8, 9, 13, 14, 15, 16)
PC_COLS = 17
DEC_B = 128


def _rowsum(x):
    return jnp.sum(x, axis=0, keepdims=True)


def _decode_kernel(
        ur_ref, uk_ref, uv_ref, ulo_ref, pr_ref, pk_ref, pv_ref, plo_ref,
        mur_ref, muk_ref, muv_ref, mulo_ref, lora_ref, srw_ref,
        mq_ref, mk_ref, mv_ref, mo_ref, mi_ref, mf_ref, bq_ref, bk_ref, cwq_ref, cwk_ref,
        c_ref, n_ref, m_ref,
        gq_ref, gk_ref, gv_ref, gg_ref, ga_ref, au_ref, ab_ref, gs_ref,
        rx_ref, ry_ref, rb_ref, wa_ref, wx_ref, h_ref,
        pc_ref,
        oa_ref, ob_ref, oc_ref, od_ref, srw_o, c_o, n_o, m_o, gs_o, h_o,
        y_scr):
    col = lambda j: pc_ref[:, j:j + 1]

    def shift(u_ref, p_ref, mu_ref):
        u = u_ref[...]
        return u + mu_ref[...] * (p_ref[...] - u)
    r = shift(ur_ref, pr_ref, mur_ref)
    k = shift(uk_ref, pk_ref, muk_ref)
    v = shift(uv_ref, pv_ref, muv_ref)
    lo = shift(ulo_ref, plo_ref, mulo_ref)
    w_log = -_softplus(-(col(PC_W0) + _dot(lora_ref[0], jnp.tanh(lo)))) - 0.5
    decay = jnp.exp(-jnp.exp(w_log))
    a = _sigmoid(col(PC_A0) + _dot(lora_ref[1], lo))
    g = _dot(lora_ref[2], _sigmoid(lo))
    kk = k * col(PC_KK)
    kk = kk / jnp.maximum(jnp.sqrt(_rowsum(kk * kk)), 1e-12)
    k2 = k * (1.0 + (a - 1.0) * col(PC_KA))
    bonus = _rowsum(r * k2 * col(PC_RK))

    def y_store(i, row):
        y_scr[i:i + 1, :] = row
    _delta_rule_step(srw_ref, HEAD_W, decay, kk, kk * a, k2, r, lambda i: v[i:i + 1, :], y_store, srw_o)
    y = y_scr[...]
    yc = y - jnp.mean(y, axis=0, keepdims=True)
    var = jnp.mean(yc * yc, axis=0, keepdims=True)
    o = yc * lax.rsqrt(var + RWKV_GN_EPS) * col(PC_LNW) + col(PC_LNB)
    oa_ref[...] = (o + bonus * v) * g

    def conv(u_ref, b_ref, cw_ref):
        out = cw_ref[:, 4:5] + cw_ref[:, 3:4] * u_ref[...]
        for j in range(3):
            out = out + cw_ref[:, j:j + 1] * b_ref[j]
        return out
    q = _silu(conv(mq_ref, bq_ref, cwq_ref))
    k = _silu(conv(mk_ref, bk_ref, cwk_ref)) * (HEAD_W ** -0.5)
    v = mv_ref[...]
    ipre = mi_ref[...]
    logf = _log_sigmoid(mf_ref[...])
    m_prev = m_ref[...]
    m_t = jnp.maximum(logf + m_prev, ipre)
    inter = jnp.exp(logf + m_prev - m_t)
    wsc = jnp.exp(ipre - m_t)
    s = _rowsum(q * k) * wsc
    n_prev = n_ref[...]
    den = jnp.maximum(jnp.abs(s + inter * _rowsum(n_prev * q)), jnp.exp(-m_t))
    for i in range(HEAD_W):
        ci = c_ref[i]
        vi = v[i:i + 1, :]
        y_scr[i:i + 1, :] = (s * vi + inter * _rowsum(ci * q)) / den
        c_o[i] = inter * ci + (wsc * vi) * k
    n_o[...] = inter * n_prev + wsc * k
    m_o[...] = m_t
    y = y_scr[...]
    yc = y - jnp.mean(y, axis=0, keepdims=True)
    var = jnp.mean(yc * yc, axis=0, keepdims=True)
    ob_ref[...] = yc * lax.rsqrt(var + 1e-6) * col(PC_MLNW) * _sigmoid(mo_ref[...])

    q = gq_ref[...] * (GLA_DK ** -0.5)
    k = gk_ref[...]
    v = gv_ref[...]
    eb = jnp.exp(_log_sigmoid(_dot(au_ref[...], ga_ref[...]) + ab_ref[...]) / GLA_TAU)
    attn = _rowsum(q * k)
    qe = q * eb
    for i in range(HEAD_W):
        si = gs_ref[i]
        vi = v[i:i + 1, :]
        y_scr[i:i + 1, :] = attn * vi + _rowsum(si * qe)
        gs_o[i] = eb * si + k * vi
    y = y_scr[...]
    ms = jnp.mean(y * y, axis=0, keepdims=True)
    oc_ref[...] = y * lax.rsqrt(ms + 1e-6) * col(PC_GLNW) * _silu(gg_ref[...])

    xc = col(PC_RCB) + col(PC_RCW + 3) * rx_ref[...]
    for j in range(3):
        xc = xc + col(PC_RCW + j) * rb_ref[j]
    rg = _sigmoid(_dot(wa_ref[...], xc) + col(PC_RBA))
    ig = _sigmoid(_dot(wx_ref[...], xc) + col(PC_RBX))
    log_a = -RG_C * rg * _softplus(-col(PC_RLAM))
    hn = jnp.exp(log_a) * h_ref[...] + jnp.sqrt(1.0 - jnp.exp(2.0 * log_a)) * (ig * xc)
    h_o[...] = hn
    od_ref[...] = hn * _gelu_tanh(ry_ref[...])


def _decode_mixers(ut_rw, prev_t, mu_c, lora_t, s_rw,
                   ut_ml, mconv_t, ml_cw, c_st, n_st, m_st,
                   ut_gl, au_t, ab_c, g_st,
                   ut_rg, rconv_t, wa_t, wx_t, h_st, pcols):
    nb = DEC_B
    H, HW = HEADS, HEAD_W

    def blk(arr, view, block, index):
        a = arr.reshape(view)
        nd = len(block)
        return a, pl.BlockSpec(block, index)

    ins = []
    v14 = (14, HW, nb)
    v7 = (7, 128, nb)
    for arr in (ut_rw, prev_t):
        ins.append(blk(arr, v14, (None, HW, nb), lambda h: (h, 0, 0)))
        ins.append(blk(arr, v14, (None, HW, nb), lambda h: (4 + h, 0, 0)))
        ins.append(blk(arr, v14, (None, HW, nb), lambda h: (8 + h, 0, 0)))
        ins.append(blk(arr, v7, (None, 128, nb), lambda h: (6, 0, 0)))
    ins.append(blk(mu_c, (14, HW, 1), (None, HW, 1), lambda h: (h, 0, 0)))
    ins.append(blk(mu_c, (14, HW, 1), (None, HW, 1), lambda h: (4 + h, 0, 0)))
    ins.append(blk(mu_c, (14, HW, 1), (None, HW, 1), lambda h: (8 + h, 0, 0)))
    ins.append(blk(mu_c, (7, 128, 1), (None, 128, 1), lambda h: (6, 0, 0)))
    ins.append(blk(lora_t, (3, BRANCH_W, 128), (3, HW, 128), lambda h: (0, h, 0)))
    st_spec = lambda: pl.BlockSpec((None, HW, HW, nb), lambda h: (h, 0, 0, 0))
    ins.append((s_rw, st_spec()))
    v20 = (20, HW, nb)
    for j in range(4):
        ins.append(blk(ut_ml, v20, (None, HW, nb), functools.partial(lambda h, j: (4 * j + h, 0, 0), j=j)))
    ins.append(blk(ut_ml[1024:1028], (4, 1, nb), (None, 1, nb), lambda h: (h, 0, 0)))
    ins.append(blk(ut_ml[1152:1156], (4, 1, nb), (None, 1, nb), lambda h: (h, 0, 0)))
    ins.append(blk(mconv_t, (3, 8, HW, nb), (3, None, HW, nb), lambda h: (0, h, 0, 0)))
    ins.append(blk(mconv_t, (3, 8, HW, nb), (3, None, HW, nb), lambda h: (0, 4 + h, 0, 0)))
    ins.append(blk(ml_cw, (8, HW, 5), (None, HW, 5), lambda h: (h, 0, 0)))
    ins.append(blk(ml_cw, (8, HW, 5), (None, HW, 5), lambda h: (4 + h, 0, 0)))
    ins.append((c_st, st_spec()))
    ins.append(blk(n_st, (H, HW, nb), (None, HW, nb), lambda h: (h, 0, 0)))
    ins.append(blk(m_st, (H, 1, nb), (None, 1, nb), lambda h: (h, 0, 0)))
    ins.append(blk(ut_gl, (28, GLA_DK, nb), (None, GLA_DK, nb), lambda h: (h, 0, 0)))
    ins.append(blk(ut_gl, (28, GLA_DK, nb), (None, GLA_DK, nb), lambda h: (4 + h, 0, 0)))
    ins.append(blk(ut_gl, v14, (None, HW, nb), lambda h: (4 + h, 0, 0)))
    ins.append(blk(ut_gl, v14, (None, HW, nb), lambda h: (8 + h, 0, 0)))
    ins.append(blk(ut_gl, v7, (None, 128, nb), lambda h: (6, 0, 0)))
    ins.append(blk(au_t, (H, GLA_DK, 128), (None, GLA_DK, 128), lambda h: (h, 0, 0)))
    ins.append(blk(ab_c, (H, GLA_DK, 1), (None, GLA_DK, 1), lambda h: (h, 0, 0)))
    ins.append((g_st, pl.BlockSpec((None, HW, GLA_DK, nb), lambda h: (h, 0, 0, 0))))
    ins.append(blk(ut_rg, (8, HW, nb), (None, HW, nb), lambda h: (h, 0, 0)))
    ins.append(blk(ut_rg, (8, HW, nb), (None, HW, nb), lambda h: (4 + h, 0, 0)))
    ins.append(blk(rconv_t, (3, H, HW, nb), (3, None, HW, nb), lambda h: (0, h, 0, 0)))
    ins.append((wa_t, pl.BlockSpec((None, HW, HW), lambda h: (h, 0, 0))))
    ins.append((wx_t, pl.BlockSpec((None, HW, HW), lambda h: (h, 0, 0))))
    ins.append(blk(h_st, (H, HW, nb), (None, HW, nb), lambda h: (h, 0, 0)))
    ins.append(blk(pcols, (H, HW, PC_COLS), (None, HW, PC_COLS), lambda h: (h, 0, 0)))

    vec = lambda: (jax.ShapeDtypeStruct((H, HW, nb), F32), pl.BlockSpec((None, HW, nb), lambda h: (h, 0, 0)))
    mat = lambda: (jax.ShapeDtypeStruct((H, HW, HW, nb), F32), st_spec())
    outs = [vec(), vec(), vec(), vec(), mat(), mat(), vec(),
            (jax.ShapeDtypeStruct((H, 1, nb), F32), pl.BlockSpec((None, 1, nb), lambda h: (h, 0, 0))),
            (jax.ShapeDtypeStruct((H, HW, GLA_DK, nb), F32),
             pl.BlockSpec((None, HW, GLA_DK, nb), lambda h: (h, 0, 0, 0))),
            vec()]
    return pl.pallas_call(
        _decode_kernel,
        out_shape=[o[0] for o in outs],
        grid=(H,),
        in_specs=[s for _, s in ins],
        out_specs=[o[1] for o in outs],
        scratch_shapes=[pltpu.VMEM((HW, nb), F32)],
        compiler_params=_cp("parallel"),
        name="decode_mixers",
    )(*[a for a, _ in ins])


def _pad_cols(a, width):
    return jnp.pad(a, ((0, 0), (0, width - a.shape[1])))


def _pad_rows(a, rows, at=0):
    return jnp.pad(a, ((at, rows - at - a.shape[0]), (0, 0)))


def _pack_layer(p, l):
    offs = np.concatenate([[0], np.cumsum(IN_SIZES)])
    w_in, b_in = p['w_in'][l], p['b_in'][l][None, :]
    seg = lambda a, i: a[:, int(offs[i]):int(offs[i + 1])]

    def regroup(a):
        return jnp.concatenate(
            [seg(a, 0), seg(a, 1), seg(a, 2), seg(a, 3), _pad_cols(seg(a, 4), 128), _pad_cols(seg(a, 5), 128),
             seg(a, 6), seg(a, 7), seg(a, 8), seg(a, 9), _pad_cols(seg(a, 10), 128), seg(a, 11), seg(a, 12)],
            axis=1)
    k = dict(
        w_pack=regroup(w_in).astype(BF16), b_pack=regroup(b_in),
        w_gate=seg(w_in, 13).astype(BF16), b_gate=seg(b_in, 13),
        w_branch=p['w_branch'][l].astype(BF16), w_out=p['w_out'][l].astype(BF16),
        ln1_g=p['ln1_g'][l][None], ln1_b=p['ln1_b'][l][None],
        ln2_g=p['ln2_g'][l][None], ln2_b=p['ln2_b'][l][None],
    )
    w_up, a_up, g_up = p['rwkv_w_up'][l], p['rwkv_a_up'][l], p['rwkv_g_up'][l]
    lora = jnp.stack([_pad_rows(w_up, 128, 0), _pad_rows(a_up, 128, 32), _pad_rows(g_up, 128, 64)])
    k['rw_lora'] = lora.astype(BF16)
    k['rw_lora_t'] = jnp.swapaxes(lora, 1, 2).astype(BF16)
    k['rw_mu'] = p['rwkv_mu'][l][None]
    rw_rows = [p['rwkv_w0'][l], p['rwkv_a0'][l], p['rwkv_k_k'][l], p['rwkv_k_a'][l],
               p['rwkv_r_k'][l].reshape(BRANCH_W)]
    k['rw_p'] = jnp.stack(rw_rows + [jnp.zeros((BRANCH_W,), F32)] * 3)
    k['rw_ln'] = jnp.stack([p['rwkv_ln_w'][l], p['rwkv_ln_b'][l]])
    k['ml_cw'], k['ml_cb'] = p['mlstm_conv_w'][l], p['mlstm_conv_b'][l][None]
    k['ml_nw'] = p['mlstm_norm_w'][l][None]
    k['gl_au'] = _pad_rows(p['gla_alpha_up'][l], 128).astype(BF16)
    k['gl_ab'] = p['gla_alpha_b'][l][None]
    k['gl_nw'] = p['gla_norm_w'][l][None]
    wa, wx = p['rglru_wa'][l], p['rglru_wx'][l]
    eye = jnp.eye(HEADS, dtype=F32)
    bd = lambda w: jnp.einsum('gh,gij->gihj', eye, w).reshape(BRANCH_W, BRANCH_W).astype(BF16)
    k['rg_cw'], k['rg_cb'] = p['rglru_conv_w'][l], p['rglru_conv_b'][l][None]
    k['rg_wa'], k['rg_wx'] = bd(wa), bd(wx)
    k['rg_ba'], k['rg_bx'] = p['rglru_ba'][l][None], p['rglru_bx'][l][None]
    k['rg_lam'] = p['rglru_lambda'][l][None]
    k['rg_wa_t'] = jnp.swapaxes(wa, 1, 2).astype(BF16)
    k['rg_wx_t'] = jnp.swapaxes(wx, 1, 2).astype(BF16)
    cols = rw_rows + [p['rwkv_ln_w'][l], p['rwkv_ln_b'][l], p['mlstm_norm_w'][l], p['gla_norm_w'][l]]
    cols += [p['rglru_conv_w'][l][j] for j in range(4)]
    cols += [p['rglru_conv_b'][l], p['rglru_ba'][l], p['rglru_bx'][l], p['rglru_lambda'][l]]
    k['pcols'] = jnp.stack(cols, axis=1)
    k['ml_cw_t'] = jnp.concatenate([p['mlstm_conv_w'][l].T, p['mlstm_conv_b'][l][:, None]], axis=1)
    return k


def _diag_blocks(a, rb, cb):
    return jnp.stack([a[:, h * rb:(h + 1) * rb, h * cb:(h + 1) * cb] for h in range(HEADS)], axis=1)


def _prompt_mixers(x, k):
    B, L, _ = x.shape
    W = BRANCH_W
    u_rw, u_ml, u_gl, u_rg = _inproj(x.reshape(B * L, D_MODEL), k['w_pack'], k['b_pack'], 512)
    u_rw, u_ml, u_gl, u_rg = (u.reshape(B, L, -1) for u in (u_rw, u_ml, u_gl, u_rg))
    nch = B * HEADS
    rep = 128 // nch
    *planes, g, bv = _rwkv_prep(u_rw, k['rw_mu'], k['rw_p'], k['rw_lora'])
    y, s1 = _rwkv_scan(planes, L)
    y = y.reshape(L, HEAD_W // rep, rep, B, HEADS).transpose(3, 0, 4, 2, 1).reshape(B, L, W)
    s_rw = s1.reshape(HEAD_W // rep, HEAD_W, rep, B, HEADS).transpose(3, 4, 2, 0, 1).reshape(B, HEADS, HEAD_W, HEAD_W)
    flat = lambda a: a.reshape(B * L, W)
    o_a = (flat(y), flat(g), flat(bv), k['rw_ln'])
    o_b, c_bd, n_row, m_row = _mlstm_prompt(u_ml, k['ml_cw'], k['ml_cb'], k['ml_nw'])
    o_c, s_bd = _gla_prompt(u_gl, k['gl_au'], k['gl_ab'], k['gl_nw'])
    o_d, h1 = _rglru_prompt(u_rg, k['rg_cw'], k['rg_cb'], k['rg_wa'], k['rg_ba'], k['rg_wx'], k['rg_bx'],
                            k['rg_lam'])
    states = (u_rw[:, L - 1], s_rw, u_ml[:, L - 3:, :2 * W], _diag_blocks(c_bd, HEAD_W, HEAD_W),
              n_row.reshape(B, HEADS, HEAD_W), m_row[:, :HEADS, 0],
              jnp.swapaxes(_diag_blocks(s_bd, HEAD_W, GLA_DK), 2, 3), u_rg[:, L - 3:, :W], h1[:, 0])
    return [o_a] + [flat(o) for o in (o_b, o_c, o_d)], states


def _sample_mixers(x, st, k):
    W = BRANCH_W
    sh0, S0, mconv0, C0, n0, m0, gS0, rconv0, h0 = st
    u_rw, u_ml, u_gl, u_rg = _inproj(x, k['w_pack'], k['b_pack'], DEC_B)
    outs = _decode_mixers(
        u_rw.T, sh0.T, k['rw_mu'].T, k['rw_lora_t'], S0.transpose(1, 2, 3, 0),
        u_ml.T, mconv0.transpose(1, 2, 0), k['ml_cw_t'], C0.transpose(1, 2, 3, 0), n0.transpose(1, 2, 0), m0.T,
        u_gl.T, k['gl_au'].T, k['gl_ab'].T, gS0.transpose(1, 3, 2, 0),
        u_rg.T, rconv0.transpose(1, 2, 0), k['rg_wa_t'], k['rg_wx_t'], h0.T, k['pcols'])
    oa, ob, oc, od, s_rw, c_st, n_st, m_st, g_st, h_st = outs
    branches = [o.reshape(W, DEC_B).T for o in (oa, ob, oc, od)]
    states = (u_rw, s_rw.transpose(3, 0, 1, 2),
              jnp.concatenate([mconv0[:, 1:], u_ml[:, None, :2 * W]], axis=1),
              c_st.transpose(3, 0, 1, 2), n_st.transpose(2, 0, 1), m_st[:, 0, :].T,
              g_st.transpose(3, 0, 2, 1),
              jnp.concatenate([rconv0[:, 1:], u_rg[:, None, :W]], axis=1), h_st.reshape(W, DEC_B).T)
    return branches, states


def _trunk(xp, bl, xs, sample_states, packs, moe):
    tms = (512, DEC_B)
    new_p, new_s = [], []
    for l in range(DEPTH):
        k = packs[l]
        br_p, st_p = _prompt_mixers(xp.reshape(bl[0], bl[1], D_MODEL), k)
        br_s, st_s = _sample_mixers(xs, tuple(s[l] for s in sample_states), k)
        new_p.append(st_p)
        new_s.append(st_s)
        xp, xs = (_merge(x, br, k['w_gate'], k['b_gate'], k['w_branch'], k['w_out'], k['ln1_g'], k['ln1_b'], tm)
                  for x, br, tm in ((xp, br_p, tms[0]), (xs, br_s, tms[1])))
        j = l // 2
        if l % 2 == 0:
            xp, xs = (_ffn(x, moe['ffn_wg'][j], moe['ffn_wu'][j], moe['ffn_wd'][j], k['ln2_g'], k['ln2_b'], tm)
                      for x, tm in ((xp, tms[0]), (xs, tms[1])))
        else:
            xp, xs = _moe_layer([xp, xs], moe['router'][j], moe['router_b'][j], moe['moe_wg'][j],
                                moe['moe_wu'][j], moe['moe_wd'][j], k['ln2_g'], k['ln2_b'])
    stack = lambda sts: [jnp.stack([st[i] for st in sts], axis=0) for i in range(9)]
    return xp, xs, stack(new_p), stack(new_s)


def kernel(x_prompt, x_sample, state_rwkv_shift, state_rwkv_S, state_mlstm_conv, state_mlstm_C,
           state_mlstm_n, state_mlstm_m, state_gla_S, state_rglru_conv, state_rglru_h,
           w_in, b_in, rwkv_mu, rwkv_w0, rwkv_w_up, rwkv_a0, rwkv_a_up, rwkv_g_up, rwkv_k_k,
           rwkv_k_a, rwkv_r_k, rwkv_ln_w, rwkv_ln_b, mlstm_conv_w, mlstm_conv_b, mlstm_norm_w,
           gla_alpha_up, gla_alpha_b, gla_norm_w, rglru_conv_w, rglru_conv_b, rglru_wa, rglru_ba,
           rglru_wx, rglru_bx, rglru_lambda, w_branch, w_out, ln1_g, ln1_b, ffn_wg, ffn_wu, ffn_wd,
           moe_router, moe_router_b, moe_wg, moe_wu, moe_wd, ln2_g, ln2_b):
    p = dict(w_in=w_in, b_in=b_in, rwkv_mu=rwkv_mu, rwkv_w0=rwkv_w0, rwkv_w_up=rwkv_w_up,
             rwkv_a0=rwkv_a0, rwkv_a_up=rwkv_a_up, rwkv_g_up=rwkv_g_up, rwkv_k_k=rwkv_k_k,
             rwkv_k_a=rwkv_k_a, rwkv_r_k=rwkv_r_k, rwkv_ln_w=rwkv_ln_w, rwkv_ln_b=rwkv_ln_b,
             mlstm_conv_w=mlstm_conv_w, mlstm_conv_b=mlstm_conv_b, mlstm_norm_w=mlstm_norm_w,
             gla_alpha_up=gla_alpha_up, gla_alpha_b=gla_alpha_b, gla_norm_w=gla_norm_w,
             rglru_conv_w=rglru_conv_w, rglru_conv_b=rglru_conv_b, rglru_wa=rglru_wa,
             rglru_ba=rglru_ba, rglru_wx=rglru_wx, rglru_bx=rglru_bx, rglru_lambda=rglru_lambda,
             w_branch=w_branch, w_out=w_out, ln1_g=ln1_g, ln1_b=ln1_b, ln2_g=ln2_g, ln2_b=ln2_b)
    packs = [_pack_layer(p, l) for l in range(DEPTH)]
    moe = dict(ffn_wg=ffn_wg.astype(BF16), ffn_wu=ffn_wu.astype(BF16), ffn_wd=ffn_wd.astype(BF16),
               router=jnp.pad(moe_router, ((0, 0), (0, 0), (0, 128 - N_EXPERTS))),
               router_b=jnp.pad(moe_router_b, ((0, 0), (0, 128 - N_EXPERTS)))[:, None, :],
               moe_wg=moe_wg.astype(BF16), moe_wu=moe_wu.astype(BF16), moe_wd=moe_wd.astype(BF16))
    B, L, _ = x_prompt.shape
    sample_states = (state_rwkv_shift, state_rwkv_S, state_mlstm_conv, state_mlstm_C, state_mlstm_n,
                     state_mlstm_m, state_gla_S, state_rglru_conv, state_rglru_h)
    nb, ls, _ = x_sample.shape
    y_p, y_s, ps, ss = _trunk(x_prompt.reshape(B * L, D_MODEL), (B, L), x_sample.reshape(nb * ls, D_MODEL),
                              sample_states, packs, moe)
    return (y_p.reshape(B, L, D_MODEL), y_s.reshape(nb, ls, D_MODEL), *ps, *ss)
```

```python
import functools

import jax
import jax.numpy as jnp
import numpy as np
from jax import lax
from jax.experimental import pallas as pl
from jax.experimental.pallas import tpu as pltpu

F32 = jnp.float32
BF16 = jnp.bfloat16

D_MODEL = 1024
DEPTH = 2
N_BRANCH = 4
BRANCH_W = 256
HEADS = 4
HEAD_W = 64
RWKV_COLS = 896
RWKV_GN_EPS = 64e-5
GLA_DK = 32
GLA_LORA = 16
GLA_TAU = 16.0
RG_C = 8.0
D_FF = 2816
N_EXPERTS = 8
DN_ALPHA = (2 * DEPTH) ** 0.25
LN_EPS = 1e-5

IN_SIZES = (RWKV_COLS, 512, 256, 256, 4, 4, 128, 128, 256, 256, 16, 256, 256, 4096)

W_RW = 896
W_ML = 1280
W_GL = 896
W_RG = 512
W_PACK = W_RW + W_ML + W_GL + W_RG

VMEM_LIMIT = 56 * 1024 * 1024


def _cp(*sem):
    return pltpu.CompilerParams(dimension_semantics=sem, vmem_limit_bytes=VMEM_LIMIT)


def _dot(a, b):
    return jnp.dot(a.astype(BF16), b.astype(BF16), preferred_element_type=F32)


def _dot_nt(a, b):
    return lax.dot_general(a.astype(BF16), b.astype(BF16), (((1,), (1,)), ((), ())),
                           preferred_element_type=F32)


def _dot_tn(a, b):
    return lax.dot_general(a.astype(BF16), b.astype(BF16), (((0,), (0,)), ((), ())),
                           preferred_element_type=F32)


def _split(x):
    hi = x.astype(BF16)
    lo = (x - hi.astype(F32)).astype(BF16)
    return hi, lo


def _dot_exact_lhs(a, x):
    hi, lo = _split(x)
    a = a.astype(BF16)
    return (jnp.dot(a, hi, preferred_element_type=F32) + jnp.dot(a, lo, preferred_element_type=F32))


def _dot_exact_rhs(x, a):
    hi, lo = _split(x)
    a = a.astype(BF16)
    return (jnp.dot(hi, a, preferred_element_type=F32) + jnp.dot(lo, a, preferred_element_type=F32))


def _dot3(x, w):
    xh, xl = _split(x)
    wh, wl = _split(w)
    return (jnp.dot(xh, wh, preferred_element_type=F32) + jnp.dot(xl, wh, preferred_element_type=F32)
            + jnp.dot(xh, wl, preferred_element_type=F32))


def _sigmoid(x):
    return 1.0 / (1.0 + jnp.exp(-x))


def _softplus(x):
    return jnp.maximum(x, 0.0) + jnp.log(1.0 + jnp.exp(-jnp.abs(x)))


def _log_sigmoid(x):
    return -_softplus(-x)


def _silu(x):
    return x * _sigmoid(x)


def _gelu_tanh(x):
    c = np.float32(np.sqrt(2.0 / np.pi))
    return 0.5 * x * (1.0 + jnp.tanh(c * (x + 0.044715 * (x * x * x))))


def _layer_norm(y, g, b):
    mu = jnp.mean(y, axis=-1, keepdims=True)
    yc = y - mu
    var = jnp.mean(yc * yc, axis=-1, keepdims=True)
    return yc * lax.rsqrt(var + LN_EPS) * g + b


def _iota(shape, dim):
    return lax.broadcasted_iota(jnp.int32, shape, dim)


def _head_mean_mat(width=BRANCH_W, group=HEAD_W):
    r = _iota((width, width), 0) // group
    c = _iota((width, width), 1) // group
    return jnp.where(r == c, 1.0 / group, 0.0).astype(BF16)


def _tril_ones(n):
    return jnp.where(_iota((n, n), 0) >= _iota((n, n), 1), 1.0, 0.0).astype(BF16)


def _const_spec(shape):
    nd = len(shape)
    return pl.BlockSpec(shape, lambda *_: (0,) * nd)


def _inproj_kernel(x_ref, w_ref, b_ref, o_rw, o_ml, o_gl, o_rg):
    x = x_ref[...].astype(BF16)
    off = 0
    for o_ref in (o_rw, o_ml, o_gl, o_rg):
        n = o_ref.shape[-1]
        o_ref[...] = (jnp.dot(x, w_ref[:, off:off + n], preferred_element_type=F32)
                      + b_ref[:, off:off + n])
        off += n


def _inproj(x, w, b, tm):
    n = x.shape[0]
    widths = (W_RW, W_ML, W_GL, W_RG)
    return pl.pallas_call(
        _inproj_kernel,
        out_shape=[jax.ShapeDtypeStruct((n, wd), F32) for wd in widths],
        grid=(n // tm,),
        in_specs=[pl.BlockSpec((tm, D_MODEL), lambda i: (i, 0)),
                  _const_spec((D_MODEL, W_PACK)), _const_spec((1, W_PACK))],
        out_specs=[pl.BlockSpec((tm, wd), lambda i: (i, 0)) for wd in widths],
        compiler_params=_cp("parallel"),
        name="inproj",
    )(x, w, b)


def _merge_body(x, outs, wg_ref, bg_ref, wb_ref, wo_ref, g_ref, b_ref, out_ref):
    xb = x.astype(BF16)
    merged = None
    for g, o in enumerate(outs):
        sl = slice(g * D_MODEL, (g + 1) * D_MODEL)
        gate = _sigmoid(jnp.dot(xb, wg_ref[:, sl], preferred_element_type=F32) + bg_ref[:, sl])
        up = jnp.dot(o.astype(BF16), wb_ref[g], preferred_element_type=F32)
        merged = gate * up if merged is None else merged + gate * up
    out = jnp.dot(merged.astype(BF16), wo_ref[...], preferred_element_type=F32)
    out_ref[...] = _layer_norm(DN_ALPHA * x + out, g_ref[...], b_ref[...])


def _merge_kernel(x_ref, oa, ob, oc, od, *rest):
    _merge_body(x_ref[...], [oa[...], ob[...], oc[...], od[...]], *rest)


def _rwkv_out(y, g, bv, ln_ref):
    pm = _head_mean_mat()
    yc = y - _dot_exact_rhs(y, pm)
    var = _dot_exact_rhs(yc * yc, pm)
    return (yc * lax.rsqrt(var + RWKV_GN_EPS) * ln_ref[0:1, :] + ln_ref[1:2, :] + bv) * g


def _merge_rwkv_kernel(x_ref, y_ref, gr_ref, bv_ref, ln_ref, ob, oc, od, *rest):
    oa = _rwkv_out(y_ref[...], gr_ref[...], bv_ref[...], ln_ref)
    _merge_body(x_ref[...], [oa, ob[...], oc[...], od[...]], *rest)


def _merge(x, branches, wg, bg, wb, wo, ln_g, ln_b, tm):
    n = x.shape[0]
    tok = lambda wd: pl.BlockSpec((tm, wd), lambda i: (i, 0))
    if isinstance(branches[0], tuple):
        y, g, bv, ln = branches[0]
        kern, first, first_specs = _merge_rwkv_kernel, (y, g, bv, ln), [tok(BRANCH_W)] * 3 + [_const_spec((2, BRANCH_W))]
    else:
        kern, first, first_specs = _merge_kernel, (branches[0],), [tok(BRANCH_W)]
    return pl.pallas_call(
        kern,
        out_shape=jax.ShapeDtypeStruct((n, D_MODEL), F32),
        grid=(n // tm,),
        in_specs=[tok(D_MODEL)] + first_specs + [tok(BRANCH_W)] * 3 + [
            _const_spec((D_MODEL, N_BRANCH * D_MODEL)), _const_spec((1, N_BRANCH * D_MODEL)),
            _const_spec((N_BRANCH, BRANCH_W, D_MODEL)), _const_spec((D_MODEL, D_MODEL)),
            _const_spec((1, D_MODEL)), _const_spec((1, D_MODEL))],
        out_specs=tok(D_MODEL),
        compiler_params=_cp("parallel"),
        name="merge",
    )(x, *first, *branches[1:], wg, bg, wb, wo, ln_g, ln_b)


FF_CHUNK = D_FF // 2


def _ffn_kernel(x_ref, wg_ref, wu_ref, wd_ref, g_ref, b_ref, out_ref):
    x = x_ref[...]
    xb = x.astype(BF16)
    acc = None
    for c in range(D_FF // FF_CHUNK):
        sl = slice(c * FF_CHUNK, (c + 1) * FF_CHUNK)
        h = (_silu(jnp.dot(xb, wg_ref[:, sl], preferred_element_type=F32))
             * jnp.dot(xb, wu_ref[:, sl], preferred_element_type=F32))
        part = jnp.dot(h.astype(BF16), wd_ref[sl, :], preferred_element_type=F32)
        acc = part if acc is None else acc + part
    out_ref[...] = _layer_norm(DN_ALPHA * x + acc, g_ref[...], b_ref[...])


def _ffn(x, wg, wu, wd, ln_g, ln_b, tm):
    n = x.shape[0]
    tok = pl.BlockSpec((tm, D_MODEL), lambda i: (i, 0))
    return pl.pallas_call(
        _ffn_kernel,
        out_shape=jax.ShapeDtypeStruct((n, D_MODEL), F32),
        grid=(n // tm,),
        in_specs=[tok, _const_spec((D_MODEL, D_FF)), _const_spec((D_MODEL, D_FF)),
                  _const_spec((D_FF, D_MODEL)), _const_spec((1, D_MODEL)), _const_spec((1, D_MODEL))],
        out_specs=tok,
        compiler_params=_cp("parallel"),
        name="ffn",
    )(x, wg, wu, wd, ln_g, ln_b)


(RT_E1, RT_E2, RT_R1, RT_R2, RT_W1, RT_W2) = range(6)
MOE_TILE = 512


def _router_kernel(x_ref, w_ref, b_ref, cin_ref, meta_ref, cnt_ref, carry_ref):
    @pl.when(pl.program_id(0) == 0)
    def _():
        carry_ref[...] = cin_ref[...]

    logits = _dot3(x_ref[...], w_ref[...]) + b_ref[...]
    tm = logits.shape[0]
    lane = _iota(logits.shape, 1)
    neg = jnp.float32(-jnp.inf)
    logits = jnp.where(lane < N_EXPERTS, logits, neg)
    m1 = jnp.max(logits, axis=-1, keepdims=True)
    i1 = jnp.min(jnp.where(logits == m1, lane, 128), axis=-1, keepdims=True)
    rest = jnp.where(lane == i1, neg, logits)
    m2 = jnp.max(rest, axis=-1, keepdims=True)
    i2 = jnp.min(jnp.where(rest == m2, lane, 128), axis=-1, keepdims=True)
    e2 = jnp.exp(m2 - m1)
    w1 = 1.0 / (1.0 + e2)
    w2 = e2 / (1.0 + e2)
    oh1 = lane == i1
    oh2 = lane == i2
    picks = jnp.where(oh1 | oh2, 1.0, 0.0)
    below = jnp.where(_iota((tm, tm), 0) > _iota((tm, tm), 1), 1.0, 0.0).astype(BF16)
    base = carry_ref[...] + jnp.dot(below, picks.astype(BF16), preferred_element_type=F32)
    r1 = jnp.sum(jnp.where(oh1, base, 0.0), axis=-1, keepdims=True)
    r2 = jnp.sum(jnp.where(oh2, base, 0.0), axis=-1, keepdims=True)
    carry_ref[...] = carry_ref[...] + jnp.sum(picks, axis=0, keepdims=True)
    cnt_ref[...] = carry_ref[...]
    meta = jnp.zeros(logits.shape, F32)
    for ln, val in ((RT_E1, i1.astype(F32)), (RT_E2, i2.astype(F32)), (RT_R1, r1), (RT_R2, r2),
                    (RT_W1, w1), (RT_W2, w2)):
        meta = jnp.where(lane == ln, val, meta)
    meta_ref[...] = meta


def _router(x, w, b, cnt_in, tm):
    n = x.shape[0]
    return pl.pallas_call(
        _router_kernel,
        out_shape=[jax.ShapeDtypeStruct((n, 128), F32), jax.ShapeDtypeStruct((1, 128), F32)],
        grid=(n // tm,),
        in_specs=[pl.BlockSpec((tm, D_MODEL), lambda i: (i, 0)),
                  _const_spec((D_MODEL, 128)), _const_spec((1, 128)), _const_spec((1, 128))],
        out_specs=[pl.BlockSpec((tm, 128), lambda i: (i, 0)), _const_spec((1, 128))],
        scratch_shapes=[pltpu.VMEM((1, 128), F32)],
        compiler_params=_cp("arbitrary"),
        name="router",
    )(x, w, b, cnt_in)


def _row_copies(n_rows, start_one, wait_shape_src, wait_shape_dst, sem, n_streams):
    def body(r, carry):
        start_one(r)
        return carry
    lax.fori_loop(0, n_rows, body, 0, unroll=8)
    for _ in range(n_streams):
        pltpu.make_async_copy(wait_shape_src, wait_shape_dst, sem).wait()


def _dispatch_kernel(d1_ref, d2_ref, x_ref, xs_in_ref, xs_ref, sem):
    del xs_in_ref
    tm = x_ref.shape[0]

    def start_one(r):
        src = x_ref.at[pl.ds(r, 1)]
        pltpu.make_async_copy(src, xs_ref.at[pl.ds(d1_ref[0, 0, r], 1)], sem).start(priority=0)
        pltpu.make_async_copy(src, xs_ref.at[pl.ds(d2_ref[0, 0, r], 1)], sem).start(priority=1)
    _row_copies(tm, start_one, x_ref, xs_ref.at[pl.ds(0, tm)], sem, 2)


def _dispatch(x, d1, d2, xs, tm):
    n = x.shape[0]
    idx = lambda a: a.reshape(n // tm, 1, tm)
    ispec = pl.BlockSpec((1, 1, tm), lambda i: (i, 0, 0), memory_space=pltpu.SMEM)
    return pl.pallas_call(
        _dispatch_kernel,
        out_shape=jax.ShapeDtypeStruct(xs.shape, F32),
        grid=(n // tm,),
        in_specs=[ispec, ispec, pl.BlockSpec((tm, D_MODEL), lambda i: (i, 0)),
                  pl.BlockSpec(memory_space=pl.ANY)],
        out_specs=pl.BlockSpec(memory_space=pl.ANY),
        scratch_shapes=[pltpu.SemaphoreType.DMA],
        input_output_aliases={3: 0},
        compiler_params=_cp("arbitrary"),
        name="moe_dispatch",
    )(idx(d1), idx(d2), x, xs)


def _experts_kernel(te_ref, nu_ref, xs_ref, wg_ref, wu_ref, wd_ref, ys_ref):
    del te_ref

    @pl.when(pl.program_id(0) < nu_ref[0])
    def _():
        xb = xs_ref[...].astype(BF16)
        acc = None
        for c in range(D_FF // FF_CHUNK):
            sl = slice(c * FF_CHUNK, (c + 1) * FF_CHUNK)
            h = (_silu(jnp.dot(xb, wg_ref[:, sl], preferred_element_type=F32))
                 * jnp.dot(xb, wu_ref[:, sl], preferred_element_type=F32))
            part = jnp.dot(h.astype(BF16), wd_ref[sl, :], preferred_element_type=F32)
            acc = part if acc is None else acc + part
        ys_ref[...] = acc

    @pl.when(pl.program_id(0) >= nu_ref[0])
    def _():
        ys_ref[...] = jnp.zeros_like(ys_ref)


def _experts(xs, tile_expert, n_used, wg, wu, wd):
    n_tiles = xs.shape[0] // MOE_TILE
    row = pl.BlockSpec((MOE_TILE, D_MODEL), lambda i, te, nu: (i, 0))
    return pl.pallas_call(
        _experts_kernel,
        out_shape=jax.ShapeDtypeStruct(xs.shape, F32),
        grid_spec=pltpu.PrefetchScalarGridSpec(
            num_scalar_prefetch=2, grid=(n_tiles,),
            in_specs=[row,
                      pl.BlockSpec((None, D_MODEL, D_FF), lambda i, te, nu: (te[i], 0, 0)),
                      pl.BlockSpec((None, D_MODEL, D_FF), lambda i, te, nu: (te[i], 0, 0)),
                      pl.BlockSpec((None, D_FF, D_MODEL), lambda i, te, nu: (te[i], 0, 0))],
            out_specs=row),
        compiler_params=_cp("arbitrary"),
        name="moe_experts",
    )(tile_expert, n_used, xs, wg, wu, wd)


def _combine_kernel(d1_ref, d2_ref, x_ref, meta_ref, ys_ref, g_ref, b_ref, out_ref, y1_ref, y2_ref, sem):
    tm = x_ref.shape[0]

    def start_one(r):
        pltpu.make_async_copy(ys_ref.at[pl.ds(d1_ref[0, 0, r], 1)], y1_ref.at[pl.ds(r, 1)], sem).start(priority=0)
        pltpu.make_async_copy(ys_ref.at[pl.ds(d2_ref[0, 0, r], 1)], y2_ref.at[pl.ds(r, 1)], sem).start(priority=1)
    _row_copies(tm, start_one, ys_ref.at[pl.ds(0, tm)], y1_ref, sem, 2)
    meta = meta_ref[...]
    f = meta[:, RT_W1:RT_W1 + 1] * y1_ref[...] + meta[:, RT_W2:RT_W2 + 1] * y2_ref[...]
    out_ref[...] = _layer_norm(DN_ALPHA * x_ref[...] + f, g_ref[...], b_ref[...])


def _combine(x, meta, d1, d2, ys, ln_g, ln_b, tm):
    n = x.shape[0]
    idx = lambda a: a.reshape(n // tm, 1, tm)
    ispec = pl.BlockSpec((1, 1, tm), lambda i: (i, 0, 0), memory_space=pltpu.SMEM)
    tok = lambda wd_: pl.BlockSpec((tm, wd_), lambda i: (i, 0))
    return pl.pallas_call(
        _combine_kernel,
        out_shape=jax.ShapeDtypeStruct((n, D_MODEL), F32),
        grid=(n // tm,),
        in_specs=[ispec, ispec, tok(D_MODEL), tok(128), pl.BlockSpec(memory_space=pl.ANY),
                  _const_spec((1, D_MODEL)), _const_spec((1, D_MODEL))],
        out_specs=tok(D_MODEL),
        scratch_shapes=[pltpu.VMEM((tm, D_MODEL), F32), pltpu.VMEM((tm, D_MODEL), F32),
                        pltpu.SemaphoreType.DMA],
        compiler_params=_cp("arbitrary"),
        name="moe_combine",
    )(idx(d1), idx(d2), x, meta, ys, ln_g, ln_b)


def _moe_layer(xs_groups, router_w, router_b, wg, wu, wd, ln_g, ln_b):
    tms = [min(512, x.shape[0]) for x in xs_groups]
    cnt = jnp.zeros((1, 128), F32)
    metas = []
    for x, tm in zip(xs_groups, tms):
        meta, cnt = _router(x, router_w, router_b, cnt, tm)
        metas.append(meta)
    n_total = sum(x.shape[0] for x in xs_groups)
    n_rows = -(-(2 * n_total + N_EXPERTS * (MOE_TILE - 1)) // MOE_TILE) * MOE_TILE
    n_tiles = n_rows // MOE_TILE
    counts = cnt[0, :N_EXPERTS].astype(jnp.int32)
    padded = (counts + MOE_TILE - 1) // MOE_TILE * MOE_TILE
    ends = jnp.cumsum(padded)
    starts = ends - padded
    tile_expert = jnp.minimum(
        jnp.sum((jnp.arange(n_tiles, dtype=jnp.int32)[:, None] * MOE_TILE >= ends[None, :]).astype(jnp.int32), axis=1),
        N_EXPERTS - 1).astype(jnp.int32)
    n_used = (ends[N_EXPERTS - 1] // MOE_TILE).astype(jnp.int32).reshape(1)
    xs = jnp.zeros((n_rows, D_MODEL), F32)
    dests = []
    for x, meta, tm in zip(xs_groups, metas, tms):
        e = meta[:, RT_E1:RT_E2 + 1].astype(jnp.int32)
        base = jnp.sum(jnp.where(e[:, :, None] == jnp.arange(N_EXPERTS, dtype=jnp.int32), starts, 0), axis=-1)
        d = base + meta[:, RT_R1:RT_R2 + 1].astype(jnp.int32)
        dests.append((d[:, 0], d[:, 1]))
        xs = _dispatch(x, d[:, 0], d[:, 1], xs, tm)
    ys = _experts(xs, tile_expert, n_used, wg, wu, wd)
    outs = []
    for x, meta, (d1, d2), tm in zip(xs_groups, metas, dests, tms):
        outs.append(_combine(x, meta, d1, d2, ys, ln_g, ln_b, tm))
    return outs


def _shift_rows(x, s, fill):
    rolled = pltpu.roll(x, s, 0)
    return jnp.where(_iota(x.shape, 0) >= s, rolled, fill)


def _causal_conv4(ext_ref, x, cw_ref, cb_ref, lc):
    ext_ref[8:8 + lc, :] = x
    out = cb_ref[...] + cw_ref[3:4, :] * x
    for j in range(3):
        out = out + cw_ref[j:j + 1, :] * ext_ref[5 + j:5 + j + lc, :]
    ext_ref[0:8, :] = ext_ref[lc:lc + 8, :]
    return out


RG_CHUNK = 256


RG_SEQS = 2


def _rglru_kernel(u_ref, cw_ref, cb_ref, wa_ref, ba_ref, wx_ref, bx_ref, lam_ref,
                  y_ref, h_ref, ext_ref, hc_ref):
    lc = RG_CHUNK

    @pl.when(pl.program_id(1) == 0)
    def _():
        ext_ref[:, 0:8, :] = jnp.zeros((ext_ref.shape[0], 8, BRANCH_W), F32)
        hc_ref[...] = jnp.zeros_like(hc_ref)

    for sq in range(u_ref.shape[0]):
        xc = _causal_conv4(ext_ref.at[sq], u_ref[sq, :, 0:BRANCH_W], cw_ref, cb_ref, lc)
        r = _sigmoid(_dot(xc, wa_ref[...]) + ba_ref[...])
        i = _sigmoid(_dot(xc, wx_ref[...]) + bx_ref[...])
        log_a = -RG_C * r * _softplus(-lam_ref[...])
        a = jnp.exp(log_a)
        u = jnp.sqrt(1.0 - jnp.exp(2.0 * log_a)) * (i * xc)
        s = 1
        while s < lc:
            u = u + a * _shift_rows(u, s, 0.0)
            a = a * _shift_rows(a, s, 1.0)
            s *= 2
        h = u + a * hc_ref[sq]
        hc_ref[sq] = h[lc - 1:lc, :]
        h_ref[sq] = h[lc - 1:lc, :]
        y_ref[sq] = h * _gelu_tanh(u_ref[sq, :, BRANCH_W:2 * BRANCH_W])


def _rglru_prompt(u, cw, cb, wa, ba, wx, bx, lam):
    bsz, seq, _ = u.shape
    lc = RG_CHUNK
    ns = RG_SEQS
    return pl.pallas_call(
        _rglru_kernel,
        out_shape=[jax.ShapeDtypeStruct((bsz, seq, BRANCH_W), F32),
                   jax.ShapeDtypeStruct((bsz, 1, BRANCH_W), F32)],
        grid=(bsz // ns, seq // lc),
        in_specs=[pl.BlockSpec((ns, lc, W_RG), lambda b, c: (b, c, 0)),
                  _const_spec((4, BRANCH_W)), _const_spec((1, BRANCH_W)),
                  _const_spec((BRANCH_W, BRANCH_W)), _const_spec((1, BRANCH_W)),
                  _const_spec((BRANCH_W, BRANCH_W)), _const_spec((1, BRANCH_W)),
                  _const_spec((1, BRANCH_W))],
        out_specs=[pl.BlockSpec((ns, lc, BRANCH_W), lambda b, c: (b, c, 0)),
                   pl.BlockSpec((ns, 1, BRANCH_W), lambda b, c: (b, 0, 0))],
        scratch_shapes=[pltpu.VMEM((ns, lc + 8, BRANCH_W), F32), pltpu.VMEM((ns, 1, BRANCH_W), F32)],
        compiler_params=_cp("parallel", "arbitrary"),
        name="rglru",
    )(u, cw, cb, wa, ba, wx, bx, lam)


ML_CHUNK = 128


ML_SEQS = 2


def _mlstm_kernel(u_ref, cw_ref, cb_ref, nw_ref, o_ref, c_ref, n_ref, m_ref, ext_ref):
    @pl.when(pl.program_id(1) == 0)
    def _():
        ext_ref[:, 0:8, :] = jnp.zeros((ext_ref.shape[0], 8, 2 * BRANCH_W), F32)
        c_ref[...] = jnp.zeros_like(c_ref)
        n_ref[...] = jnp.zeros_like(n_ref)
        m_ref[...] = jnp.zeros_like(m_ref)

    for sq in range(u_ref.shape[0]):
        _mlstm_chunk(sq, u_ref, cw_ref, cb_ref, nw_ref, o_ref, c_ref, n_ref, m_ref, ext_ref.at[sq])


def _mlstm_chunk(sq, u_ref, cw_ref, cb_ref, nw_ref, o_ref, c_ref, n_ref, m_ref, ext_ref):
    lc = ML_CHUNK
    W = BRANCH_W
    qk = _silu(_causal_conv4(ext_ref, u_ref[sq, :, 0:2 * W], cw_ref, cb_ref, lc))
    q = qk[:, 0:W]
    k = qk[:, W:2 * W] * (HEAD_W ** -0.5)
    v = u_ref[sq, :, 2 * W:3 * W]
    og = u_ref[sq, :, 3 * W:4 * W]
    ipre = u_ref[sq, :, 4 * W:4 * W + 128]
    logf = _log_sigmoid(u_ref[sq, :, 4 * W + 128:4 * W + 256])
    bcum = _dot_exact_lhs(_tril_ones(lc), logf)
    rowsrc = ipre - bcum
    rows_t = rowsrc.T
    lane = _iota((1, W), 1) // HEAD_W
    hq = _iota((lc, W), 1) // HEAD_W
    cmat = c_ref[sq]
    nrow = n_ref[sq]
    kb = k.astype(BF16)
    vb = v.astype(BF16)
    stack = lambda f: jnp.concatenate([f(h) for h in range(HEADS)], axis=0)
    b_col = stack(lambda h: bcum[:, h:h + 1])
    m_prev = stack(lambda h: jnp.broadcast_to(m_ref[sq, h:h + 1, 0:1], (lc, 1)))
    causal = (_iota((HEADS * lc, lc), 0) % lc) >= _iota((HEADS * lc, lc), 1)
    log_d = jnp.where(causal, b_col + stack(lambda h: jnp.broadcast_to(rows_t[h:h + 1, :], (lc, lc))), -jnp.inf)
    log_inter = b_col + m_prev
    m_t = jnp.maximum(log_inter, jnp.max(log_d, axis=-1, keepdims=True))
    dmat = jnp.exp(log_d - m_t)
    inter = jnp.exp(log_inter - m_t)
    qs = stack(lambda h: jnp.where(hq == h, q, 0.0))
    s = _dot_nt(qs, kb) * dmat
    num = _dot(s, vb) + inter * _dot_nt(qs, cmat)
    dot = (jnp.sum(s, axis=-1, keepdims=True)
           + inter * jnp.sum(qs * nrow, axis=-1, keepdims=True))
    hh = num / jnp.maximum(jnp.abs(dot), jnp.exp(-m_t))
    h_acc = jnp.zeros((lc, W), F32)
    ws_full = jnp.zeros((lc, W), F32)
    keep_full = jnp.zeros((1, W), F32)
    for h in range(HEADS):
        hm = lane == h
        h_acc = h_acc + jnp.where(hm, hh[h * lc:(h + 1) * lc, :], 0.0)
        m_new = m_t[(h + 1) * lc - 1:(h + 1) * lc, :]
        b_last = bcum[lc - 1:lc, h:h + 1]
        ws_col = jnp.exp(rowsrc[:, h:h + 1] + (b_last - m_new))
        keep = jnp.exp(b_last + m_ref[sq, h:h + 1, 0:1] - m_new)
        ws_full = ws_full + jnp.where(hm, ws_col, 0.0)
        keep_full = keep_full + jnp.where(hm, keep, 0.0)
        m_ref[sq, h:h + 1, :] = jnp.broadcast_to(m_new, (1, 128))
    blk = (_iota((W, W), 0) // HEAD_W) == (_iota((W, W), 1) // HEAD_W)
    vw = v * ws_full
    c_ref[sq] = keep_full * cmat + jnp.where(blk, _dot_tn(vw, kb), 0.0)
    n_ref[sq] = keep_full * nrow + jnp.sum(ws_full * k, axis=0, keepdims=True)
    pm = _head_mean_mat()
    mu = _dot_exact_rhs(h_acc, pm)
    hc = h_acc - mu
    var = _dot_exact_rhs(hc * hc, pm)
    o_ref[sq] = hc * lax.rsqrt(var + 1e-6) * nw_ref[...] * _sigmoid(og)


def _mlstm_prompt(u, cw, cb, nw):
    bsz, seq, _ = u.shape
    lc = ML_CHUNK
    W = BRANCH_W
    return pl.pallas_call(
        _mlstm_kernel,
        out_shape=[jax.ShapeDtypeStruct((bsz, seq, W), F32),
                   jax.ShapeDtypeStruct((bsz, W, W), F32),
                   jax.ShapeDtypeStruct((bsz, 1, W), F32),
                   jax.ShapeDtypeStruct((bsz, 8, 128), F32)],
        grid=(bsz // ML_SEQS, seq // lc),
        in_specs=[pl.BlockSpec((ML_SEQS, lc, W_ML), lambda b, c: (b, c, 0)),
                  _const_spec((4, 2 * W)), _const_spec((1, 2 * W)), _const_spec((1, W))],
        out_specs=[pl.BlockSpec((ML_SEQS, lc, W), lambda b, c: (b, c, 0)),
                   pl.BlockSpec((ML_SEQS, W, W), lambda b, c: (b, 0, 0)),
                   pl.BlockSpec((ML_SEQS, 1, W), lambda b, c: (b, 0, 0)),
                   pl.BlockSpec((ML_SEQS, 8, 128), lambda b, c: (b, 0, 0))],
        scratch_shapes=[pltpu.VMEM((ML_SEQS, lc + 8, 2 * W), F32)],
        compiler_params=_cp("parallel", "arbitrary"),
        name="mlstm",
    )(u, cw, cb, nw)


GLA_CHUNK = 128
GLA_QK = HEADS * GLA_DK


def _gla_level_ref(b, m, lc):
    if 2 * m == lc:
        return jnp.broadcast_to(b[m - 1:m, :], b.shape)
    if m >= 4:
        nb = lc // (2 * m)
        mid = b.reshape(nb, 2 * m, b.shape[1])[:, m - 1:m, :]
        return jnp.broadcast_to(mid, (nb, 2 * m, b.shape[1])).reshape(b.shape)
    pos = _iota(b.shape, 0) % (2 * m)
    out = b
    for p in range(2 * m):
        sh = p - (m - 1)
        if sh != 0:
            out = jnp.where(pos == p, pltpu.roll(b, sh % lc, 0), out)
    return out


GLA_SEQS = 2


def _gla_kernel(u_ref, au_ref, ab_ref, nw_ref, o_ref, s_ref):
    @pl.when(pl.program_id(1) == 0)
    def _():
        s_ref[...] = jnp.zeros_like(s_ref)

    for sq in range(u_ref.shape[0]):
        _gla_chunk(sq, u_ref, au_ref, ab_ref, nw_ref, o_ref, s_ref)


def _gla_chunk(sq, u_ref, au_ref, ab_ref, nw_ref, o_ref, s_ref):
    lc = GLA_CHUNK
    W = BRANCH_W
    q = u_ref[sq, :, 0:GLA_QK] * (GLA_DK ** -0.5)
    k = u_ref[sq, :, GLA_QK:2 * GLA_QK]
    v = u_ref[sq, :, 2 * GLA_QK:2 * GLA_QK + W]
    g = u_ref[sq, :, 2 * GLA_QK + W:2 * GLA_QK + 2 * W]
    ac = u_ref[sq, :, 2 * GLA_QK + 2 * W:2 * GLA_QK + 2 * W + 128]
    la = _log_sigmoid(_dot(ac, au_ref[...]) + ab_ref[...]) / GLA_TAU
    b = _dot_exact_lhs(_tril_ones(lc), la)
    st = s_ref[sq]
    o = _dot_nt(q * jnp.exp(b), st)
    gsum = ((_iota((GLA_QK, W), 0) // GLA_DK) == (_iota((GLA_QK, W), 1) // HEAD_W)).astype(BF16)
    o = o + jnp.dot((q * k).astype(BF16), gsum, preferred_element_type=F32) * v

    hq = _iota((lc, GLA_QK), 1) // GLA_DK
    row1 = _iota((lc, 1), 0)
    rt = _iota((HEADS * lc, lc), 0) % lc
    cs = _iota((HEADS * lc, lc), 1)
    kb_rows = None
    attn = jnp.zeros((HEADS * lc, lc), F32)
    m = lc // 2
    while m >= 1:
        ref = _gla_level_ref(b, m, lc)
        upper = (row1 % (2 * m)) >= m
        qt = q * jnp.exp(jnp.where(upper, b - ref, -jnp.inf))
        kt = k * jnp.exp(jnp.where(upper, -jnp.inf, ref - b))
        qs = jnp.concatenate([jnp.where(hq == h, qt, 0.0) for h in range(HEADS)], axis=0)
        part = _dot_nt(qs, kt)
        if 2 * m == lc:
            attn = attn + part
        else:
            attn = attn + jnp.where((rt // (2 * m)) == (cs // (2 * m)), part, 0.0)
        m //= 2
    av = _dot(attn, v)
    hv = _iota((lc, W), 1) // HEAD_W
    for h in range(HEADS):
        o = o + jnp.where(hv == h, av[h * lc:(h + 1) * lc, :], 0.0)
    b_last = b[lc - 1:lc, :]
    blk = (_iota((W, GLA_QK), 0) // HEAD_W) == (_iota((W, GLA_QK), 1) // GLA_DK)
    s_ref[sq] = jnp.exp(b_last) * st + jnp.where(blk, _dot_tn(v, k * jnp.exp(b_last - b)), 0.0)
    ms = _dot_exact_rhs(o * o, _head_mean_mat())
    o_ref[sq] = o * lax.rsqrt(ms + 1e-6) * nw_ref[...] * _silu(g)


def _gla_prompt(u, au, ab, nw):
    bsz, seq, _ = u.shape
    lc = GLA_CHUNK
    W = BRANCH_W
    return pl.pallas_call(
        _gla_kernel,
        out_shape=[jax.ShapeDtypeStruct((bsz, seq, W), F32),
                   jax.ShapeDtypeStruct((bsz, W, GLA_QK), F32)],
        grid=(bsz // GLA_SEQS, seq // lc),
        in_specs=[pl.BlockSpec((GLA_SEQS, lc, W_GL), lambda b, c: (b, c, 0)),
                  _const_spec((128, GLA_QK)), _const_spec((1, GLA_QK)), _const_spec((1, W))],
        out_specs=[pl.BlockSpec((GLA_SEQS, lc, W), lambda b, c: (b, c, 0)),
                   pl.BlockSpec((GLA_SEQS, W, GLA_QK), lambda b, c: (b, 0, 0))],
        compiler_params=_cp("parallel", "arbitrary"),
        name="gla",
    )(u, au, ab, nw)


def _head_sum_mat(width=BRANCH_W, group=HEAD_W):
    r = _iota((width, width), 0) // group
    c = _iota((width, width), 1) // group
    return jnp.where(r == c, 1.0, 0.0).astype(BF16)


def _rwkv_pointwise(um, p_ref, lora_ref):
    W = BRANCH_W
    r = um[:, 0:W]
    k = um[:, W:2 * W]
    v = um[:, 2 * W:3 * W]
    lo = um[:, 3 * W:3 * W + 128]
    w_log = -_softplus(-(p_ref[0:1, :] + _dot(jnp.tanh(lo), lora_ref[0]))) - 0.5
    decay = jnp.exp(-jnp.exp(w_log))
    a = _sigmoid(p_ref[1:2, :] + _dot(lo, lora_ref[1]))
    g = _dot(_sigmoid(lo), lora_ref[2])
    hs = _head_sum_mat()
    kk = k * p_ref[2:3, :]
    kk = kk / jnp.maximum(jnp.sqrt(_dot_exact_rhs(kk * kk, hs)), 1e-12)
    k2 = k * (1.0 + (a - 1.0) * p_ref[3:4, :])
    bonus = _dot_exact_rhs(r * k2 * p_ref[4:5, :], hs)
    return decay, kk, kk * a, k2, r, v, g, bonus * v


RW_PREP_T = 128


def _rwkv_prep_kernel(u_ref, mu_ref, p_ref, lora_ref, *rest):
    planes, g_ref, bv_ref, ext_ref, xs_ref = rest[:6], rest[6], rest[7], rest[8], rest[9]
    T = RW_PREP_T
    half = T // 2

    @pl.when(pl.program_id(0) == 0)
    def _():
        ext_ref[:, 0:8, :] = jnp.zeros((ext_ref.shape[0], 8, RWKV_COLS), F32)

    lane = _iota((half, 128), 1)

    def per_sequence(b, carry):
        u = u_ref[b]
        ext_ref[b, 8:8 + T, :] = u
        prev = ext_ref[b, 7:7 + T, :]
        ext_ref[b, 0:8, :] = ext_ref[b, T:T + 8, :]
        um = u + mu_ref[...] * (prev - u)
        dec, kk, kka, k2, r, v, g, bv = _rwkv_pointwise(um, p_ref, lora_ref)
        g_ref[b] = g
        bv_ref[b] = bv
        for z_ref, val in zip(planes, (dec, kk, kka, k2, r, v)):
            for p in range(2):
                xs_ref[p] = val[:, 128 * p:128 * (p + 1)]
            for p in range(2):
                even = xs_ref[p, pl.ds(0, half, stride=2), :]
                odd = xs_ref[p, pl.ds(1, half, stride=2), :]
                z_ref[b * HEADS + 2 * p] = jnp.where(lane < HEAD_W, even, pltpu.roll(odd, HEAD_W, 1))
                z_ref[b * HEADS + 2 * p + 1] = jnp.where(lane < HEAD_W, pltpu.roll(even, HEAD_W, 1), odd)
        return carry

    lax.fori_loop(0, u_ref.shape[0], per_sequence, 0)


def _rwkv_prep(u, mu, p, lora):
    bsz, seq, _ = u.shape
    T = RW_PREP_T
    W = BRANCH_W
    nch = bsz * HEADS
    plane = jax.ShapeDtypeStruct((nch, seq // 2, 128), F32)
    nat = jax.ShapeDtypeStruct((bsz, seq, W), F32)
    return pl.pallas_call(
        _rwkv_prep_kernel,
        out_shape=[plane] * 6 + [nat, nat],
        grid=(seq // T,),
        in_specs=[pl.BlockSpec((bsz, T, RWKV_COLS), lambda i: (0, i, 0)),
                  _const_spec((1, RWKV_COLS)), _const_spec((8, W)), _const_spec((3, 128, W))],
        out_specs=[pl.BlockSpec((nch, T // 2, 128), lambda i: (0, i, 0))] * 6
        + [pl.BlockSpec((bsz, T, W), lambda i: (0, i, 0))] * 2,
        scratch_shapes=[pltpu.VMEM((bsz, T + 8, RWKV_COLS), F32), pltpu.VMEM((2, T, 128), F32)],
        compiler_params=_cp("arbitrary"),
        name="rwkv_prep",
    )(u, mu, p, lora)


def _delta_rule_step(s_ref, ni, w, kk, kka, k, r, v_row, y_store, s_out_ref=None):
    s_out = s_ref if s_out_ref is None else s_out_ref
    sa = [jnp.sum(s_ref[i] * kk, axis=0, keepdims=True) for i in range(ni)]
    for i in range(ni):
        sn = s_ref[i] * w - sa[i] * kka + v_row(i) * k
        s_out[i] = sn
        y_store(i, jnp.sum(sn * r, axis=0, keepdims=True))


RW_SCAN_PAIRS = 16


def _rwkv_scan_kernel(w_ref, kk_ref, kka_ref, k_ref, r_ref, v_ref, y_ref, s_out_ref, s_ref, t_ref):
    nch = w_ref.shape[0]
    rep = 128 // nch
    ni = s_ref.shape[1]
    nacc = 4

    @pl.when(pl.program_id(0) == 0)
    def _():
        s_ref[...] = jnp.zeros_like(s_ref)

    zs = (w_ref, kk_ref, kka_ref, k_ref, r_ref, v_ref)
    n_pairs = w_ref.shape[1]

    def stage(r, slot):
        for n, z_ref in enumerate(zs):
            rows = z_ref[:, r, :]
            step = ni if n == 5 else 0
            copies = [rows if s * step == 0 else pltpu.roll(rows, 128 - s * step, 1) for s in range(rep)]
            t_ref[slot, n] = jnp.concatenate(copies, axis=0).T

    def tokens(r, slot):
        for t2 in range(2):
            base = t2 * HEAD_W
            bc = lambda n, j: jnp.broadcast_to(t_ref[slot, n, pl.ds(base + j, 1), :], (ni, 128))
            parts = [None] * nacc
            for j in range(HEAD_W):
                p = s_ref[j] * bc(1, j)
                parts[j % nacc] = p if parts[j % nacc] is None else parts[j % nacc] + p
            sa = (parts[0] + parts[1]) + (parts[2] + parts[3])
            v = t_ref[slot, 5, pl.ds(base, ni), :]
            parts = [None] * nacc
            for j in range(HEAD_W):
                sn = s_ref[j] * bc(0, j) - sa * bc(2, j) + v * bc(3, j)
                s_ref[j] = sn
                p = sn * bc(4, j)
                parts[j % nacc] = p if parts[j % nacc] is None else parts[j % nacc] + p
            y_ref[2 * r + t2] = (parts[0] + parts[1]) + (parts[2] + parts[3])

    stage(0, 0)

    def two_pairs(q, carry):
        stage(2 * q + 1, 1)
        tokens(2 * q, 0)
        stage(jnp.minimum(2 * q + 2, n_pairs - 1), 0)
        tokens(2 * q + 1, 1)
        return carry

    lax.fori_loop(0, n_pairs // 2, two_pairs, 0)

    @pl.when(pl.program_id(0) == pl.num_programs(0) - 1)
    def _():
        s_out_ref[...] = s_ref[...]


def _rwkv_scan(planes, seq):
    nch = planes[0].shape[0]
    ni = HEAD_W * nch // 128
    tp = min(RW_SCAN_PAIRS, seq // 2)
    zspec = pl.BlockSpec((nch, tp, 128), lambda t: (0, t, 0))
    return pl.pallas_call(
        _rwkv_scan_kernel,
        out_shape=[jax.ShapeDtypeStruct((seq, ni, 128), F32), jax.ShapeDtypeStruct((HEAD_W, ni, 128), F32)],
        grid=(seq // (2 * tp),),
        in_specs=[zspec] * 6,
        out_specs=[pl.BlockSpec((2 * tp, ni, 128), lambda t: (t, 0, 0)),
                   pl.BlockSpec((HEAD_W, ni, 128), lambda t: (0, 0, 0))],
        scratch_shapes=[pltpu.VMEM((HEAD_W, ni, 128), F32), pltpu.VMEM((2, 6, 128, 128), F32)],
        compiler_params=_cp("arbitrary"),
        name="rwkv_scan",
    )(*planes)


(PC_W0, PC_A0, PC_KK, PC_KA, PC_RK, PC_LNW, PC_LNB, PC_MLNW, PC_GLNW,
 PC_RCW, PC_RCB, PC_RBA, PC_RBX, PC_RLAM) = (0, 1, 2, 3, 4, 5, 6, 7, 8, 9, 13, 14, 15, 16)
PC_COLS = 17
DEC_B = 128


def _rowsum(x):
    return jnp.sum(x, axis=0, keepdims=True)


def _decode_kernel(
        ur_ref, uk_ref, uv_ref, ulo_ref, pr_ref, pk_ref, pv_ref, plo_ref,
        mur_ref, muk_ref, muv_ref, mulo_ref, lora_ref, srw_ref,
        mq_ref, mk_ref, mv_ref, mo_ref, mi_ref, mf_ref, bq_ref, bk_ref, cwq_ref, cwk_ref,
        c_ref, n_ref, m_ref,
        gq_ref, gk_ref, gv_ref, gg_ref, ga_ref, au_ref, ab_ref, gs_ref,
        rx_ref, ry_ref, rb_ref, wa_ref, wx_ref, h_ref,
        pc_ref,
        oa_ref, ob_ref, oc_ref, od_ref, srw_o, c_o, n_o, m_o, gs_o, h_o,
        y_scr):
    col = lambda j: pc_ref[:, j:j + 1]

    def shift(u_ref, p_ref, mu_ref):
        u = u_ref[...]
        return u + mu_ref[...] * (p_ref[...] - u)
    r = shift(ur_ref, pr_ref, mur_ref)
    k = shift(uk_ref, pk_ref, muk_ref)
    v = shift(uv_ref, pv_ref, muv_ref)
    lo = shift(ulo_ref, plo_ref, mulo_ref)
    w_log = -_softplus(-(col(PC_W0) + _dot(lora_ref[0], jnp.tanh(lo)))) - 0.5
    decay = jnp.exp(-jnp.exp(w_log))
    a = _sigmoid(col(PC_A0) + _dot(lora_ref[1], lo))
    g = _dot(lora_ref[2], _sigmoid(lo))
    kk = k * col(PC_KK)
    kk = kk / jnp.maximum(jnp.sqrt(_rowsum(kk * kk)), 1e-12)
    k2 = k * (1.0 + (a - 1.0) * col(PC_KA))
    bonus = _rowsum(r * k2 * col(PC_RK))

    def y_store(i, row):
        y_scr[i:i + 1, :] = row
    _delta_rule_step(srw_ref, HEAD_W, decay, kk, kk * a, k2, r, lambda i: v[i:i + 1, :], y_store, srw_o)
    y = y_scr[...]
    yc = y - jnp.mean(y, axis=0, keepdims=True)
    var = jnp.mean(yc * yc, axis=0, keepdims=True)
    o = yc * lax.rsqrt(var + RWKV_GN_EPS) * col(PC_LNW) + col(PC_LNB)
    oa_ref[...] = (o + bonus * v) * g

    def conv(u_ref, b_ref, cw_ref):
        out = cw_ref[:, 4:5] + cw_ref[:, 3:4] * u_ref[...]
        for j in range(3):
            out = out + cw_ref[:, j:j + 1] * b_ref[j]
        return out
    q = _silu(conv(mq_ref, bq_ref, cwq_ref))
    k = _silu(conv(mk_ref, bk_ref, cwk_ref)) * (HEAD_W ** -0.5)
    v = mv_ref[...]
    ipre = mi_ref[...]
    logf = _log_sigmoid(mf_ref[...])
    m_prev = m_ref[...]
    m_t = jnp.maximum(logf + m_prev, ipre)
    inter = jnp.exp(logf + m_prev - m_t)
    wsc = jnp.exp(ipre - m_t)
    s = _rowsum(q * k) * wsc
    n_prev = n_ref[...]
    den = jnp.maximum(jnp.abs(s + inter * _rowsum(n_prev * q)), jnp.exp(-m_t))
    for i in range(HEAD_W):
        ci = c_ref[i]
        vi = v[i:i + 1, :]
        y_scr[i:i + 1, :] = (s * vi + inter * _rowsum(ci * q)) / den
        c_o[i] = inter * ci + (wsc * vi) * k
    n_o[...] = inter * n_prev + wsc * k
    m_o[...] = m_t
    y = y_scr[...]
    yc = y - jnp.mean(y, axis=0, keepdims=True)
    var = jnp.mean(yc * yc, axis=0, keepdims=True)
    ob_ref[...] = yc * lax.rsqrt(var + 1e-6) * col(PC_MLNW) * _sigmoid(mo_ref[...])

    q = gq_ref[...] * (GLA_DK ** -0.5)
    k = gk_ref[...]
    v = gv_ref[...]
    eb = jnp.exp(_log_sigmoid(_dot(au_ref[...], ga_ref[...]) + ab_ref[...]) / GLA_TAU)
    attn = _rowsum(q * k)
    qe = q * eb
    for i in range(HEAD_W):
        si = gs_ref[i]
        vi = v[i:i + 1, :]
        y_scr[i:i + 1, :] = attn * vi + _rowsum(si * qe)
        gs_o[i] = eb * si + k * vi
    y = y_scr[...]
    ms = jnp.mean(y * y, axis=0, keepdims=True)
    oc_ref[...] = y * lax.rsqrt(ms + 1e-6) * col(PC_GLNW) * _silu(gg_ref[...])

    xc = col(PC_RCB) + col(PC_RCW + 3) * rx_ref[...]
    for j in range(3):
        xc = xc + col(PC_RCW + j) * rb_ref[j]
    rg = _sigmoid(_dot(wa_ref[...], xc) + col(PC_RBA))
    ig = _sigmoid(_dot(wx_ref[...], xc) + col(PC_RBX))
    log_a = -RG_C * rg * _softplus(-col(PC_RLAM))
    hn = jnp.exp(log_a) * h_ref[...] + jnp.sqrt(1.0 - jnp.exp(2.0 * log_a)) * (ig * xc)
    h_o[...] = hn
    od_ref[...] = hn * _gelu_tanh(ry_ref[...])


def _decode_mixers(ut_rw, prev_t, mu_c, lora_t, s_rw,
                   ut_ml, mconv_t, ml_cw, c_st, n_st, m_st,
                   ut_gl, au_t, ab_c, g_st,
                   ut_rg, rconv_t, wa_t, wx_t, h_st, pcols):
    nb = DEC_B
    H, HW = HEADS, HEAD_W

    def blk(arr, view, block, index):
        a = arr.reshape(view)
        nd = len(block)
        return a, pl.BlockSpec(block, index)

    ins = []
    v14 = (14, HW, nb)
    v7 = (7, 128, nb)
    for arr in (ut_rw, prev_t):
        ins.append(blk(arr, v14, (None, HW, nb), lambda h: (h, 0, 0)))
        ins.append(blk(arr, v14, (None, HW, nb), lambda h: (4 + h, 0, 0)))
        ins.append(blk(arr, v14, (None, HW, nb), lambda h: (8 + h, 0, 0)))
        ins.append(blk(arr, v7, (None, 128, nb), lambda h: (6, 0, 0)))
    ins.append(blk(mu_c, (14, HW, 1), (None, HW, 1), lambda h: (h, 0, 0)))
    ins.append(blk(mu_c, (14, HW, 1), (None, HW, 1), lambda h: (4 + h, 0, 0)))
    ins.append(blk(mu_c, (14, HW, 1), (None, HW, 1), lambda h: (8 + h, 0, 0)))
    ins.append(blk(mu_c, (7, 128, 1), (None, 128, 1), lambda h: (6, 0, 0)))
    ins.append(blk(lora_t, (3, BRANCH_W, 128), (3, HW, 128), lambda h: (0, h, 0)))
    st_spec = lambda: pl.BlockSpec((None, HW, HW, nb), lambda h: (h, 0, 0, 0))
    ins.append((s_rw, st_spec()))
    v20 = (20, HW, nb)
    for j in range(4):
        ins.append(blk(ut_ml, v20, (None, HW, nb), functools.partial(lambda h, j: (4 * j + h, 0, 0), j=j)))
    ins.append(blk(ut_ml[1024:1028], (4, 1, nb), (None, 1, nb), lambda h: (h, 0, 0)))
    ins.append(blk(ut_ml[1152:1156], (4, 1, nb), (None, 1, nb), lambda h: (h, 0, 0)))
    ins.append(blk(mconv_t, (3, 8, HW, nb), (3, None, HW, nb), lambda h: (0, h, 0, 0)))
    ins.append(blk(mconv_t, (3, 8, HW, nb), (3, None, HW, nb), lambda h: (0, 4 + h, 0, 0)))
    ins.append(blk(ml_cw, (8, HW, 5), (None, HW, 5), lambda h: (h, 0, 0)))
    ins.append(blk(ml_cw, (8, HW, 5), (None, HW, 5), lambda h: (4 + h, 0, 0)))
    ins.append((c_st, st_spec()))
    ins.append(blk(n_st, (H, HW, nb), (None, HW, nb), lambda h: (h, 0, 0)))
    ins.append(blk(m_st, (H, 1, nb), (None, 1, nb), lambda h: (h, 0, 0)))
    ins.append(blk(ut_gl, (28, GLA_DK, nb), (None, GLA_DK, nb), lambda h: (h, 0, 0)))
    ins.append(blk(ut_gl, (28, GLA_DK, nb), (None, GLA_DK, nb), lambda h: (4 + h, 0, 0)))
    ins.append(blk(ut_gl, v14, (None, HW, nb), lambda h: (4 + h, 0, 0)))
    ins.append(blk(ut_gl, v14, (None, HW, nb), lambda h: (8 + h, 0, 0)))
    ins.append(blk(ut_gl, v7, (None, 128, nb), lambda h: (6, 0, 0)))
    ins.append(blk(au_t, (H, GLA_DK, 128), (None, GLA_DK, 128), lambda h: (h, 0, 0)))
    ins.append(blk(ab_c, (H, GLA_DK, 1), (None, GLA_DK, 1), lambda h: (h, 0, 0)))
    ins.append((g_st, pl.BlockSpec((None, HW, GLA_DK, nb), lambda h: (h, 0, 0, 0))))
    ins.append(blk(ut_rg, (8, HW, nb), (None, HW, nb), lambda h: (h, 0, 0)))
    ins.append(blk(ut_rg, (8, HW, nb), (None, HW, nb), lambda h: (4 + h, 0, 0)))
    ins.append(blk(rconv_t, (3, H, HW, nb), (3, None, HW, nb), lambda h: (0, h, 0, 0)))
    ins.append((wa_t, pl.BlockSpec((None, HW, HW), lambda h: (h, 0, 0))))
    ins.append((wx_t, pl.BlockSpec((None, HW, HW), lambda h: (h, 0, 0))))
    ins.append(blk(h_st, (H, HW, nb), (None, HW, nb), lambda h: (h, 0, 0)))
    ins.append(blk(pcols, (H, HW, PC_COLS), (None, HW, PC_COLS), lambda h: (h, 0, 0)))

    vec = lambda: (jax.ShapeDtypeStruct((H, HW, nb), F32), pl.BlockSpec((None, HW, nb), lambda h: (h, 0, 0)))
    mat = lambda: (jax.ShapeDtypeStruct((H, HW, HW, nb), F32), st_spec())
    outs = [vec(), vec(), vec(), vec(), mat(), mat(), vec(),
            (jax.ShapeDtypeStruct((H, 1, nb), F32), pl.BlockSpec((None, 1, nb), lambda h: (h, 0, 0))),
            (jax.ShapeDtypeStruct((H, HW, GLA_DK, nb), F32),
             pl.BlockSpec((None, HW, GLA_DK, nb), lambda h: (h, 0, 0, 0))),
            vec()]
    return pl.pallas_call(
        _decode_kernel,
        out_shape=[o[0] for o in outs],
        grid=(H,),
        in_specs=[s for _, s in ins],
        out_specs=[o[1] for o in outs],
        scratch_shapes=[pltpu.VMEM((HW, nb), F32)],
        compiler_params=_cp("parallel"),
        name="decode_mixers",
    )(*[a for a, _ in ins])


def _pad_cols(a, width):
    return jnp.pad(a, ((0, 0), (0, width - a.shape[1])))


def _pad_rows(a, rows, at=0):
    return jnp.pad(a, ((at, rows - at - a.shape[0]), (0, 0)))


def _pack_layer(p, l):
    offs = np.concatenate([[0], np.cumsum(IN_SIZES)])
    w_in, b_in = p['w_in'][l], p['b_in'][l][None, :]
    seg = lambda a, i: a[:, int(offs[i]):int(offs[i + 1])]

    def regroup(a):
        return jnp.concatenate(
            [seg(a, 0), seg(a, 1), seg(a, 2), seg(a, 3), _pad_cols(seg(a, 4), 128), _pad_cols(seg(a, 5), 128),
             seg(a, 6), seg(a, 7), seg(a, 8), seg(a, 9), _pad_cols(seg(a, 10), 128), seg(a, 11), seg(a, 12)],
            axis=1)
    k = dict(
        w_pack=regroup(w_in).astype(BF16), b_pack=regroup(b_in),
        w_gate=seg(w_in, 13).astype(BF16), b_gate=seg(b_in, 13),
        w_branch=p['w_branch'][l].astype(BF16), w_out=p['w_out'][l].astype(BF16),
        ln1_g=p['ln1_g'][l][None], ln1_b=p['ln1_b'][l][None],
        ln2_g=p['ln2_g'][l][None], ln2_b=p['ln2_b'][l][None],
    )
    w_up, a_up, g_up = p['rwkv_w_up'][l], p['rwkv_a_up'][l], p['rwkv_g_up'][l]
    lora = jnp.stack([_pad_rows(w_up, 128, 0), _pad_rows(a_up, 128, 32), _pad_rows(g_up, 128, 64)])
    k['rw_lora'] = lora.astype(BF16)
    k['rw_lora_t'] = jnp.swapaxes(lora, 1, 2).astype(BF16)
    k['rw_mu'] = p['rwkv_mu'][l][None]
    rw_rows = [p['rwkv_w0'][l], p['rwkv_a0'][l], p['rwkv_k_k'][l], p['rwkv_k_a'][l],
               p['rwkv_r_k'][l].reshape(BRANCH_W)]
    k['rw_p'] = jnp.stack(rw_rows + [jnp.zeros((BRANCH_W,), F32)] * 3)
    k['rw_ln'] = jnp.stack([p['rwkv_ln_w'][l], p['rwkv_ln_b'][l]])
    k['ml_cw'], k['ml_cb'] = p['mlstm_conv_w'][l], p['mlstm_conv_b'][l][None]
    k['ml_nw'] = p['mlstm_norm_w'][l][None]
    k['gl_au'] = _pad_rows(p['gla_alpha_up'][l], 128).astype(BF16)
    k['gl_ab'] = p['gla_alpha_b'][l][None]
    k['gl_nw'] = p['gla_norm_w'][l][None]
    wa, wx = p['rglru_wa'][l], p['rglru_wx'][l]
    eye = jnp.eye(HEADS, dtype=F32)
    bd = lambda w: jnp.einsum('gh,gij->gihj', eye, w).reshape(BRANCH_W, BRANCH_W).astype(BF16)
    k['rg_cw'], k['rg_cb'] = p['rglru_conv_w'][l], p['rglru_conv_b'][l][None]
    k['rg_wa'], k['rg_wx'] = bd(wa), bd(wx)
    k['rg_ba'], k['rg_bx'] = p['rglru_ba'][l][None], p['rglru_bx'][l][None]
    k['rg_lam'] = p['rglru_lambda'][l][None]
    k['rg_wa_t'] = jnp.swapaxes(wa, 1, 2).astype(BF16)
    k['rg_wx_t'] = jnp.swapaxes(wx, 1, 2).astype(BF16)
    cols = rw_rows + [p['rwkv_ln_w'][l], p['rwkv_ln_b'][l], p['mlstm_norm_w'][l], p['gla_norm_w'][l]]
    cols += [p['rglru_conv_w'][l][j] for j in range(4)]
    cols += [p['rglru_conv_b'][l], p['rglru_ba'][l], p['rglru_bx'][l], p['rglru_lambda'][l]]
    k['pcols'] = jnp.stack(cols, axis=1)
    k['ml_cw_t'] = jnp.concatenate([p['mlstm_conv_w'][l].T, p['mlstm_conv_b'][l][:, None]], axis=1)
    return k


def _diag_blocks(a, rb, cb):
    return jnp.stack([a[:, h * rb:(h + 1) * rb, h * cb:(h + 1) * cb] for h in range(HEADS)], axis=1)


def _prompt_mixers(x, k):
    B, L, _ = x.shape
    W = BRANCH_W
    u_rw, u_ml, u_gl, u_rg = _inproj(x.reshape(B * L, D_MODEL), k['w_pack'], k['b_pack'], 512)
    u_rw, u_ml, u_gl, u_rg = (u.reshape(B, L, -1) for u in (u_rw, u_ml, u_gl, u_rg))
    nch = B * HEADS
    rep = 128 // nch
    *planes, g, bv = _rwkv_prep(u_rw, k['rw_mu'], k['rw_p'], k['rw_lora'])
    y, s1 = _rwkv_scan(planes, L)
    y = y.reshape(L, HEAD_W // rep, rep, B, HEADS).transpose(3, 0, 4, 2, 1).reshape(B, L, W)
    s_rw = s1.reshape(HEAD_W, HEAD_W // rep, rep, B, HEADS).transpose(3, 4, 2, 1, 0).reshape(B, HEADS, HEAD_W, HEAD_W)
    flat = lambda a: a.reshape(B * L, W)
    o_a = (flat(y), flat(g), flat(bv), k['rw_ln'])
    o_b, c_bd, n_row, m_row = _mlstm_prompt(u_ml, k['ml_cw'], k['ml_cb'], k['ml_nw'])
    o_c, s_bd = _gla_prompt(u_gl, k['gl_au'], k['gl_ab'], k['gl_nw'])
    o_d, h1 = _rglru_prompt(u_rg, k['rg_cw'], k['rg_cb'], k['rg_wa'], k['rg_ba'], k['rg_wx'], k['rg_bx'],
                            k['rg_lam'])
    states = (u_rw[:, L - 1], s_rw, u_ml[:, L - 3:, :2 * W], _diag_blocks(c_bd, HEAD_W, HEAD_W),
              n_row.reshape(B, HEADS, HEAD_W), m_row[:, :HEADS, 0],
              jnp.swapaxes(_diag_blocks(s_bd, HEAD_W, GLA_DK), 2, 3), u_rg[:, L - 3:, :W], h1[:, 0])
    return [o_a] + [flat(o) for o in (o_b, o_c, o_d)], states


def _sample_mixers(x, st, k):
    W = BRANCH_W
    sh0, S0, mconv0, C0, n0, m0, gS0, rconv0, h0 = st
    u_rw, u_ml, u_gl, u_rg = _inproj(x, k['w_pack'], k['b_pack'], DEC_B)
    outs = _decode_mixers(
        u_rw.T, sh0.T, k['rw_mu'].T, k['rw_lora_t'], S0.transpose(1, 2, 3, 0),
        u_ml.T, mconv0.transpose(1, 2, 0), k['ml_cw_t'], C0.transpose(1, 2, 3, 0), n0.transpose(1, 2, 0), m0.T,
        u_gl.T, k['gl_au'].T, k['gl_ab'].T, gS0.transpose(1, 3, 2, 0),
        u_rg.T, rconv0.transpose(1, 2, 0), k['rg_wa_t'], k['rg_wx_t'], h0.T, k['pcols'])
    oa, ob, oc, od, s_rw, c_st, n_st, m_st, g_st, h_st = outs
    branches = [o.reshape(W, DEC_B).T for o in (oa, ob, oc, od)]
    states = (u_rw, s_rw.transpose(3, 0, 1, 2),
              jnp.concatenate([mconv0[:, 1:], u_ml[:, None, :2 * W]], axis=1),
              c_st.transpose(3, 0, 1, 2), n_st.transpose(2, 0, 1), m_st[:, 0, :].T,
              g_st.transpose(3, 0, 2, 1),
              jnp.concatenate([rconv0[:, 1:], u_rg[:, None, :W]], axis=1), h_st.reshape(W, DEC_B).T)
    return branches, states


def _trunk(xp, bl, xs, sample_states, packs, moe):
    tms = (512, DEC_B)
    new_p, new_s = [], []
    for l in range(DEPTH):
        k = packs[l]
        br_p, st_p = _prompt_mixers(xp.reshape(bl[0], bl[1], D_MODEL), k)
        br_s, st_s = _sample_mixers(xs, tuple(s[l] for s in sample_states), k)
        new_p.append(st_p)
        new_s.append(st_s)
        xp, xs = (_merge(x, br, k['w_gate'], k['b_gate'], k['w_branch'], k['w_out'], k['ln1_g'], k['ln1_b'], tm)
                  for x, br, tm in ((xp, br_p, tms[0]), (xs, br_s, tms[1])))
        j = l // 2
        if l % 2 == 0:
            xp, xs = (_ffn(x, moe['ffn_wg'][j], moe['ffn_wu'][j], moe['ffn_wd'][j], k['ln2_g'], k['ln2_b'], tm)
                      for x, tm in ((xp, tms[0]), (xs, tms[1])))
        else:
            xp, xs = _moe_layer([xp, xs], moe['router'][j], moe['router_b'][j], moe['moe_wg'][j],
                                moe['moe_wu'][j], moe['moe_wd'][j], k['ln2_g'], k['ln2_b'])
    stack = lambda sts: [jnp.stack([st[i] for st in sts], axis=0) for i in range(9)]
    return xp, xs, stack(new_p), stack(new_s)


def kernel(x_prompt, x_sample, state_rwkv_shift, state_rwkv_S, state_mlstm_conv, state_mlstm_C,
           state_mlstm_n, state_mlstm_m, state_gla_S, state_rglru_conv, state_rglru_h,
           w_in, b_in, rwkv_mu, rwkv_w0, rwkv_w_up, rwkv_a0, rwkv_a_up, rwkv_g_up, rwkv_k_k,
           rwkv_k_a, rwkv_r_k, rwkv_ln_w, rwkv_ln_b, mlstm_conv_w, mlstm_conv_b, mlstm_norm_w,
           gla_alpha_up, gla_alpha_b, gla_norm_w, rglru_conv_w, rglru_conv_b, rglru_wa, rglru_ba,
           rglru_wx, rglru_bx, rglru_lambda, w_branch, w_out, ln1_g, ln1_b, ffn_wg, ffn_wu, ffn_wd,
           moe_router, moe_router_b, moe_wg, moe_wu, moe_wd, ln2_g, ln2_b):
    p = dict(w_in=w_in, b_in=b_in, rwkv_mu=rwkv_mu, rwkv_w0=rwkv_w0, rwkv_w_up=rwkv_w_up,
             rwkv_a0=rwkv_a0, rwkv_a_up=rwkv_a_up, rwkv_g_up=rwkv_g_up, rwkv_k_k=rwkv_k_k,
             rwkv_k_a=rwkv_k_a, rwkv_r_k=rwkv_r_k, rwkv_ln_w=rwkv_ln_w, rwkv_ln_b=rwkv_ln_b,
             mlstm_conv_w=mlstm_conv_w, mlstm_conv_b=mlstm_conv_b, mlstm_norm_w=mlstm_norm_w,
             gla_alpha_up=gla_alpha_up, gla_alpha_b=gla_alpha_b, gla_norm_w=gla_norm_w,
             rglru_conv_w=rglru_conv_w, rglru_conv_b=rglru_conv_b, rglru_wa=rglru_wa,
             rglru_ba=rglru_ba, rglru_wx=rglru_wx, rglru_bx=rglru_bx, rglru_lambda=rglru_lambda,
             w_branch=w_branch, w_out=w_out, ln1_g=ln1_g, ln1_b=ln1_b, ln2_g=ln2_g, ln2_b=ln2_b)
    packs = [_pack_layer(p, l) for l in range(DEPTH)]
    moe = dict(ffn_wg=ffn_wg.astype(BF16), ffn_wu=ffn_wu.astype(BF16), ffn_wd=ffn_wd.astype(BF16),
               router=jnp.pad(moe_router, ((0, 0), (0, 0), (0, 128 - N_EXPERTS))),
               router_b=jnp.pad(moe_router_b, ((0, 0), (0, 128 - N_EXPERTS)))[:, None, :],
               moe_wg=moe_wg.astype(BF16), moe_wu=moe_wu.astype(BF16), moe_wd=moe_wd.astype(BF16))
    B, L, _ = x_prompt.shape
    sample_states = (state_rwkv_shift, state_rwkv_S, state_mlstm_conv, state_mlstm_C, state_mlstm_n,
                     state_mlstm_m, state_gla_S, state_rglru_conv, state_rglru_h)
    nb, ls, _ = x_sample.shape
    y_p, y_s, ps, ss = _trunk(x_prompt.reshape(B * L, D_MODEL), (B, L), x_sample.reshape(nb * ls, D_MODEL),
                              sample_states, packs, moe)
    return (y_p.reshape(B, L, D_MODEL), y_s.reshape(nb, ls, D_MODEL), *ps, *ss)
```

```python
import functools

import jax
import jax.numpy as jnp
import numpy as np
from jax import lax
from jax.experimental import pallas as pl
from jax.experimental.pallas import tpu as pltpu

F32 = jnp.float32
BF16 = jnp.bfloat16

D_MODEL = 1024
DEPTH = 2
N_BRANCH = 4
BRANCH_W = 256
HEADS = 4
HEAD_W = 64
RWKV_COLS = 896
RWKV_GN_EPS = 64e-5
GLA_DK = 32
GLA_LORA = 16
GLA_TAU = 16.0
RG_C = 8.0
D_FF = 2816
N_EXPERTS = 8
DN_ALPHA = (2 * DEPTH) ** 0.25
LN_EPS = 1e-5

IN_SIZES = (RWKV_COLS, 512, 256, 256, 4, 4, 128, 128, 256, 256, 16, 256, 256, 4096)

W_RW = 896
W_ML = 1280
W_GL = 896
W_RG = 512
W_PACK = W_RW + W_ML + W_GL + W_RG

VMEM_LIMIT = 56 * 1024 * 1024


def _cp(*sem):
    return pltpu.CompilerParams(dimension_semantics=sem, vmem_limit_bytes=VMEM_LIMIT)


def _dot(a, b):
    return jnp.dot(a.astype(BF16), b.astype(BF16), preferred_element_type=F32)


def _dot_nt(a, b):
    return lax.dot_general(a.astype(BF16), b.astype(BF16), (((1,), (1,)), ((), ())),
                           preferred_element_type=F32)


def _dot_tn(a, b):
    return lax.dot_general(a.astype(BF16), b.astype(BF16), (((0,), (0,)), ((), ())),
                           preferred_element_type=F32)


def _split(x):
    hi = x.astype(BF16)
    lo = (x - hi.astype(F32)).astype(BF16)
    return hi, lo


def _dot_exact_lhs(a, x):
    hi, lo = _split(x)
    a = a.astype(BF16)
    return (jnp.dot(a, hi, preferred_element_type=F32) + jnp.dot(a, lo, preferred_element_type=F32))


def _dot_exact_rhs(x, a):
    hi, lo = _split(x)
    a = a.astype(BF16)
    return (jnp.dot(hi, a, preferred_element_type=F32) + jnp.dot(lo, a, preferred_element_type=F32))


def _dot3(x, w):
    xh, xl = _split(x)
    wh, wl = _split(w)
    return (jnp.dot(xh, wh, preferred_element_type=F32) + jnp.dot(xl, wh, preferred_element_type=F32)
            + jnp.dot(xh, wl, preferred_element_type=F32))


def _sigmoid(x):
    return 1.0 / (1.0 + jnp.exp(-x))


def _softplus(x):
    return jnp.maximum(x, 0.0) + jnp.log(1.0 + jnp.exp(-jnp.abs(x)))


def _log_sigmoid(x):
    return -_softplus(-x)


def _silu(x):
    return x * _sigmoid(x)


def _gelu_tanh(x):
    c = np.float32(np.sqrt(2.0 / np.pi))
    return 0.5 * x * (1.0 + jnp.tanh(c * (x + 0.044715 * (x * x * x))))


def _layer_norm(y, g, b):
    mu = jnp.mean(y, axis=-1, keepdims=True)
    yc = y - mu
    var = jnp.mean(yc * yc, axis=-1, keepdims=True)
    return yc * lax.rsqrt(var + LN_EPS) * g + b


def _iota(shape, dim):
    return lax.broadcasted_iota(jnp.int32, shape, dim)


def _head_mean_mat(width=BRANCH_W, group=HEAD_W):
    r = _iota((width, width), 0) // group
    c = _iota((width, width), 1) // group
    return jnp.where(r == c, 1.0 / group, 0.0).astype(BF16)


def _tril_ones(n):
    return jnp.where(_iota((n, n), 0) >= _iota((n, n), 1), 1.0, 0.0).astype(BF16)


def _const_spec(shape):
    nd = len(shape)
    return pl.BlockSpec(shape, lambda *_: (0,) * nd)


def _inproj_kernel(x_ref, w_ref, b_ref, o_rw, o_ml, o_gl, o_rg):
    x = x_ref[...].astype(BF16)
    off = 0
    for o_ref in (o_rw, o_ml, o_gl, o_rg):
        n = o_ref.shape[-1]
        o_ref[...] = (jnp.dot(x, w_ref[:, off:off + n], preferred_element_type=F32)
                      + b_ref[:, off:off + n])
        off += n


def _inproj(x, w, b, tm):
    n = x.shape[0]
    widths = (W_RW, W_ML, W_GL, W_RG)
    return pl.pallas_call(
        _inproj_kernel,
        out_shape=[jax.ShapeDtypeStruct((n, wd), F32) for wd in widths],
        grid=(n // tm,),
        in_specs=[pl.BlockSpec((tm, D_MODEL), lambda i: (i, 0)),
                  _const_spec((D_MODEL, W_PACK)), _const_spec((1, W_PACK))],
        out_specs=[pl.BlockSpec((tm, wd), lambda i: (i, 0)) for wd in widths],
        compiler_params=_cp("parallel"),
        name="inproj",
    )(x, w, b)


def _merge_body(x, outs, wg_ref, bg_ref, wb_ref, wo_ref, g_ref, b_ref, out_ref):
    xb = x.astype(BF16)
    merged = None
    for g, o in enumerate(outs):
        sl = slice(g * D_MODEL, (g + 1) * D_MODEL)
        gate = _sigmoid(jnp.dot(xb, wg_ref[:, sl], preferred_element_type=F32) + bg_ref[:, sl])
        up = jnp.dot(o.astype(BF16), wb_ref[g], preferred_element_type=F32)
        merged = gate * up if merged is None else merged + gate * up
    out = jnp.dot(merged.astype(BF16), wo_ref[...], preferred_element_type=F32)
    out_ref[...] = _layer_norm(DN_ALPHA * x + out, g_ref[...], b_ref[...])


def _merge_kernel(x_ref, oa, ob, oc, od, *rest):
    _merge_body(x_ref[...], [oa[...], ob[...], oc[...], od[...]], *rest)


def _rwkv_out(y, g, bv, ln_ref):
    pm = _head_mean_mat()
    yc = y - _dot_exact_rhs(y, pm)
    var = _dot_exact_rhs(yc * yc, pm)
    return (yc * lax.rsqrt(var + RWKV_GN_EPS) * ln_ref[0:1, :] + ln_ref[1:2, :] + bv) * g


def _merge_rwkv_kernel(x_ref, y_ref, gr_ref, bv_ref, ln_ref, ob, oc, od, *rest):
    oa = _rwkv_out(y_ref[...], gr_ref[...], bv_ref[...], ln_ref)
    _merge_body(x_ref[...], [oa, ob[...], oc[...], od[...]], *rest)


def _merge(x, branches, wg, bg, wb, wo, ln_g, ln_b, tm):
    n = x.shape[0]
    tok = lambda wd: pl.BlockSpec((tm, wd), lambda i: (i, 0))
    if isinstance(branches[0], tuple):
        y, g, bv, ln = branches[0]
        kern, first, first_specs = _merge_rwkv_kernel, (y, g, bv, ln), [tok(BRANCH_W)] * 3 + [_const_spec((2, BRANCH_W))]
    else:
        kern, first, first_specs = _merge_kernel, (branches[0],), [tok(BRANCH_W)]
    return pl.pallas_call(
        kern,
        out_shape=jax.ShapeDtypeStruct((n, D_MODEL), F32),
        grid=(n // tm,),
        in_specs=[tok(D_MODEL)] + first_specs + [tok(BRANCH_W)] * 3 + [
            _const_spec((D_MODEL, N_BRANCH * D_MODEL)), _const_spec((1, N_BRANCH * D_MODEL)),
            _const_spec((N_BRANCH, BRANCH_W, D_MODEL)), _const_spec((D_MODEL, D_MODEL)),
            _const_spec((1, D_MODEL)), _const_spec((1, D_MODEL))],
        out_specs=tok(D_MODEL),
        compiler_params=_cp("parallel"),
        name="merge",
    )(x, *first, *branches[1:], wg, bg, wb, wo, ln_g, ln_b)


FF_CHUNK = D_FF // 2


def _ffn_kernel(x_ref, wg_ref, wu_ref, wd_ref, g_ref, b_ref, out_ref):
    x = x_ref[...]
    xb = x.astype(BF16)
    acc = None
    for c in range(D_FF // FF_CHUNK):
        sl = slice(c * FF_CHUNK, (c + 1) * FF_CHUNK)
        h = (_silu(jnp.dot(xb, wg_ref[:, sl], preferred_element_type=F32))
             * jnp.dot(xb, wu_ref[:, sl], preferred_element_type=F32))
        part = jnp.dot(h.astype(BF16), wd_ref[sl, :], preferred_element_type=F32)
        acc = part if acc is None else acc + part
    out_ref[...] = _layer_norm(DN_ALPHA * x + acc, g_ref[...], b_ref[...])


def _ffn(x, wg, wu, wd, ln_g, ln_b, tm):
    n = x.shape[0]
    tok = pl.BlockSpec((tm, D_MODEL), lambda i: (i, 0))
    return pl.pallas_call(
        _ffn_kernel,
        out_shape=jax.ShapeDtypeStruct((n, D_MODEL), F32),
        grid=(n // tm,),
        in_specs=[tok, _const_spec((D_MODEL, D_FF)), _const_spec((D_MODEL, D_FF)),
                  _const_spec((D_FF, D_MODEL)), _const_spec((1, D_MODEL)), _const_spec((1, D_MODEL))],
        out_specs=tok,
        compiler_params=_cp("parallel"),
        name="ffn",
    )(x, wg, wu, wd, ln_g, ln_b)


(RT_E1, RT_E2, RT_R1, RT_R2, RT_W1, RT_W2) = range(6)
MOE_TILE = 512


def _router_kernel(x_ref, w_ref, b_ref, cin_ref, meta_ref, cnt_ref, carry_ref):
    @pl.when(pl.program_id(0) == 0)
    def _():
        carry_ref[...] = cin_ref[...]

    logits = _dot3(x_ref[...], w_ref[...]) + b_ref[...]
    tm = logits.shape[0]
    lane = _iota(logits.shape, 1)
    neg = jnp.float32(-jnp.inf)
    logits = jnp.where(lane < N_EXPERTS, logits, neg)
    m1 = jnp.max(logits, axis=-1, keepdims=True)
    i1 = jnp.min(jnp.where(logits == m1, lane, 128), axis=-1, keepdims=True)
    rest = jnp.where(lane == i1, neg, logits)
    m2 = jnp.max(rest, axis=-1, keepdims=True)
    i2 = jnp.min(jnp.where(rest == m2, lane, 128), axis=-1, keepdims=True)
    e2 = jnp.exp(m2 - m1)
    w1 = 1.0 / (1.0 + e2)
    w2 = e2 / (1.0 + e2)
    oh1 = lane == i1
    oh2 = lane == i2
    picks = jnp.where(oh1 | oh2, 1.0, 0.0)
    below = jnp.where(_iota((tm, tm), 0) > _iota((tm, tm), 1), 1.0, 0.0).astype(BF16)
    base = carry_ref[...] + jnp.dot(below, picks.astype(BF16), preferred_element_type=F32)
    r1 = jnp.sum(jnp.where(oh1, base, 0.0), axis=-1, keepdims=True)
    r2 = jnp.sum(jnp.where(oh2, base, 0.0), axis=-1, keepdims=True)
    carry_ref[...] = carry_ref[...] + jnp.sum(picks, axis=0, keepdims=True)
    cnt_ref[...] = carry_ref[...]
    meta = jnp.zeros(logits.shape, F32)
    for ln, val in ((RT_E1, i1.astype(F32)), (RT_E2, i2.astype(F32)), (RT_R1, r1), (RT_R2, r2),
                    (RT_W1, w1), (RT_W2, w2)):
        meta = jnp.where(lane == ln, val, meta)
    meta_ref[...] = meta


def _router(x, w, b, cnt_in, tm):
    n = x.shape[0]
    return pl.pallas_call(
        _router_kernel,
        out_shape=[jax.ShapeDtypeStruct((n, 128), F32), jax.ShapeDtypeStruct((1, 128), F32)],
        grid=(n // tm,),
        in_specs=[pl.BlockSpec((tm, D_MODEL), lambda i: (i, 0)),
                  _const_spec((D_MODEL, 128)), _const_spec((1, 128)), _const_spec((1, 128))],
        out_specs=[pl.BlockSpec((tm, 128), lambda i: (i, 0)), _const_spec((1, 128))],
        scratch_shapes=[pltpu.VMEM((1, 128), F32)],
        compiler_params=_cp("arbitrary"),
        name="router",
    )(x, w, b, cnt_in)


def _row_copies(n_rows, start_one, wait_shape_src, wait_shape_dst, sem, n_streams):
    def body(r, carry):
        start_one(r)
        return carry
    lax.fori_loop(0, n_rows, body, 0, unroll=8)
    for _ in range(n_streams):
        pltpu.make_async_copy(wait_shape_src, wait_shape_dst, sem).wait()


def _dispatch_kernel(d1_ref, d2_ref, x_ref, xs_in_ref, xs_ref, sem):
    del xs_in_ref
    tm = x_ref.shape[0]

    def start_one(r):
        src = x_ref.at[pl.ds(r, 1)]
        pltpu.make_async_copy(src, xs_ref.at[pl.ds(d1_ref[0, 0, r], 1)], sem).start(priority=0)
        pltpu.make_async_copy(src, xs_ref.at[pl.ds(d2_ref[0, 0, r], 1)], sem).start(priority=1)
    _row_copies(tm, start_one, x_ref, xs_ref.at[pl.ds(0, tm)], sem, 2)


def _dispatch(x, d1, d2, xs, tm):
    n = x.shape[0]
    idx = lambda a: a.reshape(n // tm, 1, tm)
    ispec = pl.BlockSpec((1, 1, tm), lambda i: (i, 0, 0), memory_space=pltpu.SMEM)
    return pl.pallas_call(
        _dispatch_kernel,
        out_shape=jax.ShapeDtypeStruct(xs.shape, F32),
        grid=(n // tm,),
        in_specs=[ispec, ispec, pl.BlockSpec((tm, D_MODEL), lambda i: (i, 0)),
                  pl.BlockSpec(memory_space=pl.ANY)],
        out_specs=pl.BlockSpec(memory_space=pl.ANY),
        scratch_shapes=[pltpu.SemaphoreType.DMA],
        input_output_aliases={3: 0},
        compiler_params=_cp("arbitrary"),
        name="moe_dispatch",
    )(idx(d1), idx(d2), x, xs)


def _experts_kernel(te_ref, nu_ref, xs_ref, wg_ref, wu_ref, wd_ref, ys_ref):
    del te_ref

    @pl.when(pl.program_id(0) < nu_ref[0])
    def _():
        xb = xs_ref[...].astype(BF16)
        acc = None
        for c in range(D_FF // FF_CHUNK):
            sl = slice(c * FF_CHUNK, (c + 1) * FF_CHUNK)
            h = (_silu(jnp.dot(xb, wg_ref[:, sl], preferred_element_type=F32))
                 * jnp.dot(xb, wu_ref[:, sl], preferred_element_type=F32))
            part = jnp.dot(h.astype(BF16), wd_ref[sl, :], preferred_element_type=F32)
            acc = part if acc is None else acc + part
        ys_ref[...] = acc

    @pl.when(pl.program_id(0) >= nu_ref[0])
    def _():
        ys_ref[...] = jnp.zeros_like(ys_ref)


def _experts(xs, tile_expert, n_used, wg, wu, wd):
    n_tiles = xs.shape[0] // MOE_TILE
    row = pl.BlockSpec((MOE_TILE, D_MODEL), lambda i, te, nu: (i, 0))
    return pl.pallas_call(
        _experts_kernel,
        out_shape=jax.ShapeDtypeStruct(xs.shape, F32),
        grid_spec=pltpu.PrefetchScalarGridSpec(
            num_scalar_prefetch=2, grid=(n_tiles,),
            in_specs=[row,
                      pl.BlockSpec((None, D_MODEL, D_FF), lambda i, te, nu: (te[i], 0, 0)),
                      pl.BlockSpec((None, D_MODEL, D_FF), lambda i, te, nu: (te[i], 0, 0)),
                      pl.BlockSpec((None, D_FF, D_MODEL), lambda i, te, nu: (te[i], 0, 0))],
            out_specs=row),
        compiler_params=_cp("arbitrary"),
        name="moe_experts",
    )(tile_expert, n_used, xs, wg, wu, wd)


def _combine_kernel(d1_ref, d2_ref, x_ref, meta_ref, ys_ref, g_ref, b_ref, out_ref, y1_ref, y2_ref, sem):
    tm = x_ref.shape[0]

    def start_one(r):
        pltpu.make_async_copy(ys_ref.at[pl.ds(d1_ref[0, 0, r], 1)], y1_ref.at[pl.ds(r, 1)], sem).start(priority=0)
        pltpu.make_async_copy(ys_ref.at[pl.ds(d2_ref[0, 0, r], 1)], y2_ref.at[pl.ds(r, 1)], sem).start(priority=1)
    _row_copies(tm, start_one, ys_ref.at[pl.ds(0, tm)], y1_ref, sem, 2)
    meta = meta_ref[...]
    f = meta[:, RT_W1:RT_W1 + 1] * y1_ref[...] + meta[:, RT_W2:RT_W2 + 1] * y2_ref[...]
    out_ref[...] = _layer_norm(DN_ALPHA * x_ref[...] + f, g_ref[...], b_ref[...])


def _combine(x, meta, d1, d2, ys, ln_g, ln_b, tm):
    n = x.shape[0]
    idx = lambda a: a.reshape(n // tm, 1, tm)
    ispec = pl.BlockSpec((1, 1, tm), lambda i: (i, 0, 0), memory_space=pltpu.SMEM)
    tok = lambda wd_: pl.BlockSpec((tm, wd_), lambda i: (i, 0))
    return pl.pallas_call(
        _combine_kernel,
        out_shape=jax.ShapeDtypeStruct((n, D_MODEL), F32),
        grid=(n // tm,),
        in_specs=[ispec, ispec, tok(D_MODEL), tok(128), pl.BlockSpec(memory_space=pl.ANY),
                  _const_spec((1, D_MODEL)), _const_spec((1, D_MODEL))],
        out_specs=tok(D_MODEL),
        scratch_shapes=[pltpu.VMEM((tm, D_MODEL), F32), pltpu.VMEM((tm, D_MODEL), F32),
                        pltpu.SemaphoreType.DMA],
        compiler_params=_cp("arbitrary"),
        name="moe_combine",
    )(idx(d1), idx(d2), x, meta, ys, ln_g, ln_b)


def _moe_layer(xs_groups, router_w, router_b, wg, wu, wd, ln_g, ln_b):
    tms = [min(512, x.shape[0]) for x in xs_groups]
    cnt = jnp.zeros((1, 128), F32)
    metas = []
    for x, tm in zip(xs_groups, tms):
        meta, cnt = _router(x, router_w, router_b, cnt, tm)
        metas.append(meta)
    n_total = sum(x.shape[0] for x in xs_groups)
    n_rows = -(-(2 * n_total + N_EXPERTS * (MOE_TILE - 1)) // MOE_TILE) * MOE_TILE
    n_tiles = n_rows // MOE_TILE
    counts = cnt[0, :N_EXPERTS].astype(jnp.int32)
    padded = (counts + MOE_TILE - 1) // MOE_TILE * MOE_TILE
    ends = jnp.cumsum(padded)
    starts = ends - padded
    tile_expert = jnp.minimum(
        jnp.sum((jnp.arange(n_tiles, dtype=jnp.int32)[:, None] * MOE_TILE >= ends[None, :]).astype(jnp.int32), axis=1),
        N_EXPERTS - 1).astype(jnp.int32)
    n_used = (ends[N_EXPERTS - 1] // MOE_TILE).astype(jnp.int32).reshape(1)
    xs = jnp.zeros((n_rows, D_MODEL), F32)
    dests = []
    for x, meta, tm in zip(xs_groups, metas, tms):
        e = meta[:, RT_E1:RT_E2 + 1].astype(jnp.int32)
        base = jnp.sum(jnp.where(e[:, :, None] == jnp.arange(N_EXPERTS, dtype=jnp.int32), starts, 0), axis=-1)
        d = base + meta[:, RT_R1:RT_R2 + 1].astype(jnp.int32)
        dests.append((d[:, 0], d[:, 1]))
        xs = _dispatch(x, d[:, 0], d[:, 1], xs, tm)
    ys = _experts(xs, tile_expert, n_used, wg, wu, wd)
    outs = []
    for x, meta, (d1, d2), tm in zip(xs_groups, metas, dests, tms):
        outs.append(_combine(x, meta, d1, d2, ys, ln_g, ln_b, tm))
    return outs


def _shift_rows(x, s, fill):
    rolled = pltpu.roll(x, s, 0)
    return jnp.where(_iota(x.shape, 0) >= s, rolled, fill)


def _causal_conv4(ext_ref, x, cw_ref, cb_ref, lc):
    ext_ref[8:8 + lc, :] = x
    out = cb_ref[...] + cw_ref[3:4, :] * x
    for j in range(3):
        out = out + cw_ref[j:j + 1, :] * ext_ref[5 + j:5 + j + lc, :]
    ext_ref[0:8, :] = ext_ref[lc:lc + 8, :]
    return out


RG_CHUNK = 256


RG_SEQS = 4


def _rglru_kernel(u_ref, cw_ref, cb_ref, wa_ref, ba_ref, wx_ref, bx_ref, lam_ref,
                  y_ref, h_ref, ext_ref, hc_ref):
    lc = RG_CHUNK

    @pl.when(pl.program_id(1) == 0)
    def _():
        ext_ref[:, 0:8, :] = jnp.zeros((ext_ref.shape[0], 8, BRANCH_W), F32)
        hc_ref[...] = jnp.zeros_like(hc_ref)

    for sq in range(u_ref.shape[0]):
        xc = _causal_conv4(ext_ref.at[sq], u_ref[sq, :, 0:BRANCH_W], cw_ref, cb_ref, lc)
        r = _sigmoid(_dot(xc, wa_ref[...]) + ba_ref[...])
        i = _sigmoid(_dot(xc, wx_ref[...]) + bx_ref[...])
        log_a = -RG_C * r * _softplus(-lam_ref[...])
        a = jnp.exp(log_a)
        u = jnp.sqrt(1.0 - jnp.exp(2.0 * log_a)) * (i * xc)
        s = 1
        while s < lc:
            u = u + a * _shift_rows(u, s, 0.0)
            a = a * _shift_rows(a, s, 1.0)
            s *= 2
        h = u + a * hc_ref[sq]
        hc_ref[sq] = h[lc - 1:lc, :]
        h_ref[sq] = h[lc - 1:lc, :]
        y_ref[sq] = h * _gelu_tanh(u_ref[sq, :, BRANCH_W:2 * BRANCH_W])


def _rglru_prompt(u, cw, cb, wa, ba, wx, bx, lam):
    bsz, seq, _ = u.shape
    lc = RG_CHUNK
    ns = RG_SEQS
    return pl.pallas_call(
        _rglru_kernel,
        out_shape=[jax.ShapeDtypeStruct((bsz, seq, BRANCH_W), F32),
                   jax.ShapeDtypeStruct((bsz, 1, BRANCH_W), F32)],
        grid=(bsz // ns, seq // lc),
        in_specs=[pl.BlockSpec((ns, lc, W_RG), lambda b, c: (b, c, 0)),
                  _const_spec((4, BRANCH_W)), _const_spec((1, BRANCH_W)),
                  _const_spec((BRANCH_W, BRANCH_W)), _const_spec((1, BRANCH_W)),
                  _const_spec((BRANCH_W, BRANCH_W)), _const_spec((1, BRANCH_W)),
                  _const_spec((1, BRANCH_W))],
        out_specs=[pl.BlockSpec((ns, lc, BRANCH_W), lambda b, c: (b, c, 0)),
                   pl.BlockSpec((ns, 1, BRANCH_W), lambda b, c: (b, 0, 0))],
        scratch_shapes=[pltpu.VMEM((ns, lc + 8, BRANCH_W), F32), pltpu.VMEM((ns, 1, BRANCH_W), F32)],
        compiler_params=_cp("parallel", "arbitrary"),
        name="rglru",
    )(u, cw, cb, wa, ba, wx, bx, lam)


ML_CHUNK = 128


ML_SEQS = 8


def _mlstm_kernel(u_ref, cw_ref, cb_ref, nw_ref, o_ref, c_ref, n_ref, m_ref, ext_ref):
    @pl.when(pl.program_id(1) == 0)
    def _():
        ext_ref[:, 0:8, :] = jnp.zeros((ext_ref.shape[0], 8, 2 * BRANCH_W), F32)
        c_ref[...] = jnp.zeros_like(c_ref)
        n_ref[...] = jnp.zeros_like(n_ref)
        m_ref[...] = jnp.zeros_like(m_ref)

    for sq in range(u_ref.shape[0]):
        _mlstm_chunk(sq, u_ref, cw_ref, cb_ref, nw_ref, o_ref, c_ref, n_ref, m_ref, ext_ref.at[sq])


def _mlstm_chunk(sq, u_ref, cw_ref, cb_ref, nw_ref, o_ref, c_ref, n_ref, m_ref, ext_ref):
    lc = ML_CHUNK
    W = BRANCH_W
    qk = _silu(_causal_conv4(ext_ref, u_ref[sq, :, 0:2 * W], cw_ref, cb_ref, lc))
    q = qk[:, 0:W]
    k = qk[:, W:2 * W] * (HEAD_W ** -0.5)
    v = u_ref[sq, :, 2 * W:3 * W]
    og = u_ref[sq, :, 3 * W:4 * W]
    ipre = u_ref[sq, :, 4 * W:4 * W + 128]
    logf = _log_sigmoid(u_ref[sq, :, 4 * W + 128:4 * W + 256])
    bcum = _dot_exact_lhs(_tril_ones(lc), logf)
    rowsrc = ipre - bcum
    rows_t = rowsrc.T
    lane = _iota((1, W), 1) // HEAD_W
    hq = _iota((lc, W), 1) // HEAD_W
    cmat = c_ref[sq]
    nrow = n_ref[sq]
    kb = k.astype(BF16)
    vb = v.astype(BF16)
    stack = lambda f: jnp.concatenate([f(h) for h in range(HEADS)], axis=0)
    b_col = stack(lambda h: bcum[:, h:h + 1])
    m_prev = stack(lambda h: jnp.broadcast_to(m_ref[sq, h:h + 1, 0:1], (lc, 1)))
    causal = (_iota((HEADS * lc, lc), 0) % lc) >= _iota((HEADS * lc, lc), 1)
    log_d = jnp.where(causal, b_col + stack(lambda h: jnp.broadcast_to(rows_t[h:h + 1, :], (lc, lc))), -jnp.inf)
    log_inter = b_col + m_prev
    m_t = jnp.maximum(log_inter, jnp.max(log_d, axis=-1, keepdims=True))
    dmat = jnp.exp(log_d - m_t)
    inter = jnp.exp(log_inter - m_t)
    qs = stack(lambda h: jnp.where(hq == h, q, 0.0))
    s = _dot_nt(qs, kb) * dmat
    num = _dot(s, vb) + inter * _dot_nt(qs, cmat)
    dot = (jnp.sum(s, axis=-1, keepdims=True)
           + inter * jnp.sum(qs * nrow, axis=-1, keepdims=True))
    hh = num / jnp.maximum(jnp.abs(dot), jnp.exp(-m_t))
    h_acc = jnp.zeros((lc, W), F32)
    ws_full = jnp.zeros((lc, W), F32)
    keep_full = jnp.zeros((1, W), F32)
    for h in range(HEADS):
        hm = lane == h
        h_acc = h_acc + jnp.where(hm, hh[h * lc:(h + 1) * lc, :], 0.0)
        m_new = m_t[(h + 1) * lc - 1:(h + 1) * lc, :]
        b_last = bcum[lc - 1:lc, h:h + 1]
        ws_col = jnp.exp(rowsrc[:, h:h + 1] + (b_last - m_new))
        keep = jnp.exp(b_last + m_ref[sq, h:h + 1, 0:1] - m_new)
        ws_full = ws_full + jnp.where(hm, ws_col, 0.0)
        keep_full = keep_full + jnp.where(hm, keep, 0.0)
        m_ref[sq, h:h + 1, :] = jnp.broadcast_to(m_new, (1, 128))
    blk = (_iota((W, W), 0) // HEAD_W) == (_iota((W, W), 1) // HEAD_W)
    vw = v * ws_full
    c_ref[sq] = keep_full * cmat + jnp.where(blk, _dot_tn(vw, kb), 0.0)
    n_ref[sq] = keep_full * nrow + jnp.sum(ws_full * k, axis=0, keepdims=True)
    pm = _head_mean_mat()
    mu = _dot_exact_rhs(h_acc, pm)
    hc = h_acc - mu
    var = _dot_exact_rhs(hc * hc, pm)
    o_ref[sq] = hc * lax.rsqrt(var + 1e-6) * nw_ref[...] * _sigmoid(og)


def _mlstm_prompt(u, cw, cb, nw):
    bsz, seq, _ = u.shape
    lc = ML_CHUNK
    W = BRANCH_W
    return pl.pallas_call(
        _mlstm_kernel,
        out_shape=[jax.ShapeDtypeStruct((bsz, seq, W), F32),
                   jax.ShapeDtypeStruct((bsz, W, W), F32),
                   jax.ShapeDtypeStruct((bsz, 1, W), F32),
                   jax.ShapeDtypeStruct((bsz, 8, 128), F32)],
        grid=(bsz // ML_SEQS, seq // lc),
        in_specs=[pl.BlockSpec((ML_SEQS, lc, W_ML), lambda b, c: (b, c, 0)),
                  _const_spec((4, 2 * W)), _const_spec((1, 2 * W)), _const_spec((1, W))],
        out_specs=[pl.BlockSpec((ML_SEQS, lc, W), lambda b, c: (b, c, 0)),
                   pl.BlockSpec((ML_SEQS, W, W), lambda b, c: (b, 0, 0)),
                   pl.BlockSpec((ML_SEQS, 1, W), lambda b, c: (b, 0, 0)),
                   pl.BlockSpec((ML_SEQS, 8, 128), lambda b, c: (b, 0, 0))],
        scratch_shapes=[pltpu.VMEM((ML_SEQS, lc + 8, 2 * W), F32)],
        compiler_params=_cp("parallel", "arbitrary"),
        name="mlstm",
    )(u, cw, cb, nw)


GLA_CHUNK = 128
GLA_QK = HEADS * GLA_DK


def _gla_level_ref(b, m, lc):
    if 2 * m == lc:
        return jnp.broadcast_to(b[m - 1:m, :], b.shape)
    if m >= 4:
        nb = lc // (2 * m)
        mid = b.reshape(nb, 2 * m, b.shape[1])[:, m - 1:m, :]
        return jnp.broadcast_to(mid, (nb, 2 * m, b.shape[1])).reshape(b.shape)
    pos = _iota(b.shape, 0) % (2 * m)
    out = b
    for p in range(2 * m):
        sh = p - (m - 1)
        if sh != 0:
            out = jnp.where(pos == p, pltpu.roll(b, sh % lc, 0), out)
    return out


GLA_SEQS = 8


def _gla_kernel(u_ref, au_ref, ab_ref, nw_ref, o_ref, s_ref):
    @pl.when(pl.program_id(1) == 0)
    def _():
        s_ref[...] = jnp.zeros_like(s_ref)

    for sq in range(u_ref.shape[0]):
        _gla_chunk(sq, u_ref, au_ref, ab_ref, nw_ref, o_ref, s_ref)


def _gla_chunk(sq, u_ref, au_ref, ab_ref, nw_ref, o_ref, s_ref):
    lc = GLA_CHUNK
    W = BRANCH_W
    q = u_ref[sq, :, 0:GLA_QK] * (GLA_DK ** -0.5)
    k = u_ref[sq, :, GLA_QK:2 * GLA_QK]
    v = u_ref[sq, :, 2 * GLA_QK:2 * GLA_QK + W]
    g = u_ref[sq, :, 2 * GLA_QK + W:2 * GLA_QK + 2 * W]
    ac = u_ref[sq, :, 2 * GLA_QK + 2 * W:2 * GLA_QK + 2 * W + 128]
    la = _log_sigmoid(_dot(ac, au_ref[...]) + ab_ref[...]) / GLA_TAU
    b = _dot_exact_lhs(_tril_ones(lc), la)
    st = s_ref[sq]
    o = _dot_nt(q * jnp.exp(b), st)
    gsum = ((_iota((GLA_QK, W), 0) // GLA_DK) == (_iota((GLA_QK, W), 1) // HEAD_W)).astype(BF16)
    o = o + jnp.dot((q * k).astype(BF16), gsum, preferred_element_type=F32) * v

    hq = _iota((lc, GLA_QK), 1) // GLA_DK
    row1 = _iota((lc, 1), 0)
    rt = _iota((HEADS * lc, lc), 0) % lc
    cs = _iota((HEADS * lc, lc), 1)
    kb_rows = None
    attn = jnp.zeros((HEADS * lc, lc), F32)
    m = lc // 2
    while m >= 1:
        ref = _gla_level_ref(b, m, lc)
        upper = (row1 % (2 * m)) >= m
        qt = q * jnp.exp(jnp.where(upper, b - ref, -jnp.inf))
        kt = k * jnp.exp(jnp.where(upper, -jnp.inf, ref - b))
        qs = jnp.concatenate([jnp.where(hq == h, qt, 0.0) for h in range(HEADS)], axis=0)
        part = _dot_nt(qs, kt)
        if 2 * m == lc:
            attn = attn + part
        else:
            attn = attn + jnp.where((rt // (2 * m)) == (cs // (2 * m)), part, 0.0)
        m //= 2
    av = _dot(attn, v)
    hv = _iota((lc, W), 1) // HEAD_W
    for h in range(HEADS):
        o = o + jnp.where(hv == h, av[h * lc:(h + 1) * lc, :], 0.0)
    b_last = b[lc - 1:lc, :]
    blk = (_iota((W, GLA_QK), 0) // HEAD_W) == (_iota((W, GLA_QK), 1) // GLA_DK)
    s_ref[sq] = jnp.exp(b_last) * st + jnp.where(blk, _dot_tn(v, k * jnp.exp(b_last - b)), 0.0)
    ms = _dot_exact_rhs(o * o, _head_mean_mat())
    o_ref[sq] = o * lax.rsqrt(ms + 1e-6) * nw_ref[...] * _silu(g)


def _gla_prompt(u, au, ab, nw):
    bsz, seq, _ = u.shape
    lc = GLA_CHUNK
    W = BRANCH_W
    return pl.pallas_call(
        _gla_kernel,
        out_shape=[jax.ShapeDtypeStruct((bsz, seq, W), F32),
                   jax.ShapeDtypeStruct((bsz, W, GLA_QK), F32)],
        grid=(bsz // GLA_SEQS, seq // lc),
        in_specs=[pl.BlockSpec((GLA_SEQS, lc, W_GL), lambda b, c: (b, c, 0)),
                  _const_spec((128, GLA_QK)), _const_spec((1, GLA_QK)), _const_spec((1, W))],
        out_specs=[pl.BlockSpec((GLA_SEQS, lc, W), lambda b, c: (b, c, 0)),
                   pl.BlockSpec((GLA_SEQS, W, GLA_QK), lambda b, c: (b, 0, 0))],
        compiler_params=_cp("parallel", "arbitrary"),
        name="gla",
    )(u, au, ab, nw)


def _head_sum_mat(width=BRANCH_W, group=HEAD_W):
    r = _iota((width, width), 0) // group
    c = _iota((width, width), 1) // group
    return jnp.where(r == c, 1.0, 0.0).astype(BF16)


def _rwkv_pointwise(um, p_ref, lora_ref):
    W = BRANCH_W
    r = um[:, 0:W]
    k = um[:, W:2 * W]
    v = um[:, 2 * W:3 * W]
    lo = um[:, 3 * W:3 * W + 128]
    w_log = -_softplus(-(p_ref[0:1, :] + _dot(jnp.tanh(lo), lora_ref[0]))) - 0.5
    decay = jnp.exp(-jnp.exp(w_log))
    a = _sigmoid(p_ref[1:2, :] + _dot(lo, lora_ref[1]))
    g = _dot(_sigmoid(lo), lora_ref[2])
    hs = _head_sum_mat()
    kk = k * p_ref[2:3, :]
    kk = kk / jnp.maximum(jnp.sqrt(_dot_exact_rhs(kk * kk, hs)), 1e-12)
    k2 = k * (1.0 + (a - 1.0) * p_ref[3:4, :])
    bonus = _dot_exact_rhs(r * k2 * p_ref[4:5, :], hs)
    return decay, kk, kk * a, k2, r, v, g, bonus * v


RW_PREP_T = 128


def _rwkv_prep_kernel(u_ref, mu_ref, p_ref, lora_ref, *rest):
    planes, g_ref, bv_ref, ext_ref, xs_ref = rest[:6], rest[6], rest[7], rest[8], rest[9]
    T = RW_PREP_T
    half = T // 2

    @pl.when(pl.program_id(0) == 0)
    def _():
        ext_ref[:, 0:8, :] = jnp.zeros((ext_ref.shape[0], 8, RWKV_COLS), F32)

    lane = _iota((half, 128), 1)

    def per_sequence(b, carry):
        u = u_ref[b]
        ext_ref[b, 8:8 + T, :] = u
        prev = ext_ref[b, 7:7 + T, :]
        ext_ref[b, 0:8, :] = ext_ref[b, T:T + 8, :]
        um = u + mu_ref[...] * (prev - u)
        dec, kk, kka, k2, r, v, g, bv = _rwkv_pointwise(um, p_ref, lora_ref)
        g_ref[b] = g
        bv_ref[b] = bv
        for z_ref, val in zip(planes, (dec, kk, kka, k2, r, v)):
            for p in range(2):
                xs_ref[p] = val[:, 128 * p:128 * (p + 1)]
            for p in range(2):
                even = xs_ref[p, pl.ds(0, half, stride=2), :]
                odd = xs_ref[p, pl.ds(1, half, stride=2), :]
                z_ref[b * HEADS + 2 * p] = jnp.where(lane < HEAD_W, even, pltpu.roll(odd, HEAD_W, 1))
                z_ref[b * HEADS + 2 * p + 1] = jnp.where(lane < HEAD_W, pltpu.roll(even, HEAD_W, 1), odd)
        return carry

    lax.fori_loop(0, u_ref.shape[0], per_sequence, 0)


def _rwkv_prep(u, mu, p, lora):
    bsz, seq, _ = u.shape
    T = RW_PREP_T
    W = BRANCH_W
    nch = bsz * HEADS
    plane = jax.ShapeDtypeStruct((nch, seq // 2, 128), F32)
    nat = jax.ShapeDtypeStruct((bsz, seq, W), F32)
    return pl.pallas_call(
        _rwkv_prep_kernel,
        out_shape=[plane] * 6 + [nat, nat],
        grid=(seq // T,),
        in_specs=[pl.BlockSpec((bsz, T, RWKV_COLS), lambda i: (0, i, 0)),
                  _const_spec((1, RWKV_COLS)), _const_spec((8, W)), _const_spec((3, 128, W))],
        out_specs=[pl.BlockSpec((nch, T // 2, 128), lambda i: (0, i, 0))] * 6
        + [pl.BlockSpec((bsz, T, W), lambda i: (0, i, 0))] * 2,
        scratch_shapes=[pltpu.VMEM((bsz, T + 8, RWKV_COLS), F32), pltpu.VMEM((2, T, 128), F32)],
        compiler_params=_cp("arbitrary"),
        name="rwkv_prep",
    )(u, mu, p, lora)


def _delta_rule_step(s_ref, ni, w, kk, kka, k, r, v_row, y_store, s_out_ref=None):
    s_out = s_ref if s_out_ref is None else s_out_ref
    sa = [jnp.sum(s_ref[i] * kk, axis=0, keepdims=True) for i in range(ni)]
    for i in range(ni):
        sn = s_ref[i] * w - sa[i] * kka + v_row(i) * k
        s_out[i] = sn
        y_store(i, jnp.sum(sn * r, axis=0, keepdims=True))


RW_SCAN_PAIRS = 16


def _rwkv_scan_kernel(w_ref, kk_ref, kka_ref, k_ref, r_ref, v_ref, y_ref, s_out_ref, s_ref, t_ref):
    nch = w_ref.shape[0]
    rep = 128 // nch
    ni = s_ref.shape[1]
    nacc = 4

    @pl.when(pl.program_id(0) == 0)
    def _():
        s_ref[...] = jnp.zeros_like(s_ref)

    zs = (w_ref, kk_ref, kka_ref, k_ref, r_ref, v_ref)
    n_pairs = w_ref.shape[1]

    def stage(r, slot):
        for n, z_ref in enumerate(zs):
            rows = z_ref[:, r, :]
            step = ni if n == 5 else 0
            copies = [rows if s * step == 0 else pltpu.roll(rows, 128 - s * step, 1) for s in range(rep)]
            t_ref[slot, n] = jnp.concatenate(copies, axis=0).T

    def tokens(r, slot):
        for t2 in range(2):
            base = t2 * HEAD_W
            bc = lambda n, j: jnp.broadcast_to(t_ref[slot, n, pl.ds(base + j, 1), :], (ni, 128))
            parts = [None] * nacc
            for j in range(HEAD_W):
                p = s_ref[j] * bc(1, j)
                parts[j % nacc] = p if parts[j % nacc] is None else parts[j % nacc] + p
            sa = (parts[0] + parts[1]) + (parts[2] + parts[3])
            v = t_ref[slot, 5, pl.ds(base, ni), :]
            parts = [None] * nacc
            for j in range(HEAD_W):
                sn = s_ref[j] * bc(0, j) - sa * bc(2, j) + v * bc(3, j)
                s_ref[j] = sn
                p = sn * bc(4, j)
                parts[j % nacc] = p if parts[j % nacc] is None else parts[j % nacc] + p
            y_ref[2 * r + t2] = (parts[0] + parts[1]) + (parts[2] + parts[3])

    stage(0, 0)

    def two_pairs(q, carry):
        stage(2 * q + 1, 1)
        tokens(2 * q, 0)
        stage(jnp.minimum(2 * q + 2, n_pairs - 1), 0)
        tokens(2 * q + 1, 1)
        return carry

    lax.fori_loop(0, n_pairs // 2, two_pairs, 0)

    @pl.when(pl.program_id(0) == pl.num_programs(0) - 1)
    def _():
        s_out_ref[...] = s_ref[...]


def _rwkv_scan(planes, seq):
    nch = planes[0].shape[0]
    ni = HEAD_W * nch // 128
    tp = min(RW_SCAN_PAIRS, seq // 2)
    zspec = pl.BlockSpec((nch, tp, 128), lambda t: (0, t, 0))
    return pl.pallas_call(
        _rwkv_scan_kernel,
        out_shape=[jax.ShapeDtypeStruct((seq, ni, 128), F32), jax.ShapeDtypeStruct((HEAD_W, ni, 128), F32)],
        grid=(seq // (2 * tp),),
        in_specs=[zspec] * 6,
        out_specs=[pl.BlockSpec((2 * tp, ni, 128), lambda t: (t, 0, 0)),
                   pl.BlockSpec((HEAD_W, ni, 128), lambda t: (0, 0, 0))],
        scratch_shapes=[pltpu.VMEM((HEAD_W, ni, 128), F32), pltpu.VMEM((2, 6, 128, 128), F32)],
        compiler_params=_cp("arbitrary"),
        name="rwkv_scan",
    )(*planes)


(PC_W0, PC_A0, PC_KK, PC_KA, PC_RK, PC_LNW, PC_LNB, PC_MLNW, PC_GLNW,
 PC_RCW, PC_RCB, PC_RBA, PC_RBX, PC_RLAM) = (0, 1, 2, 3, 4, 5, 6, 7, 8, 9, 13, 14, 15, 16)
PC_COLS = 17
DEC_B = 128


def _rowsum(x):
    return jnp.sum(x, axis=0, keepdims=True)


def _decode_kernel(
        ur_ref, uk_ref, uv_ref, ulo_ref, pr_ref, pk_ref, pv_ref, plo_ref,
        mur_ref, muk_ref, muv_ref, mulo_ref, lora_ref, srw_in,
        mq_ref, mk_ref, mv_ref, mo_ref, mi_ref, mf_ref, bq_ref, bk_ref, cwq_ref, cwk_ref,
        c_in, n_ref, m_ref,
        gq_ref, gk_ref, gv_ref, gg_ref, ga_ref, au_ref, ab_ref, gs_in,
        rx_ref, ry_ref, rb_ref, wa_ref, wx_ref, h_ref,
        pc_ref,
        oa_ref, ob_ref, oc_ref, od_ref, srw_out, c_out, n_o, m_o, gs_out, h_o,
        y_scr, srw_ref, srw_o, c_ref, c_o, gs_ref, gs_o, gv_scr):
    col = lambda j: pc_ref[:, j:j + 1]

    def load_slabs(src_ref, scr_ref, width):
        per = 128 // width
        for p in range(src_ref.shape[1] // 128):
            t = src_ref[:, 128 * p:128 * (p + 1)].T
            for q in range(per):
                scr_ref[per * p + q] = t[width * q:width * (q + 1), :]

    def store_slabs(scr_ref, dst_ref, width):
        per = 128 // width
        for p in range(dst_ref.shape[1] // 128):
            t = jnp.concatenate([scr_ref[per * p + q] for q in range(per)], axis=0)
            dst_ref[:, 128 * p:128 * (p + 1)] = t.T

    load_slabs(srw_in, srw_ref, HEAD_W)
    load_slabs(c_in, c_ref, HEAD_W)
    load_slabs(gs_in, gs_ref, HEAD_W)

    def shift(u_ref, p_ref, mu_ref):
        u = u_ref[...]
        return u + mu_ref[...] * (p_ref[...] - u)
    r = shift(ur_ref, pr_ref, mur_ref)
    k = shift(uk_ref, pk_ref, muk_ref)
    v = shift(uv_ref, pv_ref, muv_ref)
    lo = shift(ulo_ref, plo_ref, mulo_ref)
    w_log = -_softplus(-(col(PC_W0) + _dot(lora_ref[0], jnp.tanh(lo)))) - 0.5
    decay = jnp.exp(-jnp.exp(w_log))
    a = _sigmoid(col(PC_A0) + _dot(lora_ref[1], lo))
    g = _dot(lora_ref[2], _sigmoid(lo))
    kk = k * col(PC_KK)
    kk = kk / jnp.maximum(jnp.sqrt(_rowsum(kk * kk)), 1e-12)
    k2 = k * (1.0 + (a - 1.0) * col(PC_KA))
    bonus = _rowsum(r * k2 * col(PC_RK))

    def y_store(i, row):
        y_scr[i:i + 1, :] = row
    _delta_rule_step(srw_ref, HEAD_W, decay, kk, kk * a, k2, r, lambda i: v[i:i + 1, :], y_store, srw_o)
    y = y_scr[...]
    yc = y - jnp.mean(y, axis=0, keepdims=True)
    var = jnp.mean(yc * yc, axis=0, keepdims=True)
    o = yc * lax.rsqrt(var + RWKV_GN_EPS) * col(PC_LNW) + col(PC_LNB)
    oa_ref[...] = (o + bonus * v) * g

    def conv(u_ref, b_ref, cw_ref):
        out = cw_ref[:, 4:5] + cw_ref[:, 3:4] * u_ref[...]
        for j in range(3):
            out = out + cw_ref[:, j:j + 1] * b_ref[j]
        return out
    q = _silu(conv(mq_ref, bq_ref, cwq_ref))
    k = _silu(conv(mk_ref, bk_ref, cwk_ref)) * (HEAD_W ** -0.5)
    v = mv_ref[...]
    ipre = mi_ref[...]
    logf = _log_sigmoid(mf_ref[...])
    m_prev = m_ref[...]
    m_t = jnp.maximum(logf + m_prev, ipre)
    inter = jnp.exp(logf + m_prev - m_t)
    wsc = jnp.exp(ipre - m_t)
    s = _rowsum(q * k) * wsc
    n_prev = n_ref[...]
    den = jnp.maximum(jnp.abs(s + inter * _rowsum(n_prev * q)), jnp.exp(-m_t))
    for i in range(HEAD_W):
        ci = c_ref[i]
        vi = v[i:i + 1, :]
        y_scr[i:i + 1, :] = (s * vi + inter * _rowsum(ci * q)) / den
        c_o[i] = inter * ci + (wsc * vi) * k
    n_o[...] = inter * n_prev + wsc * k
    m_o[...] = m_t
    y = y_scr[...]
    yc = y - jnp.mean(y, axis=0, keepdims=True)
    var = jnp.mean(yc * yc, axis=0, keepdims=True)
    ob_ref[...] = yc * lax.rsqrt(var + 1e-6) * col(PC_MLNW) * _sigmoid(mo_ref[...])

    q = gq_ref[...] * (GLA_DK ** -0.5)
    k = gk_ref[...]
    v = gv_ref[...]
    eb = jnp.exp(_log_sigmoid(_dot(au_ref[...], ga_ref[...]) + ab_ref[...]) / GLA_TAU)
    attn = _rowsum(q * k)
    qe = q * eb
    gv_scr[0] = qe
    gv_scr[1] = eb
    gv_scr[2] = k
    row = lambda n, d: jnp.broadcast_to(gv_scr[n, pl.ds(d, 1), :], (HEAD_W, DEC_B))
    y = attn * v
    for d in range(GLA_DK):
        sd = gs_ref[d]
        y = y + row(0, d) * sd
        gs_o[d] = row(1, d) * sd + row(2, d) * v
    ms = jnp.mean(y * y, axis=0, keepdims=True)
    oc_ref[...] = y * lax.rsqrt(ms + 1e-6) * col(PC_GLNW) * _silu(gg_ref[...])

    xc = col(PC_RCB) + col(PC_RCW + 3) * rx_ref[...]
    for j in range(3):
        xc = xc + col(PC_RCW + j) * rb_ref[j]
    rg = _sigmoid(_dot(wa_ref[...], xc) + col(PC_RBA))
    ig = _sigmoid(_dot(wx_ref[...], xc) + col(PC_RBX))
    log_a = -RG_C * rg * _softplus(-col(PC_RLAM))
    hn = jnp.exp(log_a) * h_ref[...] + jnp.sqrt(1.0 - jnp.exp(2.0 * log_a)) * (ig * xc)
    h_o[...] = hn
    od_ref[...] = hn * _gelu_tanh(ry_ref[...])

    store_slabs(srw_o, srw_out, HEAD_W)
    store_slabs(c_o, c_out, HEAD_W)
    store_slabs(gs_o, gs_out, HEAD_W)


def _decode_mixers(ut_rw, prev_t, mu_c, lora_t, s_rw,
                   ut_ml, mconv_t, ml_cw, c_st, n_st, m_st,
                   ut_gl, au_t, ab_c, g_st,
                   ut_rg, rconv_t, wa_t, wx_t, h_st, pcols):
    nb = DEC_B
    H, HW = HEADS, HEAD_W

    def blk(arr, view, block, index):
        a = arr.reshape(view)
        nd = len(block)
        return a, pl.BlockSpec(block, index)

    ins = []
    v14 = (14, HW, nb)
    v7 = (7, 128, nb)
    for arr in (ut_rw, prev_t):
        ins.append(blk(arr, v14, (None, HW, nb), lambda h: (h, 0, 0)))
        ins.append(blk(arr, v14, (None, HW, nb), lambda h: (4 + h, 0, 0)))
        ins.append(blk(arr, v14, (None, HW, nb), lambda h: (8 + h, 0, 0)))
        ins.append(blk(arr, v7, (None, 128, nb), lambda h: (6, 0, 0)))
    ins.append(blk(mu_c, (14, HW, 1), (None, HW, 1), lambda h: (h, 0, 0)))
    ins.append(blk(mu_c, (14, HW, 1), (None, HW, 1), lambda h: (4 + h, 0, 0)))
    ins.append(blk(mu_c, (14, HW, 1), (None, HW, 1), lambda h: (8 + h, 0, 0)))
    ins.append(blk(mu_c, (7, 128, 1), (None, 128, 1), lambda h: (6, 0, 0)))
    ins.append(blk(lora_t, (3, BRANCH_W, 128), (3, HW, 128), lambda h: (0, h, 0)))
    st_spec = lambda n: pl.BlockSpec((nb, n), lambda h: (0, h))
    ins.append((s_rw.reshape(nb, H * HW * HW), st_spec(HW * HW)))
    v20 = (20, HW, nb)
    for j in range(4):
        ins.append(blk(ut_ml, v20, (None, HW, nb), functools.partial(lambda h, j: (4 * j + h, 0, 0), j=j)))
    ins.append(blk(ut_ml[1024:1028], (4, 1, nb), (None, 1, nb), lambda h: (h, 0, 0)))
    ins.append(blk(ut_ml[1152:1156], (4, 1, nb), (None, 1, nb), lambda h: (h, 0, 0)))
    ins.append(blk(mconv_t, (3, 8, HW, nb), (3, None, HW, nb), lambda h: (0, h, 0, 0)))
    ins.append(blk(mconv_t, (3, 8, HW, nb), (3, None, HW, nb), lambda h: (0, 4 + h, 0, 0)))
    ins.append(blk(ml_cw, (8, HW, 5), (None, HW, 5), lambda h: (h, 0, 0)))
    ins.append(blk(ml_cw, (8, HW, 5), (None, HW, 5), lambda h: (4 + h, 0, 0)))
    ins.append((c_st.reshape(nb, H * HW * HW), st_spec(HW * HW)))
    ins.append(blk(n_st, (H, HW, nb), (None, HW, nb), lambda h: (h, 0, 0)))
    ins.append(blk(m_st, (H, 1, nb), (None, 1, nb), lambda h: (h, 0, 0)))
    ins.append(blk(ut_gl, (28, GLA_DK, nb), (None, GLA_DK, nb), lambda h: (h, 0, 0)))
    ins.append(blk(ut_gl, (28, GLA_DK, nb), (None, GLA_DK, nb), lambda h: (4 + h, 0, 0)))
    ins.append(blk(ut_gl, v14, (None, HW, nb), lambda h: (4 + h, 0, 0)))
    ins.append(blk(ut_gl, v14, (None, HW, nb), lambda h: (8 + h, 0, 0)))
    ins.append(blk(ut_gl, v7, (None, 128, nb), lambda h: (6, 0, 0)))
    ins.append(blk(au_t, (H, GLA_DK, 128), (None, GLA_DK, 128), lambda h: (h, 0, 0)))
    ins.append(blk(ab_c, (H, GLA_DK, 1), (None, GLA_DK, 1), lambda h: (h, 0, 0)))
    ins.append((g_st.reshape(nb, H * GLA_DK * HW), st_spec(GLA_DK * HW)))
    ins.append(blk(ut_rg, (8, HW, nb), (None, HW, nb), lambda h: (h, 0, 0)))
    ins.append(blk(ut_rg, (8, HW, nb), (None, HW, nb), lambda h: (4 + h, 0, 0)))
    ins.append(blk(rconv_t, (3, H, HW, nb), (3, None, HW, nb), lambda h: (0, h, 0, 0)))
    ins.append((wa_t, pl.BlockSpec((None, HW, HW), lambda h: (h, 0, 0))))
    ins.append((wx_t, pl.BlockSpec((None, HW, HW), lambda h: (h, 0, 0))))
    ins.append(blk(h_st, (H, HW, nb), (None, HW, nb), lambda h: (h, 0, 0)))
    ins.append(blk(pcols, (H, HW, PC_COLS), (None, HW, PC_COLS), lambda h: (h, 0, 0)))

    vec = lambda: (jax.ShapeDtypeStruct((H, HW, nb), F32), pl.BlockSpec((None, HW, nb), lambda h: (h, 0, 0)))
    mat = lambda n: (jax.ShapeDtypeStruct((nb, H * n), F32), st_spec(n))
    outs = [vec(), vec(), vec(), vec(), mat(HW * HW), mat(HW * HW), vec(),
            (jax.ShapeDtypeStruct((H, 1, nb), F32), pl.BlockSpec((None, 1, nb), lambda h: (h, 0, 0))),
            mat(GLA_DK * HW), vec()]
    return pl.pallas_call(
        _decode_kernel,
        out_shape=[o[0] for o in outs],
        grid=(H,),
        in_specs=[s for _, s in ins],
        out_specs=[o[1] for o in outs],
        scratch_shapes=[pltpu.VMEM((HW, nb), F32)]
        + [pltpu.VMEM((HW, HW, nb), F32)] * 4 + [pltpu.VMEM((GLA_DK, HW, nb), F32)] * 2
        + [pltpu.VMEM((3, GLA_DK, nb), F32)],
        compiler_params=_cp("parallel"),
        name="decode_mixers",
    )(*[a for a, _ in ins])


def _pad_cols(a, width):
    return jnp.pad(a, ((0, 0), (0, width - a.shape[1])))


def _pad_rows(a, rows, at=0):
    return jnp.pad(a, ((at, rows - at - a.shape[0]), (0, 0)))


def _pack_layer(p, l):
    offs = np.concatenate([[0], np.cumsum(IN_SIZES)])
    w_in, b_in = p['w_in'][l], p['b_in'][l][None, :]
    seg = lambda a, i: a[:, int(offs[i]):int(offs[i + 1])]

    def regroup(a):
        return jnp.concatenate(
            [seg(a, 0), seg(a, 1), seg(a, 2), seg(a, 3), _pad_cols(seg(a, 4), 128), _pad_cols(seg(a, 5), 128),
             seg(a, 6), seg(a, 7), seg(a, 8), seg(a, 9), _pad_cols(seg(a, 10), 128), seg(a, 11), seg(a, 12)],
            axis=1)
    k = dict(
        w_pack=regroup(w_in).astype(BF16), b_pack=regroup(b_in),
        w_gate=seg(w_in, 13).astype(BF16), b_gate=seg(b_in, 13),
        w_branch=p['w_branch'][l].astype(BF16), w_out=p['w_out'][l].astype(BF16),
        ln1_g=p['ln1_g'][l][None], ln1_b=p['ln1_b'][l][None],
        ln2_g=p['ln2_g'][l][None], ln2_b=p['ln2_b'][l][None],
    )
    w_up, a_up, g_up = p['rwkv_w_up'][l], p['rwkv_a_up'][l], p['rwkv_g_up'][l]
    lora = jnp.stack([_pad_rows(w_up, 128, 0), _pad_rows(a_up, 128, 32), _pad_rows(g_up, 128, 64)])
    k['rw_lora'] = lora.astype(BF16)
    k['rw_lora_t'] = jnp.swapaxes(lora, 1, 2).astype(BF16)
    k['rw_mu'] = p['rwkv_mu'][l][None]
    rw_rows = [p['rwkv_w0'][l], p['rwkv_a0'][l], p['rwkv_k_k'][l], p['rwkv_k_a'][l],
               p['rwkv_r_k'][l].reshape(BRANCH_W)]
    k['rw_p'] = jnp.stack(rw_rows + [jnp.zeros((BRANCH_W,), F32)] * 3)
    k['rw_ln'] = jnp.stack([p['rwkv_ln_w'][l], p['rwkv_ln_b'][l]])
    k['ml_cw'], k['ml_cb'] = p['mlstm_conv_w'][l], p['mlstm_conv_b'][l][None]
    k['ml_nw'] = p['mlstm_norm_w'][l][None]
    k['gl_au'] = _pad_rows(p['gla_alpha_up'][l], 128).astype(BF16)
    k['gl_ab'] = p['gla_alpha_b'][l][None]
    k['gl_nw'] = p['gla_norm_w'][l][None]
    wa, wx = p['rglru_wa'][l], p['rglru_wx'][l]
    eye = jnp.eye(HEADS, dtype=F32)
    bd = lambda w: jnp.einsum('gh,gij->gihj', eye, w).reshape(BRANCH_W, BRANCH_W).astype(BF16)
    k['rg_cw'], k['rg_cb'] = p['rglru_conv_w'][l], p['rglru_conv_b'][l][None]
    k['rg_wa'], k['rg_wx'] = bd(wa), bd(wx)
    k['rg_ba'], k['rg_bx'] = p['rglru_ba'][l][None], p['rglru_bx'][l][None]
    k['rg_lam'] = p['rglru_lambda'][l][None]
    k['rg_wa_t'] = jnp.swapaxes(wa, 1, 2).astype(BF16)
    k['rg_wx_t'] = jnp.swapaxes(wx, 1, 2).astype(BF16)
    cols = rw_rows + [p['rwkv_ln_w'][l], p['rwkv_ln_b'][l], p['mlstm_norm_w'][l], p['gla_norm_w'][l]]
    cols += [p['rglru_conv_w'][l][j] for j in range(4)]
    cols += [p['rglru_conv_b'][l], p['rglru_ba'][l], p['rglru_bx'][l], p['rglru_lambda'][l]]
    k['pcols'] = jnp.stack(cols, axis=1)
    k['ml_cw_t'] = jnp.concatenate([p['mlstm_conv_w'][l].T, p['mlstm_conv_b'][l][:, None]], axis=1)
    return k


def _diag_blocks(a, rb, cb):
    return jnp.stack([a[:, h * rb:(h + 1) * rb, h * cb:(h + 1) * cb] for h in range(HEADS)], axis=1)


def _prompt_mixers(x, k):
    B, L, _ = x.shape
    W = BRANCH_W
    u_rw, u_ml, u_gl, u_rg = _inproj(x.reshape(B * L, D_MODEL), k['w_pack'], k['b_pack'], 512)
    u_rw, u_ml, u_gl, u_rg = (u.reshape(B, L, -1) for u in (u_rw, u_ml, u_gl, u_rg))
    nch = B * HEADS
    rep = 128 // nch
    *planes, g, bv = _rwkv_prep(u_rw, k['rw_mu'], k['rw_p'], k['rw_lora'])
    y, s1 = _rwkv_scan(planes, L)
    y = y.reshape(L, HEAD_W // rep, rep, B, HEADS).transpose(3, 0, 4, 2, 1).reshape(B, L, W)
    s_rw = s1.reshape(HEAD_W, HEAD_W // rep, rep, B, HEADS).transpose(3, 4, 2, 1, 0).reshape(B, HEADS, HEAD_W, HEAD_W)
    flat = lambda a: a.reshape(B * L, W)
    o_a = (flat(y), flat(g), flat(bv), k['rw_ln'])
    o_b, c_bd, n_row, m_row = _mlstm_prompt(u_ml, k['ml_cw'], k['ml_cb'], k['ml_nw'])
    o_c, s_bd = _gla_prompt(u_gl, k['gl_au'], k['gl_ab'], k['gl_nw'])
    o_d, h1 = _rglru_prompt(u_rg, k['rg_cw'], k['rg_cb'], k['rg_wa'], k['rg_ba'], k['rg_wx'], k['rg_bx'],
                            k['rg_lam'])
    states = (u_rw[:, L - 1], s_rw, u_ml[:, L - 3:, :2 * W], _diag_blocks(c_bd, HEAD_W, HEAD_W),
              n_row.reshape(B, HEADS, HEAD_W), m_row[:, :HEADS, 0],
              jnp.swapaxes(_diag_blocks(s_bd, HEAD_W, GLA_DK), 2, 3), u_rg[:, L - 3:, :W], h1[:, 0])
    return [o_a] + [flat(o) for o in (o_b, o_c, o_d)], states


def _sample_mixers(x, st, k):
    W = BRANCH_W
    sh0, S0, mconv0, C0, n0, m0, gS0, rconv0, h0 = st
    u_rw, u_ml, u_gl, u_rg = _inproj(x, k['w_pack'], k['b_pack'], DEC_B)
    outs = _decode_mixers(
        u_rw.T, sh0.T, k['rw_mu'].T, k['rw_lora_t'], S0,
        u_ml.T, mconv0.transpose(1, 2, 0), k['ml_cw_t'], C0, n0.transpose(1, 2, 0), m0.T,
        u_gl.T, k['gl_au'].T, k['gl_ab'].T, gS0,
        u_rg.T, rconv0.transpose(1, 2, 0), k['rg_wa_t'], k['rg_wx_t'], h0.T, k['pcols'])
    oa, ob, oc, od, s_rw, c_st, n_st, m_st, g_st, h_st = outs
    branches = [o.reshape(W, DEC_B).T for o in (oa, ob, oc, od)]
    states = (u_rw, s_rw.reshape(S0.shape),
              jnp.concatenate([mconv0[:, 1:], u_ml[:, None, :2 * W]], axis=1),
              c_st.reshape(C0.shape), n_st.transpose(2, 0, 1), m_st[:, 0, :].T,
              g_st.reshape(gS0.shape),
              jnp.concatenate([rconv0[:, 1:], u_rg[:, None, :W]], axis=1), h_st.reshape(W, DEC_B).T)
    return branches, states


def _trunk(xp, bl, xs, sample_states, packs, moe):
    tms = (512, DEC_B)
    new_p, new_s = [], []
    for l in range(DEPTH):
        k = packs[l]
        br_p, st_p = _prompt_mixers(xp.reshape(bl[0], bl[1], D_MODEL), k)
        br_s, st_s = _sample_mixers(xs, tuple(s[l] for s in sample_states), k)
        new_p.append(st_p)
        new_s.append(st_s)
        xp, xs = (_merge(x, br, k['w_gate'], k['b_gate'], k['w_branch'], k['w_out'], k['ln1_g'], k['ln1_b'], tm)
                  for x, br, tm in ((xp, br_p, tms[0]), (xs, br_s, tms[1])))
        j = l // 2
        if l % 2 == 0:
            xp, xs = (_ffn(x, moe['ffn_wg'][j], moe['ffn_wu'][j], moe['ffn_wd'][j], k['ln2_g'], k['ln2_b'], tm)
                      for x, tm in ((xp, tms[0]), (xs, tms[1])))
        else:
            xp, xs = _moe_layer([xp, xs], moe['router'][j], moe['router_b'][j], moe['moe_wg'][j],
                                moe['moe_wu'][j], moe['moe_wd'][j], k['ln2_g'], k['ln2_b'])
    stack = lambda sts: [jnp.stack([st[i] for st in sts], axis=0) for i in range(9)]
    return xp, xs, stack(new_p), stack(new_s)


def kernel(x_prompt, x_sample, state_rwkv_shift, state_rwkv_S, state_mlstm_conv, state_mlstm_C,
           state_mlstm_n, state_mlstm_m, state_gla_S, state_rglru_conv, state_rglru_h,
           w_in, b_in, rwkv_mu, rwkv_w0, rwkv_w_up, rwkv_a0, rwkv_a_up, rwkv_g_up, rwkv_k_k,
           rwkv_k_a, rwkv_r_k, rwkv_ln_w, rwkv_ln_b, mlstm_conv_w, mlstm_conv_b, mlstm_norm_w,
           gla_alpha_up, gla_alpha_b, gla_norm_w, rglru_conv_w, rglru_conv_b, rglru_wa, rglru_ba,
           rglru_wx, rglru_bx, rglru_lambda, w_branch, w_out, ln1_g, ln1_b, ffn_wg, ffn_wu, ffn_wd,
           moe_router, moe_router_b, moe_wg, moe_wu, moe_wd, ln2_g, ln2_b):
    p = dict(w_in=w_in, b_in=b_in, rwkv_mu=rwkv_mu, rwkv_w0=rwkv_w0, rwkv_w_up=rwkv_w_up,
             rwkv_a0=rwkv_a0, rwkv_a_up=rwkv_a_up, rwkv_g_up=rwkv_g_up, rwkv_k_k=rwkv_k_k,
             rwkv_k_a=rwkv_k_a, rwkv_r_k=rwkv_r_k, rwkv_ln_w=rwkv_ln_w, rwkv_ln_b=rwkv_ln_b,
             mlstm_conv_w=mlstm_conv_w, mlstm_conv_b=mlstm_conv_b, mlstm_norm_w=mlstm_norm_w,
             gla_alpha_up=gla_alpha_up, gla_alpha_b=gla_alpha_b, gla_norm_w=gla_norm_w,
             rglru_conv_w=rglru_conv_w, rglru_conv_b=rglru_conv_b, rglru_wa=rglru_wa,
             rglru_ba=rglru_ba, rglru_wx=rglru_wx, rglru_bx=rglru_bx, rglru_lambda=rglru_lambda,
             w_branch=w_branch, w_out=w_out, ln1_g=ln1_g, ln1_b=ln1_b, ln2_g=ln2_g, ln2_b=ln2_b)
    packs = [_pack_layer(p, l) for l in range(DEPTH)]
    moe = dict(ffn_wg=ffn_wg.astype(BF16), ffn_wu=ffn_wu.astype(BF16), ffn_wd=ffn_wd.astype(BF16),
               router=jnp.pad(moe_router, ((0, 0), (0, 0), (0, 128 - N_EXPERTS))),
               router_b=jnp.pad(moe_router_b, ((0, 0), (0, 128 - N_EXPERTS)))[:, None, :],
               moe_wg=moe_wg.astype(BF16), moe_wu=moe_wu.astype(BF16), moe_wd=moe_wd.astype(BF16))
    B, L, _ = x_prompt.shape
    sample_states = (state_rwkv_shift, state_rwkv_S, state_mlstm_conv, state_mlstm_C, state_mlstm_n,
                     state_mlstm_m, state_gla_S, state_rglru_conv, state_rglru_h)
    nb, ls, _ = x_sample.shape
    y_p, y_s, ps, ss = _trunk(x_prompt.reshape(B * L, D_MODEL), (B, L), x_sample.reshape(nb * ls, D_MODEL),
                              sample_states, packs, moe)
    return (y_p.reshape(B, L, D_MODEL), y_s.reshape(nb, ls, D_MODEL), *ps, *ss)
```

```python
import functools

import jax
import jax.numpy as jnp
import numpy as np
from jax import lax
from jax.experimental import pallas as pl
from jax.experimental.pallas import tpu as pltpu

F32 = jnp.float32
BF16 = jnp.bfloat16

D_MODEL = 1024
DEPTH = 2
N_BRANCH = 4
BRANCH_W = 256
HEADS = 4
HEAD_W = 64
RWKV_COLS = 896
RWKV_GN_EPS = 64e-5
GLA_DK = 32
GLA_LORA = 16
GLA_TAU = 16.0
RG_C = 8.0
D_FF = 2816
N_EXPERTS = 8
DN_ALPHA = (2 * DEPTH) ** 0.25
LN_EPS = 1e-5

IN_SIZES = (RWKV_COLS, 512, 256, 256, 4, 4, 128, 128, 256, 256, 16, 256, 256, 4096)

W_RW = 896
W_ML = 1280
W_GL = 896
W_RG = 512
W_PACK = W_RW + W_ML + W_GL + W_RG

VMEM_LIMIT = 56 * 1024 * 1024


def _cp(*sem):
    return pltpu.CompilerParams(dimension_semantics=sem, vmem_limit_bytes=VMEM_LIMIT)


def _dot(a, b):
    return jnp.dot(a.astype(BF16), b.astype(BF16), preferred_element_type=F32)


def _dot_nt(a, b):
    return lax.dot_general(a.astype(BF16), b.astype(BF16), (((1,), (1,)), ((), ())),
                           preferred_element_type=F32)


def _dot_tn(a, b):
    return lax.dot_general(a.astype(BF16), b.astype(BF16), (((0,), (0,)), ((), ())),
                           preferred_element_type=F32)


def _split(x):
    hi = x.astype(BF16)
    lo = (x - hi.astype(F32)).astype(BF16)
    return hi, lo


def _dot_exact_lhs(a, x):
    hi, lo = _split(x)
    a = a.astype(BF16)
    return (jnp.dot(a, hi, preferred_element_type=F32) + jnp.dot(a, lo, preferred_element_type=F32))


def _dot_exact_rhs(x, a):
    hi, lo = _split(x)
    a = a.astype(BF16)
    return (jnp.dot(hi, a, preferred_element_type=F32) + jnp.dot(lo, a, preferred_element_type=F32))


def _dot3(x, w):
    xh, xl = _split(x)
    wh, wl = _split(w)
    return (jnp.dot(xh, wh, preferred_element_type=F32) + jnp.dot(xl, wh, preferred_element_type=F32)
            + jnp.dot(xh, wl, preferred_element_type=F32))


def _sigmoid(x):
    return 1.0 / (1.0 + jnp.exp(-x))


def _softplus(x):
    return jnp.maximum(x, 0.0) + jnp.log(1.0 + jnp.exp(-jnp.abs(x)))


def _log_sigmoid(x):
    return -_softplus(-x)


def _silu(x):
    return x * _sigmoid(x)


def _gelu_tanh(x):
    c = np.float32(np.sqrt(2.0 / np.pi))
    return 0.5 * x * (1.0 + jnp.tanh(c * (x + 0.044715 * (x * x * x))))


def _layer_norm(y, g, b):
    mu = jnp.mean(y, axis=-1, keepdims=True)
    yc = y - mu
    var = jnp.mean(yc * yc, axis=-1, keepdims=True)
    return yc * lax.rsqrt(var + LN_EPS) * g + b


def _iota(shape, dim):
    return lax.broadcasted_iota(jnp.int32, shape, dim)


def _head_mean_mat(width=BRANCH_W, group=HEAD_W):
    r = _iota((width, width), 0) // group
    c = _iota((width, width), 1) // group
    return jnp.where(r == c, 1.0 / group, 0.0).astype(BF16)


def _tril_ones(n):
    return jnp.where(_iota((n, n), 0) >= _iota((n, n), 1), 1.0, 0.0).astype(BF16)


def _const_spec(shape):
    nd = len(shape)
    return pl.BlockSpec(shape, lambda *_: (0,) * nd)


def _inproj_kernel(x_ref, w_ref, b_ref, o_rw, o_ml, o_gl, o_rg):
    x = x_ref[...].astype(BF16)
    off = 0
    for o_ref in (o_rw, o_ml, o_gl, o_rg):
        n = o_ref.shape[-1]
        o_ref[...] = (jnp.dot(x, w_ref[:, off:off + n], preferred_element_type=F32)
                      + b_ref[:, off:off + n])
        off += n


def _inproj(x, w, b, tm):
    n = x.shape[0]
    widths = (W_RW, W_ML, W_GL, W_RG)
    return pl.pallas_call(
        _inproj_kernel,
        out_shape=[jax.ShapeDtypeStruct((n, wd), F32) for wd in widths],
        grid=(n // tm,),
        in_specs=[pl.BlockSpec((tm, D_MODEL), lambda i: (i, 0)),
                  _const_spec((D_MODEL, W_PACK)), _const_spec((1, W_PACK))],
        out_specs=[pl.BlockSpec((tm, wd), lambda i: (i, 0)) for wd in widths],
        compiler_params=_cp("parallel"),
        name="inproj",
    )(x, w, b)


def _merge_body(x, outs, wg_ref, bg_ref, wb_ref, wo_ref, g_ref, b_ref, out_ref):
    xb = x.astype(BF16)
    merged = None
    for g, o in enumerate(outs):
        sl = slice(g * D_MODEL, (g + 1) * D_MODEL)
        gate = _sigmoid(jnp.dot(xb, wg_ref[:, sl], preferred_element_type=F32) + bg_ref[:, sl])
        up = jnp.dot(o.astype(BF16), wb_ref[g], preferred_element_type=F32)
        merged = gate * up if merged is None else merged + gate * up
    out = jnp.dot(merged.astype(BF16), wo_ref[...], preferred_element_type=F32)
    out_ref[...] = _layer_norm(DN_ALPHA * x + out, g_ref[...], b_ref[...])


def _merge_kernel(x_ref, oa, ob, oc, od, *rest):
    _merge_body(x_ref[...], [oa[...], ob[...], oc[...], od[...]], *rest)


def _rwkv_out(y, g, bv, ln_ref):
    pm = _head_mean_mat()
    yc = y - _dot_exact_rhs(y, pm)
    var = _dot_exact_rhs(yc * yc, pm)
    return (yc * lax.rsqrt(var + RWKV_GN_EPS) * ln_ref[0:1, :] + ln_ref[1:2, :] + bv) * g


def _merge_rwkv_kernel(x_ref, y_ref, gr_ref, bv_ref, ln_ref, ob, oc, od, *rest):
    oa = _rwkv_out(y_ref[...], gr_ref[...], bv_ref[...], ln_ref)
    _merge_body(x_ref[...], [oa, ob[...], oc[...], od[...]], *rest)


def _merge(x, branches, wg, bg, wb, wo, ln_g, ln_b, tm):
    n = x.shape[0]
    tok = lambda wd: pl.BlockSpec((tm, wd), lambda i: (i, 0))
    if isinstance(branches[0], tuple):
        y, g, bv, ln = branches[0]
        kern, first, first_specs = _merge_rwkv_kernel, (y, g, bv, ln), [tok(BRANCH_W)] * 3 + [_const_spec((2, BRANCH_W))]
    else:
        kern, first, first_specs = _merge_kernel, (branches[0],), [tok(BRANCH_W)]
    return pl.pallas_call(
        kern,
        out_shape=jax.ShapeDtypeStruct((n, D_MODEL), F32),
        grid=(n // tm,),
        in_specs=[tok(D_MODEL)] + first_specs + [tok(BRANCH_W)] * 3 + [
            _const_spec((D_MODEL, N_BRANCH * D_MODEL)), _const_spec((1, N_BRANCH * D_MODEL)),
            _const_spec((N_BRANCH, BRANCH_W, D_MODEL)), _const_spec((D_MODEL, D_MODEL)),
            _const_spec((1, D_MODEL)), _const_spec((1, D_MODEL))],
        out_specs=tok(D_MODEL),
        compiler_params=_cp("parallel"),
        name="merge",
    )(x, *first, *branches[1:], wg, bg, wb, wo, ln_g, ln_b)


FF_CHUNK = D_FF // 2


def _ffn_kernel(x_ref, wg_ref, wu_ref, wd_ref, g_ref, b_ref, out_ref):
    x = x_ref[...]
    xb = x.astype(BF16)
    acc = None
    for c in range(D_FF // FF_CHUNK):
        sl = slice(c * FF_CHUNK, (c + 1) * FF_CHUNK)
        h = (_silu(jnp.dot(xb, wg_ref[:, sl], preferred_element_type=F32))
             * jnp.dot(xb, wu_ref[:, sl], preferred_element_type=F32))
        part = jnp.dot(h.astype(BF16), wd_ref[sl, :], preferred_element_type=F32)
        acc = part if acc is None else acc + part
    out_ref[...] = _layer_norm(DN_ALPHA * x + acc, g_ref[...], b_ref[...])


def _ffn(x, wg, wu, wd, ln_g, ln_b, tm):
    n = x.shape[0]
    tok = pl.BlockSpec((tm, D_MODEL), lambda i: (i, 0))
    return pl.pallas_call(
        _ffn_kernel,
        out_shape=jax.ShapeDtypeStruct((n, D_MODEL), F32),
        grid=(n // tm,),
        in_specs=[tok, _const_spec((D_MODEL, D_FF)), _const_spec((D_MODEL, D_FF)),
                  _const_spec((D_FF, D_MODEL)), _const_spec((1, D_MODEL)), _const_spec((1, D_MODEL))],
        out_specs=tok,
        compiler_params=_cp("parallel"),
        name="ffn",
    )(x, wg, wu, wd, ln_g, ln_b)


(RT_E1, RT_E2, RT_R1, RT_R2, RT_W1, RT_W2) = range(6)
MOE_TILE = 512


def _router_kernel(x_ref, w_ref, b_ref, cin_ref, meta_ref, cnt_ref, carry_ref):
    @pl.when(pl.program_id(0) == 0)
    def _():
        carry_ref[...] = cin_ref[...]

    logits = _dot3(x_ref[...], w_ref[...]) + b_ref[...]
    tm = logits.shape[0]
    lane = _iota(logits.shape, 1)
    neg = jnp.float32(-jnp.inf)
    logits = jnp.where(lane < N_EXPERTS, logits, neg)
    m1 = jnp.max(logits, axis=-1, keepdims=True)
    i1 = jnp.min(jnp.where(logits == m1, lane, 128), axis=-1, keepdims=True)
    rest = jnp.where(lane == i1, neg, logits)
    m2 = jnp.max(rest, axis=-1, keepdims=True)
    i2 = jnp.min(jnp.where(rest == m2, lane, 128), axis=-1, keepdims=True)
    e2 = jnp.exp(m2 - m1)
    w1 = 1.0 / (1.0 + e2)
    w2 = e2 / (1.0 + e2)
    oh1 = lane == i1
    oh2 = lane == i2
    picks = jnp.where(oh1 | oh2, 1.0, 0.0)
    below = jnp.where(_iota((tm, tm), 0) > _iota((tm, tm), 1), 1.0, 0.0).astype(BF16)
    base = carry_ref[...] + jnp.dot(below, picks.astype(BF16), preferred_element_type=F32)
    r1 = jnp.sum(jnp.where(oh1, base, 0.0), axis=-1, keepdims=True)
    r2 = jnp.sum(jnp.where(oh2, base, 0.0), axis=-1, keepdims=True)
    carry_ref[...] = carry_ref[...] + jnp.sum(picks, axis=0, keepdims=True)
    cnt_ref[...] = carry_ref[...]
    meta = jnp.zeros(logits.shape, F32)
    for ln, val in ((RT_E1, i1.astype(F32)), (RT_E2, i2.astype(F32)), (RT_R1, r1), (RT_R2, r2),
                    (RT_W1, w1), (RT_W2, w2)):
        meta = jnp.where(lane == ln, val, meta)
    meta_ref[...] = meta


def _router(x, w, b, cnt_in, tm):
    n = x.shape[0]
    return pl.pallas_call(
        _router_kernel,
        out_shape=[jax.ShapeDtypeStruct((n, 128), F32), jax.ShapeDtypeStruct((1, 128), F32)],
        grid=(n // tm,),
        in_specs=[pl.BlockSpec((tm, D_MODEL), lambda i: (i, 0)),
                  _const_spec((D_MODEL, 128)), _const_spec((1, 128)), _const_spec((1, 128))],
        out_specs=[pl.BlockSpec((tm, 128), lambda i: (i, 0)), _const_spec((1, 128))],
        scratch_shapes=[pltpu.VMEM((1, 128), F32)],
        compiler_params=_cp("arbitrary"),
        name="router",
    )(x, w, b, cnt_in)


def _row_copies(n_rows, start_one, wait_shape_src, wait_shape_dst, sem, n_streams):
    def body(r, carry):
        start_one(r)
        return carry
    lax.fori_loop(0, n_rows, body, 0, unroll=8)
    for _ in range(n_streams):
        pltpu.make_async_copy(wait_shape_src, wait_shape_dst, sem).wait()


def _dispatch_kernel(d1_ref, d2_ref, x_ref, xs_in_ref, xs_ref, sem):
    del xs_in_ref
    tm = x_ref.shape[0]

    def start_one(r):
        src = x_ref.at[pl.ds(r, 1)]
        pltpu.make_async_copy(src, xs_ref.at[pl.ds(d1_ref[0, 0, r], 1)], sem).start(priority=0)
        pltpu.make_async_copy(src, xs_ref.at[pl.ds(d2_ref[0, 0, r], 1)], sem).start(priority=1)
    _row_copies(tm, start_one, x_ref, xs_ref.at[pl.ds(0, tm)], sem, 2)


def _dispatch(x, d1, d2, xs, tm):
    n = x.shape[0]
    idx = lambda a: a.reshape(n // tm, 1, tm)
    ispec = pl.BlockSpec((1, 1, tm), lambda i: (i, 0, 0), memory_space=pltpu.SMEM)
    return pl.pallas_call(
        _dispatch_kernel,
        out_shape=jax.ShapeDtypeStruct(xs.shape, F32),
        grid=(n // tm,),
        in_specs=[ispec, ispec, pl.BlockSpec((tm, D_MODEL), lambda i: (i, 0)),
                  pl.BlockSpec(memory_space=pl.ANY)],
        out_specs=pl.BlockSpec(memory_space=pl.ANY),
        scratch_shapes=[pltpu.SemaphoreType.DMA],
        input_output_aliases={3: 0},
        compiler_params=_cp("arbitrary"),
        name="moe_dispatch",
    )(idx(d1), idx(d2), x, xs)


def _experts_kernel(te_ref, nu_ref, xs_ref, wg_ref, wu_ref, wd_ref, ys_ref):
    del te_ref

    @pl.when(pl.program_id(0) < nu_ref[0])
    def _():
        xb = xs_ref[...].astype(BF16)
        acc = None
        for c in range(D_FF // FF_CHUNK):
            sl = slice(c * FF_CHUNK, (c + 1) * FF_CHUNK)
            h = (_silu(jnp.dot(xb, wg_ref[:, sl], preferred_element_type=F32))
                 * jnp.dot(xb, wu_ref[:, sl], preferred_element_type=F32))
            part = jnp.dot(h.astype(BF16), wd_ref[sl, :], preferred_element_type=F32)
            acc = part if acc is None else acc + part
        ys_ref[...] = acc

    @pl.when(pl.program_id(0) >= nu_ref[0])
    def _():
        ys_ref[...] = jnp.zeros_like(ys_ref)


def _experts(xs, tile_expert, n_used, wg, wu, wd):
    n_tiles = xs.shape[0] // MOE_TILE
    row = pl.BlockSpec((MOE_TILE, D_MODEL), lambda i, te, nu: (i, 0))
    return pl.pallas_call(
        _experts_kernel,
        out_shape=jax.ShapeDtypeStruct(xs.shape, F32),
        grid_spec=pltpu.PrefetchScalarGridSpec(
            num_scalar_prefetch=2, grid=(n_tiles,),
            in_specs=[row,
                      pl.BlockSpec((None, D_MODEL, D_FF), lambda i, te, nu: (te[i], 0, 0)),
                      pl.BlockSpec((None, D_MODEL, D_FF), lambda i, te, nu: (te[i], 0, 0)),
                      pl.BlockSpec((None, D_FF, D_MODEL), lambda i, te, nu: (te[i], 0, 0))],
            out_specs=row),
        compiler_params=_cp("arbitrary"),
        name="moe_experts",
    )(tile_expert, n_used, xs, wg, wu, wd)


def _combine_kernel(d1_ref, d2_ref, x_ref, meta_ref, ys_ref, g_ref, b_ref, out_ref, y1_ref, y2_ref, sem):
    tm = x_ref.shape[0]

    def start_one(r):
        pltpu.make_async_copy(ys_ref.at[pl.ds(d1_ref[0, 0, r], 1)], y1_ref.at[pl.ds(r, 1)], sem).start(priority=0)
        pltpu.make_async_copy(ys_ref.at[pl.ds(d2_ref[0, 0, r], 1)], y2_ref.at[pl.ds(r, 1)], sem).start(priority=1)
    _row_copies(tm, start_one, ys_ref.at[pl.ds(0, tm)], y1_ref, sem, 2)
    meta = meta_ref[...]
    f = meta[:, RT_W1:RT_W1 + 1] * y1_ref[...] + meta[:, RT_W2:RT_W2 + 1] * y2_ref[...]
    out_ref[...] = _layer_norm(DN_ALPHA * x_ref[...] + f, g_ref[...], b_ref[...])


def _combine(x, meta, d1, d2, ys, ln_g, ln_b, tm):
    n = x.shape[0]
    idx = lambda a: a.reshape(n // tm, 1, tm)
    ispec = pl.BlockSpec((1, 1, tm), lambda i: (i, 0, 0), memory_space=pltpu.SMEM)
    tok = lambda wd_: pl.BlockSpec((tm, wd_), lambda i: (i, 0))
    return pl.pallas_call(
        _combine_kernel,
        out_shape=jax.ShapeDtypeStruct((n, D_MODEL), F32),
        grid=(n // tm,),
        in_specs=[ispec, ispec, tok(D_MODEL), tok(128), pl.BlockSpec(memory_space=pl.ANY),
                  _const_spec((1, D_MODEL)), _const_spec((1, D_MODEL))],
        out_specs=tok(D_MODEL),
        scratch_shapes=[pltpu.VMEM((tm, D_MODEL), F32), pltpu.VMEM((tm, D_MODEL), F32),
                        pltpu.SemaphoreType.DMA],
        compiler_params=_cp("arbitrary"),
        name="moe_combine",
    )(idx(d1), idx(d2), x, meta, ys, ln_g, ln_b)


def _moe_layer(xs_groups, router_w, router_b, wg, wu, wd, ln_g, ln_b):
    tms = [min(512, x.shape[0]) for x in xs_groups]
    cnt = jnp.zeros((1, 128), F32)
    metas = []
    for x, tm in zip(xs_groups, tms):
        meta, cnt = _router(x, router_w, router_b, cnt, tm)
        metas.append(meta)
    n_total = sum(x.shape[0] for x in xs_groups)
    n_rows = -(-(2 * n_total + N_EXPERTS * (MOE_TILE - 1)) // MOE_TILE) * MOE_TILE
    n_tiles = n_rows // MOE_TILE
    counts = cnt[0, :N_EXPERTS].astype(jnp.int32)
    padded = (counts + MOE_TILE - 1) // MOE_TILE * MOE_TILE
    ends = jnp.cumsum(padded)
    starts = ends - padded
    tile_expert = jnp.minimum(
        jnp.sum((jnp.arange(n_tiles, dtype=jnp.int32)[:, None] * MOE_TILE >= ends[None, :]).astype(jnp.int32), axis=1),
        N_EXPERTS - 1).astype(jnp.int32)
    n_used = (ends[N_EXPERTS - 1] // MOE_TILE).astype(jnp.int32).reshape(1)
    xs = jnp.zeros((n_rows, D_MODEL), F32)
    dests = []
    for x, meta, tm in zip(xs_groups, metas, tms):
        e = meta[:, RT_E1:RT_E2 + 1].astype(jnp.int32)
        base = jnp.sum(jnp.where(e[:, :, None] == jnp.arange(N_EXPERTS, dtype=jnp.int32), starts, 0), axis=-1)
        d = base + meta[:, RT_R1:RT_R2 + 1].astype(jnp.int32)
        dests.append((d[:, 0], d[:, 1]))
        xs = _dispatch(x, d[:, 0], d[:, 1], xs, tm)
    ys = _experts(xs, tile_expert, n_used, wg, wu, wd)
    outs = []
    for x, meta, (d1, d2), tm in zip(xs_groups, metas, dests, tms):
        outs.append(_combine(x, meta, d1, d2, ys, ln_g, ln_b, tm))
    return outs


def _shift_rows(x, s, fill):
    rolled = pltpu.roll(x, s, 0)
    return jnp.where(_iota(x.shape, 0) >= s, rolled, fill)


def _causal_conv4(ext_ref, x, cw_ref, cb_ref, lc):
    ext_ref[8:8 + lc, :] = x
    out = cb_ref[...] + cw_ref[3:4, :] * x
    for j in range(3):
        out = out + cw_ref[j:j + 1, :] * ext_ref[5 + j:5 + j + lc, :]
    ext_ref[0:8, :] = ext_ref[lc:lc + 8, :]
    return out


RG_CHUNK = 256


RG_SEQS = 4


def _rglru_kernel(u_ref, cw_ref, cb_ref, wa_ref, ba_ref, wx_ref, bx_ref, lam_ref,
                  y_ref, h_ref, ext_ref, hc_ref):
    lc = RG_CHUNK

    @pl.when(pl.program_id(1) == 0)
    def _():
        ext_ref[:, 0:8, :] = jnp.zeros((ext_ref.shape[0], 8, BRANCH_W), F32)
        hc_ref[...] = jnp.zeros_like(hc_ref)

    for sq in range(u_ref.shape[0]):
        xc = _causal_conv4(ext_ref.at[sq], u_ref[sq, :, 0:BRANCH_W], cw_ref, cb_ref, lc)
        r = _sigmoid(_dot(xc, wa_ref[...]) + ba_ref[...])
        i = _sigmoid(_dot(xc, wx_ref[...]) + bx_ref[...])
        log_a = -RG_C * r * _softplus(-lam_ref[...])
        a = jnp.exp(log_a)
        u = jnp.sqrt(1.0 - jnp.exp(2.0 * log_a)) * (i * xc)
        s = 1
        while s < lc:
            u = u + a * _shift_rows(u, s, 0.0)
            a = a * _shift_rows(a, s, 1.0)
            s *= 2
        h = u + a * hc_ref[sq]
        hc_ref[sq] = h[lc - 1:lc, :]
        h_ref[sq] = h[lc - 1:lc, :]
        y_ref[sq] = h * _gelu_tanh(u_ref[sq, :, BRANCH_W:2 * BRANCH_W])


def _rglru_prompt(u, cw, cb, wa, ba, wx, bx, lam):
    bsz, seq, _ = u.shape
    lc = RG_CHUNK
    ns = RG_SEQS
    return pl.pallas_call(
        _rglru_kernel,
        out_shape=[jax.ShapeDtypeStruct((bsz, seq, BRANCH_W), F32),
                   jax.ShapeDtypeStruct((bsz, 1, BRANCH_W), F32)],
        grid=(bsz // ns, seq // lc),
        in_specs=[pl.BlockSpec((ns, lc, W_RG), lambda b, c: (b, c, 0)),
                  _const_spec((4, BRANCH_W)), _const_spec((1, BRANCH_W)),
                  _const_spec((BRANCH_W, BRANCH_W)), _const_spec((1, BRANCH_W)),
                  _const_spec((BRANCH_W, BRANCH_W)), _const_spec((1, BRANCH_W)),
                  _const_spec((1, BRANCH_W))],
        out_specs=[pl.BlockSpec((ns, lc, BRANCH_W), lambda b, c: (b, c, 0)),
                   pl.BlockSpec((ns, 1, BRANCH_W), lambda b, c: (b, 0, 0))],
        scratch_shapes=[pltpu.VMEM((ns, lc + 8, BRANCH_W), F32), pltpu.VMEM((ns, 1, BRANCH_W), F32)],
        compiler_params=_cp("parallel", "arbitrary"),
        name="rglru",
    )(u, cw, cb, wa, ba, wx, bx, lam)


ML_CHUNK = 128


ML_SEQS = 8


def _mlstm_kernel(u_ref, cw_ref, cb_ref, nw_ref, o_ref, c_ref, n_ref, m_ref, ext_ref):
    @pl.when(pl.program_id(1) == 0)
    def _():
        ext_ref[:, 0:8, :] = jnp.zeros((ext_ref.shape[0], 8, 2 * BRANCH_W), F32)
        c_ref[...] = jnp.zeros_like(c_ref)
        n_ref[...] = jnp.zeros_like(n_ref)
        m_ref[...] = jnp.zeros_like(m_ref)

    for sq in range(u_ref.shape[0]):
        _mlstm_chunk(sq, u_ref, cw_ref, cb_ref, nw_ref, o_ref, c_ref, n_ref, m_ref, ext_ref.at[sq])


def _mlstm_chunk(sq, u_ref, cw_ref, cb_ref, nw_ref, o_ref, c_ref, n_ref, m_ref, ext_ref):
    lc = ML_CHUNK
    W = BRANCH_W
    qk = _silu(_causal_conv4(ext_ref, u_ref[sq, :, 0:2 * W], cw_ref, cb_ref, lc))
    q = qk[:, 0:W]
    k = qk[:, W:2 * W] * (HEAD_W ** -0.5)
    v = u_ref[sq, :, 2 * W:3 * W]
    og = u_ref[sq, :, 3 * W:4 * W]
    ipre = u_ref[sq, :, 4 * W:4 * W + 128]
    logf = _log_sigmoid(u_ref[sq, :, 4 * W + 128:4 * W + 256])
    bcum = _dot_exact_lhs(_tril_ones(lc), logf)
    rowsrc = ipre - bcum
    rows_t = rowsrc.T
    lane = _iota((1, W), 1) // HEAD_W
    hq = _iota((lc, W), 1) // HEAD_W
    cmat = c_ref[sq]
    nrow = n_ref[sq]
    kb = k.astype(BF16)
    vb = v.astype(BF16)
    stack = lambda f: jnp.concatenate([f(h) for h in range(HEADS)], axis=0)
    b_col = stack(lambda h: bcum[:, h:h + 1])
    m_prev = stack(lambda h: jnp.broadcast_to(m_ref[sq, h:h + 1, 0:1], (lc, 1)))
    causal = (_iota((HEADS * lc, lc), 0) % lc) >= _iota((HEADS * lc, lc), 1)
    log_d = jnp.where(causal, b_col + stack(lambda h: jnp.broadcast_to(rows_t[h:h + 1, :], (lc, lc))), -jnp.inf)
    log_inter = b_col + m_prev
    m_t = jnp.maximum(log_inter, jnp.max(log_d, axis=-1, keepdims=True))
    dmat = jnp.exp(log_d - m_t)
    inter = jnp.exp(log_inter - m_t)
    qs = stack(lambda h: jnp.where(hq == h, q, 0.0))
    s = _dot_nt(qs, kb) * dmat
    num = _dot(s, vb) + inter * _dot_nt(qs, cmat)
    dot = (jnp.sum(s, axis=-1, keepdims=True)
           + inter * jnp.sum(qs * nrow, axis=-1, keepdims=True))
    hh = num / jnp.maximum(jnp.abs(dot), jnp.exp(-m_t))
    h_acc = jnp.zeros((lc, W), F32)
    ws_full = jnp.zeros((lc, W), F32)
    keep_full = jnp.zeros((1, W), F32)
    for h in range(HEADS):
        hm = lane == h
        h_acc = h_acc + jnp.where(hm, hh[h * lc:(h + 1) * lc, :], 0.0)
        m_new = m_t[(h + 1) * lc - 1:(h + 1) * lc, :]
        b_last = bcum[lc - 1:lc, h:h + 1]
        ws_col = jnp.exp(rowsrc[:, h:h + 1] + (b_last - m_new))
        keep = jnp.exp(b_last + m_ref[sq, h:h + 1, 0:1] - m_new)
        ws_full = ws_full + jnp.where(hm, ws_col, 0.0)
        keep_full = keep_full + jnp.where(hm, keep, 0.0)
        m_ref[sq, h:h + 1, :] = jnp.broadcast_to(m_new, (1, 128))
    blk = (_iota((W, W), 0) // HEAD_W) == (_iota((W, W), 1) // HEAD_W)
    vw = v * ws_full
    c_ref[sq] = keep_full * cmat + jnp.where(blk, _dot_tn(vw, kb), 0.0)
    n_ref[sq] = keep_full * nrow + jnp.sum(ws_full * k, axis=0, keepdims=True)
    pm = _head_mean_mat()
    mu = _dot_exact_rhs(h_acc, pm)
    hc = h_acc - mu
    var = _dot_exact_rhs(hc * hc, pm)
    o_ref[sq] = hc * lax.rsqrt(var + 1e-6) * nw_ref[...] * _sigmoid(og)


def _mlstm_prompt(u, cw, cb, nw):
    bsz, seq, _ = u.shape
    lc = ML_CHUNK
    W = BRANCH_W
    return pl.pallas_call(
        _mlstm_kernel,
        out_shape=[jax.ShapeDtypeStruct((bsz, seq, W), F32),
                   jax.ShapeDtypeStruct((bsz, W, W), F32),
                   jax.ShapeDtypeStruct((bsz, 1, W), F32),
                   jax.ShapeDtypeStruct((bsz, 8, 128), F32)],
        grid=(bsz // ML_SEQS, seq // lc),
        in_specs=[pl.BlockSpec((ML_SEQS, lc, W_ML), lambda b, c: (b, c, 0)),
                  _const_spec((4, 2 * W)), _const_spec((1, 2 * W)), _const_spec((1, W))],
        out_specs=[pl.BlockSpec((ML_SEQS, lc, W), lambda b, c: (b, c, 0)),
                   pl.BlockSpec((ML_SEQS, W, W), lambda b, c: (b, 0, 0)),
                   pl.BlockSpec((ML_SEQS, 1, W), lambda b, c: (b, 0, 0)),
                   pl.BlockSpec((ML_SEQS, 8, 128), lambda b, c: (b, 0, 0))],
        scratch_shapes=[pltpu.VMEM((ML_SEQS, lc + 8, 2 * W), F32)],
        compiler_params=_cp("parallel", "arbitrary"),
        name="mlstm",
    )(u, cw, cb, nw)


GLA_CHUNK = 128
GLA_QK = HEADS * GLA_DK


def _gla_level_ref(b, m, lc):
    if 2 * m == lc:
        return jnp.broadcast_to(b[m - 1:m, :], b.shape)
    if m >= 4:
        nb = lc // (2 * m)
        mid = b.reshape(nb, 2 * m, b.shape[1])[:, m - 1:m, :]
        return jnp.broadcast_to(mid, (nb, 2 * m, b.shape[1])).reshape(b.shape)
    pos = _iota(b.shape, 0) % (2 * m)
    out = b
    for p in range(2 * m):
        sh = p - (m - 1)
        if sh != 0:
            out = jnp.where(pos == p, pltpu.roll(b, sh % lc, 0), out)
    return out


GLA_SEQS = 8


def _gla_kernel(u_ref, au_ref, ab_ref, nw_ref, o_ref, s_ref):
    @pl.when(pl.program_id(1) == 0)
    def _():
        s_ref[...] = jnp.zeros_like(s_ref)

    for sq in range(u_ref.shape[0]):
        _gla_chunk(sq, u_ref, au_ref, ab_ref, nw_ref, o_ref, s_ref)


def _gla_chunk(sq, u_ref, au_ref, ab_ref, nw_ref, o_ref, s_ref):
    lc = GLA_CHUNK
    W = BRANCH_W
    q = u_ref[sq, :, 0:GLA_QK] * (GLA_DK ** -0.5)
    k = u_ref[sq, :, GLA_QK:2 * GLA_QK]
    v = u_ref[sq, :, 2 * GLA_QK:2 * GLA_QK + W]
    g = u_ref[sq, :, 2 * GLA_QK + W:2 * GLA_QK + 2 * W]
    ac = u_ref[sq, :, 2 * GLA_QK + 2 * W:2 * GLA_QK + 2 * W + 128]
    la = _log_sigmoid(_dot(ac, au_ref[...]) + ab_ref[...]) / GLA_TAU
    b = _dot_exact_lhs(_tril_ones(lc), la)
    st = s_ref[sq]
    o = _dot_nt(q * jnp.exp(b), st)
    gsum = ((_iota((GLA_QK, W), 0) // GLA_DK) == (_iota((GLA_QK, W), 1) // HEAD_W)).astype(BF16)
    o = o + jnp.dot((q * k).astype(BF16), gsum, preferred_element_type=F32) * v

    hq = _iota((lc, GLA_QK), 1) // GLA_DK
    row1 = _iota((lc, 1), 0)
    rt = _iota((HEADS * lc, lc), 0) % lc
    cs = _iota((HEADS * lc, lc), 1)
    kb_rows = None
    attn = jnp.zeros((HEADS * lc, lc), F32)
    m = lc // 2
    while m >= 1:
        ref = _gla_level_ref(b, m, lc)
        upper = (row1 % (2 * m)) >= m
        qt = q * jnp.exp(jnp.where(upper, b - ref, -jnp.inf))
        kt = k * jnp.exp(jnp.where(upper, -jnp.inf, ref - b))
        qs = jnp.concatenate([jnp.where(hq == h, qt, 0.0) for h in range(HEADS)], axis=0)
        part = _dot_nt(qs, kt)
        if 2 * m == lc:
            attn = attn + part
        else:
            attn = attn + jnp.where((rt // (2 * m)) == (cs // (2 * m)), part, 0.0)
        m //= 2
    av = _dot(attn, v)
    hv = _iota((lc, W), 1) // HEAD_W
    for h in range(HEADS):
        o = o + jnp.where(hv == h, av[h * lc:(h + 1) * lc, :], 0.0)
    b_last = b[lc - 1:lc, :]
    blk = (_iota((W, GLA_QK), 0) // HEAD_W) == (_iota((W, GLA_QK), 1) // GLA_DK)
    s_ref[sq] = jnp.exp(b_last) * st + jnp.where(blk, _dot_tn(v, k * jnp.exp(b_last - b)), 0.0)
    ms = _dot_exact_rhs(o * o, _head_mean_mat())
    o_ref[sq] = o * lax.rsqrt(ms + 1e-6) * nw_ref[...] * _silu(g)


def _gla_prompt(u, au, ab, nw):
    bsz, seq, _ = u.shape
    lc = GLA_CHUNK
    W = BRANCH_W
    return pl.pallas_call(
        _gla_kernel,
        out_shape=[jax.ShapeDtypeStruct((bsz, seq, W), F32),
                   jax.ShapeDtypeStruct((bsz, W, GLA_QK), F32)],
        grid=(bsz // GLA_SEQS, seq // lc),
        in_specs=[pl.BlockSpec((GLA_SEQS, lc, W_GL), lambda b, c: (b, c, 0)),
                  _const_spec((128, GLA_QK)), _const_spec((1, GLA_QK)), _const_spec((1, W))],
        out_specs=[pl.BlockSpec((GLA_SEQS, lc, W), lambda b, c: (b, c, 0)),
                   pl.BlockSpec((GLA_SEQS, W, GLA_QK), lambda b, c: (b, 0, 0))],
        compiler_params=_cp("parallel", "arbitrary"),
        name="gla",
    )(u, au, ab, nw)


def _head_sum_mat(width=BRANCH_W, group=HEAD_W):
    r = _iota((width, width), 0) // group
    c = _iota((width, width), 1) // group
    return jnp.where(r == c, 1.0, 0.0).astype(BF16)


def _rwkv_pointwise(um, p_ref, lora_ref):
    W = BRANCH_W
    r = um[:, 0:W]
    k = um[:, W:2 * W]
    v = um[:, 2 * W:3 * W]
    lo = um[:, 3 * W:3 * W + 128]
    w_log = -_softplus(-(p_ref[0:1, :] + _dot(jnp.tanh(lo), lora_ref[0]))) - 0.5
    decay = jnp.exp(-jnp.exp(w_log))
    a = _sigmoid(p_ref[1:2, :] + _dot(lo, lora_ref[1]))
    g = _dot(_sigmoid(lo), lora_ref[2])
    hs = _head_sum_mat()
    kk = k * p_ref[2:3, :]
    kk = kk / jnp.maximum(jnp.sqrt(_dot_exact_rhs(kk * kk, hs)), 1e-12)
    k2 = k * (1.0 + (a - 1.0) * p_ref[3:4, :])
    bonus = _dot_exact_rhs(r * k2 * p_ref[4:5, :], hs)
    return decay, kk, kk * a, k2, r, v, g, bonus * v


RW_PREP_T = 128


def _rwkv_prep_kernel(u_ref, mu_ref, p_ref, lora_ref, *rest):
    planes, g_ref, bv_ref, ext_ref, xs_ref = rest[:6], rest[6], rest[7], rest[8], rest[9]
    T = RW_PREP_T
    half = T // 2
    tp = RW_SCAN_PAIRS

    @pl.when(pl.program_id(0) == 0)
    def _():
        ext_ref[:, 0:8, :] = jnp.zeros((ext_ref.shape[0], 8, RWKV_COLS), F32)

    lane = _iota((half, 128), 1)

    def per_sequence(b, carry):
        u = u_ref[b]
        ext_ref[b, 8:8 + T, :] = u
        prev = ext_ref[b, 7:7 + T, :]
        ext_ref[b, 0:8, :] = ext_ref[b, T:T + 8, :]
        um = u + mu_ref[...] * (prev - u)
        dec, kk, kka, k2, r, v, g, bv = _rwkv_pointwise(um, p_ref, lora_ref)
        g_ref[b] = g
        bv_ref[b] = bv
        for z_ref, val in zip(planes, (dec, kk, kka, k2, r, v)):
            for p in range(2):
                xs_ref[p] = val[:, 128 * p:128 * (p + 1)]
            for p in range(2):
                even = xs_ref[p, pl.ds(0, half, stride=2), :]
                odd = xs_ref[p, pl.ds(1, half, stride=2), :]
                tiles = (jnp.where(lane < HEAD_W, even, pltpu.roll(odd, HEAD_W, 1)),
                         jnp.where(lane < HEAD_W, pltpu.roll(even, HEAD_W, 1), odd))
                for hh, tile in enumerate(tiles):
                    row0 = pl.multiple_of((b * HEADS + 2 * p + hh) * tp, tp)
                    for q in range(half // tp):
                        z_ref[q, pl.ds(row0, tp), :] = tile[q * tp:(q + 1) * tp, :]
        return carry

    lax.fori_loop(0, u_ref.shape[0], per_sequence, 0, unroll=2)


def _rwkv_prep(u, mu, p, lora):
    bsz, seq, _ = u.shape
    T = RW_PREP_T
    W = BRANCH_W
    nch = bsz * HEADS
    tp = RW_SCAN_PAIRS
    nq = T // 2 // tp
    plane = jax.ShapeDtypeStruct((seq // 2 // tp, nch * tp, 128), F32)
    nat = jax.ShapeDtypeStruct((bsz, seq, W), F32)
    return pl.pallas_call(
        _rwkv_prep_kernel,
        out_shape=[plane] * 6 + [nat, nat],
        grid=(seq // T,),
        in_specs=[pl.BlockSpec((bsz, T, RWKV_COLS), lambda i: (0, i, 0)),
                  _const_spec((1, RWKV_COLS)), _const_spec((8, W)), _const_spec((3, 128, W))],
        out_specs=[pl.BlockSpec((nq, nch * tp, 128), lambda i: (i, 0, 0))] * 6
        + [pl.BlockSpec((bsz, T, W), lambda i: (0, i, 0))] * 2,
        scratch_shapes=[pltpu.VMEM((bsz, T + 8, RWKV_COLS), F32), pltpu.VMEM((2, T, 128), F32)],
        compiler_params=_cp("arbitrary"),
        name="rwkv_prep",
    )(u, mu, p, lora)


def _delta_rule_step(s_ref, ni, w, kk, kka, k, r, v_row, y_store, s_out_ref=None):
    s_out = s_ref if s_out_ref is None else s_out_ref
    sa = [jnp.sum(s_ref[i] * kk, axis=0, keepdims=True) for i in range(ni)]
    for i in range(ni):
        sn = s_ref[i] * w - sa[i] * kka + v_row(i) * k
        s_out[i] = sn
        y_store(i, jnp.sum(sn * r, axis=0, keepdims=True))


RW_SCAN_PAIRS = 16


def _rwkv_scan_kernel(w_ref, kk_ref, kka_ref, k_ref, r_ref, v_ref, y_ref, s_out_ref, s_ref, t_ref):
    n_pairs = RW_SCAN_PAIRS
    nch = w_ref.shape[0] // n_pairs
    rep = 128 // nch
    ni = s_ref.shape[1]
    nacc = 4

    @pl.when(pl.program_id(0) == 0)
    def _():
        s_ref[...] = jnp.zeros_like(s_ref)

    zs = (w_ref, kk_ref, kka_ref, k_ref, r_ref, v_ref)

    def stage(r, slot):
        for n, z_ref in enumerate(zs):
            rows = z_ref[pl.ds(r, nch, stride=n_pairs), :]
            step = ni if n == 5 else 0
            copies = [rows if s * step == 0 else pltpu.roll(rows, 128 - s * step, 1) for s in range(rep)]
            t_ref[slot, n] = jnp.concatenate(copies, axis=0).T

    def tokens(r, slot):
        for t2 in range(2):
            base = t2 * HEAD_W
            bc = lambda n, j: jnp.broadcast_to(t_ref[slot, n, pl.ds(base + j, 1), :], (ni, 128))
            parts = [None] * nacc
            for j in range(HEAD_W):
                p = s_ref[j] * bc(1, j)
                parts[j % nacc] = p if parts[j % nacc] is None else parts[j % nacc] + p
            sa = (parts[0] + parts[1]) + (parts[2] + parts[3])
            v = t_ref[slot, 5, pl.ds(base, ni), :]
            parts = [None] * nacc
            for j in range(HEAD_W):
                sn = s_ref[j] * bc(0, j) - sa * bc(2, j) + v * bc(3, j)
                s_ref[j] = sn
                p = sn * bc(4, j)
                parts[j % nacc] = p if parts[j % nacc] is None else parts[j % nacc] + p
            y_ref[2 * r + t2] = (parts[0] + parts[1]) + (parts[2] + parts[3])

    stage(0, 0)

    def two_pairs(q, carry):
        stage(2 * q + 1, 1)
        tokens(2 * q, 0)
        stage(jnp.minimum(2 * q + 2, n_pairs - 1), 0)
        tokens(2 * q + 1, 1)
        return carry

    lax.fori_loop(0, n_pairs // 2, two_pairs, 0)

    @pl.when(pl.program_id(0) == pl.num_programs(0) - 1)
    def _():
        s_out_ref[...] = s_ref[...]


def _rwkv_scan(planes, seq):
    tp = RW_SCAN_PAIRS
    nch = planes[0].shape[1] // tp
    ni = HEAD_W * nch // 128
    zspec = pl.BlockSpec((None, nch * tp, 128), lambda t: (t, 0, 0))
    return pl.pallas_call(
        _rwkv_scan_kernel,
        out_shape=[jax.ShapeDtypeStruct((seq, ni, 128), F32), jax.ShapeDtypeStruct((HEAD_W, ni, 128), F32)],
        grid=(seq // (2 * tp),),
        in_specs=[zspec] * 6,
        out_specs=[pl.BlockSpec((2 * tp, ni, 128), lambda t: (t, 0, 0)),
                   pl.BlockSpec((HEAD_W, ni, 128), lambda t: (0, 0, 0))],
        scratch_shapes=[pltpu.VMEM((HEAD_W, ni, 128), F32), pltpu.VMEM((2, 6, 128, 128), F32)],
        compiler_params=_cp("arbitrary"),
        name="rwkv_scan",
    )(*planes)


(PC_W0, PC_A0, PC_KK, PC_KA, PC_RK, PC_LNW, PC_LNB, PC_MLNW, PC_GLNW,
 PC_RCW, PC_RCB, PC_RBA, PC_RBX, PC_RLAM) = (0, 1, 2, 3, 4, 5, 6, 7, 8, 9, 13, 14, 15, 16)
PC_COLS = 17
DEC_B = 128


def _rowsum(x):
    return jnp.sum(x, axis=0, keepdims=True)


def _decode_kernel(
        ur_ref, uk_ref, uv_ref, ulo_ref, pr_ref, pk_ref, pv_ref, plo_ref,
        mur_ref, muk_ref, muv_ref, mulo_ref, lora_ref, srw_in,
        mq_ref, mk_ref, mv_ref, mo_ref, mi_ref, mf_ref, bq_ref, bk_ref, cwq_ref, cwk_ref,
        c_in, n_ref, m_ref,
        gq_ref, gk_ref, gv_ref, gg_ref, ga_ref, au_ref, ab_ref, gs_in,
        rx_ref, ry_ref, rb_ref, wa_ref, wx_ref, h_ref,
        pc_ref,
        oa_ref, ob_ref, oc_ref, od_ref, srw_out, c_out, n_o, m_o, gs_out, h_o,
        y_scr, srw_ref, srw_o, c_ref, c_o, gs_ref, gs_o, gv_scr):
    col = lambda j: pc_ref[:, j:j + 1]

    def load_slabs(src_ref, scr_ref, width):
        per = 128 // width
        for p in range(src_ref.shape[1] // 128):
            t = src_ref[:, 128 * p:128 * (p + 1)].T
            for q in range(per):
                scr_ref[per * p + q] = t[width * q:width * (q + 1), :]

    def store_slabs(scr_ref, dst_ref, width):
        per = 128 // width
        for p in range(dst_ref.shape[1] // 128):
            t = jnp.concatenate([scr_ref[per * p + q] for q in range(per)], axis=0)
            dst_ref[:, 128 * p:128 * (p + 1)] = t.T

    load_slabs(srw_in, srw_ref, HEAD_W)
    load_slabs(c_in, c_ref, HEAD_W)
    load_slabs(gs_in, gs_ref, HEAD_W)

    def shift(u_ref, p_ref, mu_ref):
        u = u_ref[...]
        return u + mu_ref[...] * (p_ref[...] - u)
    r = shift(ur_ref, pr_ref, mur_ref)
    k = shift(uk_ref, pk_ref, muk_ref)
    v = shift(uv_ref, pv_ref, muv_ref)
    lo = shift(ulo_ref, plo_ref, mulo_ref)
    w_log = -_softplus(-(col(PC_W0) + _dot(lora_ref[0], jnp.tanh(lo)))) - 0.5
    decay = jnp.exp(-jnp.exp(w_log))
    a = _sigmoid(col(PC_A0) + _dot(lora_ref[1], lo))
    g = _dot(lora_ref[2], _sigmoid(lo))
    kk = k * col(PC_KK)
    kk = kk / jnp.maximum(jnp.sqrt(_rowsum(kk * kk)), 1e-12)
    k2 = k * (1.0 + (a - 1.0) * col(PC_KA))
    bonus = _rowsum(r * k2 * col(PC_RK))

    def y_store(i, row):
        y_scr[i:i + 1, :] = row
    _delta_rule_step(srw_ref, HEAD_W, decay, kk, kk * a, k2, r, lambda i: v[i:i + 1, :], y_store, srw_o)
    y = y_scr[...]
    yc = y - jnp.mean(y, axis=0, keepdims=True)
    var = jnp.mean(yc * yc, axis=0, keepdims=True)
    o = yc * lax.rsqrt(var + RWKV_GN_EPS) * col(PC_LNW) + col(PC_LNB)
    oa_ref[...] = (o + bonus * v) * g

    def conv(u_ref, b_ref, cw_ref):
        out = cw_ref[:, 4:5] + cw_ref[:, 3:4] * u_ref[...]
        for j in range(3):
            out = out + cw_ref[:, j:j + 1] * b_ref[j]
        return out
    q = _silu(conv(mq_ref, bq_ref, cwq_ref))
    k = _silu(conv(mk_ref, bk_ref, cwk_ref)) * (HEAD_W ** -0.5)
    v = mv_ref[...]
    ipre = mi_ref[...]
    logf = _log_sigmoid(mf_ref[...])
    m_prev = m_ref[...]
    m_t = jnp.maximum(logf + m_prev, ipre)
    inter = jnp.exp(logf + m_prev - m_t)
    wsc = jnp.exp(ipre - m_t)
    s = _rowsum(q * k) * wsc
    n_prev = n_ref[...]
    den = jnp.maximum(jnp.abs(s + inter * _rowsum(n_prev * q)), jnp.exp(-m_t))
    for i in range(HEAD_W):
        ci = c_ref[i]
        vi = v[i:i + 1, :]
        y_scr[i:i + 1, :] = (s * vi + inter * _rowsum(ci * q)) / den
        c_o[i] = inter * ci + (wsc * vi) * k
    n_o[...] = inter * n_prev + wsc * k
    m_o[...] = m_t
    y = y_scr[...]
    yc = y - jnp.mean(y, axis=0, keepdims=True)
    var = jnp.mean(yc * yc, axis=0, keepdims=True)
    ob_ref[...] = yc * lax.rsqrt(var + 1e-6) * col(PC_MLNW) * _sigmoid(mo_ref[...])

    q = gq_ref[...] * (GLA_DK ** -0.5)
    k = gk_ref[...]
    v = gv_ref[...]
    eb = jnp.exp(_log_sigmoid(_dot(au_ref[...], ga_ref[...]) + ab_ref[...]) / GLA_TAU)
    attn = _rowsum(q * k)
    qe = q * eb
    gv_scr[0] = qe
    gv_scr[1] = eb
    gv_scr[2] = k
    row = lambda n, d: jnp.broadcast_to(gv_scr[n, pl.ds(d, 1), :], (HEAD_W, DEC_B))
    y = attn * v
    for d in range(GLA_DK):
        sd = gs_ref[d]
        y = y + row(0, d) * sd
        gs_o[d] = row(1, d) * sd + row(2, d) * v
    ms = jnp.mean(y * y, axis=0, keepdims=True)
    oc_ref[...] = y * lax.rsqrt(ms + 1e-6) * col(PC_GLNW) * _silu(gg_ref[...])

    xc = col(PC_RCB) + col(PC_RCW + 3) * rx_ref[...]
    for j in range(3):
        xc = xc + col(PC_RCW + j) * rb_ref[j]
    rg = _sigmoid(_dot(wa_ref[...], xc) + col(PC_RBA))
    ig = _sigmoid(_dot(wx_ref[...], xc) + col(PC_RBX))
    log_a = -RG_C * rg * _softplus(-col(PC_RLAM))
    hn = jnp.exp(log_a) * h_ref[...] + jnp.sqrt(1.0 - jnp.exp(2.0 * log_a)) * (ig * xc)
    h_o[...] = hn
    od_ref[...] = hn * _gelu_tanh(ry_ref[...])

    store_slabs(srw_o, srw_out, HEAD_W)
    store_slabs(c_o, c_out, HEAD_W)
    store_slabs(gs_o, gs_out, HEAD_W)


def _decode_mixers(ut_rw, prev_t, mu_c, lora_t, s_rw,
                   ut_ml, mconv_t, ml_cw, c_st, n_st, m_st,
                   ut_gl, au_t, ab_c, g_st,
                   ut_rg, rconv_t, wa_t, wx_t, h_st, pcols):
    nb = DEC_B
    H, HW = HEADS, HEAD_W

    def blk(arr, view, block, index):
        a = arr.reshape(view)
        nd = len(block)
        return a, pl.BlockSpec(block, index)

    ins = []
    v14 = (14, HW, nb)
    v7 = (7, 128, nb)
    for arr in (ut_rw, prev_t):
        ins.append(blk(arr, v14, (None, HW, nb), lambda h: (h, 0, 0)))
        ins.append(blk(arr, v14, (None, HW, nb), lambda h: (4 + h, 0, 0)))
        ins.append(blk(arr, v14, (None, HW, nb), lambda h: (8 + h, 0, 0)))
        ins.append(blk(arr, v7, (None, 128, nb), lambda h: (6, 0, 0)))
    ins.append(blk(mu_c, (14, HW, 1), (None, HW, 1), lambda h: (h, 0, 0)))
    ins.append(blk(mu_c, (14, HW, 1), (None, HW, 1), lambda h: (4 + h, 0, 0)))
    ins.append(blk(mu_c, (14, HW, 1), (None, HW, 1), lambda h: (8 + h, 0, 0)))
    ins.append(blk(mu_c, (7, 128, 1), (None, 128, 1), lambda h: (6, 0, 0)))
    ins.append(blk(lora_t, (3, BRANCH_W, 128), (3, HW, 128), lambda h: (0, h, 0)))
    st_spec = lambda n: pl.BlockSpec((nb, n), lambda h: (0, h))
    ins.append((s_rw.reshape(nb, H * HW * HW), st_spec(HW * HW)))
    v20 = (20, HW, nb)
    for j in range(4):
        ins.append(blk(ut_ml, v20, (None, HW, nb), functools.partial(lambda h, j: (4 * j + h, 0, 0), j=j)))
    ins.append(blk(ut_ml[1024:1028], (4, 1, nb), (None, 1, nb), lambda h: (h, 0, 0)))
    ins.append(blk(ut_ml[1152:1156], (4, 1, nb), (None, 1, nb), lambda h: (h, 0, 0)))
    ins.append(blk(mconv_t, (3, 8, HW, nb), (3, None, HW, nb), lambda h: (0, h, 0, 0)))
    ins.append(blk(mconv_t, (3, 8, HW, nb), (3, None, HW, nb), lambda h: (0, 4 + h, 0, 0)))
    ins.append(blk(ml_cw, (8, HW, 5), (None, HW, 5), lambda h: (h, 0, 0)))
    ins.append(blk(ml_cw, (8, HW, 5), (None, HW, 5), lambda h: (4 + h, 0, 0)))
    ins.append((c_st.reshape(nb, H * HW * HW), st_spec(HW * HW)))
    ins.append(blk(n_st, (H, HW, nb), (None, HW, nb), lambda h: (h, 0, 0)))
    ins.append(blk(m_st, (H, 1, nb), (None, 1, nb), lambda h: (h, 0, 0)))
    ins.append(blk(ut_gl, (28, GLA_DK, nb), (None, GLA_DK, nb), lambda h: (h, 0, 0)))
    ins.append(blk(ut_gl, (28, GLA_DK, nb), (None, GLA_DK, nb), lambda h: (4 + h, 0, 0)))
    ins.append(blk(ut_gl, v14, (None, HW, nb), lambda h: (4 + h, 0, 0)))
    ins.append(blk(ut_gl, v14, (None, HW, nb), lambda h: (8 + h, 0, 0)))
    ins.append(blk(ut_gl, v7, (None, 128, nb), lambda h: (6, 0, 0)))
    ins.append(blk(au_t, (H, GLA_DK, 128), (None, GLA_DK, 128), lambda h: (h, 0, 0)))
    ins.append(blk(ab_c, (H, GLA_DK, 1), (None, GLA_DK, 1), lambda h: (h, 0, 0)))
    ins.append((g_st.reshape(nb, H * GLA_DK * HW), st_spec(GLA_DK * HW)))
    ins.append(blk(ut_rg, (8, HW, nb), (None, HW, nb), lambda h: (h, 0, 0)))
    ins.append(blk(ut_rg, (8, HW, nb), (None, HW, nb), lambda h: (4 + h, 0, 0)))
    ins.append(blk(rconv_t, (3, H, HW, nb), (3, None, HW, nb), lambda h: (0, h, 0, 0)))
    ins.append((wa_t, pl.BlockSpec((None, HW, HW), lambda h: (h, 0, 0))))
    ins.append((wx_t, pl.BlockSpec((None, HW, HW), lambda h: (h, 0, 0))))
    ins.append(blk(h_st, (H, HW, nb), (None, HW, nb), lambda h: (h, 0, 0)))
    ins.append(blk(pcols, (H, HW, PC_COLS), (None, HW, PC_COLS), lambda h: (h, 0, 0)))

    vec = lambda: (jax.ShapeDtypeStruct((H, HW, nb), F32), pl.BlockSpec((None, HW, nb), lambda h: (h, 0, 0)))
    mat = lambda n: (jax.ShapeDtypeStruct((nb, H * n), F32), st_spec(n))
    outs = [vec(), vec(), vec(), vec(), mat(HW * HW), mat(HW * HW), vec(),
            (jax.ShapeDtypeStruct((H, 1, nb), F32), pl.BlockSpec((None, 1, nb), lambda h: (h, 0, 0))),
            mat(GLA_DK * HW), vec()]
    return pl.pallas_call(
        _decode_kernel,
        out_shape=[o[0] for o in outs],
        grid=(H,),
        in_specs=[s for _, s in ins],
        out_specs=[o[1] for o in outs],
        scratch_shapes=[pltpu.VMEM((HW, nb), F32)]
        + [pltpu.VMEM((HW, HW, nb), F32)] * 4 + [pltpu.VMEM((GLA_DK, HW, nb), F32)] * 2
        + [pltpu.VMEM((3, GLA_DK, nb), F32)],
        compiler_params=_cp("parallel"),
        name="decode_mixers",
    )(*[a for a, _ in ins])


def _pad_cols(a, width):
    return jnp.pad(a, ((0, 0), (0, width - a.shape[1])))


def _pad_rows(a, rows, at=0):
    return jnp.pad(a, ((at, rows - at - a.shape[0]), (0, 0)))


def _pack_layer(p, l):
    offs = np.concatenate([[0], np.cumsum(IN_SIZES)])
    w_in, b_in = p['w_in'][l], p['b_in'][l][None, :]
    seg = lambda a, i: a[:, int(offs[i]):int(offs[i + 1])]

    def regroup(a):
        return jnp.concatenate(
            [seg(a, 0), seg(a, 1), seg(a, 2), seg(a, 3), _pad_cols(seg(a, 4), 128), _pad_cols(seg(a, 5), 128),
             seg(a, 6), seg(a, 7), seg(a, 8), seg(a, 9), _pad_cols(seg(a, 10), 128), seg(a, 11), seg(a, 12)],
            axis=1)
    k = dict(
        w_pack=regroup(w_in).astype(BF16), b_pack=regroup(b_in),
        w_gate=seg(w_in, 13).astype(BF16), b_gate=seg(b_in, 13),
        w_branch=p['w_branch'][l].astype(BF16), w_out=p['w_out'][l].astype(BF16),
        ln1_g=p['ln1_g'][l][None], ln1_b=p['ln1_b'][l][None],
        ln2_g=p['ln2_g'][l][None], ln2_b=p['ln2_b'][l][None],
    )
    w_up, a_up, g_up = p['rwkv_w_up'][l], p['rwkv_a_up'][l], p['rwkv_g_up'][l]
    lora = jnp.stack([_pad_rows(w_up, 128, 0), _pad_rows(a_up, 128, 32), _pad_rows(g_up, 128, 64)])
    k['rw_lora'] = lora.astype(BF16)
    k['rw_lora_t'] = jnp.swapaxes(lora, 1, 2).astype(BF16)
    k['rw_mu'] = p['rwkv_mu'][l][None]
    rw_rows = [p['rwkv_w0'][l], p['rwkv_a0'][l], p['rwkv_k_k'][l], p['rwkv_k_a'][l],
               p['rwkv_r_k'][l].reshape(BRANCH_W)]
    k['rw_p'] = jnp.stack(rw_rows + [jnp.zeros((BRANCH_W,), F32)] * 3)
    k['rw_ln'] = jnp.stack([p['rwkv_ln_w'][l], p['rwkv_ln_b'][l]])
    k['ml_cw'], k['ml_cb'] = p['mlstm_conv_w'][l], p['mlstm_conv_b'][l][None]
    k['ml_nw'] = p['mlstm_norm_w'][l][None]
    k['gl_au'] = _pad_rows(p['gla_alpha_up'][l], 128).astype(BF16)
    k['gl_ab'] = p['gla_alpha_b'][l][None]
    k['gl_nw'] = p['gla_norm_w'][l][None]
    wa, wx = p['rglru_wa'][l], p['rglru_wx'][l]
    eye = jnp.eye(HEADS, dtype=F32)
    bd = lambda w: jnp.einsum('gh,gij->gihj', eye, w).reshape(BRANCH_W, BRANCH_W).astype(BF16)
    k['rg_cw'], k['rg_cb'] = p['rglru_conv_w'][l], p['rglru_conv_b'][l][None]
    k['rg_wa'], k['rg_wx'] = bd(wa), bd(wx)
    k['rg_ba'], k['rg_bx'] = p['rglru_ba'][l][None], p['rglru_bx'][l][None]
    k['rg_lam'] = p['rglru_lambda'][l][None]
    k['rg_wa_t'] = jnp.swapaxes(wa, 1, 2).astype(BF16)
    k['rg_wx_t'] = jnp.swapaxes(wx, 1, 2).astype(BF16)
    cols = rw_rows + [p['rwkv_ln_w'][l], p['rwkv_ln_b'][l], p['mlstm_norm_w'][l], p['gla_norm_w'][l]]
    cols += [p['rglru_conv_w'][l][j] for j in range(4)]
    cols += [p['rglru_conv_b'][l], p['rglru_ba'][l], p['rglru_bx'][l], p['rglru_lambda'][l]]
    k['pcols'] = jnp.stack(cols, axis=1)
    k['ml_cw_t'] = jnp.concatenate([p['mlstm_conv_w'][l].T, p['mlstm_conv_b'][l][:, None]], axis=1)
    return k


def _diag_blocks(a, rb, cb):
    return jnp.stack([a[:, h * rb:(h + 1) * rb, h * cb:(h + 1) * cb] for h in range(HEADS)], axis=1)


def _prompt_mixers(x, k):
    B, L, _ = x.shape
    W = BRANCH_W
    u_rw, u_ml, u_gl, u_rg = _inproj(x.reshape(B * L, D_MODEL), k['w_pack'], k['b_pack'], 512)
    u_rw, u_ml, u_gl, u_rg = (u.reshape(B, L, -1) for u in (u_rw, u_ml, u_gl, u_rg))
    nch = B * HEADS
    rep = 128 // nch
    *planes, g, bv = _rwkv_prep(u_rw, k['rw_mu'], k['rw_p'], k['rw_lora'])
    y, s1 = _rwkv_scan(planes, L)
    y = y.reshape(L, HEAD_W // rep, rep, B, HEADS).transpose(3, 0, 4, 2, 1).reshape(B, L, W)
    s_rw = s1.reshape(HEAD_W, HEAD_W // rep, rep, B, HEADS).transpose(3, 4, 2, 1, 0).reshape(B, HEADS, HEAD_W, HEAD_W)
    flat = lambda a: a.reshape(B * L, W)
    o_a = (flat(y), flat(g), flat(bv), k['rw_ln'])
    o_b, c_bd, n_row, m_row = _mlstm_prompt(u_ml, k['ml_cw'], k['ml_cb'], k['ml_nw'])
    o_c, s_bd = _gla_prompt(u_gl, k['gl_au'], k['gl_ab'], k['gl_nw'])
    o_d, h1 = _rglru_prompt(u_rg, k['rg_cw'], k['rg_cb'], k['rg_wa'], k['rg_ba'], k['rg_wx'], k['rg_bx'],
                            k['rg_lam'])
    states = (u_rw[:, L - 1], s_rw, u_ml[:, L - 3:, :2 * W], _diag_blocks(c_bd, HEAD_W, HEAD_W),
              n_row.reshape(B, HEADS, HEAD_W), m_row[:, :HEADS, 0],
              jnp.swapaxes(_diag_blocks(s_bd, HEAD_W, GLA_DK), 2, 3), u_rg[:, L - 3:, :W], h1[:, 0])
    return [o_a] + [flat(o) for o in (o_b, o_c, o_d)], states


def _sample_mixers(x, st, k):
    W = BRANCH_W
    sh0, S0, mconv0, C0, n0, m0, gS0, rconv0, h0 = st
    u_rw, u_ml, u_gl, u_rg = _inproj(x, k['w_pack'], k['b_pack'], DEC_B)
    outs = _decode_mixers(
        u_rw.T, sh0.T, k['rw_mu'].T, k['rw_lora_t'], S0,
        u_ml.T, mconv0.transpose(1, 2, 0), k['ml_cw_t'], C0, n0.transpose(1, 2, 0), m0.T,
        u_gl.T, k['gl_au'].T, k['gl_ab'].T, gS0,
        u_rg.T, rconv0.transpose(1, 2, 0), k['rg_wa_t'], k['rg_wx_t'], h0.T, k['pcols'])
    oa, ob, oc, od, s_rw, c_st, n_st, m_st, g_st, h_st = outs
    branches = [o.reshape(W, DEC_B).T for o in (oa, ob, oc, od)]
    states = (u_rw, s_rw.reshape(S0.shape),
              jnp.concatenate([mconv0[:, 1:], u_ml[:, None, :2 * W]], axis=1),
              c_st.reshape(C0.shape), n_st.transpose(2, 0, 1), m_st[:, 0, :].T,
              g_st.reshape(gS0.shape),
              jnp.concatenate([rconv0[:, 1:], u_rg[:, None, :W]], axis=1), h_st.reshape(W, DEC_B).T)
    return branches, states


def _trunk(xp, bl, xs, sample_states, packs, moe):
    tms = (512, DEC_B)
    new_p, new_s = [], []
    for l in range(DEPTH):
        k = packs[l]
        br_p, st_p = _prompt_mixers(xp.reshape(bl[0], bl[1], D_MODEL), k)
        br_s, st_s = _sample_mixers(xs, tuple(s[l] for s in sample_states), k)
        new_p.append(st_p)
        new_s.append(st_s)
        xp, xs = (_merge(x, br, k['w_gate'], k['b_gate'], k['w_branch'], k['w_out'], k['ln1_g'], k['ln1_b'], tm)
                  for x, br, tm in ((xp, br_p, tms[0]), (xs, br_s, tms[1])))
        j = l // 2
        if l % 2 == 0:
            xp, xs = (_ffn(x, moe['ffn_wg'][j], moe['ffn_wu'][j], moe['ffn_wd'][j], k['ln2_g'], k['ln2_b'], tm)
                      for x, tm in ((xp, tms[0]), (xs, tms[1])))
        else:
            xp, xs = _moe_layer([xp, xs], moe['router'][j], moe['router_b'][j], moe['moe_wg'][j],
                                moe['moe_wu'][j], moe['moe_wd'][j], k['ln2_g'], k['ln2_b'])
    stack = lambda sts: [jnp.stack([st[i] for st in sts], axis=0) for i in range(9)]
    return xp, xs, stack(new_p), stack(new_s)


def kernel(x_prompt, x_sample, state_rwkv_shift, state_rwkv_S, state_mlstm_conv, state_mlstm_C,
           state_mlstm_n, state_mlstm_m, state_gla_S, state_rglru_conv, state_rglru_h,
           w_in, b_in, rwkv_mu, rwkv_w0, rwkv_w_up, rwkv_a0, rwkv_a_up, rwkv_g_up, rwkv_k_k,
           rwkv_k_a, rwkv_r_k, rwkv_ln_w, rwkv_ln_b, mlstm_conv_w, mlstm_conv_b, mlstm_norm_w,
           gla_alpha_up, gla_alpha_b, gla_norm_w, rglru_conv_w, rglru_conv_b, rglru_wa, rglru_ba,
           rglru_wx, rglru_bx, rglru_lambda, w_branch, w_out, ln1_g, ln1_b, ffn_wg, ffn_wu, ffn_wd,
           moe_router, moe_router_b, moe_wg, moe_wu, moe_wd, ln2_g, ln2_b):
    p = dict(w_in=w_in, b_in=b_in, rwkv_mu=rwkv_mu, rwkv_w0=rwkv_w0, rwkv_w_up=rwkv_w_up,
             rwkv_a0=rwkv_a0, rwkv_a_up=rwkv_a_up, rwkv_g_up=rwkv_g_up, rwkv_k_k=rwkv_k_k,
             rwkv_k_a=rwkv_k_a, rwkv_r_k=rwkv_r_k, rwkv_ln_w=rwkv_ln_w, rwkv_ln_b=rwkv_ln_b,
             mlstm_conv_w=mlstm_conv_w, mlstm_conv_b=mlstm_conv_b, mlstm_norm_w=mlstm_norm_w,
             gla_alpha_up=gla_alpha_up, gla_alpha_b=gla_alpha_b, gla_norm_w=gla_norm_w,
             rglru_conv_w=rglru_conv_w, rglru_conv_b=rglru_conv_b, rglru_wa=rglru_wa,
             rglru_ba=rglru_ba, rglru_wx=rglru_wx, rglru_bx=rglru_bx, rglru_lambda=rglru_lambda,
             w_branch=w_branch, w_out=w_out, ln1_g=ln1_g, ln1_b=ln1_b, ln2_g=ln2_g, ln2_b=ln2_b)
    packs = [_pack_layer(p, l) for l in range(DEPTH)]
    moe = dict(ffn_wg=ffn_wg.astype(BF16), ffn_wu=ffn_wu.astype(BF16), ffn_wd=ffn_wd.astype(BF16),
               router=jnp.pad(moe_router, ((0, 0), (0, 0), (0, 128 - N_EXPERTS))),
               router_b=jnp.pad(moe_router_b, ((0, 0), (0, 128 - N_EXPERTS)))[:, None, :],
               moe_wg=moe_wg.astype(BF16), moe_wu=moe_wu.astype(BF16), moe_wd=moe_wd.astype(BF16))
    B, L, _ = x_prompt.shape
    sample_states = (state_rwkv_shift, state_rwkv_S, state_mlstm_conv, state_mlstm_C, state_mlstm_n,
                     state_mlstm_m, state_gla_S, state_rglru_conv, state_rglru_h)
    nb, ls, _ = x_sample.shape
    y_p, y_s, ps, ss = _trunk(x_prompt.reshape(B * L, D_MODEL), (B, L), x_sample.reshape(nb * ls, D_MODEL),
                              sample_states, packs, moe)
    return (y_p.reshape(B, L, D_MODEL), y_s.reshape(nb, ls, D_MODEL), *ps, *ss)
```

```python
import functools

import jax
import jax.numpy as jnp
import numpy as np
from jax import lax
from jax.experimental import pallas as pl
from jax.experimental.pallas import tpu as pltpu

F32 = jnp.float32
BF16 = jnp.bfloat16

D_MODEL = 1024
DEPTH = 2
N_BRANCH = 4
BRANCH_W = 256
HEADS = 4
HEAD_W = 64
RWKV_COLS = 896
RWKV_GN_EPS = 64e-5
GLA_DK = 32
GLA_LORA = 16
GLA_TAU = 16.0
RG_C = 8.0
D_FF = 2816
N_EXPERTS = 8
DN_ALPHA = (2 * DEPTH) ** 0.25
LN_EPS = 1e-5

IN_SIZES = (RWKV_COLS, 512, 256, 256, 4, 4, 128, 128, 256, 256, 16, 256, 256, 4096)

W_RW = 896
W_ML = 1280
W_GL = 896
W_RG = 512
W_PACK = W_RW + W_ML + W_GL + W_RG

VMEM_LIMIT = 56 * 1024 * 1024


def _cp(*sem):
    return pltpu.CompilerParams(dimension_semantics=sem, vmem_limit_bytes=VMEM_LIMIT)


def _dot(a, b):
    return jnp.dot(a.astype(BF16), b.astype(BF16), preferred_element_type=F32)


def _dot_nt(a, b):
    return lax.dot_general(a.astype(BF16), b.astype(BF16), (((1,), (1,)), ((), ())),
                           preferred_element_type=F32)


def _dot_tn(a, b):
    return lax.dot_general(a.astype(BF16), b.astype(BF16), (((0,), (0,)), ((), ())),
                           preferred_element_type=F32)


def _split(x):
    hi = x.astype(BF16)
    lo = (x - hi.astype(F32)).astype(BF16)
    return hi, lo


def _dot_exact_lhs(a, x):
    hi, lo = _split(x)
    a = a.astype(BF16)
    return (jnp.dot(a, hi, preferred_element_type=F32) + jnp.dot(a, lo, preferred_element_type=F32))


def _dot_exact_rhs(x, a):
    hi, lo = _split(x)
    a = a.astype(BF16)
    return (jnp.dot(hi, a, preferred_element_type=F32) + jnp.dot(lo, a, preferred_element_type=F32))


def _dot3(x, w):
    xh, xl = _split(x)
    wh, wl = _split(w)
    return (jnp.dot(xh, wh, preferred_element_type=F32) + jnp.dot(xl, wh, preferred_element_type=F32)
            + jnp.dot(xh, wl, preferred_element_type=F32))


def _sigmoid(x):
    return 1.0 / (1.0 + jnp.exp(-x))


def _softplus(x):
    return jnp.maximum(x, 0.0) + jnp.log(1.0 + jnp.exp(-jnp.abs(x)))


def _log_sigmoid(x):
    return -_softplus(-x)


def _silu(x):
    return x * _sigmoid(x)


def _gelu_tanh(x):
    c = np.float32(np.sqrt(2.0 / np.pi))
    return 0.5 * x * (1.0 + jnp.tanh(c * (x + 0.044715 * (x * x * x))))


def _layer_norm(y, g, b):
    mu = jnp.mean(y, axis=-1, keepdims=True)
    yc = y - mu
    var = jnp.mean(yc * yc, axis=-1, keepdims=True)
    return yc * lax.rsqrt(var + LN_EPS) * g + b


def _iota(shape, dim):
    return lax.broadcasted_iota(jnp.int32, shape, dim)


def _head_mean_mat(width=BRANCH_W, group=HEAD_W):
    r = _iota((width, width), 0) // group
    c = _iota((width, width), 1) // group
    return jnp.where(r == c, 1.0 / group, 0.0).astype(BF16)


def _tril_ones(n):
    return jnp.where(_iota((n, n), 0) >= _iota((n, n), 1), 1.0, 0.0).astype(BF16)


def _const_spec(shape):
    nd = len(shape)
    return pl.BlockSpec(shape, lambda *_: (0,) * nd)


def _inproj_kernel(x_ref, w_ref, b_ref, o_rw, o_ml, o_gl, o_rg):
    x = x_ref[...].astype(BF16)
    off = 0
    for o_ref in (o_rw, o_ml, o_gl, o_rg):
        n = o_ref.shape[-1]
        o_ref[...] = (jnp.dot(x, w_ref[:, off:off + n], preferred_element_type=F32)
                      + b_ref[:, off:off + n])
        off += n


def _inproj(x, w, b, tm):
    n = x.shape[0]
    widths = (W_RW, W_ML, W_GL, W_RG)
    return pl.pallas_call(
        _inproj_kernel,
        out_shape=[jax.ShapeDtypeStruct((n, wd), F32) for wd in widths],
        grid=(n // tm,),
        in_specs=[pl.BlockSpec((tm, D_MODEL), lambda i: (i, 0)),
                  _const_spec((D_MODEL, W_PACK)), _const_spec((1, W_PACK))],
        out_specs=[pl.BlockSpec((tm, wd), lambda i: (i, 0)) for wd in widths],
        compiler_params=_cp("parallel"),
        name="inproj",
    )(x, w, b)


def _merge_body(x, outs, wg_ref, bg_ref, wb_ref, wo_ref, g_ref, b_ref, out_ref):
    xb = x.astype(BF16)
    merged = None
    for g, o in enumerate(outs):
        sl = slice(g * D_MODEL, (g + 1) * D_MODEL)
        gate = _sigmoid(jnp.dot(xb, wg_ref[:, sl], preferred_element_type=F32) + bg_ref[:, sl])
        up = jnp.dot(o.astype(BF16), wb_ref[g], preferred_element_type=F32)
        merged = gate * up if merged is None else merged + gate * up
    out = jnp.dot(merged.astype(BF16), wo_ref[...], preferred_element_type=F32)
    out_ref[...] = _layer_norm(DN_ALPHA * x + out, g_ref[...], b_ref[...])


def _merge_kernel(x_ref, oa, ob, oc, od, *rest):
    _merge_body(x_ref[...], [oa[...], ob[...], oc[...], od[...]], *rest)


def _rwkv_out(y, g, bv, ln_ref):
    pm = _head_mean_mat()
    yc = y - _dot_exact_rhs(y, pm)
    var = _dot_exact_rhs(yc * yc, pm)
    return (yc * lax.rsqrt(var + RWKV_GN_EPS) * ln_ref[0:1, :] + ln_ref[1:2, :] + bv) * g


def _merge_rwkv_kernel(x_ref, y_ref, gr_ref, bv_ref, ln_ref, ob, oc, od, *rest):
    oa = _rwkv_out(y_ref[...], gr_ref[...], bv_ref[...], ln_ref)
    _merge_body(x_ref[...], [oa, ob[...], oc[...], od[...]], *rest)


def _merge(x, branches, wg, bg, wb, wo, ln_g, ln_b, tm):
    n = x.shape[0]
    tok = lambda wd: pl.BlockSpec((tm, wd), lambda i: (i, 0))
    if isinstance(branches[0], tuple):
        y, g, bv, ln = branches[0]
        kern, first, first_specs = _merge_rwkv_kernel, (y, g, bv, ln), [tok(BRANCH_W)] * 3 + [_const_spec((2, BRANCH_W))]
    else:
        kern, first, first_specs = _merge_kernel, (branches[0],), [tok(BRANCH_W)]
    return pl.pallas_call(
        kern,
        out_shape=jax.ShapeDtypeStruct((n, D_MODEL), F32),
        grid=(n // tm,),
        in_specs=[tok(D_MODEL)] + first_specs + [tok(BRANCH_W)] * 3 + [
            _const_spec((D_MODEL, N_BRANCH * D_MODEL)), _const_spec((1, N_BRANCH * D_MODEL)),
            _const_spec((N_BRANCH, BRANCH_W, D_MODEL)), _const_spec((D_MODEL, D_MODEL)),
            _const_spec((1, D_MODEL)), _const_spec((1, D_MODEL))],
        out_specs=tok(D_MODEL),
        compiler_params=_cp("parallel"),
        name="merge",
    )(x, *first, *branches[1:], wg, bg, wb, wo, ln_g, ln_b)


FF_CHUNK = D_FF // 2


def _ffn_kernel(x_ref, wg_ref, wu_ref, wd_ref, g_ref, b_ref, out_ref):
    x = x_ref[...]
    xb = x.astype(BF16)
    acc = None
    for c in range(D_FF // FF_CHUNK):
        sl = slice(c * FF_CHUNK, (c + 1) * FF_CHUNK)
        h = (_silu(jnp.dot(xb, wg_ref[:, sl], preferred_element_type=F32))
             * jnp.dot(xb, wu_ref[:, sl], preferred_element_type=F32))
        part = jnp.dot(h.astype(BF16), wd_ref[sl, :], preferred_element_type=F32)
        acc = part if acc is None else acc + part
    out_ref[...] = _layer_norm(DN_ALPHA * x + acc, g_ref[...], b_ref[...])


def _ffn(x, wg, wu, wd, ln_g, ln_b, tm):
    n = x.shape[0]
    tok = pl.BlockSpec((tm, D_MODEL), lambda i: (i, 0))
    return pl.pallas_call(
        _ffn_kernel,
        out_shape=jax.ShapeDtypeStruct((n, D_MODEL), F32),
        grid=(n // tm,),
        in_specs=[tok, _const_spec((D_MODEL, D_FF)), _const_spec((D_MODEL, D_FF)),
                  _const_spec((D_FF, D_MODEL)), _const_spec((1, D_MODEL)), _const_spec((1, D_MODEL))],
        out_specs=tok,
        compiler_params=_cp("parallel"),
        name="ffn",
    )(x, wg, wu, wd, ln_g, ln_b)


(RT_E1, RT_E2, RT_R1, RT_R2, RT_W1, RT_W2) = range(6)
MOE_TILE = 512


def _router_kernel(x_ref, w_ref, b_ref, cin_ref, meta_ref, cnt_ref, carry_ref):
    @pl.when(pl.program_id(0) == 0)
    def _():
        carry_ref[...] = cin_ref[...]

    logits = _dot3(x_ref[...], w_ref[...]) + b_ref[...]
    tm = logits.shape[0]
    lane = _iota(logits.shape, 1)
    neg = jnp.float32(-jnp.inf)
    logits = jnp.where(lane < N_EXPERTS, logits, neg)
    m1 = jnp.max(logits, axis=-1, keepdims=True)
    i1 = jnp.min(jnp.where(logits == m1, lane, 128), axis=-1, keepdims=True)
    rest = jnp.where(lane == i1, neg, logits)
    m2 = jnp.max(rest, axis=-1, keepdims=True)
    i2 = jnp.min(jnp.where(rest == m2, lane, 128), axis=-1, keepdims=True)
    e2 = jnp.exp(m2 - m1)
    w1 = 1.0 / (1.0 + e2)
    w2 = e2 / (1.0 + e2)
    oh1 = lane == i1
    oh2 = lane == i2
    picks = jnp.where(oh1 | oh2, 1.0, 0.0)
    below = jnp.where(_iota((tm, tm), 0) > _iota((tm, tm), 1), 1.0, 0.0).astype(BF16)
    base = carry_ref[...] + jnp.dot(below, picks.astype(BF16), preferred_element_type=F32)
    r1 = jnp.sum(jnp.where(oh1, base, 0.0), axis=-1, keepdims=True)
    r2 = jnp.sum(jnp.where(oh2, base, 0.0), axis=-1, keepdims=True)
    carry_ref[...] = carry_ref[...] + jnp.sum(picks, axis=0, keepdims=True)
    cnt_ref[...] = carry_ref[...]
    meta = jnp.zeros(logits.shape, F32)
    for ln, val in ((RT_E1, i1.astype(F32)), (RT_E2, i2.astype(F32)), (RT_R1, r1), (RT_R2, r2),
                    (RT_W1, w1), (RT_W2, w2)):
        meta = jnp.where(lane == ln, val, meta)
    meta_ref[...] = meta


def _router(x, w, b, cnt_in, tm):
    n = x.shape[0]
    return pl.pallas_call(
        _router_kernel,
        out_shape=[jax.ShapeDtypeStruct((n, 128), F32), jax.ShapeDtypeStruct((1, 128), F32)],
        grid=(n // tm,),
        in_specs=[pl.BlockSpec((tm, D_MODEL), lambda i: (i, 0)),
                  _const_spec((D_MODEL, 128)), _const_spec((1, 128)), _const_spec((1, 128))],
        out_specs=[pl.BlockSpec((tm, 128), lambda i: (i, 0)), _const_spec((1, 128))],
        scratch_shapes=[pltpu.VMEM((1, 128), F32)],
        compiler_params=_cp("arbitrary"),
        name="router",
    )(x, w, b, cnt_in)


def _row_copies(n_rows, start_one, wait_shape_src, wait_shape_dst, sem, n_streams):
    def body(r, carry):
        start_one(r)
        return carry
    lax.fori_loop(0, n_rows, body, 0, unroll=8)
    for _ in range(n_streams):
        pltpu.make_async_copy(wait_shape_src, wait_shape_dst, sem).wait()


def _dispatch_kernel(d1_ref, d2_ref, x_ref, xs_in_ref, xs_ref, sem):
    del xs_in_ref
    tm = x_ref.shape[0]

    def start_one(r):
        src = x_ref.at[pl.ds(r, 1)]
        pltpu.make_async_copy(src, xs_ref.at[pl.ds(d1_ref[0, 0, r], 1)], sem).start(priority=0)
        pltpu.make_async_copy(src, xs_ref.at[pl.ds(d2_ref[0, 0, r], 1)], sem).start(priority=1)
    _row_copies(tm, start_one, x_ref, xs_ref.at[pl.ds(0, tm)], sem, 2)


def _dispatch(x, d1, d2, xs, tm):
    n = x.shape[0]
    idx = lambda a: a.reshape(n // tm, 1, tm)
    ispec = pl.BlockSpec((1, 1, tm), lambda i: (i, 0, 0), memory_space=pltpu.SMEM)
    return pl.pallas_call(
        _dispatch_kernel,
        out_shape=jax.ShapeDtypeStruct(xs.shape, F32),
        grid=(n // tm,),
        in_specs=[ispec, ispec, pl.BlockSpec((tm, D_MODEL), lambda i: (i, 0)),
                  pl.BlockSpec(memory_space=pl.ANY)],
        out_specs=pl.BlockSpec(memory_space=pl.ANY),
        scratch_shapes=[pltpu.SemaphoreType.DMA],
        input_output_aliases={3: 0},
        compiler_params=_cp("arbitrary"),
        name="moe_dispatch",
    )(idx(d1), idx(d2), x, xs)


def _experts_kernel(te_ref, nu_ref, xs_ref, wg_ref, wu_ref, wd_ref, ys_ref):
    del te_ref

    @pl.when(pl.program_id(0) < nu_ref[0])
    def _():
        xb = xs_ref[...].astype(BF16)
        acc = None
        for c in range(D_FF // FF_CHUNK):
            sl = slice(c * FF_CHUNK, (c + 1) * FF_CHUNK)
            h = (_silu(jnp.dot(xb, wg_ref[:, sl], preferred_element_type=F32))
                 * jnp.dot(xb, wu_ref[:, sl], preferred_element_type=F32))
            part = jnp.dot(h.astype(BF16), wd_ref[sl, :], preferred_element_type=F32)
            acc = part if acc is None else acc + part
        ys_ref[...] = acc

    @pl.when(pl.program_id(0) >= nu_ref[0])
    def _():
        ys_ref[...] = jnp.zeros_like(ys_ref)


def _experts(xs, tile_expert, n_used, wg, wu, wd):
    n_tiles = xs.shape[0] // MOE_TILE
    row = pl.BlockSpec((MOE_TILE, D_MODEL), lambda i, te, nu: (i, 0))
    return pl.pallas_call(
        _experts_kernel,
        out_shape=jax.ShapeDtypeStruct(xs.shape, F32),
        grid_spec=pltpu.PrefetchScalarGridSpec(
            num_scalar_prefetch=2, grid=(n_tiles,),
            in_specs=[row,
                      pl.BlockSpec((None, D_MODEL, D_FF), lambda i, te, nu: (te[i], 0, 0)),
                      pl.BlockSpec((None, D_MODEL, D_FF), lambda i, te, nu: (te[i], 0, 0)),
                      pl.BlockSpec((None, D_FF, D_MODEL), lambda i, te, nu: (te[i], 0, 0))],
            out_specs=row),
        compiler_params=_cp("arbitrary"),
        name="moe_experts",
    )(tile_expert, n_used, xs, wg, wu, wd)


def _combine_kernel(d1_ref, d2_ref, x_ref, meta_ref, ys_ref, g_ref, b_ref, out_ref, y1_ref, y2_ref, sem):
    tm = x_ref.shape[0]

    def start_one(r):
        pltpu.make_async_copy(ys_ref.at[pl.ds(d1_ref[0, 0, r], 1)], y1_ref.at[pl.ds(r, 1)], sem).start(priority=0)
        pltpu.make_async_copy(ys_ref.at[pl.ds(d2_ref[0, 0, r], 1)], y2_ref.at[pl.ds(r, 1)], sem).start(priority=1)
    _row_copies(tm, start_one, ys_ref.at[pl.ds(0, tm)], y1_ref, sem, 2)
    meta = meta_ref[...]
    f = meta[:, RT_W1:RT_W1 + 1] * y1_ref[...] + meta[:, RT_W2:RT_W2 + 1] * y2_ref[...]
    out_ref[...] = _layer_norm(DN_ALPHA * x_ref[...] + f, g_ref[...], b_ref[...])


def _combine(x, meta, d1, d2, ys, ln_g, ln_b, tm):
    n = x.shape[0]
    idx = lambda a: a.reshape(n // tm, 1, tm)
    ispec = pl.BlockSpec((1, 1, tm), lambda i: (i, 0, 0), memory_space=pltpu.SMEM)
    tok = lambda wd_: pl.BlockSpec((tm, wd_), lambda i: (i, 0))
    return pl.pallas_call(
        _combine_kernel,
        out_shape=jax.ShapeDtypeStruct((n, D_MODEL), F32),
        grid=(n // tm,),
        in_specs=[ispec, ispec, tok(D_MODEL), tok(128), pl.BlockSpec(memory_space=pl.ANY),
                  _const_spec((1, D_MODEL)), _const_spec((1, D_MODEL))],
        out_specs=tok(D_MODEL),
        scratch_shapes=[pltpu.VMEM((tm, D_MODEL), F32), pltpu.VMEM((tm, D_MODEL), F32),
                        pltpu.SemaphoreType.DMA],
        compiler_params=_cp("arbitrary"),
        name="moe_combine",
    )(idx(d1), idx(d2), x, meta, ys, ln_g, ln_b)


def _moe_layer(xs_groups, router_w, router_b, wg, wu, wd, ln_g, ln_b):
    tms = [min(512, x.shape[0]) for x in xs_groups]
    cnt = jnp.zeros((1, 128), F32)
    metas = []
    for x, tm in zip(xs_groups, tms):
        meta, cnt = _router(x, router_w, router_b, cnt, tm)
        metas.append(meta)
    n_total = sum(x.shape[0] for x in xs_groups)
    n_rows = -(-(2 * n_total + N_EXPERTS * (MOE_TILE - 1)) // MOE_TILE) * MOE_TILE
    n_tiles = n_rows // MOE_TILE
    counts = cnt[0, :N_EXPERTS].astype(jnp.int32)
    padded = (counts + MOE_TILE - 1) // MOE_TILE * MOE_TILE
    ends = jnp.cumsum(padded)
    starts = ends - padded
    tile_expert = jnp.minimum(
        jnp.sum((jnp.arange(n_tiles, dtype=jnp.int32)[:, None] * MOE_TILE >= ends[None, :]).astype(jnp.int32), axis=1),
        N_EXPERTS - 1).astype(jnp.int32)
    n_used = (ends[N_EXPERTS - 1] // MOE_TILE).astype(jnp.int32).reshape(1)
    xs = jnp.zeros((n_rows, D_MODEL), F32)
    dests = []
    for x, meta, tm in zip(xs_groups, metas, tms):
        e = meta[:, RT_E1:RT_E2 + 1].astype(jnp.int32)
        base = jnp.sum(jnp.where(e[:, :, None] == jnp.arange(N_EXPERTS, dtype=jnp.int32), starts, 0), axis=-1)
        d = base + meta[:, RT_R1:RT_R2 + 1].astype(jnp.int32)
        dests.append((d[:, 0], d[:, 1]))
        xs = _dispatch(x, d[:, 0], d[:, 1], xs, tm)
    ys = _experts(xs, tile_expert, n_used, wg, wu, wd)
    outs = []
    for x, meta, (d1, d2), tm in zip(xs_groups, metas, dests, tms):
        outs.append(_combine(x, meta, d1, d2, ys, ln_g, ln_b, tm))
    return outs


def _shift_rows(x, s, fill):
    rolled = pltpu.roll(x, s, 0)
    return jnp.where(_iota(x.shape, 0) >= s, rolled, fill)


def _causal_conv4(ext_ref, x, cw_ref, cb_ref, lc):
    ext_ref[8:8 + lc, :] = x
    out = cb_ref[...] + cw_ref[3:4, :] * x
    for j in range(3):
        out = out + cw_ref[j:j + 1, :] * ext_ref[5 + j:5 + j + lc, :]
    ext_ref[0:8, :] = ext_ref[lc:lc + 8, :]
    return out


RG_CHUNK = 256


RG_SEQS = 4


def _rglru_kernel(u_ref, cw_ref, cb_ref, wa_ref, ba_ref, wx_ref, bx_ref, lam_ref,
                  y_ref, h_ref, ext_ref, hc_ref):
    lc = RG_CHUNK

    @pl.when(pl.program_id(1) == 0)
    def _():
        ext_ref[:, 0:8, :] = jnp.zeros((ext_ref.shape[0], 8, BRANCH_W), F32)
        hc_ref[...] = jnp.zeros_like(hc_ref)

    for sq in range(u_ref.shape[0]):
        xc = _causal_conv4(ext_ref.at[sq], u_ref[sq, :, 0:BRANCH_W], cw_ref, cb_ref, lc)
        r = _sigmoid(_dot(xc, wa_ref[...]) + ba_ref[...])
        i = _sigmoid(_dot(xc, wx_ref[...]) + bx_ref[...])
        log_a = -RG_C * r * _softplus(-lam_ref[...])
        a = jnp.exp(log_a)
        u = jnp.sqrt(1.0 - jnp.exp(2.0 * log_a)) * (i * xc)
        s = 1
        while s < lc:
            u = u + a * _shift_rows(u, s, 0.0)
            a = a * _shift_rows(a, s, 1.0)
            s *= 2
        h = u + a * hc_ref[sq]
        hc_ref[sq] = h[lc - 1:lc, :]
        h_ref[sq] = h[lc - 1:lc, :]
        y_ref[sq] = h * _gelu_tanh(u_ref[sq, :, BRANCH_W:2 * BRANCH_W])


def _rglru_prompt(u, cw, cb, wa, ba, wx, bx, lam):
    bsz, seq, _ = u.shape
    lc = RG_CHUNK
    ns = RG_SEQS
    return pl.pallas_call(
        _rglru_kernel,
        out_shape=[jax.ShapeDtypeStruct((bsz, seq, BRANCH_W), F32),
                   jax.ShapeDtypeStruct((bsz, 1, BRANCH_W), F32)],
        grid=(bsz // ns, seq // lc),
        in_specs=[pl.BlockSpec((ns, lc, W_RG), lambda b, c: (b, c, 0)),
                  _const_spec((4, BRANCH_W)), _const_spec((1, BRANCH_W)),
                  _const_spec((BRANCH_W, BRANCH_W)), _const_spec((1, BRANCH_W)),
                  _const_spec((BRANCH_W, BRANCH_W)), _const_spec((1, BRANCH_W)),
                  _const_spec((1, BRANCH_W))],
        out_specs=[pl.BlockSpec((ns, lc, BRANCH_W), lambda b, c: (b, c, 0)),
                   pl.BlockSpec((ns, 1, BRANCH_W), lambda b, c: (b, 0, 0))],
        scratch_shapes=[pltpu.VMEM((ns, lc + 8, BRANCH_W), F32), pltpu.VMEM((ns, 1, BRANCH_W), F32)],
        compiler_params=_cp("parallel", "arbitrary"),
        name="rglru",
    )(u, cw, cb, wa, ba, wx, bx, lam)


ML_CHUNK = 256


ML_SEQS = 8


def _mlstm_kernel(u_ref, cw_ref, cb_ref, nw_ref, o_ref, c_ref, n_ref, m_ref, ext_ref):
    @pl.when(pl.program_id(1) == 0)
    def _():
        ext_ref[:, 0:8, :] = jnp.zeros((ext_ref.shape[0], 8, 2 * BRANCH_W), F32)
        c_ref[...] = jnp.zeros_like(c_ref)
        n_ref[...] = jnp.zeros_like(n_ref)
        m_ref[...] = jnp.zeros_like(m_ref)

    for sq in range(u_ref.shape[0]):
        _mlstm_chunk(sq, u_ref, cw_ref, cb_ref, nw_ref, o_ref, c_ref, n_ref, m_ref, ext_ref.at[sq])


def _mlstm_chunk(sq, u_ref, cw_ref, cb_ref, nw_ref, o_ref, c_ref, n_ref, m_ref, ext_ref):
    lc = ML_CHUNK
    W = BRANCH_W
    qk = _silu(_causal_conv4(ext_ref, u_ref[sq, :, 0:2 * W], cw_ref, cb_ref, lc))
    q = qk[:, 0:W]
    k = qk[:, W:2 * W] * (HEAD_W ** -0.5)
    v = u_ref[sq, :, 2 * W:3 * W]
    og = u_ref[sq, :, 3 * W:4 * W]
    ipre = u_ref[sq, :, 4 * W:4 * W + 128]
    logf = _log_sigmoid(u_ref[sq, :, 4 * W + 128:4 * W + 256])
    bcum = _dot_exact_lhs(_tril_ones(lc), logf)
    rowsrc = ipre - bcum
    rows_t = rowsrc.T
    lane = _iota((1, W), 1) // HEAD_W
    hq = _iota((lc, W), 1) // HEAD_W
    cmat = c_ref[sq]
    nrow = n_ref[sq]
    kb = k.astype(BF16)
    vb = v.astype(BF16)
    stack = lambda f: jnp.concatenate([f(h) for h in range(HEADS)], axis=0)
    b_col = stack(lambda h: bcum[:, h:h + 1])
    m_prev = stack(lambda h: jnp.broadcast_to(m_ref[sq, h:h + 1, 0:1], (lc, 1)))
    causal = (_iota((HEADS * lc, lc), 0) % lc) >= _iota((HEADS * lc, lc), 1)
    log_d = jnp.where(causal, b_col + stack(lambda h: jnp.broadcast_to(rows_t[h:h + 1, :], (lc, lc))), -jnp.inf)
    log_inter = b_col + m_prev
    m_t = jnp.maximum(log_inter, jnp.max(log_d, axis=-1, keepdims=True))
    dmat = jnp.exp(log_d - m_t)
    inter = jnp.exp(log_inter - m_t)
    qs = stack(lambda h: jnp.where(hq == h, q, 0.0))
    s = _dot_nt(qs, kb) * dmat
    num = _dot(s, vb) + inter * _dot_nt(qs, cmat)
    dot = (jnp.sum(s, axis=-1, keepdims=True)
           + inter * jnp.sum(qs * nrow, axis=-1, keepdims=True))
    hh = num / jnp.maximum(jnp.abs(dot), jnp.exp(-m_t))
    h_acc = jnp.zeros((lc, W), F32)
    ws_full = jnp.zeros((lc, W), F32)
    keep_full = jnp.zeros((1, W), F32)
    for h in range(HEADS):
        hm = lane == h
        h_acc = h_acc + jnp.where(hm, hh[h * lc:(h + 1) * lc, :], 0.0)
        m_new = m_t[(h + 1) * lc - 1:(h + 1) * lc, :]
        b_last = bcum[lc - 1:lc, h:h + 1]
        ws_col = jnp.exp(rowsrc[:, h:h + 1] + (b_last - m_new))
        keep = jnp.exp(b_last + m_ref[sq, h:h + 1, 0:1] - m_new)
        ws_full = ws_full + jnp.where(hm, ws_col, 0.0)
        keep_full = keep_full + jnp.where(hm, keep, 0.0)
        m_ref[sq, h:h + 1, :] = jnp.broadcast_to(m_new, (1, 128))
    blk = (_iota((W, W), 0) // HEAD_W) == (_iota((W, W), 1) // HEAD_W)
    vw = v * ws_full
    c_ref[sq] = keep_full * cmat + jnp.where(blk, _dot_tn(vw, kb), 0.0)
    n_ref[sq] = keep_full * nrow + jnp.sum(ws_full * k, axis=0, keepdims=True)
    pm = _head_mean_mat()
    mu = _dot_exact_rhs(h_acc, pm)
    hc = h_acc - mu
    var = _dot_exact_rhs(hc * hc, pm)
    o_ref[sq] = hc * lax.rsqrt(var + 1e-6) * nw_ref[...] * _sigmoid(og)


def _mlstm_prompt(u, cw, cb, nw):
    bsz, seq, _ = u.shape
    lc = ML_CHUNK
    W = BRANCH_W
    return pl.pallas_call(
        _mlstm_kernel,
        out_shape=[jax.ShapeDtypeStruct((bsz, seq, W), F32),
                   jax.ShapeDtypeStruct((bsz, W, W), F32),
                   jax.ShapeDtypeStruct((bsz, 1, W), F32),
                   jax.ShapeDtypeStruct((bsz, 8, 128), F32)],
        grid=(bsz // ML_SEQS, seq // lc),
        in_specs=[pl.BlockSpec((ML_SEQS, lc, W_ML), lambda b, c: (b, c, 0)),
                  _const_spec((4, 2 * W)), _const_spec((1, 2 * W)), _const_spec((1, W))],
        out_specs=[pl.BlockSpec((ML_SEQS, lc, W), lambda b, c: (b, c, 0)),
                   pl.BlockSpec((ML_SEQS, W, W), lambda b, c: (b, 0, 0)),
                   pl.BlockSpec((ML_SEQS, 1, W), lambda b, c: (b, 0, 0)),
                   pl.BlockSpec((ML_SEQS, 8, 128), lambda b, c: (b, 0, 0))],
        scratch_shapes=[pltpu.VMEM((ML_SEQS, lc + 8, 2 * W), F32)],
        compiler_params=_cp("parallel", "arbitrary"),
        name="mlstm",
    )(u, cw, cb, nw)


GLA_CHUNK = 128
GLA_QK = HEADS * GLA_DK


def _gla_level_ref(b, m, lc):
    if 2 * m == lc:
        return jnp.broadcast_to(b[m - 1:m, :], b.shape)
    if m >= 4:
        nb = lc // (2 * m)
        mid = b.reshape(nb, 2 * m, b.shape[1])[:, m - 1:m, :]
        return jnp.broadcast_to(mid, (nb, 2 * m, b.shape[1])).reshape(b.shape)
    pos = _iota(b.shape, 0) % (2 * m)
    out = b
    for p in range(2 * m):
        sh = p - (m - 1)
        if sh != 0:
            out = jnp.where(pos == p, pltpu.roll(b, sh % lc, 0), out)
    return out


GLA_SEQS = 8


def _gla_kernel(u_ref, au_ref, ab_ref, nw_ref, o_ref, s_ref):
    @pl.when(pl.program_id(1) == 0)
    def _():
        s_ref[...] = jnp.zeros_like(s_ref)

    for sq in range(u_ref.shape[0]):
        _gla_chunk(sq, u_ref, au_ref, ab_ref, nw_ref, o_ref, s_ref)


def _gla_chunk(sq, u_ref, au_ref, ab_ref, nw_ref, o_ref, s_ref):
    lc = GLA_CHUNK
    W = BRANCH_W
    q = u_ref[sq, :, 0:GLA_QK] * (GLA_DK ** -0.5)
    k = u_ref[sq, :, GLA_QK:2 * GLA_QK]
    v = u_ref[sq, :, 2 * GLA_QK:2 * GLA_QK + W]
    g = u_ref[sq, :, 2 * GLA_QK + W:2 * GLA_QK + 2 * W]
    ac = u_ref[sq, :, 2 * GLA_QK + 2 * W:2 * GLA_QK + 2 * W + 128]
    la = _log_sigmoid(_dot(ac, au_ref[...]) + ab_ref[...]) / GLA_TAU
    b = _dot_exact_lhs(_tril_ones(lc), la)
    st = s_ref[sq]
    o = _dot_nt(q * jnp.exp(b), st)
    gsum = ((_iota((GLA_QK, W), 0) // GLA_DK) == (_iota((GLA_QK, W), 1) // HEAD_W)).astype(BF16)
    o = o + jnp.dot((q * k).astype(BF16), gsum, preferred_element_type=F32) * v

    hq = _iota((lc, GLA_QK), 1) // GLA_DK
    row1 = _iota((lc, 1), 0)
    rt = _iota((HEADS * lc, lc), 0) % lc
    cs = _iota((HEADS * lc, lc), 1)
    kb_rows = None
    attn = jnp.zeros((HEADS * lc, lc), F32)
    m = lc // 2
    while m >= 1:
        ref = _gla_level_ref(b, m, lc)
        upper = (row1 % (2 * m)) >= m
        qt = q * jnp.exp(jnp.where(upper, b - ref, -jnp.inf))
        kt = k * jnp.exp(jnp.where(upper, -jnp.inf, ref - b))
        qs = jnp.concatenate([jnp.where(hq == h, qt, 0.0) for h in range(HEADS)], axis=0)
        part = _dot_nt(qs, kt)
        if 2 * m == lc:
            attn = attn + part
        else:
            attn = attn + jnp.where((rt // (2 * m)) == (cs // (2 * m)), part, 0.0)
        m //= 2
    av = _dot(attn, v)
    hv = _iota((lc, W), 1) // HEAD_W
    for h in range(HEADS):
        o = o + jnp.where(hv == h, av[h * lc:(h + 1) * lc, :], 0.0)
    b_last = b[lc - 1:lc, :]
    blk = (_iota((W, GLA_QK), 0) // HEAD_W) == (_iota((W, GLA_QK), 1) // GLA_DK)
    s_ref[sq] = jnp.exp(b_last) * st + jnp.where(blk, _dot_tn(v, k * jnp.exp(b_last - b)), 0.0)
    ms = _dot_exact_rhs(o * o, _head_mean_mat())
    o_ref[sq] = o * lax.rsqrt(ms + 1e-6) * nw_ref[...] * _silu(g)


def _gla_prompt(u, au, ab, nw):
    bsz, seq, _ = u.shape
    lc = GLA_CHUNK
    W = BRANCH_W
    return pl.pallas_call(
        _gla_kernel,
        out_shape=[jax.ShapeDtypeStruct((bsz, seq, W), F32),
                   jax.ShapeDtypeStruct((bsz, W, GLA_QK), F32)],
        grid=(bsz // GLA_SEQS, seq // lc),
        in_specs=[pl.BlockSpec((GLA_SEQS, lc, W_GL), lambda b, c: (b, c, 0)),
                  _const_spec((128, GLA_QK)), _const_spec((1, GLA_QK)), _const_spec((1, W))],
        out_specs=[pl.BlockSpec((GLA_SEQS, lc, W), lambda b, c: (b, c, 0)),
                   pl.BlockSpec((GLA_SEQS, W, GLA_QK), lambda b, c: (b, 0, 0))],
        compiler_params=_cp("parallel", "arbitrary"),
        name="gla",
    )(u, au, ab, nw)


def _head_sum_mat(width=BRANCH_W, group=HEAD_W):
    r = _iota((width, width), 0) // group
    c = _iota((width, width), 1) // group
    return jnp.where(r == c, 1.0, 0.0).astype(BF16)


def _rwkv_pointwise(um, p_ref, lora_ref):
    W = BRANCH_W
    r = um[:, 0:W]
    k = um[:, W:2 * W]
    v = um[:, 2 * W:3 * W]
    lo = um[:, 3 * W:3 * W + 128]
    w_log = -_softplus(-(p_ref[0:1, :] + _dot(jnp.tanh(lo), lora_ref[0]))) - 0.5
    decay = jnp.exp(-jnp.exp(w_log))
    a = _sigmoid(p_ref[1:2, :] + _dot(lo, lora_ref[1]))
    g = _dot(_sigmoid(lo), lora_ref[2])
    hs = _head_sum_mat()
    kk = k * p_ref[2:3, :]
    kk = kk / jnp.maximum(jnp.sqrt(_dot_exact_rhs(kk * kk, hs)), 1e-12)
    k2 = k * (1.0 + (a - 1.0) * p_ref[3:4, :])
    bonus = _dot_exact_rhs(r * k2 * p_ref[4:5, :], hs)
    return decay, kk, kk * a, k2, r, v, g, bonus * v


RW_PREP_T = 128


def _rwkv_prep_kernel(u_ref, mu_ref, p_ref, lora_ref, *rest):
    planes, g_ref, bv_ref, ext_ref, xs_ref = rest[:6], rest[6], rest[7], rest[8], rest[9]
    T = RW_PREP_T
    half = T // 2
    tp = RW_SCAN_PAIRS

    @pl.when(pl.program_id(0) == 0)
    def _():
        ext_ref[:, 0:8, :] = jnp.zeros((ext_ref.shape[0], 8, RWKV_COLS), F32)

    lane = _iota((half, 128), 1)

    def per_sequence(b, carry):
        u = u_ref[b]
        ext_ref[b, 8:8 + T, :] = u
        prev = ext_ref[b, 7:7 + T, :]
        ext_ref[b, 0:8, :] = ext_ref[b, T:T + 8, :]
        um = u + mu_ref[...] * (prev - u)
        dec, kk, kka, k2, r, v, g, bv = _rwkv_pointwise(um, p_ref, lora_ref)
        g_ref[b] = g
        bv_ref[b] = bv
        for z_ref, val in zip(planes, (dec, kk, kka, k2, r, v)):
            for p in range(2):
                xs_ref[p] = val[:, 128 * p:128 * (p + 1)]
            for p in range(2):
                even = xs_ref[p, pl.ds(0, half, stride=2), :]
                odd = xs_ref[p, pl.ds(1, half, stride=2), :]
                tiles = (jnp.where(lane < HEAD_W, even, pltpu.roll(odd, HEAD_W, 1)),
                         jnp.where(lane < HEAD_W, pltpu.roll(even, HEAD_W, 1), odd))
                for hh, tile in enumerate(tiles):
                    row0 = pl.multiple_of((b * HEADS + 2 * p + hh) * tp, tp)
                    for q in range(half // tp):
                        z_ref[q, pl.ds(row0, tp), :] = tile[q * tp:(q + 1) * tp, :]
        return carry

    lax.fori_loop(0, u_ref.shape[0], per_sequence, 0, unroll=2)


def _rwkv_prep(u, mu, p, lora):
    bsz, seq, _ = u.shape
    T = RW_PREP_T
    W = BRANCH_W
    nch = bsz * HEADS
    tp = RW_SCAN_PAIRS
    nq = T // 2 // tp
    plane = jax.ShapeDtypeStruct((seq // 2 // tp, nch * tp, 128), F32)
    nat = jax.ShapeDtypeStruct((bsz, seq, W), F32)
    return pl.pallas_call(
        _rwkv_prep_kernel,
        out_shape=[plane] * 6 + [nat, nat],
        grid=(seq // T,),
        in_specs=[pl.BlockSpec((bsz, T, RWKV_COLS), lambda i: (0, i, 0)),
                  _const_spec((1, RWKV_COLS)), _const_spec((8, W)), _const_spec((3, 128, W))],
        out_specs=[pl.BlockSpec((nq, nch * tp, 128), lambda i: (i, 0, 0))] * 6
        + [pl.BlockSpec((bsz, T, W), lambda i: (0, i, 0))] * 2,
        scratch_shapes=[pltpu.VMEM((bsz, T + 8, RWKV_COLS), F32), pltpu.VMEM((2, T, 128), F32)],
        compiler_params=_cp("arbitrary"),
        name="rwkv_prep",
    )(u, mu, p, lora)


def _delta_rule_step(s_ref, ni, w, kk, kka, k, r, v_row, y_store, s_out_ref=None):
    s_out = s_ref if s_out_ref is None else s_out_ref
    sa = [jnp.sum(s_ref[i] * kk, axis=0, keepdims=True) for i in range(ni)]
    for i in range(ni):
        sn = s_ref[i] * w - sa[i] * kka + v_row(i) * k
        s_out[i] = sn
        y_store(i, jnp.sum(sn * r, axis=0, keepdims=True))


RW_SCAN_PAIRS = 16


def _rwkv_scan_kernel(w_ref, kk_ref, kka_ref, k_ref, r_ref, v_ref, y_ref, s_out_ref, s_ref, t_ref):
    n_pairs = RW_SCAN_PAIRS
    nch = w_ref.shape[0] // n_pairs
    rep = 128 // nch
    ni = s_ref.shape[1]
    nacc = 4

    @pl.when(pl.program_id(0) == 0)
    def _():
        s_ref[...] = jnp.zeros_like(s_ref)

    zs = (w_ref, kk_ref, kka_ref, k_ref, r_ref, v_ref)

    def stage(r, slot):
        for n, z_ref in enumerate(zs):
            rows = z_ref[pl.ds(r, nch, stride=n_pairs), :]
            step = ni if n == 5 else 0
            copies = [rows if s * step == 0 else pltpu.roll(rows, 128 - s * step, 1) for s in range(rep)]
            t_ref[slot, n] = jnp.concatenate(copies, axis=0).T

    def tokens(r, slot):
        for t2 in range(2):
            base = t2 * HEAD_W
            bc = lambda n, j: jnp.broadcast_to(t_ref[slot, n, pl.ds(base + j, 1), :], (ni, 128))
            parts = [None] * nacc
            for j in range(HEAD_W):
                p = s_ref[j] * bc(1, j)
                parts[j % nacc] = p if parts[j % nacc] is None else parts[j % nacc] + p
            sa = (parts[0] + parts[1]) + (parts[2] + parts[3])
            v = t_ref[slot, 5, pl.ds(base, ni), :]
            parts = [None] * nacc
            for j in range(HEAD_W):
                sn = s_ref[j] * bc(0, j) - sa * bc(2, j) + v * bc(3, j)
                s_ref[j] = sn
                p = sn * bc(4, j)
                parts[j % nacc] = p if parts[j % nacc] is None else parts[j % nacc] + p
            y_ref[2 * r + t2] = (parts[0] + parts[1]) + (parts[2] + parts[3])

    stage(0, 0)

    def two_pairs(q, carry):
        stage(2 * q + 1, 1)
        tokens(2 * q, 0)
        stage(jnp.minimum(2 * q + 2, n_pairs - 1), 0)
        tokens(2 * q + 1, 1)
        return carry

    lax.fori_loop(0, n_pairs // 2, two_pairs, 0)

    @pl.when(pl.program_id(0) == pl.num_programs(0) - 1)
    def _():
        s_out_ref[...] = s_ref[...]


def _rwkv_scan(planes, seq):
    tp = RW_SCAN_PAIRS
    nch = planes[0].shape[1] // tp
    ni = HEAD_W * nch // 128
    zspec = pl.BlockSpec((None, nch * tp, 128), lambda t: (t, 0, 0))
    return pl.pallas_call(
        _rwkv_scan_kernel,
        out_shape=[jax.ShapeDtypeStruct((seq, ni, 128), F32), jax.ShapeDtypeStruct((HEAD_W, ni, 128), F32)],
        grid=(seq // (2 * tp),),
        in_specs=[zspec] * 6,
        out_specs=[pl.BlockSpec((2 * tp, ni, 128), lambda t: (t, 0, 0)),
                   pl.BlockSpec((HEAD_W, ni, 128), lambda t: (0, 0, 0))],
        scratch_shapes=[pltpu.VMEM((HEAD_W, ni, 128), F32), pltpu.VMEM((2, 6, 128, 128), F32)],
        compiler_params=_cp("arbitrary"),
        name="rwkv_scan",
    )(*planes)


(PC_W0, PC_A0, PC_KK, PC_KA, PC_RK, PC_LNW, PC_LNB, PC_MLNW, PC_GLNW,
 PC_RCW, PC_RCB, PC_RBA, PC_RBX, PC_RLAM) = (0, 1, 2, 3, 4, 5, 6, 7, 8, 9, 13, 14, 15, 16)
PC_COLS = 17
DEC_B = 128


def _rowsum(x):
    return jnp.sum(x, axis=0, keepdims=True)


def _decode_kernel(
        ur_ref, uk_ref, uv_ref, ulo_ref, pr_ref, pk_ref, pv_ref, plo_ref,
        mur_ref, muk_ref, muv_ref, mulo_ref, lora_ref, srw_in,
        mq_ref, mk_ref, mv_ref, mo_ref, mi_ref, mf_ref, bq_ref, bk_ref, cwq_ref, cwk_ref,
        c_in, n_ref, m_ref,
        gq_ref, gk_ref, gv_ref, gg_ref, ga_ref, au_ref, ab_ref, gs_in,
        rx_ref, ry_ref, rb_ref, wa_ref, wx_ref, h_ref,
        pc_ref,
        oa_ref, ob_ref, oc_ref, od_ref, srw_out, c_out, n_o, m_o, gs_out, h_o,
        y_scr, srw_ref, srw_o, c_ref, c_o, gs_ref, gs_o, gv_scr):
    col = lambda j: pc_ref[:, j:j + 1]

    def load_slabs(src_ref, scr_ref, width):
        per = 128 // width
        for p in range(src_ref.shape[1] // 128):
            t = src_ref[:, 128 * p:128 * (p + 1)].T
            for q in range(per):
                scr_ref[per * p + q] = t[width * q:width * (q + 1), :]

    def store_slabs(scr_ref, dst_ref, width):
        per = 128 // width
        for p in range(dst_ref.shape[1] // 128):
            t = jnp.concatenate([scr_ref[per * p + q] for q in range(per)], axis=0)
            dst_ref[:, 128 * p:128 * (p + 1)] = t.T

    load_slabs(srw_in, srw_ref, HEAD_W)
    load_slabs(c_in, c_ref, HEAD_W)
    load_slabs(gs_in, gs_ref, HEAD_W)

    def shift(u_ref, p_ref, mu_ref):
        u = u_ref[...]
        return u + mu_ref[...] * (p_ref[...] - u)
    r = shift(ur_ref, pr_ref, mur_ref)
    k = shift(uk_ref, pk_ref, muk_ref)
    v = shift(uv_ref, pv_ref, muv_ref)
    lo = shift(ulo_ref, plo_ref, mulo_ref)
    w_log = -_softplus(-(col(PC_W0) + _dot(lora_ref[0], jnp.tanh(lo)))) - 0.5
    decay = jnp.exp(-jnp.exp(w_log))
    a = _sigmoid(col(PC_A0) + _dot(lora_ref[1], lo))
    g = _dot(lora_ref[2], _sigmoid(lo))
    kk = k * col(PC_KK)
    kk = kk / jnp.maximum(jnp.sqrt(_rowsum(kk * kk)), 1e-12)
    k2 = k * (1.0 + (a - 1.0) * col(PC_KA))
    bonus = _rowsum(r * k2 * col(PC_RK))

    def y_store(i, row):
        y_scr[i:i + 1, :] = row
    _delta_rule_step(srw_ref, HEAD_W, decay, kk, kk * a, k2, r, lambda i: v[i:i + 1, :], y_store, srw_o)
    y = y_scr[...]
    yc = y - jnp.mean(y, axis=0, keepdims=True)
    var = jnp.mean(yc * yc, axis=0, keepdims=True)
    o = yc * lax.rsqrt(var + RWKV_GN_EPS) * col(PC_LNW) + col(PC_LNB)
    oa_ref[...] = (o + bonus * v) * g

    def conv(u_ref, b_ref, cw_ref):
        out = cw_ref[:, 4:5] + cw_ref[:, 3:4] * u_ref[...]
        for j in range(3):
            out = out + cw_ref[:, j:j + 1] * b_ref[j]
        return out
    q = _silu(conv(mq_ref, bq_ref, cwq_ref))
    k = _silu(conv(mk_ref, bk_ref, cwk_ref)) * (HEAD_W ** -0.5)
    v = mv_ref[...]
    ipre = mi_ref[...]
    logf = _log_sigmoid(mf_ref[...])
    m_prev = m_ref[...]
    m_t = jnp.maximum(logf + m_prev, ipre)
    inter = jnp.exp(logf + m_prev - m_t)
    wsc = jnp.exp(ipre - m_t)
    s = _rowsum(q * k) * wsc
    n_prev = n_ref[...]
    den = jnp.maximum(jnp.abs(s + inter * _rowsum(n_prev * q)), jnp.exp(-m_t))
    for i in range(HEAD_W):
        ci = c_ref[i]
        vi = v[i:i + 1, :]
        y_scr[i:i + 1, :] = (s * vi + inter * _rowsum(ci * q)) / den
        c_o[i] = inter * ci + (wsc * vi) * k
    n_o[...] = inter * n_prev + wsc * k
    m_o[...] = m_t
    y = y_scr[...]
    yc = y - jnp.mean(y, axis=0, keepdims=True)
    var = jnp.mean(yc * yc, axis=0, keepdims=True)
    ob_ref[...] = yc * lax.rsqrt(var + 1e-6) * col(PC_MLNW) * _sigmoid(mo_ref[...])

    q = gq_ref[...] * (GLA_DK ** -0.5)
    k = gk_ref[...]
    v = gv_ref[...]
    eb = jnp.exp(_log_sigmoid(_dot(au_ref[...], ga_ref[...]) + ab_ref[...]) / GLA_TAU)
    attn = _rowsum(q * k)
    qe = q * eb
    gv_scr[0] = qe
    gv_scr[1] = eb
    gv_scr[2] = k
    row = lambda n, d: jnp.broadcast_to(gv_scr[n, pl.ds(d, 1), :], (HEAD_W, DEC_B))
    y = attn * v
    for d in range(GLA_DK):
        sd = gs_ref[d]
        y = y + row(0, d) * sd
        gs_o[d] = row(1, d) * sd + row(2, d) * v
    ms = jnp.mean(y * y, axis=0, keepdims=True)
    oc_ref[...] = y * lax.rsqrt(ms + 1e-6) * col(PC_GLNW) * _silu(gg_ref[...])

    xc = col(PC_RCB) + col(PC_RCW + 3) * rx_ref[...]
    for j in range(3):
        xc = xc + col(PC_RCW + j) * rb_ref[j]
    rg = _sigmoid(_dot(wa_ref[...], xc) + col(PC_RBA))
    ig = _sigmoid(_dot(wx_ref[...], xc) + col(PC_RBX))
    log_a = -RG_C * rg * _softplus(-col(PC_RLAM))
    hn = jnp.exp(log_a) * h_ref[...] + jnp.sqrt(1.0 - jnp.exp(2.0 * log_a)) * (ig * xc)
    h_o[...] = hn
    od_ref[...] = hn * _gelu_tanh(ry_ref[...])

    store_slabs(srw_o, srw_out, HEAD_W)
    store_slabs(c_o, c_out, HEAD_W)
    store_slabs(gs_o, gs_out, HEAD_W)


def _decode_mixers(ut_rw, prev_t, mu_c, lora_t, s_rw,
                   ut_ml, mconv_t, ml_cw, c_st, n_st, m_st,
                   ut_gl, au_t, ab_c, g_st,
                   ut_rg, rconv_t, wa_t, wx_t, h_st, pcols):
    nb = DEC_B
    H, HW = HEADS, HEAD_W

    def blk(arr, view, block, index):
        a = arr.reshape(view)
        nd = len(block)
        return a, pl.BlockSpec(block, index)

    ins = []
    v14 = (14, HW, nb)
    v7 = (7, 128, nb)
    for arr in (ut_rw, prev_t):
        ins.append(blk(arr, v14, (None, HW, nb), lambda h: (h, 0, 0)))
        ins.append(blk(arr, v14, (None, HW, nb), lambda h: (4 + h, 0, 0)))
        ins.append(blk(arr, v14, (None, HW, nb), lambda h: (8 + h, 0, 0)))
        ins.append(blk(arr, v7, (None, 128, nb), lambda h: (6, 0, 0)))
    ins.append(blk(mu_c, (14, HW, 1), (None, HW, 1), lambda h: (h, 0, 0)))
    ins.append(blk(mu_c, (14, HW, 1), (None, HW, 1), lambda h: (4 + h, 0, 0)))
    ins.append(blk(mu_c, (14, HW, 1), (None, HW, 1), lambda h: (8 + h, 0, 0)))
    ins.append(blk(mu_c, (7, 128, 1), (None, 128, 1), lambda h: (6, 0, 0)))
    ins.append(blk(lora_t, (3, BRANCH_W, 128), (3, HW, 128), lambda h: (0, h, 0)))
    st_spec = lambda n: pl.BlockSpec((nb, n), lambda h: (0, h))
    ins.append((s_rw.reshape(nb, H * HW * HW), st_spec(HW * HW)))
    v20 = (20, HW, nb)
    for j in range(4):
        ins.append(blk(ut_ml, v20, (None, HW, nb), functools.partial(lambda h, j: (4 * j + h, 0, 0), j=j)))
    ins.append(blk(ut_ml[1024:1028], (4, 1, nb), (None, 1, nb), lambda h: (h, 0, 0)))
    ins.append(blk(ut_ml[1152:1156], (4, 1, nb), (None, 1, nb), lambda h: (h, 0, 0)))
    ins.append(blk(mconv_t, (3, 8, HW, nb), (3, None, HW, nb), lambda h: (0, h, 0, 0)))
    ins.append(blk(mconv_t, (3, 8, HW, nb), (3, None, HW, nb), lambda h: (0, 4 + h, 0, 0)))
    ins.append(blk(ml_cw, (8, HW, 5), (None, HW, 5), lambda h: (h, 0, 0)))
    ins.append(blk(ml_cw, (8, HW, 5), (None, HW, 5), lambda h: (4 + h, 0, 0)))
    ins.append((c_st.reshape(nb, H * HW * HW), st_spec(HW * HW)))
    ins.append(blk(n_st, (H, HW, nb), (None, HW, nb), lambda h: (h, 0, 0)))
    ins.append(blk(m_st, (H, 1, nb), (None, 1, nb), lambda h: (h, 0, 0)))
    ins.append(blk(ut_gl, (28, GLA_DK, nb), (None, GLA_DK, nb), lambda h: (h, 0, 0)))
    ins.append(blk(ut_gl, (28, GLA_DK, nb), (None, GLA_DK, nb), lambda h: (4 + h, 0, 0)))
    ins.append(blk(ut_gl, v14, (None, HW, nb), lambda h: (4 + h, 0, 0)))
    ins.append(blk(ut_gl, v14, (None, HW, nb), lambda h: (8 + h, 0, 0)))
    ins.append(blk(ut_gl, v7, (None, 128, nb), lambda h: (6, 0, 0)))
    ins.append(blk(au_t, (H, GLA_DK, 128), (None, GLA_DK, 128), lambda h: (h, 0, 0)))
    ins.append(blk(ab_c, (H, GLA_DK, 1), (None, GLA_DK, 1), lambda h: (h, 0, 0)))
    ins.append((g_st.reshape(nb, H * GLA_DK * HW), st_spec(GLA_DK * HW)))
    ins.append(blk(ut_rg, (8, HW, nb), (None, HW, nb), lambda h: (h, 0, 0)))
    ins.append(blk(ut_rg, (8, HW, nb), (None, HW, nb), lambda h: (4 + h, 0, 0)))
    ins.append(blk(rconv_t, (3, H, HW, nb), (3, None, HW, nb), lambda h: (0, h, 0, 0)))
    ins.append((wa_t, pl.BlockSpec((None, HW, HW), lambda h: (h, 0, 0))))
    ins.append((wx_t, pl.BlockSpec((None, HW, HW), lambda h: (h, 0, 0))))
    ins.append(blk(h_st, (H, HW, nb), (None, HW, nb), lambda h: (h, 0, 0)))
    ins.append(blk(pcols, (H, HW, PC_COLS), (None, HW, PC_COLS), lambda h: (h, 0, 0)))

    vec = lambda: (jax.ShapeDtypeStruct((H, HW, nb), F32), pl.BlockSpec((None, HW, nb), lambda h: (h, 0, 0)))
    mat = lambda n: (jax.ShapeDtypeStruct((nb, H * n), F32), st_spec(n))
    outs = [vec(), vec(), vec(), vec(), mat(HW * HW), mat(HW * HW), vec(),
            (jax.ShapeDtypeStruct((H, 1, nb), F32), pl.BlockSpec((None, 1, nb), lambda h: (h, 0, 0))),
            mat(GLA_DK * HW), vec()]
    return pl.pallas_call(
        _decode_kernel,
        out_shape=[o[0] for o in outs],
        grid=(H,),
        in_specs=[s for _, s in ins],
        out_specs=[o[1] for o in outs],
        scratch_shapes=[pltpu.VMEM((HW, nb), F32)]
        + [pltpu.VMEM((HW, HW, nb), F32)] * 4 + [pltpu.VMEM((GLA_DK, HW, nb), F32)] * 2
        + [pltpu.VMEM((3, GLA_DK, nb), F32)],
        compiler_params=_cp("parallel"),
        name="decode_mixers",
    )(*[a for a, _ in ins])


def _pad_cols(a, width):
    return jnp.pad(a, ((0, 0), (0, width - a.shape[1])))


def _pad_rows(a, rows, at=0):
    return jnp.pad(a, ((at, rows - at - a.shape[0]), (0, 0)))


def _pack_layer(p, l):
    offs = np.concatenate([[0], np.cumsum(IN_SIZES)])
    w_in, b_in = p['w_in'][l], p['b_in'][l][None, :]
    seg = lambda a, i: a[:, int(offs[i]):int(offs[i + 1])]

    def regroup(a):
        return jnp.concatenate(
            [seg(a, 0), seg(a, 1), seg(a, 2), seg(a, 3), _pad_cols(seg(a, 4), 128), _pad_cols(seg(a, 5), 128),
             seg(a, 6), seg(a, 7), seg(a, 8), seg(a, 9), _pad_cols(seg(a, 10), 128), seg(a, 11), seg(a, 12)],
            axis=1)
    k = dict(
        w_pack=regroup(w_in.astype(BF16)), b_pack=regroup(b_in),
        w_gate=seg(w_in.astype(BF16), 13), b_gate=seg(b_in, 13),
        w_branch=p['w_branch'][l].astype(BF16), w_out=p['w_out'][l].astype(BF16),
        ln1_g=p['ln1_g'][l][None], ln1_b=p['ln1_b'][l][None],
        ln2_g=p['ln2_g'][l][None], ln2_b=p['ln2_b'][l][None],
    )
    w_up, a_up, g_up = p['rwkv_w_up'][l], p['rwkv_a_up'][l], p['rwkv_g_up'][l]
    lora = jnp.stack([_pad_rows(w_up, 128, 0), _pad_rows(a_up, 128, 32), _pad_rows(g_up, 128, 64)])
    k['rw_lora'] = lora.astype(BF16)
    k['rw_lora_t'] = jnp.swapaxes(lora, 1, 2).astype(BF16)
    k['rw_mu'] = p['rwkv_mu'][l][None]
    rw_rows = [p['rwkv_w0'][l], p['rwkv_a0'][l], p['rwkv_k_k'][l], p['rwkv_k_a'][l],
               p['rwkv_r_k'][l].reshape(BRANCH_W)]
    k['rw_p'] = jnp.stack(rw_rows + [jnp.zeros((BRANCH_W,), F32)] * 3)
    k['rw_ln'] = jnp.stack([p['rwkv_ln_w'][l], p['rwkv_ln_b'][l]])
    k['ml_cw'], k['ml_cb'] = p['mlstm_conv_w'][l], p['mlstm_conv_b'][l][None]
    k['ml_nw'] = p['mlstm_norm_w'][l][None]
    k['gl_au'] = _pad_rows(p['gla_alpha_up'][l], 128).astype(BF16)
    k['gl_ab'] = p['gla_alpha_b'][l][None]
    k['gl_nw'] = p['gla_norm_w'][l][None]
    wa, wx = p['rglru_wa'][l], p['rglru_wx'][l]
    eye = jnp.eye(HEADS, dtype=F32)
    bd = lambda w: jnp.einsum('gh,gij->gihj', eye, w).reshape(BRANCH_W, BRANCH_W).astype(BF16)
    k['rg_cw'], k['rg_cb'] = p['rglru_conv_w'][l], p['rglru_conv_b'][l][None]
    k['rg_wa'], k['rg_wx'] = bd(wa), bd(wx)
    k['rg_ba'], k['rg_bx'] = p['rglru_ba'][l][None], p['rglru_bx'][l][None]
    k['rg_lam'] = p['rglru_lambda'][l][None]
    k['rg_wa_t'] = jnp.swapaxes(wa, 1, 2).astype(BF16)
    k['rg_wx_t'] = jnp.swapaxes(wx, 1, 2).astype(BF16)
    cols = rw_rows + [p['rwkv_ln_w'][l], p['rwkv_ln_b'][l], p['mlstm_norm_w'][l], p['gla_norm_w'][l]]
    cols += [p['rglru_conv_w'][l][j] for j in range(4)]
    cols += [p['rglru_conv_b'][l], p['rglru_ba'][l], p['rglru_bx'][l], p['rglru_lambda'][l]]
    k['pcols'] = jnp.stack(cols, axis=1)
    k['ml_cw_t'] = jnp.concatenate([p['mlstm_conv_w'][l].T, p['mlstm_conv_b'][l][:, None]], axis=1)
    return k


def _diag_blocks(a, rb, cb):
    return jnp.stack([a[:, h * rb:(h + 1) * rb, h * cb:(h + 1) * cb] for h in range(HEADS)], axis=1)


def _prompt_mixers(x, k):
    B, L, _ = x.shape
    W = BRANCH_W
    u_rw, u_ml, u_gl, u_rg = _inproj(x.reshape(B * L, D_MODEL), k['w_pack'], k['b_pack'], 512)
    u_rw, u_ml, u_gl, u_rg = (u.reshape(B, L, -1) for u in (u_rw, u_ml, u_gl, u_rg))
    nch = B * HEADS
    rep = 128 // nch
    *planes, g, bv = _rwkv_prep(u_rw, k['rw_mu'], k['rw_p'], k['rw_lora'])
    y, s1 = _rwkv_scan(planes, L)
    y = y.reshape(L, HEAD_W // rep, rep, B, HEADS).transpose(3, 0, 4, 2, 1).reshape(B, L, W)
    s_rw = s1.reshape(HEAD_W, HEAD_W // rep, rep, B, HEADS).transpose(3, 4, 2, 1, 0).reshape(B, HEADS, HEAD_W, HEAD_W)
    flat = lambda a: a.reshape(B * L, W)
    o_a = (flat(y), flat(g), flat(bv), k['rw_ln'])
    o_b, c_bd, n_row, m_row = _mlstm_prompt(u_ml, k['ml_cw'], k['ml_cb'], k['ml_nw'])
    o_c, s_bd = _gla_prompt(u_gl, k['gl_au'], k['gl_ab'], k['gl_nw'])
    o_d, h1 = _rglru_prompt(u_rg, k['rg_cw'], k['rg_cb'], k['rg_wa'], k['rg_ba'], k['rg_wx'], k['rg_bx'],
                            k['rg_lam'])
    states = (u_rw[:, L - 1], s_rw, u_ml[:, L - 3:, :2 * W], _diag_blocks(c_bd, HEAD_W, HEAD_W),
              n_row.reshape(B, HEADS, HEAD_W), m_row[:, :HEADS, 0],
              jnp.swapaxes(_diag_blocks(s_bd, HEAD_W, GLA_DK), 2, 3), u_rg[:, L - 3:, :W], h1[:, 0])
    return [o_a] + [flat(o) for o in (o_b, o_c, o_d)], states


def _sample_mixers(x, st, k):
    W = BRANCH_W
    sh0, S0, mconv0, C0, n0, m0, gS0, rconv0, h0 = st
    u_rw, u_ml, u_gl, u_rg = _inproj(x, k['w_pack'], k['b_pack'], DEC_B)
    outs = _decode_mixers(
        u_rw.T, sh0.T, k['rw_mu'].T, k['rw_lora_t'], S0,
        u_ml.T, mconv0.transpose(1, 2, 0), k['ml_cw_t'], C0, n0.transpose(1, 2, 0), m0.T,
        u_gl.T, k['gl_au'].T, k['gl_ab'].T, gS0,
        u_rg.T, rconv0.transpose(1, 2, 0), k['rg_wa_t'], k['rg_wx_t'], h0.T, k['pcols'])
    oa, ob, oc, od, s_rw, c_st, n_st, m_st, g_st, h_st = outs
    branches = [o.reshape(W, DEC_B).T for o in (oa, ob, oc, od)]
    states = (u_rw, s_rw.reshape(S0.shape),
              jnp.concatenate([mconv0[:, 1:], u_ml[:, None, :2 * W]], axis=1),
              c_st.reshape(C0.shape), n_st.transpose(2, 0, 1), m_st[:, 0, :].T,
              g_st.reshape(gS0.shape),
              jnp.concatenate([rconv0[:, 1:], u_rg[:, None, :W]], axis=1), h_st.reshape(W, DEC_B).T)
    return branches, states


def _trunk(xp, bl, xs, sample_states, packs, moe):
    tms = (512, DEC_B)
    new_p, new_s = [], []
    for l in range(DEPTH):
        k = packs[l]
        br_p, st_p = _prompt_mixers(xp.reshape(bl[0], bl[1], D_MODEL), k)
        br_s, st_s = _sample_mixers(xs, tuple(s[l] for s in sample_states), k)
        new_p.append(st_p)
        new_s.append(st_s)
        xp, xs = (_merge(x, br, k['w_gate'], k['b_gate'], k['w_branch'], k['w_out'], k['ln1_g'], k['ln1_b'], tm)
                  for x, br, tm in ((xp, br_p, tms[0]), (xs, br_s, tms[1])))
        j = l // 2
        if l % 2 == 0:
            xp, xs = (_ffn(x, moe['ffn_wg'][j], moe['ffn_wu'][j], moe['ffn_wd'][j], k['ln2_g'], k['ln2_b'], tm)
                      for x, tm in ((xp, tms[0]), (xs, tms[1])))
        else:
            xp, xs = _moe_layer([xp, xs], moe['router'][j], moe['router_b'][j], moe['moe_wg'][j],
                                moe['moe_wu'][j], moe['moe_wd'][j], k['ln2_g'], k['ln2_b'])
    stack = lambda sts: [jnp.stack([st[i] for st in sts], axis=0) for i in range(9)]
    return xp, xs, stack(new_p), stack(new_s)


def kernel(x_prompt, x_sample, state_rwkv_shift, state_rwkv_S, state_mlstm_conv, state_mlstm_C,
           state_mlstm_n, state_mlstm_m, state_gla_S, state_rglru_conv, state_rglru_h,
           w_in, b_in, rwkv_mu, rwkv_w0, rwkv_w_up, rwkv_a0, rwkv_a_up, rwkv_g_up, rwkv_k_k,
           rwkv_k_a, rwkv_r_k, rwkv_ln_w, rwkv_ln_b, mlstm_conv_w, mlstm_conv_b, mlstm_norm_w,
           gla_alpha_up, gla_alpha_b, gla_norm_w, rglru_conv_w, rglru_conv_b, rglru_wa, rglru_ba,
           rglru_wx, rglru_bx, rglru_lambda, w_branch, w_out, ln1_g, ln1_b, ffn_wg, ffn_wu, ffn_wd,
           moe_router, moe_router_b, moe_wg, moe_wu, moe_wd, ln2_g, ln2_b):
    p = dict(w_in=w_in, b_in=b_in, rwkv_mu=rwkv_mu, rwkv_w0=rwkv_w0, rwkv_w_up=rwkv_w_up,
             rwkv_a0=rwkv_a0, rwkv_a_up=rwkv_a_up, rwkv_g_up=rwkv_g_up, rwkv_k_k=rwkv_k_k,
             rwkv_k_a=rwkv_k_a, rwkv_r_k=rwkv_r_k, rwkv_ln_w=rwkv_ln_w, rwkv_ln_b=rwkv_ln_b,
             mlstm_conv_w=mlstm_conv_w, mlstm_conv_b=mlstm_conv_b, mlstm_norm_w=mlstm_norm_w,
             gla_alpha_up=gla_alpha_up, gla_alpha_b=gla_alpha_b, gla_norm_w=gla_norm_w,
             rglru_conv_w=rglru_conv_w, rglru_conv_b=rglru_conv_b, rglru_wa=rglru_wa,
             rglru_ba=rglru_ba, rglru_wx=rglru_wx, rglru_bx=rglru_bx, rglru_lambda=rglru_lambda,
             w_branch=w_branch, w_out=w_out, ln1_g=ln1_g, ln1_b=ln1_b, ln2_g=ln2_g, ln2_b=ln2_b)
    packs = [_pack_layer(p, l) for l in range(DEPTH)]
    moe = dict(ffn_wg=ffn_wg.astype(BF16), ffn_wu=ffn_wu.astype(BF16), ffn_wd=ffn_wd.astype(BF16),
               router=jnp.pad(moe_router, ((0, 0), (0, 0), (0, 128 - N_EXPERTS))),
               router_b=jnp.pad(moe_router_b, ((0, 0), (0, 128 - N_EXPERTS)))[:, None, :],
               moe_wg=moe_wg.astype(BF16), moe_wu=moe_wu.astype(BF16), moe_wd=moe_wd.astype(BF16))
    B, L, _ = x_prompt.shape
    sample_states = (state_rwkv_shift, state_rwkv_S, state_mlstm_conv, state_mlstm_C, state_mlstm_n,
                     state_mlstm_m, state_gla_S, state_rglru_conv, state_rglru_h)
    nb, ls, _ = x_sample.shape
    y_p, y_s, ps, ss = _trunk(x_prompt.reshape(B * L, D_MODEL), (B, L), x_sample.reshape(nb * ls, D_MODEL),
                              sample_states, packs, moe)
    return (y_p.reshape(B, L, D_MODEL), y_s.reshape(nb, ls, D_MODEL), *ps, *ss)
```

```python
import functools

import jax
import jax.numpy as jnp
import numpy as np
from jax import lax
from jax.experimental import pallas as pl
from jax.experimental.pallas import tpu as pltpu

F32 = jnp.float32
BF16 = jnp.bfloat16

D_MODEL = 1024
DEPTH = 2
N_BRANCH = 4
BRANCH_W = 256
HEADS = 4
HEAD_W = 64
RWKV_COLS = 896
RWKV_GN_EPS = 64e-5
GLA_DK = 32
GLA_LORA = 16
GLA_TAU = 16.0
RG_C = 8.0
D_FF = 2816
N_EXPERTS = 8
DN_ALPHA = (2 * DEPTH) ** 0.25
LN_EPS = 1e-5

IN_SIZES = (RWKV_COLS, 512, 256, 256, 4, 4, 128, 128, 256, 256, 16, 256, 256, 4096)

W_RW = 896
W_ML = 1280
W_GL = 896
W_RG = 512
W_PACK = W_RW + W_ML + W_GL + W_RG

VMEM_LIMIT = 56 * 1024 * 1024


def _cp(*sem):
    return pltpu.CompilerParams(dimension_semantics=sem, vmem_limit_bytes=VMEM_LIMIT)


def _dot(a, b):
    return jnp.dot(a.astype(BF16), b.astype(BF16), preferred_element_type=F32)


def _dot_nt(a, b):
    return lax.dot_general(a.astype(BF16), b.astype(BF16), (((1,), (1,)), ((), ())),
                           preferred_element_type=F32)


def _dot_tn(a, b):
    return lax.dot_general(a.astype(BF16), b.astype(BF16), (((0,), (0,)), ((), ())),
                           preferred_element_type=F32)


def _split(x):
    hi = x.astype(BF16)
    lo = (x - hi.astype(F32)).astype(BF16)
    return hi, lo


def _dot_exact_lhs(a, x):
    hi, lo = _split(x)
    a = a.astype(BF16)
    return (jnp.dot(a, hi, preferred_element_type=F32) + jnp.dot(a, lo, preferred_element_type=F32))


def _dot_exact_rhs(x, a):
    hi, lo = _split(x)
    a = a.astype(BF16)
    return (jnp.dot(hi, a, preferred_element_type=F32) + jnp.dot(lo, a, preferred_element_type=F32))


def _dot3(x, w):
    xh, xl = _split(x)
    wh, wl = _split(w)
    return (jnp.dot(xh, wh, preferred_element_type=F32) + jnp.dot(xl, wh, preferred_element_type=F32)
            + jnp.dot(xh, wl, preferred_element_type=F32))


def _sigmoid(x):
    return 1.0 / (1.0 + jnp.exp(-x))


def _softplus(x):
    return jnp.maximum(x, 0.0) + jnp.log(1.0 + jnp.exp(-jnp.abs(x)))


def _log_sigmoid(x):
    return -_softplus(-x)


def _silu(x):
    return x * _sigmoid(x)


def _gelu_tanh(x):
    c = np.float32(np.sqrt(2.0 / np.pi))
    return 0.5 * x * (1.0 + jnp.tanh(c * (x + 0.044715 * (x * x * x))))


def _layer_norm(y, g, b):
    mu = jnp.mean(y, axis=-1, keepdims=True)
    yc = y - mu
    var = jnp.mean(yc * yc, axis=-1, keepdims=True)
    return yc * lax.rsqrt(var + LN_EPS) * g + b


def _iota(shape, dim):
    return lax.broadcasted_iota(jnp.int32, shape, dim)


def _head_mean_mat(width=BRANCH_W, group=HEAD_W):
    r = _iota((width, width), 0) // group
    c = _iota((width, width), 1) // group
    return jnp.where(r == c, 1.0 / group, 0.0).astype(BF16)


def _tril_ones(n):
    return jnp.where(_iota((n, n), 0) >= _iota((n, n), 1), 1.0, 0.0).astype(BF16)


def _const_spec(shape):
    nd = len(shape)
    return pl.BlockSpec(shape, lambda *_: (0,) * nd)


def _inproj_kernel(x_ref, w_ref, b_ref, o_rw, o_ml, o_gl, o_rg):
    x = x_ref[...].astype(BF16)
    off = 0
    for o_ref in (o_rw, o_ml, o_gl, o_rg):
        n = o_ref.shape[-1]
        o_ref[...] = (jnp.dot(x, w_ref[:, off:off + n], preferred_element_type=F32)
                      + b_ref[:, off:off + n])
        off += n


def _inproj(x, w, b, tm):
    n = x.shape[0]
    widths = (W_RW, W_ML, W_GL, W_RG)
    return pl.pallas_call(
        _inproj_kernel,
        out_shape=[jax.ShapeDtypeStruct((n, wd), F32) for wd in widths],
        grid=(n // tm,),
        in_specs=[pl.BlockSpec((tm, D_MODEL), lambda i: (i, 0)),
                  _const_spec((D_MODEL, W_PACK)), _const_spec((1, W_PACK))],
        out_specs=[pl.BlockSpec((tm, wd), lambda i: (i, 0)) for wd in widths],
        compiler_params=_cp("parallel"),
        name="inproj",
    )(x, w, b)


def _merge_body(x, outs, wg_ref, bg_ref, wb_ref, wo_ref, g_ref, b_ref, out_ref):
    xb = x.astype(BF16)
    merged = None
    for g, o in enumerate(outs):
        sl = slice(g * D_MODEL, (g + 1) * D_MODEL)
        gate = _sigmoid(jnp.dot(xb, wg_ref[:, sl], preferred_element_type=F32) + bg_ref[:, sl])
        up = jnp.dot(o.astype(BF16), wb_ref[g], preferred_element_type=F32)
        merged = gate * up if merged is None else merged + gate * up
    out = jnp.dot(merged.astype(BF16), wo_ref[...], preferred_element_type=F32)
    out_ref[...] = _layer_norm(DN_ALPHA * x + out, g_ref[...], b_ref[...])


def _merge_kernel(x_ref, oa, ob, oc, od, *rest):
    _merge_body(x_ref[...], [oa[...], ob[...], oc[...], od[...]], *rest)


def _rwkv_out(y, g, bv, ln_ref):
    pm = _head_mean_mat()
    yc = y - _dot_exact_rhs(y, pm)
    var = _dot_exact_rhs(yc * yc, pm)
    return (yc * lax.rsqrt(var + RWKV_GN_EPS) * ln_ref[0:1, :] + ln_ref[1:2, :] + bv) * g


def _merge_rwkv_kernel(x_ref, y_ref, gr_ref, bv_ref, ln_ref, ob, oc, od, *rest):
    oa = _rwkv_out(y_ref[...], gr_ref[...], bv_ref[...], ln_ref)
    _merge_body(x_ref[...], [oa, ob[...], oc[...], od[...]], *rest)


def _merge(x, branches, wg, bg, wb, wo, ln_g, ln_b, tm):
    n = x.shape[0]
    tok = lambda wd: pl.BlockSpec((tm, wd), lambda i: (i, 0))
    if isinstance(branches[0], tuple):
        y, g, bv, ln = branches[0]
        kern, first, first_specs = _merge_rwkv_kernel, (y, g, bv, ln), [tok(BRANCH_W)] * 3 + [_const_spec((2, BRANCH_W))]
    else:
        kern, first, first_specs = _merge_kernel, (branches[0],), [tok(BRANCH_W)]
    return pl.pallas_call(
        kern,
        out_shape=jax.ShapeDtypeStruct((n, D_MODEL), F32),
        grid=(n // tm,),
        in_specs=[tok(D_MODEL)] + first_specs + [tok(BRANCH_W)] * 3 + [
            _const_spec((D_MODEL, N_BRANCH * D_MODEL)), _const_spec((1, N_BRANCH * D_MODEL)),
            _const_spec((N_BRANCH, BRANCH_W, D_MODEL)), _const_spec((D_MODEL, D_MODEL)),
            _const_spec((1, D_MODEL)), _const_spec((1, D_MODEL))],
        out_specs=tok(D_MODEL),
        compiler_params=_cp("parallel"),
        name="merge",
    )(x, *first, *branches[1:], wg, bg, wb, wo, ln_g, ln_b)


FF_CHUNK = D_FF // 2


def _ffn_kernel(x_ref, wg_ref, wu_ref, wd_ref, g_ref, b_ref, out_ref):
    x = x_ref[...]
    xb = x.astype(BF16)
    acc = None
    for c in range(D_FF // FF_CHUNK):
        sl = slice(c * FF_CHUNK, (c + 1) * FF_CHUNK)
        h = (_silu(jnp.dot(xb, wg_ref[:, sl], preferred_element_type=F32))
             * jnp.dot(xb, wu_ref[:, sl], preferred_element_type=F32))
        part = jnp.dot(h.astype(BF16), wd_ref[sl, :], preferred_element_type=F32)
        acc = part if acc is None else acc + part
    out_ref[...] = _layer_norm(DN_ALPHA * x + acc, g_ref[...], b_ref[...])


def _ffn(x, wg, wu, wd, ln_g, ln_b, tm):
    n = x.shape[0]
    tok = pl.BlockSpec((tm, D_MODEL), lambda i: (i, 0))
    return pl.pallas_call(
        _ffn_kernel,
        out_shape=jax.ShapeDtypeStruct((n, D_MODEL), F32),
        grid=(n // tm,),
        in_specs=[tok, _const_spec((D_MODEL, D_FF)), _const_spec((D_MODEL, D_FF)),
                  _const_spec((D_FF, D_MODEL)), _const_spec((1, D_MODEL)), _const_spec((1, D_MODEL))],
        out_specs=tok,
        compiler_params=_cp("parallel"),
        name="ffn",
    )(x, wg, wu, wd, ln_g, ln_b)


(RT_E1, RT_E2, RT_R1, RT_R2, RT_W1, RT_W2) = range(6)
MOE_TILE = 512


def _router_kernel(x_ref, w_ref, b_ref, cin_ref, meta_ref, cnt_ref, carry_ref):
    @pl.when(pl.program_id(0) == 0)
    def _():
        carry_ref[...] = cin_ref[...]

    logits = _dot3(x_ref[...], w_ref[...]) + b_ref[...]
    tm = logits.shape[0]
    lane = _iota(logits.shape, 1)
    neg = jnp.float32(-jnp.inf)
    logits = jnp.where(lane < N_EXPERTS, logits, neg)
    m1 = jnp.max(logits, axis=-1, keepdims=True)
    i1 = jnp.min(jnp.where(logits == m1, lane, 128), axis=-1, keepdims=True)
    rest = jnp.where(lane == i1, neg, logits)
    m2 = jnp.max(rest, axis=-1, keepdims=True)
    i2 = jnp.min(jnp.where(rest == m2, lane, 128), axis=-1, keepdims=True)
    e2 = jnp.exp(m2 - m1)
    w1 = 1.0 / (1.0 + e2)
    w2 = e2 / (1.0 + e2)
    oh1 = lane == i1
    oh2 = lane == i2
    picks = jnp.where(oh1 | oh2, 1.0, 0.0)
    below = jnp.where(_iota((tm, tm), 0) > _iota((tm, tm), 1), 1.0, 0.0).astype(BF16)
    base = carry_ref[...] + jnp.dot(below, picks.astype(BF16), preferred_element_type=F32)
    r1 = jnp.sum(jnp.where(oh1, base, 0.0), axis=-1, keepdims=True)
    r2 = jnp.sum(jnp.where(oh2, base, 0.0), axis=-1, keepdims=True)
    carry_ref[...] = carry_ref[...] + jnp.sum(picks, axis=0, keepdims=True)
    cnt_ref[...] = carry_ref[...]
    meta = jnp.zeros(logits.shape, F32)
    for ln, val in ((RT_E1, i1.astype(F32)), (RT_E2, i2.astype(F32)), (RT_R1, r1), (RT_R2, r2),
                    (RT_W1, w1), (RT_W2, w2)):
        meta = jnp.where(lane == ln, val, meta)
    meta_ref[...] = meta


def _router(x, w, b, cnt_in, tm):
    n = x.shape[0]
    return pl.pallas_call(
        _router_kernel,
        out_shape=[jax.ShapeDtypeStruct((n, 128), F32), jax.ShapeDtypeStruct((1, 128), F32)],
        grid=(n // tm,),
        in_specs=[pl.BlockSpec((tm, D_MODEL), lambda i: (i, 0)),
                  _const_spec((D_MODEL, 128)), _const_spec((1, 128)), _const_spec((1, 128))],
        out_specs=[pl.BlockSpec((tm, 128), lambda i: (i, 0)), _const_spec((1, 128))],
        scratch_shapes=[pltpu.VMEM((1, 128), F32)],
        compiler_params=_cp("arbitrary"),
        name="router",
    )(x, w, b, cnt_in)


def _row_copies(n_rows, start_one, wait_shape_src, wait_shape_dst, sem, n_streams):
    def body(r, carry):
        start_one(r)
        return carry
    lax.fori_loop(0, n_rows, body, 0, unroll=8)
    for _ in range(n_streams):
        pltpu.make_async_copy(wait_shape_src, wait_shape_dst, sem).wait()


def _dispatch_kernel(d1_ref, d2_ref, x_ref, xs_in_ref, xs_ref, sem):
    del xs_in_ref
    tm = x_ref.shape[0]

    def start_one(r):
        src = x_ref.at[pl.ds(r, 1)]
        pltpu.make_async_copy(src, xs_ref.at[pl.ds(d1_ref[0, 0, r], 1)], sem).start(priority=0)
        pltpu.make_async_copy(src, xs_ref.at[pl.ds(d2_ref[0, 0, r], 1)], sem).start(priority=1)
    _row_copies(tm, start_one, x_ref, xs_ref.at[pl.ds(0, tm)], sem, 2)


def _dispatch_first_kernel(z_ref, d1_ref, d2_ref, x_ref, xs_ref, zero_ref, sem):
    n_tiles = xs_ref.shape[0] // MOE_TILE

    @pl.when(pl.program_id(0) == 0)
    def _():
        zero_ref[...] = jnp.zeros_like(zero_ref)
        for e in range(N_EXPERTS):
            row0 = pl.multiple_of(z_ref[0, 0, e], MOE_TILE)
            pltpu.make_async_copy(zero_ref, xs_ref.at[pl.ds(row0, MOE_TILE)], sem).start()
        for e in range(N_EXPERTS):
            pltpu.make_async_copy(zero_ref, xs_ref.at[pl.ds(0, MOE_TILE)], sem).wait()

        def unused(t, carry):
            cp = pltpu.make_async_copy(zero_ref, xs_ref.at[pl.ds(pl.multiple_of(t * MOE_TILE, MOE_TILE), MOE_TILE)], sem)
            cp.start()
            cp.wait()
            return carry
        lax.fori_loop(z_ref[0, 0, N_EXPERTS], n_tiles, unused, 0)

    _dispatch_kernel(d1_ref, d2_ref, x_ref, None, xs_ref, sem)


def _dispatch_first(x, d1, d2, zinfo, n_rows, tm):
    n = x.shape[0]
    idx = lambda a: a.reshape(n // tm, 1, tm)
    ispec = pl.BlockSpec((1, 1, tm), lambda i: (i, 0, 0), memory_space=pltpu.SMEM)
    return pl.pallas_call(
        _dispatch_first_kernel,
        out_shape=jax.ShapeDtypeStruct((n_rows, D_MODEL), F32),
        grid=(n // tm,),
        in_specs=[pl.BlockSpec((1, 1, 16), lambda i: (0, 0, 0), memory_space=pltpu.SMEM),
                  ispec, ispec, pl.BlockSpec((tm, D_MODEL), lambda i: (i, 0))],
        out_specs=pl.BlockSpec(memory_space=pl.ANY),
        scratch_shapes=[pltpu.VMEM((MOE_TILE, D_MODEL), F32), pltpu.SemaphoreType.DMA],
        compiler_params=_cp("arbitrary"),
        name="moe_dispatch_first",
    )(zinfo.reshape(1, 1, 16), idx(d1), idx(d2), x)


def _dispatch(x, d1, d2, xs, tm):
    n = x.shape[0]
    idx = lambda a: a.reshape(n // tm, 1, tm)
    ispec = pl.BlockSpec((1, 1, tm), lambda i: (i, 0, 0), memory_space=pltpu.SMEM)
    return pl.pallas_call(
        _dispatch_kernel,
        out_shape=jax.ShapeDtypeStruct(xs.shape, F32),
        grid=(n // tm,),
        in_specs=[ispec, ispec, pl.BlockSpec((tm, D_MODEL), lambda i: (i, 0)),
                  pl.BlockSpec(memory_space=pl.ANY)],
        out_specs=pl.BlockSpec(memory_space=pl.ANY),
        scratch_shapes=[pltpu.SemaphoreType.DMA],
        input_output_aliases={3: 0},
        compiler_params=_cp("arbitrary"),
        name="moe_dispatch",
    )(idx(d1), idx(d2), x, xs)


def _experts_kernel(te_ref, nu_ref, xs_ref, wg_ref, wu_ref, wd_ref, ys_ref):
    del te_ref

    @pl.when(pl.program_id(0) < nu_ref[0])
    def _():
        xb = xs_ref[...].astype(BF16)
        acc = None
        for c in range(D_FF // FF_CHUNK):
            sl = slice(c * FF_CHUNK, (c + 1) * FF_CHUNK)
            h = (_silu(jnp.dot(xb, wg_ref[:, sl], preferred_element_type=F32))
                 * jnp.dot(xb, wu_ref[:, sl], preferred_element_type=F32))
            part = jnp.dot(h.astype(BF16), wd_ref[sl, :], preferred_element_type=F32)
            acc = part if acc is None else acc + part
        ys_ref[...] = acc

    @pl.when(pl.program_id(0) >= nu_ref[0])
    def _():
        ys_ref[...] = jnp.zeros_like(ys_ref)


def _experts(xs, tile_expert, n_used, wg, wu, wd):
    n_tiles = xs.shape[0] // MOE_TILE
    row = pl.BlockSpec((MOE_TILE, D_MODEL), lambda i, te, nu: (i, 0))
    return pl.pallas_call(
        _experts_kernel,
        out_shape=jax.ShapeDtypeStruct(xs.shape, F32),
        grid_spec=pltpu.PrefetchScalarGridSpec(
            num_scalar_prefetch=2, grid=(n_tiles,),
            in_specs=[row,
                      pl.BlockSpec((None, D_MODEL, D_FF), lambda i, te, nu: (te[i], 0, 0)),
                      pl.BlockSpec((None, D_MODEL, D_FF), lambda i, te, nu: (te[i], 0, 0)),
                      pl.BlockSpec((None, D_FF, D_MODEL), lambda i, te, nu: (te[i], 0, 0))],
            out_specs=row),
        compiler_params=_cp("arbitrary"),
        name="moe_experts",
    )(tile_expert, n_used, xs, wg, wu, wd)


def _combine_kernel(d1_ref, d2_ref, x_ref, meta_ref, ys_ref, g_ref, b_ref, out_ref, y1_ref, y2_ref, sem):
    tm = x_ref.shape[0]

    def start_one(r):
        pltpu.make_async_copy(ys_ref.at[pl.ds(d1_ref[0, 0, r], 1)], y1_ref.at[pl.ds(r, 1)], sem).start(priority=0)
        pltpu.make_async_copy(ys_ref.at[pl.ds(d2_ref[0, 0, r], 1)], y2_ref.at[pl.ds(r, 1)], sem).start(priority=1)
    _row_copies(tm, start_one, ys_ref.at[pl.ds(0, tm)], y1_ref, sem, 2)
    meta = meta_ref[...]
    f = meta[:, RT_W1:RT_W1 + 1] * y1_ref[...] + meta[:, RT_W2:RT_W2 + 1] * y2_ref[...]
    out_ref[...] = _layer_norm(DN_ALPHA * x_ref[...] + f, g_ref[...], b_ref[...])


def _combine(x, meta, d1, d2, ys, ln_g, ln_b, tm):
    n = x.shape[0]
    idx = lambda a: a.reshape(n // tm, 1, tm)
    ispec = pl.BlockSpec((1, 1, tm), lambda i: (i, 0, 0), memory_space=pltpu.SMEM)
    tok = lambda wd_: pl.BlockSpec((tm, wd_), lambda i: (i, 0))
    return pl.pallas_call(
        _combine_kernel,
        out_shape=jax.ShapeDtypeStruct((n, D_MODEL), F32),
        grid=(n // tm,),
        in_specs=[ispec, ispec, tok(D_MODEL), tok(128), pl.BlockSpec(memory_space=pl.ANY),
                  _const_spec((1, D_MODEL)), _const_spec((1, D_MODEL))],
        out_specs=tok(D_MODEL),
        scratch_shapes=[pltpu.VMEM((tm, D_MODEL), F32), pltpu.VMEM((tm, D_MODEL), F32),
                        pltpu.SemaphoreType.DMA],
        compiler_params=_cp("arbitrary"),
        name="moe_combine",
    )(idx(d1), idx(d2), x, meta, ys, ln_g, ln_b)


def _moe_layer(xs_groups, router_w, router_b, wg, wu, wd, ln_g, ln_b):
    tms = [min(512, x.shape[0]) for x in xs_groups]
    cnt = jnp.zeros((1, 128), F32)
    metas = []
    for x, tm in zip(xs_groups, tms):
        meta, cnt = _router(x, router_w, router_b, cnt, tm)
        metas.append(meta)
    n_total = sum(x.shape[0] for x in xs_groups)
    n_rows = -(-(2 * n_total + N_EXPERTS * (MOE_TILE - 1)) // MOE_TILE) * MOE_TILE
    n_tiles = n_rows // MOE_TILE
    counts = cnt[0, :N_EXPERTS].astype(jnp.int32)
    padded = (counts + MOE_TILE - 1) // MOE_TILE * MOE_TILE
    ends = jnp.cumsum(padded)
    starts = ends - padded
    tile_expert = jnp.minimum(
        jnp.sum((jnp.arange(n_tiles, dtype=jnp.int32)[:, None] * MOE_TILE >= ends[None, :]).astype(jnp.int32), axis=1),
        N_EXPERTS - 1).astype(jnp.int32)
    n_used = (ends[N_EXPERTS - 1] // MOE_TILE).astype(jnp.int32).reshape(1)
    zinfo = jnp.concatenate([jnp.maximum(ends - MOE_TILE, 0).astype(jnp.int32), n_used,
                             jnp.zeros((16 - N_EXPERTS - 1,), jnp.int32)])
    xs = None
    dests = []
    for x, meta, tm in zip(xs_groups, metas, tms):
        e = meta[:, RT_E1:RT_E2 + 1].astype(jnp.int32)
        base = jnp.sum(jnp.where(e[:, :, None] == jnp.arange(N_EXPERTS, dtype=jnp.int32), starts, 0), axis=-1)
        d = base + meta[:, RT_R1:RT_R2 + 1].astype(jnp.int32)
        dests.append((d[:, 0], d[:, 1]))
        if xs is None:
            xs = _dispatch_first(x, d[:, 0], d[:, 1], zinfo, n_rows, tm)
        else:
            xs = _dispatch(x, d[:, 0], d[:, 1], xs, tm)
    ys = _experts(xs, tile_expert, n_used, wg, wu, wd)
    outs = []
    for x, meta, (d1, d2), tm in zip(xs_groups, metas, dests, tms):
        outs.append(_combine(x, meta, d1, d2, ys, ln_g, ln_b, tm))
    return outs


def _shift_rows(x, s, fill):
    rolled = pltpu.roll(x, s, 0)
    return jnp.where(_iota(x.shape, 0) >= s, rolled, fill)


def _causal_conv4(ext_ref, x, cw_ref, cb_ref, lc):
    ext_ref[8:8 + lc, :] = x
    out = cb_ref[...] + cw_ref[3:4, :] * x
    for j in range(3):
        out = out + cw_ref[j:j + 1, :] * ext_ref[5 + j:5 + j + lc, :]
    ext_ref[0:8, :] = ext_ref[lc:lc + 8, :]
    return out


RG_CHUNK = 256


RG_SEQS = 4


def _rglru_kernel(u_ref, cw_ref, cb_ref, wa_ref, ba_ref, wx_ref, bx_ref, lam_ref,
                  y_ref, h_ref, ext_ref, hc_ref):
    lc = RG_CHUNK

    @pl.when(pl.program_id(1) == 0)
    def _():
        ext_ref[:, 0:8, :] = jnp.zeros((ext_ref.shape[0], 8, BRANCH_W), F32)
        hc_ref[...] = jnp.zeros_like(hc_ref)

    for sq in range(u_ref.shape[0]):
        xc = _causal_conv4(ext_ref.at[sq], u_ref[sq, :, 0:BRANCH_W], cw_ref, cb_ref, lc)
        r = _sigmoid(_dot(xc, wa_ref[...]) + ba_ref[...])
        i = _sigmoid(_dot(xc, wx_ref[...]) + bx_ref[...])
        log_a = -RG_C * r * _softplus(-lam_ref[...])
        a = jnp.exp(log_a)
        u = jnp.sqrt(1.0 - jnp.exp(2.0 * log_a)) * (i * xc)
        s = 1
        while s < lc:
            u = u + a * _shift_rows(u, s, 0.0)
            a = a * _shift_rows(a, s, 1.0)
            s *= 2
        h = u + a * hc_ref[sq]
        hc_ref[sq] = h[lc - 1:lc, :]
        h_ref[sq] = h[lc - 1:lc, :]
        y_ref[sq] = h * _gelu_tanh(u_ref[sq, :, BRANCH_W:2 * BRANCH_W])


def _rglru_prompt(u, cw, cb, wa, ba, wx, bx, lam):
    bsz, seq, _ = u.shape
    lc = RG_CHUNK
    ns = RG_SEQS
    return pl.pallas_call(
        _rglru_kernel,
        out_shape=[jax.ShapeDtypeStruct((bsz, seq, BRANCH_W), F32),
                   jax.ShapeDtypeStruct((bsz, 1, BRANCH_W), F32)],
        grid=(bsz // ns, seq // lc),
        in_specs=[pl.BlockSpec((ns, lc, W_RG), lambda b, c: (b, c, 0)),
                  _const_spec((4, BRANCH_W)), _const_spec((1, BRANCH_W)),
                  _const_spec((BRANCH_W, BRANCH_W)), _const_spec((1, BRANCH_W)),
                  _const_spec((BRANCH_W, BRANCH_W)), _const_spec((1, BRANCH_W)),
                  _const_spec((1, BRANCH_W))],
        out_specs=[pl.BlockSpec((ns, lc, BRANCH_W), lambda b, c: (b, c, 0)),
                   pl.BlockSpec((ns, 1, BRANCH_W), lambda b, c: (b, 0, 0))],
        scratch_shapes=[pltpu.VMEM((ns, lc + 8, BRANCH_W), F32), pltpu.VMEM((ns, 1, BRANCH_W), F32)],
        compiler_params=_cp("parallel", "arbitrary"),
        name="rglru",
    )(u, cw, cb, wa, ba, wx, bx, lam)


ML_CHUNK = 256


ML_SEQS = 8


def _mlstm_kernel(u_ref, cw_ref, cb_ref, nw_ref, o_ref, c_ref, n_ref, m_ref, ext_ref):
    @pl.when(pl.program_id(1) == 0)
    def _():
        ext_ref[:, 0:8, :] = jnp.zeros((ext_ref.shape[0], 8, 2 * BRANCH_W), F32)
        c_ref[...] = jnp.zeros_like(c_ref)
        n_ref[...] = jnp.zeros_like(n_ref)
        m_ref[...] = jnp.zeros_like(m_ref)

    for sq in range(u_ref.shape[0]):
        _mlstm_chunk(sq, u_ref, cw_ref, cb_ref, nw_ref, o_ref, c_ref, n_ref, m_ref, ext_ref.at[sq])


def _mlstm_chunk(sq, u_ref, cw_ref, cb_ref, nw_ref, o_ref, c_ref, n_ref, m_ref, ext_ref):
    lc = ML_CHUNK
    W = BRANCH_W
    qk = _silu(_causal_conv4(ext_ref, u_ref[sq, :, 0:2 * W], cw_ref, cb_ref, lc))
    q = qk[:, 0:W]
    k = qk[:, W:2 * W] * (HEAD_W ** -0.5)
    v = u_ref[sq, :, 2 * W:3 * W]
    og = u_ref[sq, :, 3 * W:4 * W]
    ipre = u_ref[sq, :, 4 * W:4 * W + 128]
    logf = _log_sigmoid(u_ref[sq, :, 4 * W + 128:4 * W + 256])
    bcum = _dot_exact_lhs(_tril_ones(lc), logf)
    rowsrc = ipre - bcum
    rows_t = rowsrc.T
    lane = _iota((1, W), 1) // HEAD_W
    hq = _iota((lc, W), 1) // HEAD_W
    cmat = c_ref[sq]
    nrow = n_ref[sq]
    kb = k.astype(BF16)
    vb = v.astype(BF16)
    stack = lambda f: jnp.concatenate([f(h) for h in range(HEADS)], axis=0)
    b_col = stack(lambda h: bcum[:, h:h + 1])
    m_prev = stack(lambda h: jnp.broadcast_to(m_ref[sq, h:h + 1, 0:1], (lc, 1)))
    causal = (_iota((HEADS * lc, lc), 0) % lc) >= _iota((HEADS * lc, lc), 1)
    log_d = jnp.where(causal, b_col + stack(lambda h: jnp.broadcast_to(rows_t[h:h + 1, :], (lc, lc))), -jnp.inf)
    log_inter = b_col + m_prev
    m_t = jnp.maximum(log_inter, jnp.max(log_d, axis=-1, keepdims=True))
    dmat = jnp.exp(log_d - m_t)
    inter = jnp.exp(log_inter - m_t)
    qs = stack(lambda h: jnp.where(hq == h, q, 0.0))
    s = _dot_nt(qs, kb) * dmat
    num = _dot(s, vb) + inter * _dot_nt(qs, cmat)
    dot = (jnp.sum(s, axis=-1, keepdims=True)
           + inter * jnp.sum(qs * nrow, axis=-1, keepdims=True))
    hh = num / jnp.maximum(jnp.abs(dot), jnp.exp(-m_t))
    h_acc = jnp.zeros((lc, W), F32)
    ws_full = jnp.zeros((lc, W), F32)
    keep_full = jnp.zeros((1, W), F32)
    for h in range(HEADS):
        hm = lane == h
        h_acc = h_acc + jnp.where(hm, hh[h * lc:(h + 1) * lc, :], 0.0)
        m_new = m_t[(h + 1) * lc - 1:(h + 1) * lc, :]
        b_last = bcum[lc - 1:lc, h:h + 1]
        ws_col = jnp.exp(rowsrc[:, h:h + 1] + (b_last - m_new))
        keep = jnp.exp(b_last + m_ref[sq, h:h + 1, 0:1] - m_new)
        ws_full = ws_full + jnp.where(hm, ws_col, 0.0)
        keep_full = keep_full + jnp.where(hm, keep, 0.0)
        m_ref[sq, h:h + 1, :] = jnp.broadcast_to(m_new, (1, 128))
    blk = (_iota((W, W), 0) // HEAD_W) == (_iota((W, W), 1) // HEAD_W)
    vw = v * ws_full
    c_ref[sq] = keep_full * cmat + jnp.where(blk, _dot_tn(vw, kb), 0.0)
    n_ref[sq] = keep_full * nrow + jnp.sum(ws_full * k, axis=0, keepdims=True)
    pm = _head_mean_mat()
    mu = _dot_exact_rhs(h_acc, pm)
    hc = h_acc - mu
    var = _dot_exact_rhs(hc * hc, pm)
    o_ref[sq] = hc * lax.rsqrt(var + 1e-6) * nw_ref[...] * _sigmoid(og)


def _mlstm_prompt(u, cw, cb, nw):
    bsz, seq, _ = u.shape
    lc = ML_CHUNK
    W = BRANCH_W
    return pl.pallas_call(
        _mlstm_kernel,
        out_shape=[jax.ShapeDtypeStruct((bsz, seq, W), F32),
                   jax.ShapeDtypeStruct((bsz, W, W), F32),
                   jax.ShapeDtypeStruct((bsz, 1, W), F32),
                   jax.ShapeDtypeStruct((bsz, 8, 128), F32)],
        grid=(bsz // ML_SEQS, seq // lc),
        in_specs=[pl.BlockSpec((ML_SEQS, lc, W_ML), lambda b, c: (b, c, 0)),
                  _const_spec((4, 2 * W)), _const_spec((1, 2 * W)), _const_spec((1, W))],
        out_specs=[pl.BlockSpec((ML_SEQS, lc, W), lambda b, c: (b, c, 0)),
                   pl.BlockSpec((ML_SEQS, W, W), lambda b, c: (b, 0, 0)),
                   pl.BlockSpec((ML_SEQS, 1, W), lambda b, c: (b, 0, 0)),
                   pl.BlockSpec((ML_SEQS, 8, 128), lambda b, c: (b, 0, 0))],
        scratch_shapes=[pltpu.VMEM((ML_SEQS, lc + 8, 2 * W), F32)],
        compiler_params=_cp("parallel", "arbitrary"),
        name="mlstm",
    )(u, cw, cb, nw)


GLA_CHUNK = 128
GLA_QK = HEADS * GLA_DK


def _gla_level_ref(b, m, lc):
    if 2 * m == lc:
        return jnp.broadcast_to(b[m - 1:m, :], b.shape)
    if m >= 4:
        nb = lc // (2 * m)
        mid = b.reshape(nb, 2 * m, b.shape[1])[:, m - 1:m, :]
        return jnp.broadcast_to(mid, (nb, 2 * m, b.shape[1])).reshape(b.shape)
    pos = _iota(b.shape, 0) % (2 * m)
    out = b
    for p in range(2 * m):
        sh = p - (m - 1)
        if sh != 0:
            out = jnp.where(pos == p, pltpu.roll(b, sh % lc, 0), out)
    return out


GLA_SEQS = 8


def _gla_kernel(u_ref, au_ref, ab_ref, nw_ref, o_ref, s_ref):
    @pl.when(pl.program_id(1) == 0)
    def _():
        s_ref[...] = jnp.zeros_like(s_ref)

    for sq in range(u_ref.shape[0]):
        _gla_chunk(sq, u_ref, au_ref, ab_ref, nw_ref, o_ref, s_ref)


def _gla_chunk(sq, u_ref, au_ref, ab_ref, nw_ref, o_ref, s_ref):
    lc = GLA_CHUNK
    W = BRANCH_W
    q = u_ref[sq, :, 0:GLA_QK] * (GLA_DK ** -0.5)
    k = u_ref[sq, :, GLA_QK:2 * GLA_QK]
    v = u_ref[sq, :, 2 * GLA_QK:2 * GLA_QK + W]
    g = u_ref[sq, :, 2 * GLA_QK + W:2 * GLA_QK + 2 * W]
    ac = u_ref[sq, :, 2 * GLA_QK + 2 * W:2 * GLA_QK + 2 * W + 128]
    la = _log_sigmoid(_dot(ac, au_ref[...]) + ab_ref[...]) / GLA_TAU
    b = _dot_exact_lhs(_tril_ones(lc), la)
    st = s_ref[sq]
    o = _dot_nt(q * jnp.exp(b), st)
    gsum = ((_iota((GLA_QK, W), 0) // GLA_DK) == (_iota((GLA_QK, W), 1) // HEAD_W)).astype(BF16)
    o = o + jnp.dot((q * k).astype(BF16), gsum, preferred_element_type=F32) * v

    hq = _iota((lc, GLA_QK), 1) // GLA_DK
    row1 = _iota((lc, 1), 0)
    rt = _iota((HEADS * lc, lc), 0) % lc
    cs = _iota((HEADS * lc, lc), 1)
    kb_rows = None
    attn = jnp.zeros((HEADS * lc, lc), F32)
    m = lc // 2
    while m >= 1:
        ref = _gla_level_ref(b, m, lc)
        upper = (row1 % (2 * m)) >= m
        qt = q * jnp.exp(jnp.where(upper, b - ref, -jnp.inf))
        kt = k * jnp.exp(jnp.where(upper, -jnp.inf, ref - b))
        qs = jnp.concatenate([jnp.where(hq == h, qt, 0.0) for h in range(HEADS)], axis=0)
        part = _dot_nt(qs, kt)
        if 2 * m == lc:
            attn = attn + part
        else:
            attn = attn + jnp.where((rt // (2 * m)) == (cs // (2 * m)), part, 0.0)
        m //= 2
    av = _dot(attn, v)
    hv = _iota((lc, W), 1) // HEAD_W
    for h in range(HEADS):
        o = o + jnp.where(hv == h, av[h * lc:(h + 1) * lc, :], 0.0)
    b_last = b[lc - 1:lc, :]
    blk = (_iota((W, GLA_QK), 0) // HEAD_W) == (_iota((W, GLA_QK), 1) // GLA_DK)
    s_ref[sq] = jnp.exp(b_last) * st + jnp.where(blk, _dot_tn(v, k * jnp.exp(b_last - b)), 0.0)
    ms = _dot_exact_rhs(o * o, _head_mean_mat())
    o_ref[sq] = o * lax.rsqrt(ms + 1e-6) * nw_ref[...] * _silu(g)


def _gla_prompt(u, au, ab, nw):
    bsz, seq, _ = u.shape
    lc = GLA_CHUNK
    W = BRANCH_W
    return pl.pallas_call(
        _gla_kernel,
        out_shape=[jax.ShapeDtypeStruct((bsz, seq, W), F32),
                   jax.ShapeDtypeStruct((bsz, W, GLA_QK), F32)],
        grid=(bsz // GLA_SEQS, seq // lc),
        in_specs=[pl.BlockSpec((GLA_SEQS, lc, W_GL), lambda b, c: (b, c, 0)),
                  _const_spec((128, GLA_QK)), _const_spec((1, GLA_QK)), _const_spec((1, W))],
        out_specs=[pl.BlockSpec((GLA_SEQS, lc, W), lambda b, c: (b, c, 0)),
                   pl.BlockSpec((GLA_SEQS, W, GLA_QK), lambda b, c: (b, 0, 0))],
        compiler_params=_cp("parallel", "arbitrary"),
        name="gla",
    )(u, au, ab, nw)


def _head_sum_mat(width=BRANCH_W, group=HEAD_W):
    r = _iota((width, width), 0) // group
    c = _iota((width, width), 1) // group
    return jnp.where(r == c, 1.0, 0.0).astype(BF16)


def _rwkv_pointwise(um, p_ref, lora_ref):
    W = BRANCH_W
    r = um[:, 0:W]
    k = um[:, W:2 * W]
    v = um[:, 2 * W:3 * W]
    lo = um[:, 3 * W:3 * W + 128]
    w_log = -_softplus(-(p_ref[0:1, :] + _dot(jnp.tanh(lo), lora_ref[0]))) - 0.5
    decay = jnp.exp(-jnp.exp(w_log))
    a = _sigmoid(p_ref[1:2, :] + _dot(lo, lora_ref[1]))
    g = _dot(_sigmoid(lo), lora_ref[2])
    hs = _head_sum_mat()
    kk = k * p_ref[2:3, :]
    kk = kk / jnp.maximum(jnp.sqrt(_dot_exact_rhs(kk * kk, hs)), 1e-12)
    k2 = k * (1.0 + (a - 1.0) * p_ref[3:4, :])
    bonus = _dot_exact_rhs(r * k2 * p_ref[4:5, :], hs)
    return decay, kk, kk * a, k2, r, v, g, bonus * v


RW_PREP_T = 128


def _rwkv_prep_kernel(u_ref, mu_ref, p_ref, lora_ref, *rest):
    planes, g_ref, bv_ref, ext_ref, xs_ref = rest[:6], rest[6], rest[7], rest[8], rest[9]
    T = RW_PREP_T
    half = T // 2
    tp = RW_SCAN_PAIRS

    @pl.when(pl.program_id(0) == 0)
    def _():
        ext_ref[:, 0:8, :] = jnp.zeros((ext_ref.shape[0], 8, RWKV_COLS), F32)

    lane = _iota((half, 128), 1)

    def per_sequence(b, carry):
        u = u_ref[b]
        ext_ref[b, 8:8 + T, :] = u
        prev = ext_ref[b, 7:7 + T, :]
        ext_ref[b, 0:8, :] = ext_ref[b, T:T + 8, :]
        um = u + mu_ref[...] * (prev - u)
        dec, kk, kka, k2, r, v, g, bv = _rwkv_pointwise(um, p_ref, lora_ref)
        g_ref[b] = g
        bv_ref[b] = bv
        for z_ref, val in zip(planes, (dec, kk, kka, k2, r, v)):
            for p in range(2):
                xs_ref[p] = val[:, 128 * p:128 * (p + 1)]
            for p in range(2):
                even = xs_ref[p, pl.ds(0, half, stride=2), :]
                odd = xs_ref[p, pl.ds(1, half, stride=2), :]
                tiles = (jnp.where(lane < HEAD_W, even, pltpu.roll(odd, HEAD_W, 1)),
                         jnp.where(lane < HEAD_W, pltpu.roll(even, HEAD_W, 1), odd))
                for hh, tile in enumerate(tiles):
                    row0 = pl.multiple_of((b * HEADS + 2 * p + hh) * tp, tp)
                    for q in range(half // tp):
                        z_ref[q, pl.ds(row0, tp), :] = tile[q * tp:(q + 1) * tp, :]
        return carry

    lax.fori_loop(0, u_ref.shape[0], per_sequence, 0, unroll=2)


def _rwkv_prep(u, mu, p, lora):
    bsz, seq, _ = u.shape
    T = RW_PREP_T
    W = BRANCH_W
    nch = bsz * HEADS
    tp = RW_SCAN_PAIRS
    nq = T // 2 // tp
    plane = jax.ShapeDtypeStruct((seq // 2 // tp, nch * tp, 128), F32)
    nat = jax.ShapeDtypeStruct((bsz, seq, W), F32)
    return pl.pallas_call(
        _rwkv_prep_kernel,
        out_shape=[plane] * 6 + [nat, nat],
        grid=(seq // T,),
        in_specs=[pl.BlockSpec((bsz, T, RWKV_COLS), lambda i: (0, i, 0)),
                  _const_spec((1, RWKV_COLS)), _const_spec((8, W)), _const_spec((3, 128, W))],
        out_specs=[pl.BlockSpec((nq, nch * tp, 128), lambda i: (i, 0, 0))] * 6
        + [pl.BlockSpec((bsz, T, W), lambda i: (0, i, 0))] * 2,
        scratch_shapes=[pltpu.VMEM((bsz, T + 8, RWKV_COLS), F32), pltpu.VMEM((2, T, 128), F32)],
        compiler_params=_cp("arbitrary"),
        name="rwkv_prep",
    )(u, mu, p, lora)


def _delta_rule_step(s_ref, ni, w, kk, kka, k, r, v_row, y_store, s_out_ref=None):
    s_out = s_ref if s_out_ref is None else s_out_ref
    sa = [jnp.sum(s_ref[i] * kk, axis=0, keepdims=True) for i in range(ni)]
    for i in range(ni):
        sn = s_ref[i] * w - sa[i] * kka + v_row(i) * k
        s_out[i] = sn
        y_store(i, jnp.sum(sn * r, axis=0, keepdims=True))


RW_SCAN_PAIRS = 16


def _rwkv_scan_kernel(w_ref, kk_ref, kka_ref, k_ref, r_ref, v_ref, y_ref, s_out_ref, s_ref, t_ref):
    n_pairs = RW_SCAN_PAIRS
    nch = w_ref.shape[0] // n_pairs
    rep = 128 // nch
    ni = s_ref.shape[1]
    nacc = 4

    @pl.when(pl.program_id(0) == 0)
    def _():
        s_ref[...] = jnp.zeros_like(s_ref)

    zs = (w_ref, kk_ref, kka_ref, k_ref, r_ref, v_ref)

    def stage(r, slot):
        for n, z_ref in enumerate(zs):
            rows = z_ref[pl.ds(r, nch, stride=n_pairs), :]
            step = ni if n == 5 else 0
            copies = [rows if s * step == 0 else pltpu.roll(rows, 128 - s * step, 1) for s in range(rep)]
            t_ref[slot, n] = jnp.concatenate(copies, axis=0).T

    def tokens(r, slot):
        for t2 in range(2):
            base = t2 * HEAD_W
            bc = lambda n, j: jnp.broadcast_to(t_ref[slot, n, pl.ds(base + j, 1), :], (ni, 128))
            parts = [None] * nacc
            for j in range(HEAD_W):
                p = s_ref[j] * bc(1, j)
                parts[j % nacc] = p if parts[j % nacc] is None else parts[j % nacc] + p
            sa = (parts[0] + parts[1]) + (parts[2] + parts[3])
            v = t_ref[slot, 5, pl.ds(base, ni), :]
            parts = [None] * nacc
            for j in range(HEAD_W):
                sn = s_ref[j] * bc(0, j) - sa * bc(2, j) + v * bc(3, j)
                s_ref[j] = sn
                p = sn * bc(4, j)
                parts[j % nacc] = p if parts[j % nacc] is None else parts[j % nacc] + p
            y_ref[2 * r + t2] = (parts[0] + parts[1]) + (parts[2] + parts[3])

    stage(0, 0)

    def two_pairs(q, carry):
        stage(2 * q + 1, 1)
        tokens(2 * q, 0)
        stage(jnp.minimum(2 * q + 2, n_pairs - 1), 0)
        tokens(2 * q + 1, 1)
        return carry

    lax.fori_loop(0, n_pairs // 2, two_pairs, 0)

    @pl.when(pl.program_id(0) == pl.num_programs(0) - 1)
    def _():
        s_out_ref[...] = s_ref[...]


def _rwkv_scan(planes, seq):
    tp = RW_SCAN_PAIRS
    nch = planes[0].shape[1] // tp
    ni = HEAD_W * nch // 128
    zspec = pl.BlockSpec((None, nch * tp, 128), lambda t: (t, 0, 0))
    return pl.pallas_call(
        _rwkv_scan_kernel,
        out_shape=[jax.ShapeDtypeStruct((seq, ni, 128), F32), jax.ShapeDtypeStruct((HEAD_W, ni, 128), F32)],
        grid=(seq // (2 * tp),),
        in_specs=[zspec] * 6,
        out_specs=[pl.BlockSpec((2 * tp, ni, 128), lambda t: (t, 0, 0)),
                   pl.BlockSpec((HEAD_W, ni, 128), lambda t: (0, 0, 0))],
        scratch_shapes=[pltpu.VMEM((HEAD_W, ni, 128), F32), pltpu.VMEM((2, 6, 128, 128), F32)],
        compiler_params=_cp("arbitrary"),
        name="rwkv_scan",
    )(*planes)


(PC_W0, PC_A0, PC_KK, PC_KA, PC_RK, PC_LNW, PC_LNB, PC_MLNW, PC_GLNW,
 PC_RCW, PC_RCB, PC_RBA, PC_RBX, PC_RLAM) = (0, 1, 2, 3, 4, 5, 6, 7, 8, 9, 13, 14, 15, 16)
PC_COLS = 17
DEC_B = 128


def _rowsum(x):
    return jnp.sum(x, axis=0, keepdims=True)


def _decode_kernel(
        ur_ref, uk_ref, uv_ref, ulo_ref, pr_ref, pk_ref, pv_ref, plo_ref,
        mur_ref, muk_ref, muv_ref, mulo_ref, lora_ref, srw_in,
        mq_ref, mk_ref, mv_ref, mo_ref, mi_ref, mf_ref, bq_ref, bk_ref, cwq_ref, cwk_ref,
        c_in, n_ref, m_ref,
        gq_ref, gk_ref, gv_ref, gg_ref, ga_ref, au_ref, ab_ref, gs_in,
        rx_ref, ry_ref, rb_ref, wa_ref, wx_ref, h_ref,
        pc_ref,
        oa_ref, ob_ref, oc_ref, od_ref, srw_out, c_out, n_o, m_o, gs_out, h_o,
        y_scr, srw_ref, srw_o, c_ref, c_o, gs_ref, gs_o, gv_scr):
    col = lambda j: pc_ref[:, j:j + 1]

    def load_slabs(src_ref, scr_ref, width):
        per = 128 // width
        for p in range(src_ref.shape[1] // 128):
            t = src_ref[:, 128 * p:128 * (p + 1)].T
            for q in range(per):
                scr_ref[per * p + q] = t[width * q:width * (q + 1), :]

    def store_slabs(scr_ref, dst_ref, width):
        per = 128 // width
        for p in range(dst_ref.shape[1] // 128):
            t = jnp.concatenate([scr_ref[per * p + q] for q in range(per)], axis=0)
            dst_ref[:, 128 * p:128 * (p + 1)] = t.T

    load_slabs(srw_in, srw_ref, HEAD_W)
    load_slabs(c_in, c_ref, HEAD_W)
    load_slabs(gs_in, gs_ref, HEAD_W)

    def shift(u_ref, p_ref, mu_ref):
        u = u_ref[...]
        return u + mu_ref[...] * (p_ref[...] - u)
    r = shift(ur_ref, pr_ref, mur_ref)
    k = shift(uk_ref, pk_ref, muk_ref)
    v = shift(uv_ref, pv_ref, muv_ref)
    lo = shift(ulo_ref, plo_ref, mulo_ref)
    w_log = -_softplus(-(col(PC_W0) + _dot(lora_ref[0], jnp.tanh(lo)))) - 0.5
    decay = jnp.exp(-jnp.exp(w_log))
    a = _sigmoid(col(PC_A0) + _dot(lora_ref[1], lo))
    g = _dot(lora_ref[2], _sigmoid(lo))
    kk = k * col(PC_KK)
    kk = kk / jnp.maximum(jnp.sqrt(_rowsum(kk * kk)), 1e-12)
    k2 = k * (1.0 + (a - 1.0) * col(PC_KA))
    bonus = _rowsum(r * k2 * col(PC_RK))

    def y_store(i, row):
        y_scr[i:i + 1, :] = row
    _delta_rule_step(srw_ref, HEAD_W, decay, kk, kk * a, k2, r, lambda i: v[i:i + 1, :], y_store, srw_o)
    y = y_scr[...]
    yc = y - jnp.mean(y, axis=0, keepdims=True)
    var = jnp.mean(yc * yc, axis=0, keepdims=True)
    o = yc * lax.rsqrt(var + RWKV_GN_EPS) * col(PC_LNW) + col(PC_LNB)
    oa_ref[...] = (o + bonus * v) * g

    def conv(u_ref, b_ref, cw_ref):
        out = cw_ref[:, 4:5] + cw_ref[:, 3:4] * u_ref[...]
        for j in range(3):
            out = out + cw_ref[:, j:j + 1] * b_ref[j]
        return out
    q = _silu(conv(mq_ref, bq_ref, cwq_ref))
    k = _silu(conv(mk_ref, bk_ref, cwk_ref)) * (HEAD_W ** -0.5)
    v = mv_ref[...]
    ipre = mi_ref[...]
    logf = _log_sigmoid(mf_ref[...])
    m_prev = m_ref[...]
    m_t = jnp.maximum(logf + m_prev, ipre)
    inter = jnp.exp(logf + m_prev - m_t)
    wsc = jnp.exp(ipre - m_t)
    s = _rowsum(q * k) * wsc
    n_prev = n_ref[...]
    den = jnp.maximum(jnp.abs(s + inter * _rowsum(n_prev * q)), jnp.exp(-m_t))
    for i in range(HEAD_W):
        ci = c_ref[i]
        vi = v[i:i + 1, :]
        y_scr[i:i + 1, :] = (s * vi + inter * _rowsum(ci * q)) / den
        c_o[i] = inter * ci + (wsc * vi) * k
    n_o[...] = inter * n_prev + wsc * k
    m_o[...] = m_t
    y = y_scr[...]
    yc = y - jnp.mean(y, axis=0, keepdims=True)
    var = jnp.mean(yc * yc, axis=0, keepdims=True)
    ob_ref[...] = yc * lax.rsqrt(var + 1e-6) * col(PC_MLNW) * _sigmoid(mo_ref[...])

    q = gq_ref[...] * (GLA_DK ** -0.5)
    k = gk_ref[...]
    v = gv_ref[...]
    eb = jnp.exp(_log_sigmoid(_dot(au_ref[...], ga_ref[...]) + ab_ref[...]) / GLA_TAU)
    attn = _rowsum(q * k)
    qe = q * eb
    gv_scr[0] = qe
    gv_scr[1] = eb
    gv_scr[2] = k
    row = lambda n, d: jnp.broadcast_to(gv_scr[n, pl.ds(d, 1), :], (HEAD_W, DEC_B))
    y = attn * v
    for d in range(GLA_DK):
        sd = gs_ref[d]
        y = y + row(0, d) * sd
        gs_o[d] = row(1, d) * sd + row(2, d) * v
    ms = jnp.mean(y * y, axis=0, keepdims=True)
    oc_ref[...] = y * lax.rsqrt(ms + 1e-6) * col(PC_GLNW) * _silu(gg_ref[...])

    xc = col(PC_RCB) + col(PC_RCW + 3) * rx_ref[...]
    for j in range(3):
        xc = xc + col(PC_RCW + j) * rb_ref[j]
    rg = _sigmoid(_dot(wa_ref[...], xc) + col(PC_RBA))
    ig = _sigmoid(_dot(wx_ref[...], xc) + col(PC_RBX))
    log_a = -RG_C * rg * _softplus(-col(PC_RLAM))
    hn = jnp.exp(log_a) * h_ref[...] + jnp.sqrt(1.0 - jnp.exp(2.0 * log_a)) * (ig * xc)
    h_o[...] = hn
    od_ref[...] = hn * _gelu_tanh(ry_ref[...])

    store_slabs(srw_o, srw_out, HEAD_W)
    store_slabs(c_o, c_out, HEAD_W)
    store_slabs(gs_o, gs_out, HEAD_W)


def _decode_mixers(ut_rw, prev_t, mu_c, lora_t, s_rw,
                   ut_ml, mconv_t, ml_cw, c_st, n_st, m_st,
                   ut_gl, au_t, ab_c, g_st,
                   ut_rg, rconv_t, wa_t, wx_t, h_st, pcols):
    nb = DEC_B
    H, HW = HEADS, HEAD_W

    def blk(arr, view, block, index):
        a = arr.reshape(view)
        nd = len(block)
        return a, pl.BlockSpec(block, index)

    ins = []
    v14 = (14, HW, nb)
    v7 = (7, 128, nb)
    for arr in (ut_rw, prev_t):
        ins.append(blk(arr, v14, (None, HW, nb), lambda h: (h, 0, 0)))
        ins.append(blk(arr, v14, (None, HW, nb), lambda h: (4 + h, 0, 0)))
        ins.append(blk(arr, v14, (None, HW, nb), lambda h: (8 + h, 0, 0)))
        ins.append(blk(arr, v7, (None, 128, nb), lambda h: (6, 0, 0)))
    ins.append(blk(mu_c, (14, HW, 1), (None, HW, 1), lambda h: (h, 0, 0)))
    ins.append(blk(mu_c, (14, HW, 1), (None, HW, 1), lambda h: (4 + h, 0, 0)))
    ins.append(blk(mu_c, (14, HW, 1), (None, HW, 1), lambda h: (8 + h, 0, 0)))
    ins.append(blk(mu_c, (7, 128, 1), (None, 128, 1), lambda h: (6, 0, 0)))
    ins.append(blk(lora_t, (3, BRANCH_W, 128), (3, HW, 128), lambda h: (0, h, 0)))
    st_spec = lambda n: pl.BlockSpec((nb, n), lambda h: (0, h))
    ins.append((s_rw.reshape(nb, H * HW * HW), st_spec(HW * HW)))
    v20 = (20, HW, nb)
    for j in range(4):
        ins.append(blk(ut_ml, v20, (None, HW, nb), functools.partial(lambda h, j: (4 * j + h, 0, 0), j=j)))
    ins.append(blk(ut_ml[1024:1028], (4, 1, nb), (None, 1, nb), lambda h: (h, 0, 0)))
    ins.append(blk(ut_ml[1152:1156], (4, 1, nb), (None, 1, nb), lambda h: (h, 0, 0)))
    ins.append(blk(mconv_t, (3, 8, HW, nb), (3, None, HW, nb), lambda h: (0, h, 0, 0)))
    ins.append(blk(mconv_t, (3, 8, HW, nb), (3, None, HW, nb), lambda h: (0, 4 + h, 0, 0)))
    ins.append(blk(ml_cw, (8, HW, 5), (None, HW, 5), lambda h: (h, 0, 0)))
    ins.append(blk(ml_cw, (8, HW, 5), (None, HW, 5), lambda h: (4 + h, 0, 0)))
    ins.append((c_st.reshape(nb, H * HW * HW), st_spec(HW * HW)))
    ins.append(blk(n_st, (H, HW, nb), (None, HW, nb), lambda h: (h, 0, 0)))
    ins.append(blk(m_st, (H, 1, nb), (None, 1, nb), lambda h: (h, 0, 0)))
    ins.append(blk(ut_gl, (28, GLA_DK, nb), (None, GLA_DK, nb), lambda h: (h, 0, 0)))
    ins.append(blk(ut_gl, (28, GLA_DK, nb), (None, GLA_DK, nb), lambda h: (4 + h, 0, 0)))
    ins.append(blk(ut_gl, v14, (None, HW, nb), lambda h: (4 + h, 0, 0)))
    ins.append(blk(ut_gl, v14, (None, HW, nb), lambda h: (8 + h, 0, 0)))
    ins.append(blk(ut_gl, v7, (None, 128, nb), lambda h: (6, 0, 0)))
    ins.append(blk(au_t, (H, GLA_DK, 128), (None, GLA_DK, 128), lambda h: (h, 0, 0)))
    ins.append(blk(ab_c, (H, GLA_DK, 1), (None, GLA_DK, 1), lambda h: (h, 0, 0)))
    ins.append((g_st.reshape(nb, H * GLA_DK * HW), st_spec(GLA_DK * HW)))
    ins.append(blk(ut_rg, (8, HW, nb), (None, HW, nb), lambda h: (h, 0, 0)))
    ins.append(blk(ut_rg, (8, HW, nb), (None, HW, nb), lambda h: (4 + h, 0, 0)))
    ins.append(blk(rconv_t, (3, H, HW, nb), (3, None, HW, nb), lambda h: (0, h, 0, 0)))
    ins.append((wa_t, pl.BlockSpec((None, HW, HW), lambda h: (h, 0, 0))))
    ins.append((wx_t, pl.BlockSpec((None, HW, HW), lambda h: (h, 0, 0))))
    ins.append(blk(h_st, (H, HW, nb), (None, HW, nb), lambda h: (h, 0, 0)))
    ins.append(blk(pcols, (H, HW, PC_COLS), (None, HW, PC_COLS), lambda h: (h, 0, 0)))

    vec = lambda: (jax.ShapeDtypeStruct((H, HW, nb), F32), pl.BlockSpec((None, HW, nb), lambda h: (h, 0, 0)))
    mat = lambda n: (jax.ShapeDtypeStruct((nb, H * n), F32), st_spec(n))
    outs = [vec(), vec(), vec(), vec(), mat(HW * HW), mat(HW * HW), vec(),
            (jax.ShapeDtypeStruct((H, 1, nb), F32), pl.BlockSpec((None, 1, nb), lambda h: (h, 0, 0))),
            mat(GLA_DK * HW), vec()]
    return pl.pallas_call(
        _decode_kernel,
        out_shape=[o[0] for o in outs],
        grid=(H,),
        in_specs=[s for _, s in ins],
        out_specs=[o[1] for o in outs],
        scratch_shapes=[pltpu.VMEM((HW, nb), F32)]
        + [pltpu.VMEM((HW, HW, nb), F32)] * 4 + [pltpu.VMEM((GLA_DK, HW, nb), F32)] * 2
        + [pltpu.VMEM((3, GLA_DK, nb), F32)],
        compiler_params=_cp("parallel"),
        name="decode_mixers",
    )(*[a for a, _ in ins])


def _pad_cols(a, width):
    return jnp.pad(a, ((0, 0), (0, width - a.shape[1])))


def _pad_rows(a, rows, at=0):
    return jnp.pad(a, ((at, rows - at - a.shape[0]), (0, 0)))


def _pack_layer(p, l):
    offs = np.concatenate([[0], np.cumsum(IN_SIZES)])
    w_in, b_in = p['w_in'][l], p['b_in'][l][None, :]
    seg = lambda a, i: a[:, int(offs[i]):int(offs[i + 1])]

    def regroup(a):
        return jnp.concatenate(
            [seg(a, 0), seg(a, 1), seg(a, 2), seg(a, 3), _pad_cols(seg(a, 4), 128), _pad_cols(seg(a, 5), 128),
             seg(a, 6), seg(a, 7), seg(a, 8), seg(a, 9), _pad_cols(seg(a, 10), 128), seg(a, 11), seg(a, 12)],
            axis=1)
    k = dict(
        w_pack=regroup(w_in.astype(BF16)), b_pack=regroup(b_in),
        w_gate=seg(w_in.astype(BF16), 13), b_gate=seg(b_in, 13),
        w_branch=p['w_branch'][l].astype(BF16), w_out=p['w_out'][l].astype(BF16),
        ln1_g=p['ln1_g'][l][None], ln1_b=p['ln1_b'][l][None],
        ln2_g=p['ln2_g'][l][None], ln2_b=p['ln2_b'][l][None],
    )
    w_up, a_up, g_up = p['rwkv_w_up'][l], p['rwkv_a_up'][l], p['rwkv_g_up'][l]
    lora = jnp.stack([_pad_rows(w_up, 128, 0), _pad_rows(a_up, 128, 32), _pad_rows(g_up, 128, 64)])
    k['rw_lora'] = lora.astype(BF16)
    k['rw_lora_t'] = jnp.swapaxes(lora, 1, 2).astype(BF16)
    k['rw_mu'] = p['rwkv_mu'][l][None]
    rw_rows = [p['rwkv_w0'][l], p['rwkv_a0'][l], p['rwkv_k_k'][l], p['rwkv_k_a'][l],
               p['rwkv_r_k'][l].reshape(BRANCH_W)]
    k['rw_p'] = jnp.stack(rw_rows + [jnp.zeros((BRANCH_W,), F32)] * 3)
    k['rw_ln'] = jnp.stack([p['rwkv_ln_w'][l], p['rwkv_ln_b'][l]])
    k['ml_cw'], k['ml_cb'] = p['mlstm_conv_w'][l], p['mlstm_conv_b'][l][None]
    k['ml_nw'] = p['mlstm_norm_w'][l][None]
    k['gl_au'] = _pad_rows(p['gla_alpha_up'][l], 128).astype(BF16)
    k['gl_ab'] = p['gla_alpha_b'][l][None]
    k['gl_nw'] = p['gla_norm_w'][l][None]
    wa, wx = p['rglru_wa'][l], p['rglru_wx'][l]
    eye = jnp.eye(HEADS, dtype=F32)
    bd = lambda w: jnp.einsum('gh,gij->gihj', eye, w).reshape(BRANCH_W, BRANCH_W).astype(BF16)
    k['rg_cw'], k['rg_cb'] = p['rglru_conv_w'][l], p['rglru_conv_b'][l][None]
    k['rg_wa'], k['rg_wx'] = bd(wa), bd(wx)
    k['rg_ba'], k['rg_bx'] = p['rglru_ba'][l][None], p['rglru_bx'][l][None]
    k['rg_lam'] = p['rglru_lambda'][l][None]
    k['rg_wa_t'] = jnp.swapaxes(wa, 1, 2).astype(BF16)
    k['rg_wx_t'] = jnp.swapaxes(wx, 1, 2).astype(BF16)
    cols = rw_rows + [p['rwkv_ln_w'][l], p['rwkv_ln_b'][l], p['mlstm_norm_w'][l], p['gla_norm_w'][l]]
    cols += [p['rglru_conv_w'][l][j] for j in range(4)]
    cols += [p['rglru_conv_b'][l], p['rglru_ba'][l], p['rglru_bx'][l], p['rglru_lambda'][l]]
    k['pcols'] = jnp.stack(cols, axis=1)
    k['ml_cw_t'] = jnp.concatenate([p['mlstm_conv_w'][l].T, p['mlstm_conv_b'][l][:, None]], axis=1)
    return k


def _diag_blocks(a, rb, cb):
    return jnp.stack([a[:, h * rb:(h + 1) * rb, h * cb:(h + 1) * cb] for h in range(HEADS)], axis=1)


def _prompt_mixers(x, k):
    B, L, _ = x.shape
    W = BRANCH_W
    u_rw, u_ml, u_gl, u_rg = _inproj(x.reshape(B * L, D_MODEL), k['w_pack'], k['b_pack'], 512)
    u_rw, u_ml, u_gl, u_rg = (u.reshape(B, L, -1) for u in (u_rw, u_ml, u_gl, u_rg))
    nch = B * HEADS
    rep = 128 // nch
    *planes, g, bv = _rwkv_prep(u_rw, k['rw_mu'], k['rw_p'], k['rw_lora'])
    y, s1 = _rwkv_scan(planes, L)
    y = y.reshape(L, HEAD_W // rep, rep, B, HEADS).transpose(3, 0, 4, 2, 1).reshape(B, L, W)
    s_rw = s1.reshape(HEAD_W, HEAD_W // rep, rep, B, HEADS).transpose(3, 4, 2, 1, 0).reshape(B, HEADS, HEAD_W, HEAD_W)
    flat = lambda a: a.reshape(B * L, W)
    o_a = (flat(y), flat(g), flat(bv), k['rw_ln'])
    o_b, c_bd, n_row, m_row = _mlstm_prompt(u_ml, k['ml_cw'], k['ml_cb'], k['ml_nw'])
    o_c, s_bd = _gla_prompt(u_gl, k['gl_au'], k['gl_ab'], k['gl_nw'])
    o_d, h1 = _rglru_prompt(u_rg, k['rg_cw'], k['rg_cb'], k['rg_wa'], k['rg_ba'], k['rg_wx'], k['rg_bx'],
                            k['rg_lam'])
    states = (u_rw[:, L - 1], s_rw, u_ml[:, L - 3:, :2 * W], _diag_blocks(c_bd, HEAD_W, HEAD_W),
              n_row.reshape(B, HEADS, HEAD_W), m_row[:, :HEADS, 0],
              jnp.swapaxes(_diag_blocks(s_bd, HEAD_W, GLA_DK), 2, 3), u_rg[:, L - 3:, :W], h1[:, 0])
    return [o_a] + [flat(o) for o in (o_b, o_c, o_d)], states


def _sample_mixers(x, st, k):
    W = BRANCH_W
    sh0, S0, mconv0, C0, n0, m0, gS0, rconv0, h0 = st
    u_rw, u_ml, u_gl, u_rg = _inproj(x, k['w_pack'], k['b_pack'], DEC_B)
    outs = _decode_mixers(
        u_rw.T, sh0.T, k['rw_mu'].T, k['rw_lora_t'], S0,
        u_ml.T, mconv0.transpose(1, 2, 0), k['ml_cw_t'], C0, n0.transpose(1, 2, 0), m0.T,
        u_gl.T, k['gl_au'].T, k['gl_ab'].T, gS0,
        u_rg.T, rconv0.transpose(1, 2, 0), k['rg_wa_t'], k['rg_wx_t'], h0.T, k['pcols'])
    oa, ob, oc, od, s_rw, c_st, n_st, m_st, g_st, h_st = outs
    branches = [o.reshape(W, DEC_B).T for o in (oa, ob, oc, od)]
    states = (u_rw, s_rw.reshape(S0.shape),
              jnp.concatenate([mconv0[:, 1:], u_ml[:, None, :2 * W]], axis=1),
              c_st.reshape(C0.shape), n_st.transpose(2, 0, 1), m_st[:, 0, :].T,
              g_st.reshape(gS0.shape),
              jnp.concatenate([rconv0[:, 1:], u_rg[:, None, :W]], axis=1), h_st.reshape(W, DEC_B).T)
    return branches, states


def _trunk(xp, bl, xs, sample_states, packs, moe):
    tms = (512, DEC_B)
    new_p, new_s = [], []
    for l in range(DEPTH):
        k = packs[l]
        br_p, st_p = _prompt_mixers(xp.reshape(bl[0], bl[1], D_MODEL), k)
        br_s, st_s = _sample_mixers(xs, tuple(s[l] for s in sample_states), k)
        new_p.append(st_p)
        new_s.append(st_s)
        xp, xs = (_merge(x, br, k['w_gate'], k['b_gate'], k['w_branch'], k['w_out'], k['ln1_g'], k['ln1_b'], tm)
                  for x, br, tm in ((xp, br_p, tms[0]), (xs, br_s, tms[1])))
        j = l // 2
        if l % 2 == 0:
            xp, xs = (_ffn(x, moe['ffn_wg'][j], moe['ffn_wu'][j], moe['ffn_wd'][j], k['ln2_g'], k['ln2_b'], tm)
                      for x, tm in ((xp, tms[0]), (xs, tms[1])))
        else:
            xp, xs = _moe_layer([xp, xs], moe['router'][j], moe['router_b'][j], moe['moe_wg'][j],
                                moe['moe_wu'][j], moe['moe_wd'][j], k['ln2_g'], k['ln2_b'])
    stack = lambda sts: [jnp.stack([st[i] for st in sts], axis=0) for i in range(9)]
    return xp, xs, stack(new_p), stack(new_s)


def kernel(x_prompt, x_sample, state_rwkv_shift, state_rwkv_S, state_mlstm_conv, state_mlstm_C,
           state_mlstm_n, state_mlstm_m, state_gla_S, state_rglru_conv, state_rglru_h,
           w_in, b_in, rwkv_mu, rwkv_w0, rwkv_w_up, rwkv_a0, rwkv_a_up, rwkv_g_up, rwkv_k_k,
           rwkv_k_a, rwkv_r_k, rwkv_ln_w, rwkv_ln_b, mlstm_conv_w, mlstm_conv_b, mlstm_norm_w,
           gla_alpha_up, gla_alpha_b, gla_norm_w, rglru_conv_w, rglru_conv_b, rglru_wa, rglru_ba,
           rglru_wx, rglru_bx, rglru_lambda, w_branch, w_out, ln1_g, ln1_b, ffn_wg, ffn_wu, ffn_wd,
           moe_router, moe_router_b, moe_wg, moe_wu, moe_wd, ln2_g, ln2_b):
    p = dict(w_in=w_in, b_in=b_in, rwkv_mu=rwkv_mu, rwkv_w0=rwkv_w0, rwkv_w_up=rwkv_w_up,
             rwkv_a0=rwkv_a0, rwkv_a_up=rwkv_a_up, rwkv_g_up=rwkv_g_up, rwkv_k_k=rwkv_k_k,
             rwkv_k_a=rwkv_k_a, rwkv_r_k=rwkv_r_k, rwkv_ln_w=rwkv_ln_w, rwkv_ln_b=rwkv_ln_b,
             mlstm_conv_w=mlstm_conv_w, mlstm_conv_b=mlstm_conv_b, mlstm_norm_w=mlstm_norm_w,
             gla_alpha_up=gla_alpha_up, gla_alpha_b=gla_alpha_b, gla_norm_w=gla_norm_w,
             rglru_conv_w=rglru_conv_w, rglru_conv_b=rglru_conv_b, rglru_wa=rglru_wa,
             rglru_ba=rglru_ba, rglru_wx=rglru_wx, rglru_bx=rglru_bx, rglru_lambda=rglru_lambda,
             w_branch=w_branch, w_out=w_out, ln1_g=ln1_g, ln1_b=ln1_b, ln2_g=ln2_g, ln2_b=ln2_b)
    packs = [_pack_layer(p, l) for l in range(DEPTH)]
    moe = dict(ffn_wg=ffn_wg.astype(BF16), ffn_wu=ffn_wu.astype(BF16), ffn_wd=ffn_wd.astype(BF16),
               router=jnp.pad(moe_router, ((0, 0), (0, 0), (0, 128 - N_EXPERTS))),
               router_b=jnp.pad(moe_router_b, ((0, 0), (0, 128 - N_EXPERTS)))[:, None, :],
               moe_wg=moe_wg.astype(BF16), moe_wu=moe_wu.astype(BF16), moe_wd=moe_wd.astype(BF16))
    B, L, _ = x_prompt.shape
    sample_states = (state_rwkv_shift, state_rwkv_S, state_mlstm_conv, state_mlstm_C, state_mlstm_n,
                     state_mlstm_m, state_gla_S, state_rglru_conv, state_rglru_h)
    nb, ls, _ = x_sample.shape
    y_p, y_s, ps, ss = _trunk(x_prompt.reshape(B * L, D_MODEL), (B, L), x_sample.reshape(nb * ls, D_MODEL),
                              sample_states, packs, moe)
    return (y_p.reshape(B, L, D_MODEL), y_s.reshape(nb, ls, D_MODEL), *ps, *ss)
```
